```python
import math, functools
import jax, jax.numpy as jnp
from jax import lax
import numpy as np

D_MODEL = 2048
BATCH = 4
SEQ = 4096
DEPTH = 2

GRID_W = 64
CTX_LEN = 256
N_BRANCH = 4
BRANCH_W = 512
NORM_EPS = 1e-6
Q_BLOCK = 128
ROPE_THETA = 10000.0
SHORT_CONV = 3

GDN_HEADS = 4
GDN_HEAD_DIM = 128
GDN_CHUNK = 64

MLA_HEADS = 4
MLA_Q_LORA = 512
MLA_KV_LORA = 256
MLA_NOPE = 128
MLA_ROPE = 64
MLA_V = 128

GQA_HEADS = 8
GQA_KV_HEADS = 2
GQA_HEAD_DIM = 64

HY_WIDTH = 512
HY_ORDER = 2
HY_EMB = 33
HY_HIDDEN = 64
HY_DECAY_TARGET = 1e-2
HY_FAST_DECAY = 0.3
HY_SLOW_DECAY = 1.5

N_EXPERTS = 16
N_GROUPS = 4
EXPERTS_PER_GROUP = N_EXPERTS // N_GROUPS
TOP_K = 2
D_EXPERT = 512

GDN_W = GDN_HEADS * GDN_HEAD_DIM
IN_WIDTHS = (GDN_W, GDN_W, GDN_W, GDN_W, 2 * GDN_HEADS, 2 * GDN_HEADS,
             MLA_Q_LORA, MLA_KV_LORA, MLA_ROPE,
             GQA_HEADS * GQA_HEAD_DIM, GQA_KV_HEADS * GQA_HEAD_DIM, GQA_KV_HEADS * GQA_HEAD_DIM,
             (HY_ORDER + 1) * HY_WIDTH)
MIX_IN = sum(IN_WIDTHS)
IN_DIM = MIX_IN + N_BRANCH * D_MODEL

kernel_name = 'hybrid_diffusion_trunk'


def rms_norm(x, w):
    xf = x.astype(jnp.float32)
    y = xf * lax.rsqrt(jnp.mean(xf * xf, axis=-1, keepdims=True) + NORM_EPS)
    return (y * w.astype(jnp.float32)).astype(x.dtype)


def l2_normalize(x):
    xf = x.astype(jnp.float32)
    return xf * lax.rsqrt(jnp.sum(xf * xf, axis=-1, keepdims=True) + NORM_EPS)


def depthwise_conv_centred(u, w):
    k = w.shape[0]
    return lax.conv_general_dilated(u, w[:, None, :].astype(u.dtype), window_strides=(1,),
                                    padding=[(k // 2, k // 2)],
                                    dimension_numbers=('NWC', 'WIO', 'NWC'),
                                    feature_group_count=u.shape[-1])


def axial_rope_tables(rows, rot_dim):
    n_freq = rot_dim // 4
    freqs = ROPE_THETA ** (-jnp.arange(n_freq, dtype=jnp.float32) / n_freq)
    row = jnp.repeat(jnp.arange(rows, dtype=jnp.float32), GRID_W)
    col = jnp.tile(jnp.arange(GRID_W, dtype=jnp.float32), rows)
    ang = jnp.concatenate([row[:, None] * freqs, col[:, None] * freqs], axis=-1)
    return jnp.cos(ang), jnp.sin(ang)


def apply_rope(x, cos, sin):
    half = x.shape[-1] // 2
    x1, x2 = x[..., :half], x[..., half:]
    cos, sin = cos[None, :, None, :], sin[None, :, None, :]
    return jnp.concatenate([x1 * cos - x2 * sin, x2 * cos + x1 * sin], axis=-1).astype(x.dtype)


def block_attention(q, k, v):
    b, sq = q.shape[:2]
    scale = q.shape[-1] ** -0.5
    qb = q.reshape(b, sq // Q_BLOCK, Q_BLOCK, *q.shape[2:]).swapaxes(0, 1)

    def attend(qi):
        s = jnp.einsum('bqhgd,bkhd->bhgqk', qi, k).astype(jnp.float32) * scale
        p = jax.nn.softmax(s, axis=-1).astype(v.dtype)
        return jnp.einsum('bhgqk,bkhe->bqhge', p, v)

    ob = lax.map(attend, qb)
    return ob.swapaxes(0, 1).reshape(b, sq, *ob.shape[3:])


def gdn_prep(q, k, v, a, bt, conv_w, a_log, dt_bias):
    b, l = q.shape[:2]
    qkv = jax.nn.silu(depthwise_conv_centred(jnp.concatenate([q, k, v], axis=-1), conv_w)).astype(jnp.float32)
    q, k, v = jnp.split(qkv, 3, axis=-1)
    hd = (b, l, GDN_HEADS, GDN_HEAD_DIM)
    q = l2_normalize(q.reshape(hd)) * GDN_HEAD_DIM ** -0.5
    k = l2_normalize(k.reshape(hd))
    v = v.reshape(hd)
    a = a.astype(jnp.float32).reshape(b, l, 2, GDN_HEADS)
    g = -jnp.exp(a_log.astype(jnp.float32)) * jax.nn.softplus(a + dt_bias.astype(jnp.float32))
    beta = jax.nn.sigmoid(bt.astype(jnp.float32).reshape(b, l, 2, GDN_HEADS))
    return q, k, v, g, beta


def gated_delta_rule(q, k, v, g, beta, state, with_out):
    b, l, h, _ = q.shape
    dv = v.shape[-1]
    c = GDN_CHUNK
    n = l // c

    def to_chunks(t):
        t = t.reshape(b, n, c, h, *t.shape[3:])
        return jnp.moveaxis(t, (1, 3), (0, 2))

    qc, kc, vc, bc = to_chunks(q), to_chunks(k), to_chunks(v), to_chunks(beta)
    gc = jnp.cumsum(to_chunks(g), axis=-1)
    idx = jnp.arange(c)
    lower = idx[:, None] >= idx[None, :]
    strict = idx[:, None] > idx[None, :]
    diff = gc[..., :, None] - gc[..., None, :]
    decay = jnp.where(lower, jnp.exp(jnp.where(lower, diff, 0.0)), 0.0)
    kb = kc * bc[..., None]
    a = jnp.where(strict, jnp.einsum('nbhid,nbhjd->nbhij', kb, kc) * decay, 0.0)
    solve = functools.partial(lax.linalg.triangular_solve, left_side=True, lower=True, unit_diagonal=True)
    u = solve(a, vc * bc[..., None])
    w = solve(a, kb * jnp.exp(gc)[..., None])
    g_last = gc[..., -1]
    k_dec = kc * jnp.exp(g_last[..., None] - gc)[..., None]
    xs = (u, w, k_dec, g_last)
    if with_out:
        qk = jnp.where(lower, jnp.einsum('nbhid,nbhjd->nbhij', qc, kc) * decay, 0.0)
        xs = xs + (qc * jnp.exp(gc)[..., None], qk)

    def step(s, inp):
        u_i, w_i, kd_i, gl_i = inp[:4]
        v_new = u_i - jnp.einsum('bhck,bhkv->bhcv', w_i, s)
        s_new = s * jnp.exp(gl_i)[..., None, None] + jnp.einsum('bhck,bhcv->bhkv', kd_i, v_new)
        if not with_out:
            return s_new, None
        qd_i, qk_i = inp[4:]
        o = jnp.einsum('bhck,bhkv->bhcv', qd_i, s) + jnp.einsum('bhij,bhjv->bhiv', qk_i, v_new)
        return s_new, o

    state, o = lax.scan(step, state, xs)
    if not with_out:
        return None, state
    o = jnp.moveaxis(o, (0, 2), (1, 3)).reshape(b, l, h, dv)
    return o, state


def gdn_output(o, z, norm_w):
    b, l = z.shape[:2]
    zh = z.reshape(b, l, GDN_HEADS, GDN_HEAD_DIM).astype(jnp.float32)
    y = rms_norm(o, norm_w) * jax.nn.silu(zh)
    return y.reshape(b, l, GDN_W).astype(z.dtype)


def gdn_mixer(p_lat, p_ctx, conv_w, a_log, dt_bias, norm_w, ctx_out):
    lat = gdn_prep(p_lat[0], p_lat[1], p_lat[2], p_lat[4], p_lat[5], conv_w, a_log, dt_bias)
    ctx = gdn_prep(p_ctx[0], p_ctx[1], p_ctx[2], p_ctx[4], p_ctx[5], conv_w, a_log, dt_bias)
    b = p_lat[0].shape[0]
    s0 = jnp.zeros((b, GDN_HEADS, GDN_HEAD_DIM, GDN_HEAD_DIM), jnp.float32)
    o_lat, o_ctx = 0.0, 0.0
    for direction in range(2):
        flip = (lambda t: t[:, ::-1]) if direction else (lambda t: t)

        def seq_args(s):
            q, k, v, g, beta = s
            return flip(q), flip(k), flip(v), flip(g[:, :, direction]), flip(beta[:, :, direction])

        oc, s_ctx = gated_delta_rule(*seq_args(ctx), s0, ctx_out)
        ol, _ = gated_delta_rule(*seq_args(lat), s_ctx, True)
        o_lat = o_lat + flip(ol)
        if ctx_out:
            o_ctx = o_ctx + flip(oc)
    out_lat = gdn_output(o_lat, p_lat[3], norm_w)
    out_ctx = gdn_output(o_ctx, p_ctx[3], norm_w) if ctx_out else None
    return out_lat, out_ctx


def mla_q(cq, q_norm_w, w_uq, rope):
    q = jnp.einsum('blr,rhe->blhe', rms_norm(cq, q_norm_w), w_uq)
    if rope is not None:
        q = jnp.concatenate([q[..., :MLA_NOPE], apply_rope(q[..., MLA_NOPE:], *rope)], axis=-1)
    return q[:, :, :, None, :]


def mla_kv(ckv, k_rope, kv_norm_w, w_ukv, rope):
    kv = jnp.einsum('blr,rhe->blhe', rms_norm(ckv, kv_norm_w), w_ukv)
    k_nope, v = kv[..., :MLA_NOPE], kv[..., MLA_NOPE:]
    k_rope = k_rope[:, :, None, :]
    if rope is not None:
        k_rope = apply_rope(k_rope, *rope)
    k_rope = jnp.broadcast_to(k_rope, k_nope.shape[:3] + (MLA_ROPE,)).astype(k_nope.dtype)
    return jnp.concatenate([k_nope, k_rope], axis=-1), v


def gqa_q(q, q_norm_w, rope):
    b, l = q.shape[:2]
    q = rms_norm(q.reshape(b, l, GQA_HEADS, GQA_HEAD_DIM), q_norm_w)
    if rope is not None:
        q = apply_rope(q, *rope)
    return q.reshape(b, l, GQA_KV_HEADS, GQA_HEADS // GQA_KV_HEADS, GQA_HEAD_DIM)


def gqa_kv(k, v, k_norm_w, rope):
    b, l = k.shape[:2]
    k = rms_norm(k.reshape(b, l, GQA_KV_HEADS, GQA_HEAD_DIM), k_norm_w)
    if rope is not None:
        k = apply_rope(k, *rope)
    return k, v.reshape(b, l, GQA_KV_HEADS, GQA_HEAD_DIM)


def hyena_filters(length, w1, b1, w2, b2, w3, sin_freq):
    t = jnp.arange(length, dtype=jnp.float32)
    bands = (HY_EMB - 1) // 2
    f = jnp.linspace(1e-4, bands - 1, bands, dtype=jnp.float32)
    phase = (2.0 * math.pi / length) * t[:, None] * f[None, :]
    feats = jnp.concatenate([t[:, None] / (length - 1), jnp.cos(phase), -jnp.sin(phase)], axis=-1)
    hid = jnp.sin(sin_freq[0] * (feats @ w1 + b1))
    hid = jnp.sin(sin_freq[1] * (hid @ w2 + b2))
    filt = (hid @ w3).astype(jnp.float32)
    centre = length // 2
    dist = jnp.abs(t - centre) / centre
    deltas = jnp.abs(jnp.linspace(math.log(HY_DECAY_TARGET) / HY_SLOW_DECAY,
                                  math.log(HY_DECAY_TARGET) / HY_FAST_DECAY,
                                  HY_ORDER * HY_WIDTH, dtype=jnp.float32))
    filt = filt * jnp.exp(-dist[:, None] * deltas[None, :])
    filt = filt / jnp.sum(jnp.abs(filt), axis=0, keepdims=True)
    return filt.reshape(length, HY_ORDER, HY_WIDTH)


def fft_conv_centred(u, h):
    l = u.shape[1]
    n = 2 * l
    uf = jnp.fft.rfft(u.astype(jnp.float32), n=n, axis=1)
    hf = jnp.fft.rfft(h.astype(jnp.float32), n=n, axis=0)
    y = jnp.fft.irfft(uf * hf[None], n=n, axis=1)
    return y[:, l // 2: l // 2 + l]


def hyena_mixer(u, conv_w, filt, bias):
    parts = jnp.split(depthwise_conv_centred(u, conv_w).astype(jnp.float32), HY_ORDER + 1, axis=-1)
    z = parts[0]
    for o in range(HY_ORDER):
        z = parts[o + 1] * (fft_conv_centred(z, filt[:, o]) + bias[o] * z)
    return z


def merge_branches(branches, gate_logits, w_branch, w_out):
    stacked = jnp.stack(branches, axis=-2)
    gates = jax.nn.sigmoid(gate_logits.reshape(*gate_logits.shape[:-1], N_BRANCH, D_MODEL))
    merged = jnp.einsum('blnd,blnd->bld', gates, jnp.einsum('blnw,nwd->blnd', stacked, w_branch))
    return merged @ w_out


def moe_ffn(h, w_router, router_bias, w_gate, w_up, w_down):
    scores = jax.nn.sigmoid(jnp.einsum('bld,de->ble', h, w_router).astype(jnp.float32))
    sel = scores + router_bias.astype(jnp.float32)
    grp = sel.reshape(*sel.shape[:-1], N_GROUPS, EXPERTS_PER_GROUP)
    grp_score = jnp.sum(lax.top_k(grp, TOP_K)[0], axis=-1)
    grp_mask = jnp.argmax(grp_score, axis=-1)[..., None] == jnp.arange(N_GROUPS)
    expert_mask = jnp.repeat(grp_mask, EXPERTS_PER_GROUP, axis=-1)
    _, e_idx = lax.top_k(jnp.where(expert_mask, sel, -jnp.inf), TOP_K)
    wts = jnp.take_along_axis(scores, e_idx, axis=-1)
    wts = wts / jnp.sum(wts, axis=-1, keepdims=True)
    gate = jnp.sum(jax.nn.one_hot(e_idx, N_EXPERTS, dtype=jnp.float32) * wts[..., None], axis=-2)
    hg = jnp.einsum('bld,edf->blef', h, w_gate)
    hu = jnp.einsum('bld,edf->blef', h, w_up)
    act = jax.nn.silu(hg) * hu * gate[..., None].astype(h.dtype)
    return jnp.einsum('blef,efd->bld', act, w_down)


def setup_inputs(seed: int = 0) -> dict:
    key = jax.random.key(seed)
    keys = iter(jax.random.split(key, 48))

    def normal(shape, scale):
        return jax.random.normal(next(keys), shape, jnp.float32) * scale

    def gain(shape):
        return 1.0 + normal(shape, 0.05)

    a_log = jnp.log(jax.random.uniform(next(keys), (DEPTH, 2, GDN_HEADS), jnp.float32, 1.0, 16.0))
    dt = jnp.exp(jax.random.uniform(next(keys), (DEPTH, 2, GDN_HEADS), jnp.float32,
                                    math.log(1e-3), math.log(1e-1)))
    dt_bias = dt + jnp.log(-jnp.expm1(-dt))
    d = D_MODEL
    return {
        'x': normal((BATCH, SEQ, d), 1.0),
        'c': normal((BATCH, d), 1.0),
        'ctx': normal((BATCH, CTX_LEN, d), 1.0),
        'c_ctx': normal((d,), 1.0),
        'w_ada': normal((DEPTH, d, 6 * d), 0.5 * d ** -0.5),
        'b_ada': normal((DEPTH, 6 * d), 0.02),
        'norm1_w': gain((DEPTH, d)),
        'norm2_w': gain((DEPTH, d)),
        'w_in': normal((DEPTH, d, IN_DIM), d ** -0.5),
        'gdn_conv_w': normal((DEPTH, SHORT_CONV, 3 * GDN_W), SHORT_CONV ** -0.5),
        'gdn_a_log': a_log,
        'gdn_dt_bias': dt_bias,
        'gdn_norm_w': gain((DEPTH, GDN_HEAD_DIM)),
        'mla_q_norm_w': gain((DEPTH, MLA_Q_LORA)),
        'mla_kv_norm_w': gain((DEPTH, MLA_KV_LORA)),
        'mla_w_uq': normal((DEPTH, MLA_Q_LORA, MLA_HEADS, MLA_NOPE + MLA_ROPE), MLA_Q_LORA ** -0.5),
        'mla_w_ukv': normal((DEPTH, MLA_KV_LORA, MLA_HEADS, MLA_NOPE + MLA_V), MLA_KV_LORA ** -0.5),
        'gqa_q_norm_w': gain((DEPTH, GQA_HEAD_DIM)),
        'gqa_k_norm_w': gain((DEPTH, GQA_HEAD_DIM)),
        'hy_conv_w': normal((DEPTH, SHORT_CONV, (HY_ORDER + 1) * HY_WIDTH), SHORT_CONV ** -0.5),
        'hy_w1': normal((DEPTH, HY_EMB, HY_HIDDEN), HY_EMB ** -0.5),
        'hy_b1': normal((DEPTH, HY_HIDDEN), 0.1),
        'hy_w2': normal((DEPTH, HY_HIDDEN, HY_HIDDEN), HY_HIDDEN ** -0.5),
        'hy_b2': normal((DEPTH, HY_HIDDEN), 0.1),
        'hy_w3': normal((DEPTH, HY_HIDDEN, HY_ORDER * HY_WIDTH), HY_HIDDEN ** -0.5),
        'hy_sin_freq': gain((DEPTH, 2, HY_HIDDEN)),
        'hy_bias': normal((DEPTH, HY_ORDER, HY_WIDTH), 0.5),
        'w_branch': normal((DEPTH, N_BRANCH, BRANCH_W, d), BRANCH_W ** -0.5),
        'w_out': normal((DEPTH, d, d), d ** -0.5),
        'w_router': normal((d, N_EXPERTS), d ** -0.5),
        'router_bias': normal((N_EXPERTS,), 0.01),
        'moe_w_gate': normal((DEPTH, N_EXPERTS, d, D_EXPERT), d ** -0.5),
        'moe_w_up': normal((DEPTH, N_EXPERTS, d, D_EXPERT), d ** -0.5),
        'moe_w_down': normal((DEPTH, N_EXPERTS, D_EXPERT, d), D_EXPERT ** -0.5),
        'final_norm_w': gain((d,)),
    }


def reference(x, c, ctx, c_ctx, w_ada, b_ada, norm1_w, norm2_w, w_in,
              gdn_conv_w, gdn_a_log, gdn_dt_bias, gdn_norm_w,
              mla_q_norm_w, mla_kv_norm_w, mla_w_uq, mla_w_ukv,
              gqa_q_norm_w, gqa_k_norm_w,
              hy_conv_w, hy_w1, hy_b1, hy_w2, hy_b2, hy_w3, hy_sin_freq, hy_bias,
              w_branch, w_out, w_router, router_bias,
              moe_w_gate, moe_w_up, moe_w_down, final_norm_w):
    b, seq, d = x.shape
    ctx_len = ctx.shape[1]
    rows = seq // GRID_W
    rope_mla = axial_rope_tables(rows, MLA_ROPE)
    rope_gqa = axial_rope_tables(rows, GQA_HEAD_DIM)
    split_at = np.cumsum(IN_WIDTHS)[:-1].tolist()
    xl, xc = x, ctx
    for layer in range(DEPTH):
        ctx_out = layer < DEPTH - 1
        mod_l = (jax.nn.silu(c) @ w_ada[layer] + b_ada[layer]).reshape(b, 6, 1, d)
        mod_c = (jax.nn.silu(c_ctx) @ w_ada[layer] + b_ada[layer]).reshape(6, 1, d)

        hl = rms_norm(xl, norm1_w[layer]) * (1 + mod_l[:, 1]) + mod_l[:, 0]
        hc = rms_norm(xc, norm1_w[layer]) * (1 + mod_c[1]) + mod_c[0]
        pl = hl @ w_in[layer]
        pc = hc @ (w_in[layer] if ctx_out else w_in[layer][:, :MIX_IN])
        sl = jnp.split(pl[..., :MIX_IN], split_at, axis=-1)
        sc = jnp.split(pc[..., :MIX_IN], split_at, axis=-1)

        gdn_l, gdn_c = gdn_mixer(sl[0:6], sc[0:6], gdn_conv_w[layer], gdn_a_log[layer],
                                 gdn_dt_bias[layer], gdn_norm_w[layer], ctx_out)
        mk_c, mv_c = mla_kv(sc[7], sc[8], mla_kv_norm_w[layer], mla_w_ukv[layer], None)
        mk_l, mv_l = mla_kv(sl[7], sl[8], mla_kv_norm_w[layer], mla_w_ukv[layer], rope_mla)
        mq_l = mla_q(sl[6], mla_q_norm_w[layer], mla_w_uq[layer], rope_mla)
        mla_l = block_attention(mq_l, jnp.concatenate([mk_l, mk_c], axis=1),
                                jnp.concatenate([mv_l, mv_c], axis=1)).reshape(b, seq, BRANCH_W)
        gk_c, gv_c = gqa_kv(sc[10], sc[11], gqa_k_norm_w[layer], None)
        gk_l, gv_l = gqa_kv(sl[10], sl[11], gqa_k_norm_w[layer], rope_gqa)
        gq_l = gqa_q(sl[9], gqa_q_norm_w[layer], rope_gqa)
        gqa_l = block_attention(gq_l, jnp.concatenate([gk_l, gk_c], axis=1),
                                jnp.concatenate([gv_l, gv_c], axis=1)).reshape(b, seq, BRANCH_W)
        hy_params = (hy_w1[layer], hy_b1[layer], hy_w2[layer], hy_b2[layer], hy_w3[layer], hy_sin_freq[layer])
        hy_l = hyena_mixer(sl[12], hy_conv_w[layer], hyena_filters(seq, *hy_params), hy_bias[layer])

        branches_l = [t.astype(xl.dtype) for t in (gdn_l, mla_l, gqa_l, hy_l)]
        mix_l = merge_branches(branches_l, pl[..., MIX_IN:], w_branch[layer], w_out[layer])

        if ctx_out:
            mla_c = block_attention(mla_q(sc[6], mla_q_norm_w[layer], mla_w_uq[layer], None),
                                    mk_c, mv_c).reshape(b, ctx_len, BRANCH_W)
            gqa_c = block_attention(gqa_q(sc[9], gqa_q_norm_w[layer], None),
                                    gk_c, gv_c).reshape(b, ctx_len, BRANCH_W)
            hy_c = hyena_mixer(sc[12], hy_conv_w[layer], hyena_filters(ctx_len, *hy_params), hy_bias[layer])
            branches_c = [t.astype(xc.dtype) for t in (gdn_c, mla_c, gqa_c, hy_c)]
            xc = xc + mod_c[2] * merge_branches(branches_c, pc[..., MIX_IN:], w_branch[layer], w_out[layer])
            hc = rms_norm(xc, norm2_w[layer]) * (1 + mod_c[4]) + mod_c[3]
            xc = xc + mod_c[5] * moe_ffn(hc, w_router, router_bias, moe_w_gate[layer],
                                         moe_w_up[layer], moe_w_down[layer])

        xl = xl + mod_l[:, 2] * mix_l
        hl = rms_norm(xl, norm2_w[layer]) * (1 + mod_l[:, 4]) + mod_l[:, 3]
        xl = xl + mod_l[:, 5] * moe_ffn(hl, w_router, router_bias, moe_w_gate[layer],
                                        moe_w_up[layer], moe_w_down[layer])
    return rms_norm(xl, final_norm_w)
```

```python
import math, functools
import jax, jax.numpy as jnp
from jax import lax
import numpy as np
from jax.experimental import pallas as pl
from jax.experimental.pallas import tpu as pltpu

D_MODEL = 2048
BATCH = 4
SEQ = 4096
DEPTH = 2

GRID_W = 64
CTX_LEN = 256
N_BRANCH = 4
BRANCH_W = 512
NORM_EPS = 1e-6
Q_BLOCK = 128
ROPE_THETA = 10000.0
SHORT_CONV = 3

GDN_HEADS = 4
GDN_HEAD_DIM = 128
GDN_CHUNK = 64

MLA_HEADS = 4
MLA_Q_LORA = 512
MLA_KV_LORA = 256
MLA_NOPE = 128
MLA_ROPE = 64
MLA_V = 128

GQA_HEADS = 8
GQA_KV_HEADS = 2
GQA_HEAD_DIM = 64

HY_WIDTH = 512
HY_ORDER = 2
HY_EMB = 33
HY_HIDDEN = 64
HY_DECAY_TARGET = 1e-2
HY_FAST_DECAY = 0.3
HY_SLOW_DECAY = 1.5

N_EXPERTS = 16
N_GROUPS = 4
EXPERTS_PER_GROUP = N_EXPERTS // N_GROUPS
TOP_K = 2
D_EXPERT = 512

GDN_W = GDN_HEADS * GDN_HEAD_DIM
IN_WIDTHS = (GDN_W, GDN_W, GDN_W, GDN_W, 2 * GDN_HEADS, 2 * GDN_HEADS,
             MLA_Q_LORA, MLA_KV_LORA, MLA_ROPE,
             GQA_HEADS * GQA_HEAD_DIM, GQA_KV_HEADS * GQA_HEAD_DIM, GQA_KV_HEADS * GQA_HEAD_DIM,
             (HY_ORDER + 1) * HY_WIDTH)
MIX_IN = sum(IN_WIDTHS)
IN_DIM = MIX_IN + N_BRANCH * D_MODEL


def rms_norm(x, w):
    xf = x.astype(jnp.float32)
    y = xf * lax.rsqrt(jnp.mean(xf * xf, axis=-1, keepdims=True) + NORM_EPS)
    return (y * w.astype(jnp.float32)).astype(x.dtype)


def l2_normalize(x):
    xf = x.astype(jnp.float32)
    return xf * lax.rsqrt(jnp.sum(xf * xf, axis=-1, keepdims=True) + NORM_EPS)


def depthwise_conv_centred(u, w):
    k = w.shape[0]
    return lax.conv_general_dilated(u, w[:, None, :].astype(u.dtype), window_strides=(1,),
                                    padding=[(k // 2, k // 2)],
                                    dimension_numbers=('NWC', 'WIO', 'NWC'),
                                    feature_group_count=u.shape[-1])


def axial_rope_tables(rows, rot_dim):
    n_freq = rot_dim // 4
    freqs = ROPE_THETA ** (-jnp.arange(n_freq, dtype=jnp.float32) / n_freq)
    row = jnp.repeat(jnp.arange(rows, dtype=jnp.float32), GRID_W)
    col = jnp.tile(jnp.arange(GRID_W, dtype=jnp.float32), rows)
    ang = jnp.concatenate([row[:, None] * freqs, col[:, None] * freqs], axis=-1)
    return jnp.cos(ang), jnp.sin(ang)


def apply_rope(x, cos, sin):
    half = x.shape[-1] // 2
    x1, x2 = x[..., :half], x[..., half:]
    cos, sin = cos[None, :, None, :], sin[None, :, None, :]
    return jnp.concatenate([x1 * cos - x2 * sin, x2 * cos + x1 * sin], axis=-1).astype(x.dtype)


def block_attention(q, k, v):
    b, sq = q.shape[:2]
    scale = q.shape[-1] ** -0.5
    qb = q.reshape(b, sq // Q_BLOCK, Q_BLOCK, *q.shape[2:]).swapaxes(0, 1)

    def attend(qi):
        s = jnp.einsum('bqhgd,bkhd->bhgqk', qi, k).astype(jnp.float32) * scale
        p = jax.nn.softmax(s, axis=-1).astype(v.dtype)
        return jnp.einsum('bhgqk,bkhe->bqhge', p, v)

    ob = lax.map(attend, qb)
    return ob.swapaxes(0, 1).reshape(b, sq, *ob.shape[3:])


def gdn_prep(q, k, v, a, bt, conv_w, a_log, dt_bias):
    b, l = q.shape[:2]
    qkv = jax.nn.silu(depthwise_conv_centred(jnp.concatenate([q, k, v], axis=-1), conv_w)).astype(jnp.float32)
    q, k, v = jnp.split(qkv, 3, axis=-1)
    hd = (b, l, GDN_HEADS, GDN_HEAD_DIM)
    q = l2_normalize(q.reshape(hd)) * GDN_HEAD_DIM ** -0.5
    k = l2_normalize(k.reshape(hd))
    v = v.reshape(hd)
    a = a.astype(jnp.float32).reshape(b, l, 2, GDN_HEADS)
    g = -jnp.exp(a_log.astype(jnp.float32)) * jax.nn.softplus(a + dt_bias.astype(jnp.float32))
    beta = jax.nn.sigmoid(bt.astype(jnp.float32).reshape(b, l, 2, GDN_HEADS))
    return q, k, v, g, beta


def gated_delta_rule(q, k, v, g, beta, state, with_out):
    b, l, h, _ = q.shape
    dv = v.shape[-1]
    c = GDN_CHUNK
    n = l // c

    def to_chunks(t):
        t = t.reshape(b, n, c, h, *t.shape[3:])
        return jnp.moveaxis(t, (1, 3), (0, 2))

    qc, kc, vc, bc = to_chunks(q), to_chunks(k), to_chunks(v), to_chunks(beta)
    gc = jnp.cumsum(to_chunks(g), axis=-1)
    idx = jnp.arange(c)
    lower = idx[:, None] >= idx[None, :]
    strict = idx[:, None] > idx[None, :]
    diff = gc[..., :, None] - gc[..., None, :]
    decay = jnp.where(lower, jnp.exp(jnp.where(lower, diff, 0.0)), 0.0)
    kb = kc * bc[..., None]
    a = jnp.where(strict, jnp.einsum('nbhid,nbhjd->nbhij', kb, kc) * decay, 0.0)
    solve = functools.partial(lax.linalg.triangular_solve, left_side=True, lower=True, unit_diagonal=True)
    u = solve(a, vc * bc[..., None])
    w = solve(a, kb * jnp.exp(gc)[..., None])
    g_last = gc[..., -1]
    k_dec = kc * jnp.exp(g_last[..., None] - gc)[..., None]
    xs = (u, w, k_dec, g_last)
    if with_out:
        qk = jnp.where(lower, jnp.einsum('nbhid,nbhjd->nbhij', qc, kc) * decay, 0.0)
        xs = xs + (qc * jnp.exp(gc)[..., None], qk)

    def step(s, inp):
        u_i, w_i, kd_i, gl_i = inp[:4]
        v_new = u_i - jnp.einsum('bhck,bhkv->bhcv', w_i, s)
        s_new = s * jnp.exp(gl_i)[..., None, None] + jnp.einsum('bhck,bhcv->bhkv', kd_i, v_new)
        if not with_out:
            return s_new, None
        qd_i, qk_i = inp[4:]
        o = jnp.einsum('bhck,bhkv->bhcv', qd_i, s) + jnp.einsum('bhij,bhjv->bhiv', qk_i, v_new)
        return s_new, o

    state, o = lax.scan(step, state, xs)
    if not with_out:
        return None, state
    o = jnp.moveaxis(o, (0, 2), (1, 3)).reshape(b, l, h, dv)
    return o, state


def gdn_output(o, z, norm_w):
    b, l = z.shape[:2]
    zh = z.reshape(b, l, GDN_HEADS, GDN_HEAD_DIM).astype(jnp.float32)
    y = rms_norm(o, norm_w) * jax.nn.silu(zh)
    return y.reshape(b, l, GDN_W).astype(z.dtype)


def gdn_mixer(p_lat, p_ctx, conv_w, a_log, dt_bias, norm_w, ctx_out):
    lat = gdn_prep(p_lat[0], p_lat[1], p_lat[2], p_lat[4], p_lat[5], conv_w, a_log, dt_bias)
    ctx = gdn_prep(p_ctx[0], p_ctx[1], p_ctx[2], p_ctx[4], p_ctx[5], conv_w, a_log, dt_bias)
    b = p_lat[0].shape[0]
    s0 = jnp.zeros((b, GDN_HEADS, GDN_HEAD_DIM, GDN_HEAD_DIM), jnp.float32)
    o_lat, o_ctx = 0.0, 0.0
    for direction in range(2):
        flip = (lambda t: t[:, ::-1]) if direction else (lambda t: t)

        def seq_args(s):
            q, k, v, g, beta = s
            return flip(q), flip(k), flip(v), flip(g[:, :, direction]), flip(beta[:, :, direction])

        oc, s_ctx = gated_delta_rule(*seq_args(ctx), s0, ctx_out)
        ol, _ = gated_delta_rule(*seq_args(lat), s_ctx, True)
        o_lat = o_lat + flip(ol)
        if ctx_out:
            o_ctx = o_ctx + flip(oc)
    out_lat = gdn_output(o_lat, p_lat[3], norm_w)
    out_ctx = gdn_output(o_ctx, p_ctx[3], norm_w) if ctx_out else None
    return out_lat, out_ctx


def mla_q(cq, q_norm_w, w_uq, rope):
    q = jnp.einsum('blr,rhe->blhe', rms_norm(cq, q_norm_w), w_uq)
    if rope is not None:
        q = jnp.concatenate([q[..., :MLA_NOPE], apply_rope(q[..., MLA_NOPE:], *rope)], axis=-1)
    return q[:, :, :, None, :]


def mla_kv(ckv, k_rope, kv_norm_w, w_ukv, rope):
    kv = jnp.einsum('blr,rhe->blhe', rms_norm(ckv, kv_norm_w), w_ukv)
    k_nope, v = kv[..., :MLA_NOPE], kv[..., MLA_NOPE:]
    k_rope = k_rope[:, :, None, :]
    if rope is not None:
        k_rope = apply_rope(k_rope, *rope)
    k_rope = jnp.broadcast_to(k_rope, k_nope.shape[:3] + (MLA_ROPE,)).astype(k_nope.dtype)
    return jnp.concatenate([k_nope, k_rope], axis=-1), v


def gqa_q(q, q_norm_w, rope):
    b, l = q.shape[:2]
    q = rms_norm(q.reshape(b, l, GQA_HEADS, GQA_HEAD_DIM), q_norm_w)
    if rope is not None:
        q = apply_rope(q, *rope)
    return q.reshape(b, l, GQA_KV_HEADS, GQA_HEADS // GQA_KV_HEADS, GQA_HEAD_DIM)


def gqa_kv(k, v, k_norm_w, rope):
    b, l = k.shape[:2]
    k = rms_norm(k.reshape(b, l, GQA_KV_HEADS, GQA_HEAD_DIM), k_norm_w)
    if rope is not None:
        k = apply_rope(k, *rope)
    return k, v.reshape(b, l, GQA_KV_HEADS, GQA_HEAD_DIM)


def hyena_filters(length, w1, b1, w2, b2, w3, sin_freq):
    t = jnp.arange(length, dtype=jnp.float32)
    bands = (HY_EMB - 1) // 2
    f = jnp.linspace(1e-4, bands - 1, bands, dtype=jnp.float32)
    phase = (2.0 * math.pi / length) * t[:, None] * f[None, :]
    feats = jnp.concatenate([t[:, None] / (length - 1), jnp.cos(phase), -jnp.sin(phase)], axis=-1)
    hid = jnp.sin(sin_freq[0] * (feats @ w1 + b1))
    hid = jnp.sin(sin_freq[1] * (hid @ w2 + b2))
    filt = (hid @ w3).astype(jnp.float32)
    centre = length // 2
    dist = jnp.abs(t - centre) / centre
    deltas = jnp.abs(jnp.linspace(math.log(HY_DECAY_TARGET) / HY_SLOW_DECAY,
                                  math.log(HY_DECAY_TARGET) / HY_FAST_DECAY,
                                  HY_ORDER * HY_WIDTH, dtype=jnp.float32))
    filt = filt * jnp.exp(-dist[:, None] * deltas[None, :])
    filt = filt / jnp.sum(jnp.abs(filt), axis=0, keepdims=True)
    return filt.reshape(length, HY_ORDER, HY_WIDTH)


def fft_conv_centred(u, h):
    l = u.shape[1]
    n = 2 * l
    uf = jnp.fft.rfft(u.astype(jnp.float32), n=n, axis=1)
    hf = jnp.fft.rfft(h.astype(jnp.float32), n=n, axis=0)
    y = jnp.fft.irfft(uf * hf[None], n=n, axis=1)
    return y[:, l // 2: l // 2 + l]


def hyena_mixer(u, conv_w, filt, bias):
    parts = jnp.split(depthwise_conv_centred(u, conv_w).astype(jnp.float32), HY_ORDER + 1, axis=-1)
    z = parts[0]
    for o in range(HY_ORDER):
        z = parts[o + 1] * (fft_conv_centred(z, filt[:, o]) + bias[o] * z)
    return z


def merge_branches(branches, gate_logits, w_branch, w_out):
    stacked = jnp.stack(branches, axis=-2)
    gates = jax.nn.sigmoid(gate_logits.reshape(*gate_logits.shape[:-1], N_BRANCH, D_MODEL))
    merged = jnp.einsum('blnd,blnd->bld', gates, jnp.einsum('blnw,nwd->blnd', stacked, w_branch))
    return merged @ w_out


def moe_ffn(h, w_router, router_bias, w_gate, w_up, w_down):
    scores = jax.nn.sigmoid(jnp.einsum('bld,de->ble', h, w_router).astype(jnp.float32))
    sel = scores + router_bias.astype(jnp.float32)
    grp = sel.reshape(*sel.shape[:-1], N_GROUPS, EXPERTS_PER_GROUP)
    grp_score = jnp.sum(lax.top_k(grp, TOP_K)[0], axis=-1)
    grp_mask = jnp.argmax(grp_score, axis=-1)[..., None] == jnp.arange(N_GROUPS)
    expert_mask = jnp.repeat(grp_mask, EXPERTS_PER_GROUP, axis=-1)
    _, e_idx = lax.top_k(jnp.where(expert_mask, sel, -jnp.inf), TOP_K)
    wts = jnp.take_along_axis(scores, e_idx, axis=-1)
    wts = wts / jnp.sum(wts, axis=-1, keepdims=True)
    gate = jnp.sum(jax.nn.one_hot(e_idx, N_EXPERTS, dtype=jnp.float32) * wts[..., None], axis=-2)
    hg = jnp.einsum('bld,edf->blef', h, w_gate)
    hu = jnp.einsum('bld,edf->blef', h, w_up)
    act = jax.nn.silu(hg) * hu * gate[..., None].astype(h.dtype)
    return jnp.einsum('blef,efd->bld', act, w_down)


def _final_norm_body(x_ref, w_ref, o_ref):
    xf = x_ref[...]
    y = xf * lax.rsqrt(jnp.mean(xf * xf, axis=-1, keepdims=True) + NORM_EPS)
    o_ref[...] = y * w_ref[...]


def final_rms_norm(x, w):
    b, l, d = x.shape
    rows = 512
    x2 = x.reshape(b * l, d)
    out = pl.pallas_call(
        _final_norm_body,
        grid=(b * l // rows,),
        in_specs=[pl.BlockSpec((rows, d), lambda i: (i, 0)), pl.BlockSpec((1, d), lambda i: (0, 0))],
        out_specs=pl.BlockSpec((rows, d), lambda i: (i, 0)),
        out_shape=jax.ShapeDtypeStruct((b * l, d), x.dtype),
        name="final_norm",
    )(x2, w.reshape(1, d))
    return out.reshape(b, l, d)


def kernel(x, c, ctx, c_ctx, w_ada, b_ada, norm1_w, norm2_w, w_in,
           gdn_conv_w, gdn_a_log, gdn_dt_bias, gdn_norm_w,
           mla_q_norm_w, mla_kv_norm_w, mla_w_uq, mla_w_ukv,
           gqa_q_norm_w, gqa_k_norm_w,
           hy_conv_w, hy_w1, hy_b1, hy_w2, hy_b2, hy_w3, hy_sin_freq, hy_bias,
           w_branch, w_out, w_router, router_bias,
           moe_w_gate, moe_w_up, moe_w_down, final_norm_w):
    b, seq, d = x.shape
    ctx_len = ctx.shape[1]
    rows = seq // GRID_W
    rope_mla = axial_rope_tables(rows, MLA_ROPE)
    rope_gqa = axial_rope_tables(rows, GQA_HEAD_DIM)
    split_at = np.cumsum(IN_WIDTHS)[:-1].tolist()
    xl, xc = x, ctx
    for layer in range(DEPTH):
        ctx_out = layer < DEPTH - 1
        mod_l = (jax.nn.silu(c) @ w_ada[layer] + b_ada[layer]).reshape(b, 6, 1, d)
        mod_c = (jax.nn.silu(c_ctx) @ w_ada[layer] + b_ada[layer]).reshape(6, 1, d)

        hl = rms_norm(xl, norm1_w[layer]) * (1 + mod_l[:, 1]) + mod_l[:, 0]
        hc = rms_norm(xc, norm1_w[layer]) * (1 + mod_c[1]) + mod_c[0]
        pl_ = hl @ w_in[layer]
        pc = hc @ (w_in[layer] if ctx_out else w_in[layer][:, :MIX_IN])
        sl = jnp.split(pl_[..., :MIX_IN], split_at, axis=-1)
        sc = jnp.split(pc[..., :MIX_IN], split_at, axis=-1)

        gdn_l, gdn_c = gdn_mixer(sl[0:6], sc[0:6], gdn_conv_w[layer], gdn_a_log[layer],
                                 gdn_dt_bias[layer], gdn_norm_w[layer], ctx_out)
        mk_c, mv_c = mla_kv(sc[7], sc[8], mla_kv_norm_w[layer], mla_w_ukv[layer], None)
        mk_l, mv_l = mla_kv(sl[7], sl[8], mla_kv_norm_w[layer], mla_w_ukv[layer], rope_mla)
        mq_l = mla_q(sl[6], mla_q_norm_w[layer], mla_w_uq[layer], rope_mla)
        mla_l = block_attention(mq_l, jnp.concatenate([mk_l, mk_c], axis=1),
                                jnp.concatenate([mv_l, mv_c], axis=1)).reshape(b, seq, BRANCH_W)
        gk_c, gv_c = gqa_kv(sc[10], sc[11], gqa_k_norm_w[layer], None)
        gk_l, gv_l = gqa_kv(sl[10], sl[11], gqa_k_norm_w[layer], rope_gqa)
        gq_l = gqa_q(sl[9], gqa_q_norm_w[layer], rope_gqa)
        gqa_l = block_attention(gq_l, jnp.concatenate([gk_l, gk_c], axis=1),
                                jnp.concatenate([gv_l, gv_c], axis=1)).reshape(b, seq, BRANCH_W)
        hy_params = (hy_w1[layer], hy_b1[layer], hy_w2[layer], hy_b2[layer], hy_w3[layer], hy_sin_freq[layer])
        hy_l = hyena_mixer(sl[12], hy_conv_w[layer], hyena_filters(seq, *hy_params), hy_bias[layer])

        branches_l = [t.astype(xl.dtype) for t in (gdn_l, mla_l, gqa_l, hy_l)]
        mix_l = merge_branches(branches_l, pl_[..., MIX_IN:], w_branch[layer], w_out[layer])

        if ctx_out:
            mla_c = block_attention(mla_q(sc[6], mla_q_norm_w[layer], mla_w_uq[layer], None),
                                    mk_c, mv_c).reshape(b, ctx_len, BRANCH_W)
            gqa_c = block_attention(gqa_q(sc[9], gqa_q_norm_w[layer], None),
                                    gk_c, gv_c).reshape(b, ctx_len, BRANCH_W)
            hy_c = hyena_mixer(sc[12], hy_conv_w[layer], hyena_filters(ctx_len, *hy_params), hy_bias[layer])
            branches_c = [t.astype(xc.dtype) for t in (gdn_c, mla_c, gqa_c, hy_c)]
            xc = xc + mod_c[2] * merge_branches(branches_c, pc[..., MIX_IN:], w_branch[layer], w_out[layer])
            hc = rms_norm(xc, norm2_w[layer]) * (1 + mod_c[4]) + mod_c[3]
            xc = xc + mod_c[5] * moe_ffn(hc, w_router, router_bias, moe_w_gate[layer],
                                         moe_w_up[layer], moe_w_down[layer])

        xl = xl + mod_l[:, 2] * mix_l
        hl = rms_norm(xl, norm2_w[layer]) * (1 + mod_l[:, 4]) + mod_l[:, 3]
        xl = xl + mod_l[:, 5] * moe_ffn(hl, w_router, router_bias, moe_w_gate[layer],
                                        moe_w_up[layer], moe_w_down[layer])
    return final_rms_norm(xl, final_norm_w)
```

```python
import math, functools
import jax, jax.numpy as jnp
from jax import lax
import numpy as np
from jax.experimental import pallas as pl
from jax.experimental.pallas import tpu as pltpu

D_MODEL = 2048
BATCH = 4
SEQ = 4096
DEPTH = 2

GRID_W = 64
CTX_LEN = 256
N_BRANCH = 4
BRANCH_W = 512
NORM_EPS = 1e-6
Q_BLOCK = 128
ROPE_THETA = 10000.0
SHORT_CONV = 3

GDN_HEADS = 4
GDN_HEAD_DIM = 128
GDN_CHUNK = 64

MLA_HEADS = 4
MLA_Q_LORA = 512
MLA_KV_LORA = 256
MLA_NOPE = 128
MLA_ROPE = 64
MLA_V = 128

GQA_HEADS = 8
GQA_KV_HEADS = 2
GQA_HEAD_DIM = 64

HY_WIDTH = 512
HY_ORDER = 2
HY_EMB = 33
HY_HIDDEN = 64
HY_DECAY_TARGET = 1e-2
HY_FAST_DECAY = 0.3
HY_SLOW_DECAY = 1.5

N_EXPERTS = 16
N_GROUPS = 4
EXPERTS_PER_GROUP = N_EXPERTS // N_GROUPS
TOP_K = 2
D_EXPERT = 512

GDN_W = GDN_HEADS * GDN_HEAD_DIM
IN_WIDTHS = (GDN_W, GDN_W, GDN_W, GDN_W, 2 * GDN_HEADS, 2 * GDN_HEADS,
             MLA_Q_LORA, MLA_KV_LORA, MLA_ROPE,
             GQA_HEADS * GQA_HEAD_DIM, GQA_KV_HEADS * GQA_HEAD_DIM, GQA_KV_HEADS * GQA_HEAD_DIM,
             (HY_ORDER + 1) * HY_WIDTH)
MIX_IN = sum(IN_WIDTHS)
IN_DIM = MIX_IN + N_BRANCH * D_MODEL

F32 = jnp.float32
BF16 = jnp.bfloat16
LANES = 128
MOD_ROWS = 8
CTX_MOD_ROW = BATCH
MOE_TILE = 512
VMEM_LIMIT = 56 << 20

MAIN_W = 5120
OFF_GDN, OFF_CQ, OFF_GQ, OFF_HY, OFF_CKV, OFF_GK, OFF_GV = 0, 2048, 2560, 3072, 4608, 4864, 4992
MISC_W = LANES


def _cp(sem):
    return pltpu.CompilerParams(dimension_semantics=sem, vmem_limit_bytes=VMEM_LIMIT)


def _sigmoid(v):
    return 0.5 * jnp.tanh(0.5 * v) + 0.5


def rms_norm(x, w):
    xf = x.astype(jnp.float32)
    y = xf * lax.rsqrt(jnp.mean(xf * xf, axis=-1, keepdims=True) + NORM_EPS)
    return (y * w.astype(jnp.float32)).astype(x.dtype)


def l2_normalize(x):
    xf = x.astype(jnp.float32)
    return xf * lax.rsqrt(jnp.sum(xf * xf, axis=-1, keepdims=True) + NORM_EPS)


def depthwise_conv_centred(u, w):
    k = w.shape[0]
    return lax.conv_general_dilated(u, w[:, None, :].astype(u.dtype), window_strides=(1,),
                                    padding=[(k // 2, k // 2)],
                                    dimension_numbers=('NWC', 'WIO', 'NWC'),
                                    feature_group_count=u.shape[-1])


def axial_rope_tables(rows, rot_dim):
    n_freq = rot_dim // 4
    freqs = ROPE_THETA ** (-jnp.arange(n_freq, dtype=jnp.float32) / n_freq)
    row = jnp.repeat(jnp.arange(rows, dtype=jnp.float32), GRID_W)
    col = jnp.tile(jnp.arange(GRID_W, dtype=jnp.float32), rows)
    ang = jnp.concatenate([row[:, None] * freqs, col[:, None] * freqs], axis=-1)
    return jnp.cos(ang), jnp.sin(ang)


def gdn_prep(q, k, v, a, bt, conv_w, a_log, dt_bias):
    b, l = q.shape[:2]
    qkv = jax.nn.silu(depthwise_conv_centred(jnp.concatenate([q, k, v], axis=-1), conv_w)).astype(jnp.float32)
    q, k, v = jnp.split(qkv, 3, axis=-1)
    hd = (b, l, GDN_HEADS, GDN_HEAD_DIM)
    q = l2_normalize(q.reshape(hd)) * GDN_HEAD_DIM ** -0.5
    k = l2_normalize(k.reshape(hd))
    v = v.reshape(hd)
    a = a.astype(jnp.float32).reshape(b, l, 2, GDN_HEADS)
    g = -jnp.exp(a_log.astype(jnp.float32)) * jax.nn.softplus(a + dt_bias.astype(jnp.float32))
    beta = jax.nn.sigmoid(bt.astype(jnp.float32).reshape(b, l, 2, GDN_HEADS))
    return q, k, v, g, beta


def gated_delta_rule(q, k, v, g, beta, state, with_out):
    b, l, h, _ = q.shape
    dv = v.shape[-1]
    c = GDN_CHUNK
    n = l // c

    def to_chunks(t):
        t = t.reshape(b, n, c, h, *t.shape[3:])
        return jnp.moveaxis(t, (1, 3), (0, 2))

    qc, kc, vc, bc = to_chunks(q), to_chunks(k), to_chunks(v), to_chunks(beta)
    gc = jnp.cumsum(to_chunks(g), axis=-1)
    idx = jnp.arange(c)
    lower = idx[:, None] >= idx[None, :]
    strict = idx[:, None] > idx[None, :]
    diff = gc[..., :, None] - gc[..., None, :]
    decay = jnp.where(lower, jnp.exp(jnp.where(lower, diff, 0.0)), 0.0)
    kb = kc * bc[..., None]
    a = jnp.where(strict, jnp.einsum('nbhid,nbhjd->nbhij', kb, kc) * decay, 0.0)
    solve = functools.partial(lax.linalg.triangular_solve, left_side=True, lower=True, unit_diagonal=True)
    u = solve(a, vc * bc[..., None])
    w = solve(a, kb * jnp.exp(gc)[..., None])
    g_last = gc[..., -1]
    k_dec = kc * jnp.exp(g_last[..., None] - gc)[..., None]
    xs = (u, w, k_dec, g_last)
    if with_out:
        qk = jnp.where(lower, jnp.einsum('nbhid,nbhjd->nbhij', qc, kc) * decay, 0.0)
        xs = xs + (qc * jnp.exp(gc)[..., None], qk)

    def step(s, inp):
        u_i, w_i, kd_i, gl_i = inp[:4]
        v_new = u_i - jnp.einsum('bhck,bhkv->bhcv', w_i, s)
        s_new = s * jnp.exp(gl_i)[..., None, None] + jnp.einsum('bhck,bhcv->bhkv', kd_i, v_new)
        if not with_out:
            return s_new, None
        qd_i, qk_i = inp[4:]
        o = jnp.einsum('bhck,bhkv->bhcv', qd_i, s) + jnp.einsum('bhij,bhjv->bhiv', qk_i, v_new)
        return s_new, o

    state, o = lax.scan(step, state, xs)
    if not with_out:
        return None, state
    o = jnp.moveaxis(o, (0, 2), (1, 3)).reshape(b, l, h, dv)
    return o, state


def gdn_output(o, z, norm_w):
    b, l = z.shape[:2]
    zh = z.reshape(b, l, GDN_HEADS, GDN_HEAD_DIM).astype(jnp.float32)
    y = rms_norm(o, norm_w) * jax.nn.silu(zh)
    return y.reshape(b, l, GDN_W).astype(z.dtype)


def gdn_mixer(p_lat, p_ctx, conv_w, a_log, dt_bias, norm_w, ctx_out):
    lat = gdn_prep(p_lat[0], p_lat[1], p_lat[2], p_lat[4], p_lat[5], conv_w, a_log, dt_bias)
    ctx = gdn_prep(p_ctx[0], p_ctx[1], p_ctx[2], p_ctx[4], p_ctx[5], conv_w, a_log, dt_bias)
    b = p_lat[0].shape[0]
    s0 = jnp.zeros((b, GDN_HEADS, GDN_HEAD_DIM, GDN_HEAD_DIM), jnp.float32)
    o_lat, o_ctx = 0.0, 0.0
    for direction in range(2):
        flip = (lambda t: t[:, ::-1]) if direction else (lambda t: t)

        def seq_args(s):
            q, k, v, g, beta = s
            return flip(q), flip(k), flip(v), flip(g[:, :, direction]), flip(beta[:, :, direction])

        oc, s_ctx = gated_delta_rule(*seq_args(ctx), s0, ctx_out)
        ol, _ = gated_delta_rule(*seq_args(lat), s_ctx, True)
        o_lat = o_lat + flip(ol)
        if ctx_out:
            o_ctx = o_ctx + flip(oc)
    out_lat = gdn_output(o_lat, p_lat[3], norm_w)
    out_ctx = gdn_output(o_ctx, p_ctx[3], norm_w) if ctx_out else None
    return out_lat, out_ctx


def hyena_filters(length, w1, b1, w2, b2, w3, sin_freq):
    t = jnp.arange(length, dtype=jnp.float32)
    bands = (HY_EMB - 1) // 2
    f = jnp.linspace(1e-4, bands - 1, bands, dtype=jnp.float32)
    phase = (2.0 * math.pi / length) * t[:, None] * f[None, :]
    feats = jnp.concatenate([t[:, None] / (length - 1), jnp.cos(phase), -jnp.sin(phase)], axis=-1)
    hid = jnp.sin(sin_freq[0] * (feats @ w1 + b1))
    hid = jnp.sin(sin_freq[1] * (hid @ w2 + b2))
    filt = (hid @ w3).astype(jnp.float32)
    centre = length // 2
    dist = jnp.abs(t - centre) / centre
    deltas = jnp.abs(jnp.linspace(math.log(HY_DECAY_TARGET) / HY_SLOW_DECAY,
                                  math.log(HY_DECAY_TARGET) / HY_FAST_DECAY,
                                  HY_ORDER * HY_WIDTH, dtype=jnp.float32))
    filt = filt * jnp.exp(-dist[:, None] * deltas[None, :])
    filt = filt / jnp.sum(jnp.abs(filt), axis=0, keepdims=True)
    return filt.reshape(length, HY_ORDER, HY_WIDTH)


def fft_conv_centred(u, h):
    l = u.shape[1]
    n = 2 * l
    uf = jnp.fft.rfft(u.astype(jnp.float32), n=n, axis=1)
    hf = jnp.fft.rfft(h.astype(jnp.float32), n=n, axis=0)
    y = jnp.fft.irfft(uf * hf[None], n=n, axis=1)
    return y[:, l // 2: l // 2 + l]


def hyena_mixer(u, conv_w, filt, bias):
    parts = jnp.split(depthwise_conv_centred(u, conv_w).astype(jnp.float32), HY_ORDER + 1, axis=-1)
    z = parts[0]
    for o in range(HY_ORDER):
        z = parts[o + 1] * (fft_conv_centred(z, filt[:, o]) + bias[o] * z)
    return z


def _ada_body(c_ref, w_ref, b_ref, o_ref):
    cv = c_ref[...]
    s = cv * _sigmoid(cv)
    o_ref[0] = jnp.dot(s, w_ref[0], precision=lax.Precision.HIGHEST, preferred_element_type=F32) + b_ref[0]


def ada_modulation(c, c_ctx, w_ada, b_ada):
    depth, d, d6 = w_ada.shape
    c8 = jnp.zeros((MOD_ROWS, d), F32).at[:c.shape[0]].set(c).at[CTX_MOD_ROW].set(c_ctx)
    tn = 512
    out = pl.pallas_call(
        _ada_body,
        grid=(depth, d6 // tn),
        in_specs=[pl.BlockSpec((MOD_ROWS, d), lambda l, j: (0, 0)),
                  pl.BlockSpec((1, d, tn), lambda l, j: (l, 0, j)),
                  pl.BlockSpec((1, 1, tn), lambda l, j: (l, 0, j))],
        out_specs=pl.BlockSpec((1, MOD_ROWS, tn), lambda l, j: (l, 0, j)),
        out_shape=jax.ShapeDtypeStruct((depth, MOD_ROWS, d6), F32),
        compiler_params=_cp(("parallel", "parallel")),
        name="ada_mod",
    )(c8, w_ada, b_ada.reshape(depth, 1, d6))
    return out.reshape(depth, MOD_ROWS, 6, d)


def _row_tile(cap, n_rows, per_batch):
    return min(cap, n_rows if per_batch is None else per_batch)


def _mod_row_fn(n_rows, tm, per_batch):
    if per_batch is None:
        return lambda i: CTX_MOD_ROW
    tiles = per_batch // tm
    return lambda i: i // tiles


def _normmod_body(x_ref, nw_ref, sh_ref, sc_ref, o_ref):
    xf = x_ref[...]
    y = xf * lax.rsqrt(jnp.mean(xf * xf, axis=-1, keepdims=True) + NORM_EPS) * nw_ref[...]
    o_ref[...] = (y * (1.0 + sc_ref[0]) + sh_ref[0]).astype(o_ref.dtype)


def norm_modulate(x, nw, shift, scale, per_batch, out_dtype=BF16):
    n, d = x.shape
    tm = _row_tile(512, n, per_batch)
    rf = _mod_row_fn(n, tm, per_batch)
    return pl.pallas_call(
        _normmod_body,
        grid=(n // tm,),
        in_specs=[pl.BlockSpec((tm, d), lambda i: (i, 0)),
                  pl.BlockSpec((1, d), lambda i: (0, 0)),
                  pl.BlockSpec((1, 1, d), lambda i: (rf(i), 0, 0)),
                  pl.BlockSpec((1, 1, d), lambda i: (rf(i), 0, 0))],
        out_specs=pl.BlockSpec((tm, d), lambda i: (i, 0)),
        out_shape=jax.ShapeDtypeStruct((n, d), out_dtype),
        compiler_params=_cp(("parallel",)),
        name="norm_mod",
    )(x, nw.reshape(1, d), shift.reshape(MOD_ROWS, 1, d), scale.reshape(MOD_ROWS, 1, d))


def _mm_body(a_ref, w_ref, o_ref):
    o_ref[...] = jnp.dot(a_ref[...], w_ref[...], preferred_element_type=F32).astype(o_ref.dtype)


def matmul(a, w, out_dtype, tn):
    n, k = a.shape
    m = w.shape[1]
    tm = min(2048, n)
    return pl.pallas_call(
        _mm_body,
        grid=(n // tm, m // tn),
        in_specs=[pl.BlockSpec((tm, k), lambda i, j: (i, 0)),
                  pl.BlockSpec((k, tn), lambda i, j: (0, j))],
        out_specs=pl.BlockSpec((tm, tn), lambda i, j: (i, j)),
        out_shape=jax.ShapeDtypeStruct((n, m), out_dtype),
        compiler_params=_cp(("parallel", "parallel")),
        name="proj",
    )(a, w)


def _mm_res_body(a_ref, w_ref, x_ref, g_ref, o_ref):
    y = jnp.dot(a_ref[...], w_ref[...], preferred_element_type=F32)
    o_ref[...] = x_ref[...] + g_ref[0] * y


def matmul_gated_residual(a, w, x, gate, per_batch):
    n, k = a.shape
    d = w.shape[1]
    tm = _row_tile(1024, n, per_batch)
    tn = min(512, d)
    rf = _mod_row_fn(n, tm, per_batch)
    return pl.pallas_call(
        _mm_res_body,
        grid=(n // tm, d // tn),
        in_specs=[pl.BlockSpec((tm, k), lambda i, j: (i, 0)),
                  pl.BlockSpec((k, tn), lambda i, j: (0, j)),
                  pl.BlockSpec((tm, tn), lambda i, j: (i, j)),
                  pl.BlockSpec((1, 1, tn), lambda i, j: (rf(i), 0, j))],
        out_specs=pl.BlockSpec((tm, tn), lambda i, j: (i, j)),
        out_shape=jax.ShapeDtypeStruct((n, d), F32),
        compiler_params=_cp(("parallel", "parallel")),
        name="out_proj_residual",
    )(a, w, x, gate.reshape(MOD_ROWS, 1, d))


def _merge_body(h_ref, b0_ref, b1_ref, b2_ref, b3_ref, wg_ref, wb_ref, o_ref, acc_ref):
    n = pl.program_id(2)

    @pl.when(n == 0)
    def _():
        acc_ref[...] = jnp.zeros_like(acc_ref)

    gate = _sigmoid(jnp.dot(h_ref[...], wg_ref[...], preferred_element_type=F32))
    for idx, b_ref in enumerate((b0_ref, b1_ref, b2_ref, b3_ref)):
        @pl.when(n == idx)
        def _(b_ref=b_ref):
            acc_ref[...] += gate * jnp.dot(b_ref[...], wb_ref[0], preferred_element_type=F32)

    @pl.when(n == N_BRANCH - 1)
    def _():
        o_ref[...] = acc_ref[...].astype(o_ref.dtype)


def merge_branches_gated(h, branches, w_gate, w_branch):
    n, d = h.shape
    bw = branches[0].shape[1]
    tm = min(1024, n)
    tn = min(512, d)
    nj = d // tn
    return pl.pallas_call(
        _merge_body,
        grid=(n // tm, nj, N_BRANCH),
        in_specs=[pl.BlockSpec((tm, d), lambda i, j, b: (i, 0))]
                 + [pl.BlockSpec((tm, bw), lambda i, j, b: (i, 0))] * N_BRANCH
                 + [pl.BlockSpec((d, tn), lambda i, j, b: (0, b * nj + j)),
                    pl.BlockSpec((1, bw, tn), lambda i, j, b: (b, 0, j))],
        out_specs=pl.BlockSpec((tm, tn), lambda i, j, b: (i, j)),
        out_shape=jax.ShapeDtypeStruct((n, d), BF16),
        scratch_shapes=[pltpu.VMEM((tm, tn), F32)],
        compiler_params=_cp(("parallel", "parallel", "arbitrary")),
        name="gate_merge",
    )(h, *branches, w_gate, w_branch)


def _mla_prep_body(cq_ref, ckv_ref, misc_ref, qnw_ref, kvnw_ref, wqa_ref, wqb_ref, wk_ref, wv_ref,
                   ska_ref, skb_ref, cq_tab, sq_tab, q_out, k_out, v_out):
    def norm(v, w_ref):
        vf = v.astype(F32)
        return (vf * lax.rsqrt(jnp.mean(vf * vf, axis=-1, keepdims=True) + NORM_EPS) * w_ref[...]).astype(BF16)

    xq = norm(cq_ref[...], qnw_ref)
    xkv = norm(ckv_ref[...], kvnw_ref)
    cos, sin = cq_tab[...], sq_tab[...]
    misc = misc_ref[...].astype(BF16)
    kr = (jnp.dot(misc, ska_ref[...], preferred_element_type=F32) * cos
          + jnp.dot(misc, skb_ref[...], preferred_element_type=F32) * sin)
    for h in range(MLA_HEADS):
        qa = jnp.dot(xq, wqa_ref[h], preferred_element_type=F32)
        qb = jnp.dot(xq, wqb_ref[h], preferred_element_type=F32)
        q_out[0, h] = (qa * cos + qb * sin).astype(BF16)
        k_out[0, h] = (jnp.dot(xkv, wk_ref[h], preferred_element_type=F32) + kr).astype(BF16)
        v_out[0, h] = jnp.dot(xkv, wv_ref[h], preferred_element_type=F32).astype(BF16)


def _mla_weights(q_norm_w, kv_norm_w, w_uq, w_ukv):
    dk = MLA_NOPE + MLA_ROPE
    half = MLA_ROPE // 2
    scale = dk ** -0.5
    wq = jnp.transpose(w_uq, (1, 0, 2)) * scale
    nope0 = jnp.zeros(wq.shape[:2] + (MLA_NOPE,), F32)
    wq_rot = jnp.concatenate([nope0, -wq[..., MLA_NOPE + half:], wq[..., MLA_NOPE:MLA_NOPE + half]], axis=-1)
    wkv = jnp.transpose(w_ukv, (1, 0, 2))
    wk = jnp.concatenate([wkv[..., :MLA_NOPE], jnp.zeros(wkv.shape[:2] + (MLA_ROPE,), F32)], axis=-1)
    wv = wkv[..., MLA_NOPE:]
    eye = jnp.eye(MLA_ROPE, dtype=F32)
    rot = jnp.concatenate([-eye[:, half:], eye[:, :half]], axis=-1)
    pad_r = MISC_W - MLA_ROPE
    ska = jnp.pad(eye, ((0, pad_r), (MLA_NOPE, 0)))
    skb = jnp.pad(rot, ((0, pad_r), (MLA_NOPE, 0)))
    return tuple(t.astype(BF16) for t in (wq, wq_rot, wk, wv, ska, skb))


def _mla_tables(rope, length):
    dk = MLA_NOPE + MLA_ROPE
    if rope is None:
        return jnp.ones((length, dk), F32), jnp.zeros((length, dk), F32)
    cos, sin = rope
    ones = jnp.ones((length, MLA_NOPE), F32)
    return (jnp.concatenate([ones, cos, cos], axis=-1),
            jnp.concatenate([0.0 * ones, sin, sin], axis=-1))


def mla_prepare(main, misc, b, length, weights, q_norm_w, kv_norm_w, tables):
    wq, wq_rot, wk, wv, ska, skb = weights
    cos, sin = tables
    dk = MLA_NOPE + MLA_ROPE
    tm = min(512, length)
    nt = length // tm
    full = lambda a: pl.BlockSpec(a.shape, lambda bi, i: (0,) * a.ndim)
    qnw = q_norm_w.reshape(1, -1)
    kvnw = kv_norm_w.reshape(1, -1)
    outs = pl.pallas_call(
        _mla_prep_body,
        grid=(b, nt),
        in_specs=[pl.BlockSpec((tm, MLA_Q_LORA), lambda bi, i: (bi * nt + i, OFF_CQ // MLA_Q_LORA)),
                  pl.BlockSpec((tm, MLA_KV_LORA), lambda bi, i: (bi * nt + i, OFF_CKV // MLA_KV_LORA)),
                  pl.BlockSpec((tm, MISC_W), lambda bi, i: (bi * nt + i, 0)),
                  full(qnw), full(kvnw), full(wq), full(wq_rot), full(wk), full(wv), full(ska), full(skb),
                  pl.BlockSpec((tm, dk), lambda bi, i: (i, 0)),
                  pl.BlockSpec((tm, dk), lambda bi, i: (i, 0))],
        out_specs=[pl.BlockSpec((1, MLA_HEADS, tm, dk), lambda bi, i: (bi, 0, i, 0)),
                   pl.BlockSpec((1, MLA_HEADS, tm, dk), lambda bi, i: (bi, 0, i, 0)),
                   pl.BlockSpec((1, MLA_HEADS, tm, MLA_V), lambda bi, i: (bi, 0, i, 0))],
        out_shape=[jax.ShapeDtypeStruct((b, MLA_HEADS, length, dk), BF16),
                   jax.ShapeDtypeStruct((b, MLA_HEADS, length, dk), BF16),
                   jax.ShapeDtypeStruct((b, MLA_HEADS, length, MLA_V), BF16)],
        compiler_params=_cp(("parallel", "parallel")),
        name="mla_prep",
    )(main, main, misc, qnw, kvnw, wq, wq_rot, wk, wv, ska, skb, cos, sin)
    return outs


def _gqa_prep_body(q_ref, k_ref, v_ref, qnw_ref, knw_ref, gsum_ref, rot_ref, cos_ref, sin_ref,
                   q_out, k_out, v_out):
    cos, sin = cos_ref[...], sin_ref[...]

    def prep(v, nw, width):
        vf = v.astype(F32)
        sq = vf * vf
        hi = sq.astype(BF16)
        lo = (sq - hi.astype(F32)).astype(BF16)
        g = gsum_ref[:width, :width]
        ss = jnp.dot(hi, g, preferred_element_type=F32) + jnp.dot(lo, g, preferred_element_type=F32)
        xn = vf * lax.rsqrt(ss * (1.0 / GQA_HEAD_DIM) + NORM_EPS) * nw
        xr = jnp.dot(xn.astype(BF16), rot_ref[:width, :width], preferred_element_type=F32)
        return xn * cos[:, :width] + xr * sin[:, :width]

    qf = prep(q_ref[...], qnw_ref[...], GQA_HEADS * GQA_HEAD_DIM) * GQA_HEAD_DIM ** -0.5
    kf = prep(k_ref[...], knw_ref[...], LANES)
    q_out[...] = qf.astype(BF16)
    k_out[...] = kf.astype(BF16)
    v_out[...] = v_ref[...]


def gqa_prepare(main, b, length, q_norm_w, k_norm_w, rope):
    n = b * length
    qw = GQA_HEADS * GQA_HEAD_DIM
    kw = GQA_KV_HEADS * GQA_HEAD_DIM
    half = GQA_HEAD_DIM // 2
    if rope is None:
        cos = jnp.ones((length, qw), F32)
        sin = jnp.zeros((length, qw), F32)
    else:
        cos = jnp.tile(jnp.concatenate([rope[0], rope[0]], axis=-1), (1, GQA_HEADS))
        sin = jnp.tile(jnp.concatenate([rope[1], rope[1]], axis=-1), (1, GQA_HEADS))
    head = np.arange(qw) // GQA_HEAD_DIM
    gsum = jnp.asarray(head[:, None] == head[None, :], BF16)
    eye = np.eye(GQA_HEAD_DIM, dtype=np.float32)
    rot1 = np.concatenate([-eye[:, half:], eye[:, :half]], axis=-1)
    rot = jnp.asarray(np.kron(np.eye(GQA_HEADS, dtype=np.float32), rot1), BF16)
    tm = min(512, length)
    nt = length // tm
    full = lambda a: pl.BlockSpec(a.shape, lambda bi, i: (0,) * a.ndim)
    qnw = jnp.tile(q_norm_w, GQA_HEADS).reshape(1, qw)
    knw = jnp.tile(k_norm_w, GQA_KV_HEADS).reshape(1, kw)
    return pl.pallas_call(
        _gqa_prep_body,
        grid=(b, nt),
        in_specs=[pl.BlockSpec((tm, qw), lambda bi, i: (bi * nt + i, OFF_GQ // qw)),
                  pl.BlockSpec((tm, kw), lambda bi, i: (bi * nt + i, OFF_GK // kw)),
                  pl.BlockSpec((tm, kw), lambda bi, i: (bi * nt + i, OFF_GV // kw)),
                  full(qnw), full(knw), full(gsum), full(rot),
                  pl.BlockSpec((tm, qw), lambda bi, i: (i, 0)),
                  pl.BlockSpec((tm, qw), lambda bi, i: (i, 0))],
        out_specs=[pl.BlockSpec((tm, qw), lambda bi, i: (bi * nt + i, 0)),
                   pl.BlockSpec((tm, kw), lambda bi, i: (bi * nt + i, 0)),
                   pl.BlockSpec((tm, kw), lambda bi, i: (bi * nt + i, 0))],
        out_shape=[jax.ShapeDtypeStruct((n, qw), BF16),
                   jax.ShapeDtypeStruct((n, kw), BF16),
                   jax.ShapeDtypeStruct((n, kw), BF16)],
        compiler_params=_cp(("parallel", "parallel")),
        name="gqa_prep",
    )(main, main, main, qnw, knw, gsum, rot, cos, sin)


def _flash_body(q_ref, k_ref, v_ref, kc_ref, vc_ref, o_ref, m_ref, l_ref, acc_ref, *, nk, has_ctx, groups):
    ki = pl.program_id(3)

    @pl.when(ki == 0)
    def _():
        m_ref[...] = jnp.full_like(m_ref, -jnp.inf)
        l_ref[...] = jnp.zeros_like(l_ref)
        acc_ref[...] = jnp.zeros_like(acc_ref)

    q = q_ref[0, 0]
    q = q.reshape(q.shape[0] * q.shape[1], q.shape[2])

    def step(k, v):
        s = lax.dot_general(q, k, (((1,), (1,)), ((), ())), preferred_element_type=F32)
        m_prev = m_ref[...]
        m_new = jnp.maximum(m_prev, jnp.max(s, axis=-1, keepdims=True))
        p = jnp.exp(s - m_new)
        alpha = jnp.exp(m_prev - m_new)
        l_ref[...] = alpha * l_ref[...] + jnp.sum(p, axis=-1, keepdims=True)
        acc_ref[...] = alpha * acc_ref[...] + jnp.dot(p.astype(BF16), v, preferred_element_type=F32)
        m_ref[...] = m_new

    @pl.when(ki < nk)
    def _():
        step(k_ref[0, 0], v_ref[0, 0])

    if has_ctx:
        @pl.when(ki == nk)
        def _():
            step(kc_ref[0, 0], vc_ref[0, 0])

    @pl.when(ki == nk - 1 + int(has_ctx))
    def _():
        out = acc_ref[...] / l_ref[...]
        o_ref[0, 0] = out.reshape(o_ref.shape[2:]).astype(o_ref.dtype)


def flash_attention(q, k, v, kc, vc, tq, tk):
    b, hkv, g, sq, dk = q.shape
    sk = k.shape[2]
    dv = v.shape[3]
    tq = min(tq, sq)
    tk = min(tk, sk)
    nk = sk // tk
    has_ctx = kc is not None
    if not has_ctx:
        kc, vc = k[:, :, :16], v[:, :, :16]
    skc = kc.shape[2]
    rows = g * tq
    body = functools.partial(_flash_body, nk=nk, has_ctx=has_ctx, groups=g)
    return pl.pallas_call(
        body,
        grid=(b, hkv, sq // tq, nk + int(has_ctx)),
        in_specs=[pl.BlockSpec((1, 1, g, tq, dk), lambda bi, h, qi, ki: (bi, h, 0, qi, 0)),
                  pl.BlockSpec((1, 1, tk, dk), lambda bi, h, qi, ki: (bi, h, jnp.minimum(ki, nk - 1), 0)),
                  pl.BlockSpec((1, 1, tk, dv), lambda bi, h, qi, ki: (bi, h, jnp.minimum(ki, nk - 1), 0)),
                  pl.BlockSpec((1, 1, skc, dk), lambda bi, h, qi, ki: (bi, h, 0, 0)),
                  pl.BlockSpec((1, 1, skc, dv), lambda bi, h, qi, ki: (bi, h, 0, 0))],
        out_specs=pl.BlockSpec((1, 1, g, tq, dv), lambda bi, h, qi, ki: (bi, h, 0, qi, 0)),
        out_shape=jax.ShapeDtypeStruct((b, hkv, g, sq, dv), BF16),
        scratch_shapes=[pltpu.VMEM((rows, 1), F32), pltpu.VMEM((rows, 1), F32), pltpu.VMEM((rows, dv), F32)],
        compiler_params=_cp(("parallel", "parallel", "parallel", "arbitrary")),
        name="flash_attention",
    )(q, k, v, kc, vc)


def _router_body(x_ref, nw_ref, sh_ref, sc_ref, wrh_ref, wrl_ref, rb_ref, tri_ref,
                 h_ref, ri_ref, rw_ref, cnt_ref, carry_ref):
    i = pl.program_id(0)

    @pl.when(i == 0)
    def _():
        carry_ref[...] = jnp.zeros_like(carry_ref)

    xf = x_ref[...]
    y = xf * lax.rsqrt(jnp.mean(xf * xf, axis=-1, keepdims=True) + NORM_EPS) * nw_ref[...]
    h = y * (1.0 + sc_ref[0]) + sh_ref[0]
    h_ref[...] = h
    hi = h.astype(BF16)
    lo = (h - hi.astype(F32)).astype(BF16)
    nt = (((1,), (1,)), ((), ()))
    logits = (lax.dot_general(wrh_ref[...], hi, nt, preferred_element_type=F32)
              + lax.dot_general(wrh_ref[...], lo, nt, preferred_element_type=F32)
              + lax.dot_general(wrl_ref[...], hi, nt, preferred_element_type=F32))
    scores = _sigmoid(logits)
    sel = scores + rb_ref[...]
    s = [scores[e:e + 1] for e in range(N_EXPERTS)]
    v = [sel[e:e + 1] for e in range(N_EXPERTS)]
    epg = EXPERTS_PER_GROUP
    gscore = []
    for gi in range(N_GROUPS):
        mem = v[gi * epg:(gi + 1) * epg]
        best = None
        for a in range(epg):
            for c in range(a + 1, epg):
                pair = mem[a] + mem[c]
                best = pair if best is None else jnp.maximum(best, pair)
        gscore.append(best)
    is_best = []
    for gi in range(N_GROUPS):
        ok = None
        for gj in range(N_GROUPS):
            if gj == gi:
                continue
            c = (gscore[gi] > gscore[gj]) if gj < gi else (gscore[gi] >= gscore[gj])
            ok = c if ok is None else (ok & c)
        is_best.append(ok)
    chosen = []
    for e in range(N_EXPERTS):
        gi = e // epg
        rank = jnp.zeros_like(v[e])
        for e2 in range(gi * epg, (gi + 1) * epg):
            if e2 == e:
                continue
            ahead = (v[e2] >= v[e]) if e2 < e else (v[e2] > v[e])
            rank = rank + jnp.where(ahead, 1.0, 0.0)
        chosen.append(is_best[gi] & (rank < TOP_K))
    chosen_f = jnp.concatenate([jnp.where(cm, 1.0, 0.0) for cm in chosen], axis=0)
    total = None
    for e in range(N_EXPERTS):
        t = jnp.where(chosen[e], s[e], 0.0)
        total = t if total is None else total + t
    pos = jnp.dot(chosen_f.astype(BF16), tri_ref[...], preferred_element_type=F32) + carry_ref[...]
    carry_ref[...] += jnp.sum(chosen_f, axis=-1, keepdims=True)
    cnt_ref[...] = jnp.broadcast_to(carry_ref[...], cnt_ref.shape)
    e_lo = jnp.full_like(v[0], float(N_EXPERTS))
    e_hi = jnp.full_like(v[0], -1.0)
    for e in range(N_EXPERTS):
        e_lo = jnp.where(chosen[e], jnp.minimum(e_lo, float(e)), e_lo)
        e_hi = jnp.where(chosen[e], jnp.maximum(e_hi, float(e)), e_hi)
    zero = jnp.zeros_like(v[0])
    w_lo, w_hi, p_lo, p_hi = zero, zero, zero, zero
    for e in range(N_EXPERTS):
        pe = pos[e:e + 1]
        w_lo = jnp.where(e_lo == float(e), s[e], w_lo)
        w_hi = jnp.where(e_hi == float(e), s[e], w_hi)
        p_lo = jnp.where(e_lo == float(e), pe, p_lo)
        p_hi = jnp.where(e_hi == float(e), pe, p_hi)
    ri_ref[...] = jnp.concatenate([e_lo, e_hi, p_lo, p_hi, zero, zero, zero, zero], axis=0).astype(jnp.int32)
    rw_ref[...] = jnp.concatenate([w_lo / total, w_hi / total, zero, zero, zero, zero, zero, zero], axis=0)


def norm_route(x, nw, shift, scale, per_batch, w_router, router_bias):
    n, d = x.shape
    tm = _row_tile(512, n, per_batch)
    rf = _mod_row_fn(n, tm, per_batch)
    wr_t = w_router.T
    wr_hi = wr_t.astype(BF16)
    wr_lo = (wr_t - wr_hi.astype(F32)).astype(BF16)
    tri = jnp.asarray(np.triu(np.ones((tm, tm), np.float32), 1), BF16)
    const = lambda a: pl.BlockSpec(a.shape, lambda i: (0,) * a.ndim)
    rb = router_bias.reshape(N_EXPERTS, 1).astype(F32)
    return pl.pallas_call(
        _router_body,
        grid=(n // tm,),
        in_specs=[pl.BlockSpec((tm, d), lambda i: (i, 0)),
                  pl.BlockSpec((1, d), lambda i: (0, 0)),
                  pl.BlockSpec((1, 1, d), lambda i: (rf(i), 0, 0)),
                  pl.BlockSpec((1, 1, d), lambda i: (rf(i), 0, 0)),
                  const(wr_hi), const(wr_lo), const(rb), const(tri)],
        out_specs=[pl.BlockSpec((tm, d), lambda i: (i, 0)),
                   pl.BlockSpec((8, tm), lambda i: (0, i)),
                   pl.BlockSpec((8, tm), lambda i: (0, i)),
                   pl.BlockSpec((N_EXPERTS, LANES), lambda i: (0, 0))],
        out_shape=[jax.ShapeDtypeStruct((n, d), F32),
                   jax.ShapeDtypeStruct((8, n), jnp.int32),
                   jax.ShapeDtypeStruct((8, n), F32),
                   jax.ShapeDtypeStruct((N_EXPERTS, LANES), F32)],
        scratch_shapes=[pltpu.VMEM((N_EXPERTS, 1), F32)],
        compiler_params=_cp(("arbitrary",)),
        name="norm_route",
    )(x, nw.reshape(1, d), shift.reshape(MOD_ROWS, 1, d), scale.reshape(MOD_ROWS, 1, d),
      wr_hi, wr_lo, rb, tri)


def _dispatch_body(sa_ref, sb_ref, pad_ref, h_ref, xs_ref, zero_ref, sem, *, tm, n_pad):
    i = pl.program_id(0)
    base = i * tm

    def row_copy(src, r, slot):
        return pltpu.make_async_copy(src.at[pl.ds(r, 1)], xs_ref.at[pl.ds(slot, 1)], sem)

    @pl.when(i == 0)
    def _():
        zero_ref[...] = jnp.zeros_like(zero_ref)

        def fill(j, carry):
            row_copy(zero_ref, 0, pad_ref[j]).start()
            return carry
        lax.fori_loop(0, n_pad, fill, 0)

        def drain(j, carry):
            row_copy(zero_ref, 0, 0).wait()
            return carry
        lax.fori_loop(0, n_pad, drain, 0)

    def issue(r, carry):
        row_copy(h_ref, r, sa_ref[base + r]).start()
        row_copy(h_ref, r, sb_ref[base + r]).start()
        return carry
    lax.fori_loop(0, tm, issue, 0)

    def drain2(r, carry):
        row_copy(h_ref, 0, 0).wait()
        row_copy(h_ref, 0, 0).wait()
        return carry
    lax.fori_loop(0, tm, drain2, 0)


def moe_dispatch(h, slot_a, slot_b, pad_slots, n_slots):
    n, d = h.shape
    tm = min(256, n)
    n_pad = pad_slots.shape[0]
    body = functools.partial(_dispatch_body, tm=tm, n_pad=n_pad)
    return pl.pallas_call(
        body,
        grid_spec=pltpu.PrefetchScalarGridSpec(
            num_scalar_prefetch=3,
            grid=(n // tm,),
            in_specs=[pl.BlockSpec((tm, d), lambda i, sa, sb, pd: (i, 0))],
            out_specs=pl.BlockSpec(memory_space=pl.ANY),
            scratch_shapes=[pltpu.VMEM((8, d), F32), pltpu.SemaphoreType.DMA(())]),
        out_shape=jax.ShapeDtypeStruct((n_slots, d), F32),
        compiler_params=_cp(("arbitrary",)),
        name="moe_dispatch",
    )(slot_a, slot_b, pad_slots, h)


def _experts_body(te_ref, nu_ref, xs_ref, wg_ref, wu_ref, wd_ref, y_ref):
    i = pl.program_id(0)

    @pl.when(i < nu_ref[0])
    def _():
        xb = xs_ref[...].astype(BF16)
        hg = jnp.dot(xb, wg_ref[0], preferred_element_type=F32)
        hu = jnp.dot(xb, wu_ref[0], preferred_element_type=F32)
        act = (hg * _sigmoid(hg) * hu).astype(BF16)
        y_ref[...] = jnp.dot(act, wd_ref[0], preferred_element_type=F32)

    @pl.when(i >= nu_ref[0])
    def _():
        y_ref[...] = jnp.zeros_like(y_ref)


def moe_experts(xs, tile_expert, n_used, w_gate, w_up, w_down):
    s, d = xs.shape
    f = w_gate.shape[2]
    tm = MOE_TILE
    return pl.pallas_call(
        _experts_body,
        grid_spec=pltpu.PrefetchScalarGridSpec(
            num_scalar_prefetch=2,
            grid=(s // tm,),
            in_specs=[pl.BlockSpec((tm, d), lambda i, te, nu: (jnp.minimum(i, nu[0] - 1), 0)),
                      pl.BlockSpec((1, d, f), lambda i, te, nu: (te[i], 0, 0)),
                      pl.BlockSpec((1, d, f), lambda i, te, nu: (te[i], 0, 0)),
                      pl.BlockSpec((1, f, d), lambda i, te, nu: (te[i], 0, 0))],
            out_specs=pl.BlockSpec((tm, d), lambda i, te, nu: (i, 0))),
        out_shape=jax.ShapeDtypeStruct((s, d), F32),
        compiler_params=_cp(("arbitrary",)),
        name="moe_experts",
    )(tile_expert, n_used, xs, w_gate, w_up, w_down)


def _combine_body(sa_ref, sb_ref, x_ref, w_ref, g_ref, y_ref, o_ref, ba_ref, bb_ref, sem, *, tm):
    i = pl.program_id(0)
    base = i * tm

    def row_copy(slot, dst, r):
        return pltpu.make_async_copy(y_ref.at[pl.ds(slot, 1)], dst.at[pl.ds(r, 1)], sem)

    def issue(r, carry):
        row_copy(sa_ref[base + r], ba_ref, r).start()
        row_copy(sb_ref[base + r], bb_ref, r).start()
        return carry
    lax.fori_loop(0, tm, issue, 0)

    def drain(r, carry):
        row_copy(0, ba_ref, 0).wait()
        row_copy(0, bb_ref, 0).wait()
        return carry
    lax.fori_loop(0, tm, drain, 0)

    w = w_ref[...]
    mix = w[:, 0:1] * ba_ref[...] + w[:, 1:2] * bb_ref[...]
    o_ref[...] = x_ref[...] + g_ref[0] * mix


def moe_combine(x, y, slot_a, slot_b, wts, gate, per_batch):
    n, d = x.shape
    tm = _row_tile(256, n, per_batch)
    rf = _mod_row_fn(n, tm, per_batch)
    body = functools.partial(_combine_body, tm=tm)
    return pl.pallas_call(
        body,
        grid_spec=pltpu.PrefetchScalarGridSpec(
            num_scalar_prefetch=2,
            grid=(n // tm,),
            in_specs=[pl.BlockSpec((tm, d), lambda i, sa, sb: (i, 0)),
                      pl.BlockSpec((tm, 8), lambda i, sa, sb: (i, 0)),
                      pl.BlockSpec((1, 1, d), lambda i, sa, sb: (rf(i), 0, 0)),
                      pl.BlockSpec(memory_space=pl.ANY)],
            out_specs=pl.BlockSpec((tm, d), lambda i, sa, sb: (i, 0)),
            scratch_shapes=[pltpu.VMEM((tm, d), F32), pltpu.VMEM((tm, d), F32), pltpu.SemaphoreType.DMA(())]),
        out_shape=jax.ShapeDtypeStruct((n, d), F32),
        compiler_params=_cp(("arbitrary",)),
        name="moe_combine",
    )(slot_a, slot_b, x, wts, gate.reshape(MOD_ROWS, 1, d), y)


def moe_layer(x, nw, mod, per_batch, w_router, router_bias, w_gate, w_up, w_down):
    n, d = x.shape
    h, route_i, route_w, counts = norm_route(x, nw, mod[:, 3], mod[:, 4], per_batch, w_router, router_bias)
    cnt = counts[:, 0].astype(jnp.int32)
    seg = ((cnt + MOE_TILE - 1) // MOE_TILE) * MOE_TILE
    off = jnp.concatenate([jnp.zeros((1,), jnp.int32), jnp.cumsum(seg)])
    n_slots = TOP_K * n + N_EXPERTS * MOE_TILE
    slot_a = off[route_i[0]] + route_i[2]
    slot_b = off[route_i[1]] + route_i[3]
    n_pad = n_slots - TOP_K * n
    padcnt = seg - cnt
    padstart = jnp.concatenate([jnp.zeros((1,), jnp.int32), jnp.cumsum(padcnt)])
    j = jnp.arange(n_pad, dtype=jnp.int32)
    e_of = jnp.clip(jnp.searchsorted(padstart, j, side='right') - 1, 0, N_EXPERTS)
    in_seg = off[jnp.minimum(e_of, N_EXPERTS - 1)] + cnt[jnp.minimum(e_of, N_EXPERTS - 1)] + (j - padstart[e_of])
    tail = off[N_EXPERTS] + (j - padstart[N_EXPERTS])
    pad_slots = jnp.where(e_of < N_EXPERTS, in_seg, tail).astype(jnp.int32)
    n_tiles = n_slots // MOE_TILE
    tile_start = jnp.arange(n_tiles, dtype=jnp.int32) * MOE_TILE
    n_used = (off[N_EXPERTS] // MOE_TILE).astype(jnp.int32).reshape(1)
    tile_expert = jnp.clip(jnp.searchsorted(off, tile_start, side='right') - 1, 0, N_EXPERTS - 1).astype(jnp.int32)
    last_used = tile_expert[jnp.maximum(n_used[0] - 1, 0)]
    tile_expert = jnp.where(jnp.arange(n_tiles) < n_used[0], tile_expert, last_used)

    xs = moe_dispatch(h, slot_a, slot_b, pad_slots, n_slots)
    y = moe_experts(xs, tile_expert, n_used, w_gate, w_up, w_down)
    wts = jnp.transpose(route_w)
    return moe_combine(x, y, slot_a, slot_b, wts, mod[:, 5], per_batch)


def _final_norm_body(x_ref, w_ref, o_ref):
    xf = x_ref[...]
    y = xf * lax.rsqrt(jnp.mean(xf * xf, axis=-1, keepdims=True) + NORM_EPS)
    o_ref[...] = y * w_ref[...]


def final_rms_norm(x, w):
    n, d = x.shape
    rows = 512
    return pl.pallas_call(
        _final_norm_body,
        grid=(n // rows,),
        in_specs=[pl.BlockSpec((rows, d), lambda i: (i, 0)), pl.BlockSpec((1, d), lambda i: (0, 0))],
        out_specs=pl.BlockSpec((rows, d), lambda i: (i, 0)),
        out_shape=jax.ShapeDtypeStruct((n, d), x.dtype),
        compiler_params=_cp(("parallel",)),
        name="final_norm",
    )(x, w.reshape(1, d))


def _reorder_w_in(w):
    o = np.cumsum((0,) + IN_WIDTHS)
    seg = lambda i: w[:, o[i]:o[i + 1]]
    main = jnp.concatenate([seg(0), seg(1), seg(2), seg(3), seg(6), seg(9), seg(12), seg(7), seg(10), seg(11)], axis=1)
    misc = jnp.concatenate([seg(8), seg(4), seg(5)], axis=1)
    misc = jnp.pad(misc, ((0, 0), (0, MISC_W - misc.shape[1])))
    return main.astype(BF16), misc.astype(BF16)


def _attention_branches(main_l, misc_l, main_c, misc_c, b, seq, ctx_len, ctx_out, rope_mla, rope_gqa,
                        mla_w, mla_qn, mla_kvn, gqa_qn, gqa_kn):
    g = GQA_HEADS // GQA_KV_HEADS
    hd = GQA_HEAD_DIM

    mq_l, mk_l, mv_l = mla_prepare(main_l, misc_l, b, seq, mla_w, mla_qn, mla_kvn, _mla_tables(rope_mla, seq))
    mq_c, mk_c, mv_c = mla_prepare(main_c, misc_c, b, ctx_len, mla_w, mla_qn, mla_kvn, _mla_tables(None, ctx_len))
    mla_l = flash_attention(mq_l[:, :, None], mk_l, mv_l, mk_c, mv_c, 512, 512)
    mla_l = jnp.transpose(mla_l[:, :, 0], (0, 2, 1, 3)).reshape(b * seq, BRANCH_W)

    def split_heads(t, length, heads):
        return jnp.transpose(t.reshape(b, length, heads, hd), (0, 2, 1, 3))

    gq_l, gk_l, gv_l = gqa_prepare(main_l, b, seq, gqa_qn, gqa_kn, rope_gqa)
    gq_c, gk_c, gv_c = gqa_prepare(main_c, b, ctx_len, gqa_qn, gqa_kn, None)
    gq_l5 = split_heads(gq_l, seq, GQA_HEADS).reshape(b, GQA_KV_HEADS, g, seq, hd)
    gk_l4, gv_l4 = split_heads(gk_l, seq, GQA_KV_HEADS), split_heads(gv_l, seq, GQA_KV_HEADS)
    gk_c4, gv_c4 = split_heads(gk_c, ctx_len, GQA_KV_HEADS), split_heads(gv_c, ctx_len, GQA_KV_HEADS)
    gqa_l = flash_attention(gq_l5, gk_l4, gv_l4, gk_c4, gv_c4, 256, 512)
    gqa_l = jnp.transpose(gqa_l.reshape(b, GQA_HEADS, seq, hd), (0, 2, 1, 3)).reshape(b * seq, BRANCH_W)

    mla_c = gqa_c = None
    if ctx_out:
        mla_c = flash_attention(mq_c[:, :, None], mk_c, mv_c, None, None, 256, 256)
        mla_c = jnp.transpose(mla_c[:, :, 0], (0, 2, 1, 3)).reshape(b * ctx_len, BRANCH_W)
        gq_c5 = split_heads(gq_c, ctx_len, GQA_HEADS).reshape(b, GQA_KV_HEADS, g, ctx_len, hd)
        gqa_c = flash_attention(gq_c5, gk_c4, gv_c4, None, None, 256, 256)
        gqa_c = jnp.transpose(gqa_c.reshape(b, GQA_HEADS, ctx_len, hd), (0, 2, 1, 3)).reshape(b * ctx_len, BRANCH_W)
    return mla_l, gqa_l, mla_c, gqa_c


def kernel(x, c, ctx, c_ctx, w_ada, b_ada, norm1_w, norm2_w, w_in,
           gdn_conv_w, gdn_a_log, gdn_dt_bias, gdn_norm_w,
           mla_q_norm_w, mla_kv_norm_w, mla_w_uq, mla_w_ukv,
           gqa_q_norm_w, gqa_k_norm_w,
           hy_conv_w, hy_w1, hy_b1, hy_w2, hy_b2, hy_w3, hy_sin_freq, hy_bias,
           w_branch, w_out, w_router, router_bias,
           moe_w_gate, moe_w_up, moe_w_down, final_norm_w):
    b, seq, d = x.shape
    ctx_len = ctx.shape[1]
    rows = seq // GRID_W
    rope_mla = axial_rope_tables(rows, MLA_ROPE)
    rope_gqa = axial_rope_tables(rows, GQA_HEAD_DIM)
    mod_all = ada_modulation(c, c_ctx, w_ada, b_ada)
    xl = x.reshape(b * seq, d)
    xc = ctx.reshape(b * ctx_len, d)
    f32 = lambda t: t.astype(F32)
    for layer in range(DEPTH):
        ctx_out = layer < DEPTH - 1
        mod = mod_all[layer]
        w_main, w_misc = _reorder_w_in(w_in[layer][:, :MIX_IN])
        w_gates = w_in[layer][:, MIX_IN:].astype(BF16)
        w_br = w_branch[layer].astype(BF16)
        w_o = w_out[layer].astype(BF16)
        wg, wu, wd = (t[layer].astype(BF16) for t in (moe_w_gate, moe_w_up, moe_w_down))

        hl = norm_modulate(xl, norm1_w[layer], mod[:, 0], mod[:, 1], seq)
        hc = norm_modulate(xc, norm1_w[layer], mod[:, 0], mod[:, 1], None)
        main_l, misc_l = matmul(hl, w_main, BF16, 512), matmul(hl, w_misc, F32, MISC_W)
        main_c, misc_c = matmul(hc, w_main, BF16, 512), matmul(hc, w_misc, F32, MISC_W)

        def gdn_parts(main, misc, length):
            t = main[:, OFF_GDN:OFF_GDN + 4 * GDN_W].reshape(b, length, 4, GDN_W)
            a = misc[:, MLA_ROPE:MLA_ROPE + 2 * GDN_HEADS].reshape(b, length, 2 * GDN_HEADS)
            bt = misc[:, MLA_ROPE + 2 * GDN_HEADS:MLA_ROPE + 4 * GDN_HEADS].reshape(b, length, 2 * GDN_HEADS)
            return [f32(t[:, :, 0]), f32(t[:, :, 1]), f32(t[:, :, 2]), f32(t[:, :, 3]), a, bt]

        gdn_l, gdn_c = gdn_mixer(gdn_parts(main_l, misc_l, seq), gdn_parts(main_c, misc_c, ctx_len),
                                 gdn_conv_w[layer], gdn_a_log[layer], gdn_dt_bias[layer], gdn_norm_w[layer], ctx_out)

        mla_w = _mla_weights(mla_q_norm_w[layer], mla_kv_norm_w[layer], mla_w_uq[layer], mla_w_ukv[layer])
        mla_l, gqa_l, mla_c, gqa_c = _attention_branches(
            main_l, misc_l, main_c, misc_c, b, seq, ctx_len, ctx_out, rope_mla, rope_gqa,
            mla_w, mla_q_norm_w[layer], mla_kv_norm_w[layer], gqa_q_norm_w[layer], gqa_k_norm_w[layer])

        hy_params = (hy_w1[layer], hy_b1[layer], hy_w2[layer], hy_b2[layer], hy_w3[layer], hy_sin_freq[layer])
        hy_in_l = f32(main_l[:, OFF_HY:OFF_HY + 3 * HY_WIDTH]).reshape(b, seq, 3 * HY_WIDTH)
        hy_l = hyena_mixer(hy_in_l, hy_conv_w[layer], hyena_filters(seq, *hy_params), hy_bias[layer])

        branches_l = [gdn_l.reshape(b * seq, BRANCH_W).astype(BF16), mla_l, gqa_l,
                      hy_l.reshape(b * seq, BRANCH_W).astype(BF16)]
        merged_l = merge_branches_gated(hl, branches_l, w_gates, w_br)

        if ctx_out:
            hy_in_c = f32(main_c[:, OFF_HY:OFF_HY + 3 * HY_WIDTH]).reshape(b, ctx_len, 3 * HY_WIDTH)
            hy_c = hyena_mixer(hy_in_c, hy_conv_w[layer], hyena_filters(ctx_len, *hy_params), hy_bias[layer])
            branches_c = [gdn_c.reshape(b * ctx_len, BRANCH_W).astype(BF16), mla_c, gqa_c,
                          hy_c.reshape(b * ctx_len, BRANCH_W).astype(BF16)]
            merged_c = merge_branches_gated(hc, branches_c, w_gates, w_br)
            xc = matmul_gated_residual(merged_c, w_o, xc, mod[:, 2], None)
            xc = moe_layer(xc, norm2_w[layer], mod, None, w_router, router_bias, wg, wu, wd)

        xl = matmul_gated_residual(merged_l, w_o, xl, mod[:, 2], seq)
        xl = moe_layer(xl, norm2_w[layer], mod, seq, w_router, router_bias, wg, wu, wd)
    return final_rms_norm(xl, final_norm_w).reshape(b, seq, d)
```

```python
import math, functools
import jax, jax.numpy as jnp
from jax import lax
import numpy as np
from jax.experimental import pallas as pl
from jax.experimental.pallas import tpu as pltpu

D_MODEL = 2048
BATCH = 4
SEQ = 4096
DEPTH = 2

GRID_W = 64
CTX_LEN = 256
N_BRANCH = 4
BRANCH_W = 512
NORM_EPS = 1e-6
Q_BLOCK = 128
ROPE_THETA = 10000.0
SHORT_CONV = 3

GDN_HEADS = 4
GDN_HEAD_DIM = 128
GDN_CHUNK = 64

MLA_HEADS = 4
MLA_Q_LORA = 512
MLA_KV_LORA = 256
MLA_NOPE = 128
MLA_ROPE = 64
MLA_V = 128

GQA_HEADS = 8
GQA_KV_HEADS = 2
GQA_HEAD_DIM = 64

HY_WIDTH = 512
HY_ORDER = 2
HY_EMB = 33
HY_HIDDEN = 64
HY_DECAY_TARGET = 1e-2
HY_FAST_DECAY = 0.3
HY_SLOW_DECAY = 1.5

N_EXPERTS = 16
N_GROUPS = 4
EXPERTS_PER_GROUP = N_EXPERTS // N_GROUPS
TOP_K = 2
D_EXPERT = 512

GDN_W = GDN_HEADS * GDN_HEAD_DIM
IN_WIDTHS = (GDN_W, GDN_W, GDN_W, GDN_W, 2 * GDN_HEADS, 2 * GDN_HEADS,
             MLA_Q_LORA, MLA_KV_LORA, MLA_ROPE,
             GQA_HEADS * GQA_HEAD_DIM, GQA_KV_HEADS * GQA_HEAD_DIM, GQA_KV_HEADS * GQA_HEAD_DIM,
             (HY_ORDER + 1) * HY_WIDTH)
MIX_IN = sum(IN_WIDTHS)
IN_DIM = MIX_IN + N_BRANCH * D_MODEL

F32 = jnp.float32
BF16 = jnp.bfloat16
LANES = 128
MOD_ROWS = 8
CTX_MOD_ROW = BATCH
MOE_TILE = 512
VMEM_LIMIT = 56 << 20

MAIN_W = 5120
OFF_GDN, OFF_CQ, OFF_GQ, OFF_HY, OFF_CKV, OFF_GK, OFF_GV = 0, 2048, 2560, 3072, 4608, 4864, 4992
MISC_W = LANES


def _cp(sem):
    return pltpu.CompilerParams(dimension_semantics=sem, vmem_limit_bytes=VMEM_LIMIT)


def _sigmoid(v):
    return 0.5 * jnp.tanh(0.5 * v) + 0.5


def rms_norm(x, w):
    xf = x.astype(jnp.float32)
    y = xf * lax.rsqrt(jnp.mean(xf * xf, axis=-1, keepdims=True) + NORM_EPS)
    return (y * w.astype(jnp.float32)).astype(x.dtype)


def l2_normalize(x):
    xf = x.astype(jnp.float32)
    return xf * lax.rsqrt(jnp.sum(xf * xf, axis=-1, keepdims=True) + NORM_EPS)


def depthwise_conv_centred(u, w):
    k = w.shape[0]
    return lax.conv_general_dilated(u, w[:, None, :].astype(u.dtype), window_strides=(1,),
                                    padding=[(k // 2, k // 2)],
                                    dimension_numbers=('NWC', 'WIO', 'NWC'),
                                    feature_group_count=u.shape[-1])


def axial_rope_tables(rows, rot_dim):
    n_freq = rot_dim // 4
    freqs = ROPE_THETA ** (-jnp.arange(n_freq, dtype=jnp.float32) / n_freq)
    row = jnp.repeat(jnp.arange(rows, dtype=jnp.float32), GRID_W)
    col = jnp.tile(jnp.arange(GRID_W, dtype=jnp.float32), rows)
    ang = jnp.concatenate([row[:, None] * freqs, col[:, None] * freqs], axis=-1)
    return jnp.cos(ang), jnp.sin(ang)


def gdn_prep(q, k, v, a, bt, conv_w, a_log, dt_bias):
    b, l = q.shape[:2]
    qkv = jax.nn.silu(depthwise_conv_centred(jnp.concatenate([q, k, v], axis=-1), conv_w)).astype(jnp.float32)
    q, k, v = jnp.split(qkv, 3, axis=-1)
    hd = (b, l, GDN_HEADS, GDN_HEAD_DIM)
    q = l2_normalize(q.reshape(hd)) * GDN_HEAD_DIM ** -0.5
    k = l2_normalize(k.reshape(hd))
    v = v.reshape(hd)
    a = a.astype(jnp.float32).reshape(b, l, 2, GDN_HEADS)
    g = -jnp.exp(a_log.astype(jnp.float32)) * jax.nn.softplus(a + dt_bias.astype(jnp.float32))
    beta = jax.nn.sigmoid(bt.astype(jnp.float32).reshape(b, l, 2, GDN_HEADS))
    return q, k, v, g, beta


def gated_delta_rule(q, k, v, g, beta, state, with_out):
    b, l, h, _ = q.shape
    dv = v.shape[-1]
    c = GDN_CHUNK
    n = l // c

    def to_chunks(t):
        t = t.reshape(b, n, c, h, *t.shape[3:])
        return jnp.moveaxis(t, (1, 3), (0, 2))

    qc, kc, vc, bc = to_chunks(q), to_chunks(k), to_chunks(v), to_chunks(beta)
    gc = jnp.cumsum(to_chunks(g), axis=-1)
    idx = jnp.arange(c)
    lower = idx[:, None] >= idx[None, :]
    strict = idx[:, None] > idx[None, :]
    diff = gc[..., :, None] - gc[..., None, :]
    decay = jnp.where(lower, jnp.exp(jnp.where(lower, diff, 0.0)), 0.0)
    kb = kc * bc[..., None]
    a = jnp.where(strict, jnp.einsum('nbhid,nbhjd->nbhij', kb, kc) * decay, 0.0)
    solve = functools.partial(lax.linalg.triangular_solve, left_side=True, lower=True, unit_diagonal=True)
    u = solve(a, vc * bc[..., None])
    w = solve(a, kb * jnp.exp(gc)[..., None])
    g_last = gc[..., -1]
    k_dec = kc * jnp.exp(g_last[..., None] - gc)[..., None]
    xs = (u, w, k_dec, g_last)
    if with_out:
        qk = jnp.where(lower, jnp.einsum('nbhid,nbhjd->nbhij', qc, kc) * decay, 0.0)
        xs = xs + (qc * jnp.exp(gc)[..., None], qk)

    def step(s, inp):
        u_i, w_i, kd_i, gl_i = inp[:4]
        v_new = u_i - jnp.einsum('bhck,bhkv->bhcv', w_i, s)
        s_new = s * jnp.exp(gl_i)[..., None, None] + jnp.einsum('bhck,bhcv->bhkv', kd_i, v_new)
        if not with_out:
            return s_new, None
        qd_i, qk_i = inp[4:]
        o = jnp.einsum('bhck,bhkv->bhcv', qd_i, s) + jnp.einsum('bhij,bhjv->bhiv', qk_i, v_new)
        return s_new, o

    state, o = lax.scan(step, state, xs)
    if not with_out:
        return None, state
    o = jnp.moveaxis(o, (0, 2), (1, 3)).reshape(b, l, h, dv)
    return o, state


def gdn_output(o, z, norm_w):
    b, l = z.shape[:2]
    zh = z.reshape(b, l, GDN_HEADS, GDN_HEAD_DIM).astype(jnp.float32)
    y = rms_norm(o, norm_w) * jax.nn.silu(zh)
    return y.reshape(b, l, GDN_W).astype(z.dtype)


def gdn_mixer(p_lat, p_ctx, conv_w, a_log, dt_bias, norm_w, ctx_out):
    lat = gdn_prep(p_lat[0], p_lat[1], p_lat[2], p_lat[4], p_lat[5], conv_w, a_log, dt_bias)
    ctx = gdn_prep(p_ctx[0], p_ctx[1], p_ctx[2], p_ctx[4], p_ctx[5], conv_w, a_log, dt_bias)
    b = p_lat[0].shape[0]
    s0 = jnp.zeros((b, GDN_HEADS, GDN_HEAD_DIM, GDN_HEAD_DIM), jnp.float32)
    o_lat, o_ctx = 0.0, 0.0
    for direction in range(2):
        flip = (lambda t: t[:, ::-1]) if direction else (lambda t: t)

        def seq_args(s):
            q, k, v, g, beta = s
            return flip(q), flip(k), flip(v), flip(g[:, :, direction]), flip(beta[:, :, direction])

        oc, s_ctx = gated_delta_rule(*seq_args(ctx), s0, ctx_out)
        ol, _ = gated_delta_rule(*seq_args(lat), s_ctx, True)
        o_lat = o_lat + flip(ol)
        if ctx_out:
            o_ctx = o_ctx + flip(oc)
    out_lat = gdn_output(o_lat, p_lat[3], norm_w)
    out_ctx = gdn_output(o_ctx, p_ctx[3], norm_w) if ctx_out else None
    return out_lat, out_ctx


def hyena_filters(length, w1, b1, w2, b2, w3, sin_freq):
    t = jnp.arange(length, dtype=jnp.float32)
    bands = (HY_EMB - 1) // 2
    f = jnp.linspace(1e-4, bands - 1, bands, dtype=jnp.float32)
    phase = (2.0 * math.pi / length) * t[:, None] * f[None, :]
    feats = jnp.concatenate([t[:, None] / (length - 1), jnp.cos(phase), -jnp.sin(phase)], axis=-1)
    hid = jnp.sin(sin_freq[0] * (feats @ w1 + b1))
    hid = jnp.sin(sin_freq[1] * (hid @ w2 + b2))
    filt = (hid @ w3).astype(jnp.float32)
    centre = length // 2
    dist = jnp.abs(t - centre) / centre
    deltas = jnp.abs(jnp.linspace(math.log(HY_DECAY_TARGET) / HY_SLOW_DECAY,
                                  math.log(HY_DECAY_TARGET) / HY_FAST_DECAY,
                                  HY_ORDER * HY_WIDTH, dtype=jnp.float32))
    filt = filt * jnp.exp(-dist[:, None] * deltas[None, :])
    filt = filt / jnp.sum(jnp.abs(filt), axis=0, keepdims=True)
    return filt.reshape(length, HY_ORDER, HY_WIDTH)


def fft_conv_centred(u, h):
    l = u.shape[1]
    n = 2 * l
    uf = jnp.fft.rfft(u.astype(jnp.float32), n=n, axis=1)
    hf = jnp.fft.rfft(h.astype(jnp.float32), n=n, axis=0)
    y = jnp.fft.irfft(uf * hf[None], n=n, axis=1)
    return y[:, l // 2: l // 2 + l]


def hyena_mixer(u, conv_w, filt, bias):
    parts = jnp.split(depthwise_conv_centred(u, conv_w).astype(jnp.float32), HY_ORDER + 1, axis=-1)
    z = parts[0]
    for o in range(HY_ORDER):
        z = parts[o + 1] * (fft_conv_centred(z, filt[:, o]) + bias[o] * z)
    return z


HALO_ROWS = 16


def _conv3(x, prev_row, next_row, w_ref):
    tm = x.shape[0]
    rows = lax.broadcasted_iota(jnp.int32, x.shape, 0)
    up = jnp.where(rows == 0, prev_row, pltpu.roll(x, 1, 0))
    dn = jnp.where(rows == tm - 1, next_row, pltpu.roll(x, tm - 1, 0))
    return w_ref[0:1, :] * up + w_ref[1:2, :] * x + w_ref[2:3, :] * dn


def _halo_rows(xp_ref, xn_ref):
    i = pl.program_id(1)
    prev = jnp.where(i == 0, 0.0, xp_ref[HALO_ROWS - 1:HALO_ROWS, :].astype(F32))
    nxt = jnp.where(i == pl.num_programs(1) - 1, 0.0, xn_ref[0:1, :].astype(F32))
    return prev, nxt


def _halo_specs(tm, width, col_block, nt, n_rows):
    per = tm // HALO_ROWS
    last = n_rows // HALO_ROWS - 1
    return [pl.BlockSpec((tm, width), lambda bi, i: (bi * nt + i, col_block)),
            pl.BlockSpec((HALO_ROWS, width), lambda bi, i: (jnp.maximum((bi * nt + i) * per - 1, 0), col_block)),
            pl.BlockSpec((HALO_ROWS, width), lambda bi, i: (jnp.minimum((bi * nt + i + 1) * per, last), col_block))]


def _split3(v):
    hi = v.astype(BF16)
    r1 = v - hi.astype(F32)
    mid = r1.astype(BF16)
    lo = (r1 - mid.astype(F32)).astype(BF16)
    return hi, mid, lo


def _gdn_prep_body(x_ref, xp_ref, xn_ref, misc_ref, cw_ref, alog_ref, dt_ref, gmask_ref, tp_ref, ts_ref,
                   q_out, k_out, v_out, gcf_out, gcb_out, beta_out):
    prev, nxt = _halo_rows(xp_ref, xn_ref)
    y = _conv3(x_ref[...].astype(F32), prev, nxt, cw_ref)
    y = y * _sigmoid(y)
    hd = GDN_HEAD_DIM
    for h in range(GDN_HEADS):
        qh = y[:, h * hd:(h + 1) * hd]
        kh = y[:, GDN_W + h * hd:GDN_W + (h + 1) * hd]
        qn = qh * lax.rsqrt(jnp.sum(qh * qh, axis=-1, keepdims=True) + NORM_EPS) * hd ** -0.5
        kn = kh * lax.rsqrt(jnp.sum(kh * kh, axis=-1, keepdims=True) + NORM_EPS)
        q_out[:, h * hd:(h + 1) * hd] = qn.astype(BF16)
        k_out[:, h * hd:(h + 1) * hd] = kn.astype(BF16)
    v_out[...] = y[:, 2 * GDN_W:3 * GDN_W].astype(BF16)
    m = misc_ref[...]
    a = m + dt_ref[...]
    softplus = jnp.maximum(a, 0.0) + jnp.log(1.0 + jnp.exp(-jnp.abs(a)))
    g = -(jnp.exp(alog_ref[...]) * gmask_ref[...]) * softplus
    beta_out[...] = _sigmoid(m)
    parts = _split3(g)
    gcf_out[...] = sum(jnp.dot(tp_ref[...], p, preferred_element_type=F32) for p in parts)
    gcb_out[...] = sum(jnp.dot(ts_ref[...], p, preferred_element_type=F32) for p in parts)


def gdn_prepare(main, misc, b, length, conv_w, a_log, dt_bias):
    n = b * length
    w3 = 3 * GDN_W
    tm = min(256, length)
    nt = length // tm
    lane0 = MLA_ROPE
    vec = lambda v: jnp.zeros((1, MISC_W), F32).at[0, lane0:lane0 + 2 * GDN_HEADS].set(v.reshape(-1))
    alog, dtb = vec(a_log), vec(dt_bias)
    gmask = vec(jnp.ones((2 * GDN_HEADS,), F32))
    r = np.arange(tm)
    same = (r[:, None] // GDN_CHUNK) == (r[None, :] // GDN_CHUNK)
    tpre = jnp.asarray(same & (r[None, :] <= r[:, None]), BF16)
    tsuf = jnp.asarray(same & (r[None, :] >= r[:, None]), BF16)
    const = lambda a: pl.BlockSpec(a.shape, lambda bi, i: (0,) * a.ndim)
    row = lambda width: pl.BlockSpec((tm, width), lambda bi, i: (bi * nt + i, 0))
    cw = conv_w.astype(F32)
    return pl.pallas_call(
        _gdn_prep_body,
        grid=(b, nt),
        in_specs=_halo_specs(tm, w3, OFF_GDN // w3, nt, n)
                 + [row(MISC_W), const(cw), const(alog), const(dtb), const(gmask), const(tpre), const(tsuf)],
        out_specs=[row(GDN_W), row(GDN_W), row(GDN_W), row(MISC_W), row(MISC_W), row(MISC_W)],
        out_shape=[jax.ShapeDtypeStruct((n, GDN_W), BF16)] * 3 + [jax.ShapeDtypeStruct((n, MISC_W), F32)] * 3,
        compiler_params=_cp(("parallel", "parallel")),
        name="gdn_prep",
    )(main, main, main, misc, cw, alog, dtb, gmask, tpre, tsuf)


def _gdn_chunk_body(qf_ref, kf_ref, vf_ref, qb_ref, kb_ref, vb_ref, gcf_ref, gcb_ref, bcf_ref, bcb_ref,
                    grf_ref, grb_ref, s0_ref, *rest, nc, with_out):
    if with_out:
        of_ref, ob_ref, sfin_ref, s_ref = rest
    else:
        sfin_ref, s_ref = rest
        of_ref = ob_ref = None
    c = pl.program_id(1)

    @pl.when(c == 0)
    def _():
        s_ref[...] = s0_ref[0]

    ch = GDN_CHUNK
    hd = GDN_HEAD_DIM
    ii = lax.broadcasted_iota(jnp.int32, (ch, ch), 0)
    jj = lax.broadcasted_iota(jnp.int32, (ch, ch), 1)
    nt_dims = (((1,), (1,)), ((), ()))
    tn_dims = (((0,), (0,)), ((), ()))
    bdot = lambda a, b_: jnp.dot(a.astype(BF16), b_.astype(BF16), preferred_element_type=F32)
    eye = jnp.where(ii == jj, 1.0, 0.0)
    pair_masks = [((ii >> (l + 1)) == (jj >> (l + 1))) & ((ii >> l) != (jj >> l))
                  for l in range(int(math.log2(ch)))]
    dirs = ((qf_ref, kf_ref, vf_ref, gcf_ref, bcf_ref, grf_ref, of_ref, ii >= jj, ii > jj, ch - 1),
            (qb_ref, kb_ref, vb_ref, gcb_ref, bcb_ref, grb_ref, ob_ref, ii <= jj, ii < jj, 0))
    for d, (q_ref, k_ref, v_ref, gc_ref, bc_ref, gr_ref, o_ref, incl, strict, last_row) in enumerate(dirs):
        for h in range(GDN_HEADS):
            j = d * GDN_HEADS + h
            cols = slice(h * hd, (h + 1) * hd)
            q, k, v = q_ref[:, cols], k_ref[:, cols], v_ref[:, cols]
            gc = gc_ref[:, j:j + 1]
            gr = gr_ref[0, j:j + 1, :]
            beta = bc_ref[:, j:j + 1]
            g_last = gc_ref[last_row:last_row + 1, j:j + 1]
            decay = jnp.where(incl, jnp.exp(jnp.where(incl, gc - gr, 0.0)), 0.0)
            kf = k.astype(F32)
            kbeta = kf * beta
            a = jnp.where(strict, lax.dot_general(kbeta.astype(BF16), k, nt_dims, preferred_element_type=F32) * decay, 0.0)
            t = eye - jnp.where(pair_masks[0], a, 0.0)
            for pm in pair_masks[1:]:
                t = t - bdot(bdot(t, jnp.where(pm, a, 0.0)), t)
            x = bdot(t, jnp.concatenate([v.astype(F32) * beta, kbeta * jnp.exp(gc)], axis=1))
            u, w = x[:, :hd], x[:, hd:]
            s = s_ref[j]
            v_new = u - bdot(w, s)
            if with_out:
                qk = jnp.where(incl, lax.dot_general(q, k, nt_dims, preferred_element_type=F32) * decay, 0.0)
                o_ref[:, cols] = bdot(q.astype(F32) * jnp.exp(gc), s) + bdot(qk, v_new)
            kdec = kf * jnp.exp(g_last - gc)
            s_ref[j] = s * jnp.exp(g_last) + lax.dot_general(kdec.astype(BF16), v_new.astype(BF16), tn_dims,
                                                             preferred_element_type=F32)

    @pl.when(c == nc - 1)
    def _():
        sfin_ref[0] = s_ref[...]


def gdn_scan(q, k, v, gcol, bcol, grow, s0, b, length, with_out):
    n = b * length
    ch = GDN_CHUNK
    nc = length // ch
    nst = 2 * GDN_HEADS
    fwd = lambda bi, c: (bi * nc + c, 0)
    bwd = lambda bi, c: (bi * nc + nc - 1 - c, 0)
    fwd3 = lambda bi, c: (bi * nc + c, 0, 0)
    bwd3 = lambda bi, c: (bi * nc + nc - 1 - c, 0, 0)
    wide = lambda f: pl.BlockSpec((ch, GDN_W), f)
    narrow = lambda f: pl.BlockSpec((ch, nst), f)
    rows = lambda f: pl.BlockSpec((1, nst, ch), f)
    state = pl.BlockSpec((1, nst, GDN_HEAD_DIM, GDN_HEAD_DIM), lambda bi, c: (bi, 0, 0, 0))
    out_specs = [state]
    out_shape = [jax.ShapeDtypeStruct((b, nst, GDN_HEAD_DIM, GDN_HEAD_DIM), F32)]
    if with_out:
        out_specs = [wide(fwd), wide(bwd)] + out_specs
        out_shape = [jax.ShapeDtypeStruct((n, GDN_W), F32)] * 2 + out_shape
    body = functools.partial(_gdn_chunk_body, nc=nc, with_out=with_out)
    outs = pl.pallas_call(
        body,
        grid=(b, nc),
        in_specs=[wide(fwd), wide(fwd), wide(fwd), wide(bwd), wide(bwd), wide(bwd),
                  narrow(fwd), narrow(bwd), narrow(fwd), narrow(bwd), rows(fwd3), rows(bwd3), state],
        out_specs=out_specs,
        out_shape=out_shape,
        scratch_shapes=[pltpu.VMEM((nst, GDN_HEAD_DIM, GDN_HEAD_DIM), F32)],
        compiler_params=_cp(("parallel", "arbitrary")),
        name="gdn_scan",
    )(q, k, v, q, k, v, gcol, gcol, bcol, bcol, grow, grow, s0)
    return outs


def _gdn_out_body(of_ref, ob_ref, z_ref, nw_ref, y_ref):
    o = of_ref[...] + ob_ref[...]
    z = z_ref[...].astype(F32)
    hd = GDN_HEAD_DIM
    for h in range(GDN_HEADS):
        cols = slice(h * hd, (h + 1) * hd)
        oh = o[:, cols]
        yh = oh * lax.rsqrt(jnp.mean(oh * oh, axis=-1, keepdims=True) + NORM_EPS) * nw_ref[...]
        zh = z[:, cols]
        y_ref[:, cols] = (yh * (zh * _sigmoid(zh))).astype(BF16)


def gdn_output_gate(o_f, o_b, main, norm_w):
    n = o_f.shape[0]
    tm = min(512, n)
    nw = norm_w.reshape(1, GDN_HEAD_DIM).astype(F32)
    return pl.pallas_call(
        _gdn_out_body,
        grid=(n // tm,),
        in_specs=[pl.BlockSpec((tm, GDN_W), lambda i: (i, 0)),
                  pl.BlockSpec((tm, GDN_W), lambda i: (i, 0)),
                  pl.BlockSpec((tm, GDN_W), lambda i: (i, (OFF_GDN + 3 * GDN_W) // GDN_W)),
                  pl.BlockSpec((1, GDN_HEAD_DIM), lambda i: (0, 0))],
        out_specs=pl.BlockSpec((tm, GDN_W), lambda i: (i, 0)),
        out_shape=jax.ShapeDtypeStruct((n, GDN_W), BF16),
        compiler_params=_cp(("parallel",)),
        name="gdn_out",
    )(o_f, o_b, main, nw)


def gdn_branch(main_l, misc_l, main_c, misc_c, b, seq, ctx_len, conv_w, a_log, dt_bias, norm_w, ctx_out):
    nst = 2 * GDN_HEADS
    lane0 = MLA_ROPE

    def gates(gcf, gcb, beta, length):
        gcol = jnp.concatenate([gcf[:, lane0:lane0 + GDN_HEADS], gcb[:, lane0 + GDN_HEADS:lane0 + nst]], axis=1)
        bcol = beta[:, lane0 + nst:lane0 + 2 * nst]
        grow = jnp.transpose(gcol.reshape(-1, GDN_CHUNK, nst), (0, 2, 1))
        return gcol, bcol, grow

    qc, kc, vc, gcf, gcb, beta = gdn_prepare(main_c, misc_c, b, ctx_len, conv_w, a_log, dt_bias)
    gcol_c, bcol_c, grow_c = gates(gcf, gcb, beta, ctx_len)
    ql, kl, vl, gcf, gcb, beta = gdn_prepare(main_l, misc_l, b, seq, conv_w, a_log, dt_bias)
    gcol_l, bcol_l, grow_l = gates(gcf, gcb, beta, seq)
    s0 = jnp.zeros((b, nst, GDN_HEAD_DIM, GDN_HEAD_DIM), F32)
    outs_c = gdn_scan(qc, kc, vc, gcol_c, bcol_c, grow_c, s0, b, ctx_len, ctx_out)
    s_ctx = outs_c[-1]
    of_l, ob_l, _ = gdn_scan(ql, kl, vl, gcol_l, bcol_l, grow_l, s_ctx, b, seq, True)
    out_l = gdn_output_gate(of_l, ob_l, main_l, norm_w)
    out_c = gdn_output_gate(outs_c[0], outs_c[1], main_c, norm_w) if ctx_out else None
    return out_l, out_c


def _hy_conv_body(x_ref, xp_ref, xn_ref, cw_ref, v_out, x1_out, x2_out):
    prev, nxt = _halo_rows(xp_ref, xn_ref)
    y = _conv3(x_ref[...].astype(F32), prev, nxt, cw_ref)
    w = HY_WIDTH
    v_out[...] = y[:, :w].astype(BF16)
    x1_out[...] = y[:, w:2 * w].astype(BF16)
    x2_out[...] = y[:, 2 * w:3 * w].astype(BF16)


def hyena_short_conv(main, b, length, conv_w):
    n = b * length
    w3 = (HY_ORDER + 1) * HY_WIDTH
    tm = min(256, length)
    nt = length // tm
    cw = conv_w.astype(F32)
    row = pl.BlockSpec((tm, HY_WIDTH), lambda bi, i: (bi * nt + i, 0))
    return pl.pallas_call(
        _hy_conv_body,
        grid=(b, nt),
        in_specs=_halo_specs(tm, w3, OFF_HY // w3, nt, n) + [pl.BlockSpec(cw.shape, lambda bi, i: (0, 0))],
        out_specs=[row, row, row],
        out_shape=[jax.ShapeDtypeStruct((n, HY_WIDTH), BF16)] * 3,
        compiler_params=_cp(("parallel", "parallel")),
        name="hyena_conv",
    )(main, main, main, cw)


def _dft_consts(length):
    n = 2 * length
    n2 = 64 if length >= 2048 else 16
    n1 = n // n2
    k1 = np.arange(n1)
    t1 = np.arange(n1 // 2)
    ang1 = 2.0 * np.pi * np.outer(k1, t1) / n1
    f_first = np.concatenate([np.cos(ang1), -np.sin(ang1)], axis=0)
    t2 = np.arange(n2)
    ang2 = 2.0 * np.pi * np.outer(t2, t2) / n2
    c2, s2 = np.cos(ang2), np.sin(ang2)
    g_fwd = np.block([[c2, s2], [-s2, c2]])
    g_inv = g_fwd.T
    angt = 2.0 * np.pi * np.outer(k1, t2) / n
    tw_r, tw_i = np.cos(angt)[:, :, None], -np.sin(angt)[:, :, None]
    tt = np.arange(n1 // 4, 3 * n1 // 4)
    ang3 = 2.0 * np.pi * np.outer(tt, k1) / n1
    f_last = np.concatenate([np.cos(ang3), -np.sin(ang3)], axis=1) / n
    bf = lambda a: jnp.asarray(a, BF16)
    return dict(n1=n1, n2=n2, f_first=bf(f_first), g_fwd=bf(g_fwd), g_inv=bf(g_inv),
                tw_r=jnp.asarray(tw_r, F32), tw_i=jnp.asarray(tw_i, F32), f_last=bf(f_last))


def _hy_first_body(f_ref, z_ref, a_ref):
    a_ref[0] = jnp.dot(f_ref[...], z_ref[0], preferred_element_type=F32).astype(BF16)


def hyena_dft_first(zv, consts):
    b, half, cols = zv.shape
    n1 = consts['n1']
    tn = min(4096, cols)
    f = consts['f_first']
    return pl.pallas_call(
        _hy_first_body,
        grid=(b, cols // tn),
        in_specs=[pl.BlockSpec(f.shape, lambda bi, j: (0, 0)),
                  pl.BlockSpec((1, half, tn), lambda bi, j: (bi, 0, j))],
        out_specs=pl.BlockSpec((1, 2 * n1, tn), lambda bi, j: (bi, 0, j)),
        out_shape=jax.ShapeDtypeStruct((b, 2 * n1, cols), BF16),
        compiler_params=_cp(("parallel", "parallel")),
        name="hyena_dft_first",
    )(f, zv)


def _hy_mid_body(a_ref, twr_ref, twi_ref, gf_ref, *rest, kt, spectrum_only):
    if spectrum_only:
        (o_ref,) = rest
    else:
        gi_ref, h_ref, o_ref = rest
    n2 = gf_ref.shape[0] // 2

    def one(i, carry):
        ar = a_ref[0, 0, i].astype(F32)
        ai = a_ref[0, 1, i].astype(F32)
        twr, twi = twr_ref[i], twi_ref[i]
        br = ar * twr - ai * twi
        bi = ar * twi + ai * twr
        z = jnp.dot(gf_ref[...], jnp.concatenate([br, bi], axis=0).astype(BF16), preferred_element_type=F32)
        zr, zi = z[:n2], z[n2:]
        if spectrum_only:
            o_ref[0, 0, i] = zr
            o_ref[0, 1, i] = zi
            return carry
        hr, hi = h_ref[0, i], h_ref[1, i]
        yr = zr * hr - zi * hi
        yi = zr * hi + zi * hr
        w = jnp.dot(gi_ref[...], jnp.concatenate([yr, yi], axis=0).astype(BF16), preferred_element_type=F32)
        wr, wi = w[:n2], w[n2:]
        o_ref[0, 0, i] = (wr * twr + wi * twi).astype(BF16)
        o_ref[0, 1, i] = (wi * twr - wr * twi).astype(BF16)
        return carry

    lax.fori_loop(0, kt, one, 0)


def hyena_dft_mid(a5, consts, spectrum=None):
    b, _, n1, n2, c = a5.shape
    kt = 8
    only = spectrum is None
    blk = pl.BlockSpec((1, 2, kt, n2, c), lambda bi, j: (bi, 0, j, 0, 0))
    tw = pl.BlockSpec((kt, n2, 1), lambda bi, j: (j, 0, 0))
    g = pl.BlockSpec((2 * n2, 2 * n2), lambda bi, j: (0, 0))
    in_specs = [blk, tw, tw, g]
    args = [a5, consts['tw_r'], consts['tw_i'], consts['g_fwd']]
    if not only:
        in_specs += [g, pl.BlockSpec((2, kt, n2, c), lambda bi, j: (0, j, 0, 0))]
        args += [consts['g_inv'], spectrum]
    body = functools.partial(_hy_mid_body, kt=kt, spectrum_only=only)
    return pl.pallas_call(
        body,
        grid=(b, n1 // kt),
        in_specs=in_specs,
        out_specs=blk,
        out_shape=jax.ShapeDtypeStruct(a5.shape, F32 if only else BF16),
        compiler_params=_cp(("parallel", "parallel")),
        name="hyena_dft_mid",
    )(*args)


def _hy_last_body(f_ref, b_ref, z_ref, x_ref, bias_ref, o_ref):
    y = jnp.dot(f_ref[...], b_ref[0], preferred_element_type=F32)
    z = z_ref[0].astype(F32)
    o_ref[0] = (x_ref[0].astype(F32) * (y + bias_ref[...] * z)).astype(BF16)


def hyena_dft_last(bv, zv, xv, bias_row, consts):
    b, rows2, cols = bv.shape
    half = rows2 // 4
    tn = min(4096, cols)
    f = consts['f_last']
    sig = pl.BlockSpec((1, half, tn), lambda bi, j: (bi, 0, j))
    return pl.pallas_call(
        _hy_last_body,
        grid=(b, cols // tn),
        in_specs=[pl.BlockSpec(f.shape, lambda bi, j: (0, 0)),
                  pl.BlockSpec((1, rows2, tn), lambda bi, j: (bi, 0, j)),
                  sig, sig,
                  pl.BlockSpec((1, tn), lambda bi, j: (0, j))],
        out_specs=sig,
        out_shape=jax.ShapeDtypeStruct((b, half, cols), BF16),
        compiler_params=_cp(("parallel", "parallel")),
        name="hyena_dft_last",
    )(f, bv, zv, xv, bias_row)


def hyena_branch(main, b, length, conv_w, filt, bias):
    consts = _dft_consts(length)
    n1, n2 = consts['n1'], consts['n2']
    c = HY_WIDTH
    cols = n2 * c
    view = lambda t: t.reshape(b, n1 // 2, cols)
    v, x1, x2 = (view(t) for t in hyena_short_conv(main, b, length, conv_w))
    hv = jnp.transpose(filt, (1, 0, 2)).astype(BF16).reshape(HY_ORDER, n1 // 2, cols)
    h_first = hyena_dft_first(hv, consts).reshape(HY_ORDER, 2, n1, n2, c)
    spectra = hyena_dft_mid(h_first, consts)
    z = v
    for o, gate in enumerate((x1, x2)):
        a5 = hyena_dft_first(z, consts).reshape(b, 2, n1, n2, c)
        bm = hyena_dft_mid(a5, consts, spectra[o]).reshape(b, 2 * n1, cols)
        bias_row = jnp.tile(bias[o].astype(F32), n2).reshape(1, cols)
        z = hyena_dft_last(bm, z, gate, bias_row, consts)
    return z.reshape(b * length, c)


def _ada_body(c_ref, w_ref, b_ref, o_ref):
    cv = c_ref[...]
    s = cv * _sigmoid(cv)
    o_ref[0] = jnp.dot(s, w_ref[0], precision=lax.Precision.HIGHEST, preferred_element_type=F32) + b_ref[0]


def ada_modulation(c, c_ctx, w_ada, b_ada):
    depth, d, d6 = w_ada.shape
    c8 = jnp.zeros((MOD_ROWS, d), F32).at[:c.shape[0]].set(c).at[CTX_MOD_ROW].set(c_ctx)
    tn = 512
    out = pl.pallas_call(
        _ada_body,
        grid=(depth, d6 // tn),
        in_specs=[pl.BlockSpec((MOD_ROWS, d), lambda l, j: (0, 0)),
                  pl.BlockSpec((1, d, tn), lambda l, j: (l, 0, j)),
                  pl.BlockSpec((1, 1, tn), lambda l, j: (l, 0, j))],
        out_specs=pl.BlockSpec((1, MOD_ROWS, tn), lambda l, j: (l, 0, j)),
        out_shape=jax.ShapeDtypeStruct((depth, MOD_ROWS, d6), F32),
        compiler_params=_cp(("parallel", "parallel")),
        name="ada_mod",
    )(c8, w_ada, b_ada.reshape(depth, 1, d6))
    return out.reshape(depth, MOD_ROWS, 6, d)


def _row_tile(cap, n_rows, per_batch):
    return min(cap, n_rows if per_batch is None else per_batch)


def _mod_row_fn(n_rows, tm, per_batch):
    if per_batch is None:
        return lambda i: CTX_MOD_ROW
    tiles = per_batch // tm
    return lambda i: i // tiles


def _normmod_body(x_ref, nw_ref, sh_ref, sc_ref, o_ref):
    xf = x_ref[...]
    y = xf * lax.rsqrt(jnp.mean(xf * xf, axis=-1, keepdims=True) + NORM_EPS) * nw_ref[...]
    o_ref[...] = (y * (1.0 + sc_ref[0]) + sh_ref[0]).astype(o_ref.dtype)


def norm_modulate(x, nw, shift, scale, per_batch, out_dtype=BF16):
    n, d = x.shape
    tm = _row_tile(512, n, per_batch)
    rf = _mod_row_fn(n, tm, per_batch)
    return pl.pallas_call(
        _normmod_body,
        grid=(n // tm,),
        in_specs=[pl.BlockSpec((tm, d), lambda i: (i, 0)),
                  pl.BlockSpec((1, d), lambda i: (0, 0)),
                  pl.BlockSpec((1, 1, d), lambda i: (rf(i), 0, 0)),
                  pl.BlockSpec((1, 1, d), lambda i: (rf(i), 0, 0))],
        out_specs=pl.BlockSpec((tm, d), lambda i: (i, 0)),
        out_shape=jax.ShapeDtypeStruct((n, d), out_dtype),
        compiler_params=_cp(("parallel",)),
        name="norm_mod",
    )(x, nw.reshape(1, d), shift.reshape(MOD_ROWS, 1, d), scale.reshape(MOD_ROWS, 1, d))


def _mm_body(a_ref, w_ref, o_ref):
    o_ref[...] = jnp.dot(a_ref[...], w_ref[...], preferred_element_type=F32).astype(o_ref.dtype)


def matmul(a, w, out_dtype, tn):
    n, k = a.shape
    m = w.shape[1]
    tm = min(2048, n)
    return pl.pallas_call(
        _mm_body,
        grid=(n // tm, m // tn),
        in_specs=[pl.BlockSpec((tm, k), lambda i, j: (i, 0)),
                  pl.BlockSpec((k, tn), lambda i, j: (0, j))],
        out_specs=pl.BlockSpec((tm, tn), lambda i, j: (i, j)),
        out_shape=jax.ShapeDtypeStruct((n, m), out_dtype),
        compiler_params=_cp(("parallel", "parallel")),
        name="proj",
    )(a, w)


def _mm_res_body(a_ref, w_ref, x_ref, g_ref, o_ref):
    y = jnp.dot(a_ref[...], w_ref[...], preferred_element_type=F32)
    o_ref[...] = x_ref[...] + g_ref[0] * y


def matmul_gated_residual(a, w, x, gate, per_batch):
    n, k = a.shape
    d = w.shape[1]
    tm = _row_tile(1024, n, per_batch)
    tn = min(512, d)
    rf = _mod_row_fn(n, tm, per_batch)
    return pl.pallas_call(
        _mm_res_body,
        grid=(n // tm, d // tn),
        in_specs=[pl.BlockSpec((tm, k), lambda i, j: (i, 0)),
                  pl.BlockSpec((k, tn), lambda i, j: (0, j)),
                  pl.BlockSpec((tm, tn), lambda i, j: (i, j)),
                  pl.BlockSpec((1, 1, tn), lambda i, j: (rf(i), 0, j))],
        out_specs=pl.BlockSpec((tm, tn), lambda i, j: (i, j)),
        out_shape=jax.ShapeDtypeStruct((n, d), F32),
        compiler_params=_cp(("parallel", "parallel")),
        name="out_proj_residual",
    )(a, w, x, gate.reshape(MOD_ROWS, 1, d))


def _merge_body(h_ref, b0_ref, b1_ref, b2_ref, b3_ref, wg_ref, wb_ref, o_ref, acc_ref):
    n = pl.program_id(2)

    @pl.when(n == 0)
    def _():
        acc_ref[...] = jnp.zeros_like(acc_ref)

    gate = _sigmoid(jnp.dot(h_ref[...], wg_ref[...], preferred_element_type=F32))
    for idx, b_ref in enumerate((b0_ref, b1_ref, b2_ref, b3_ref)):
        @pl.when(n == idx)
        def _(b_ref=b_ref):
            acc_ref[...] += gate * jnp.dot(b_ref[...], wb_ref[0], preferred_element_type=F32)

    @pl.when(n == N_BRANCH - 1)
    def _():
        o_ref[...] = acc_ref[...].astype(o_ref.dtype)


def merge_branches_gated(h, branches, w_gate, w_branch):
    n, d = h.shape
    bw = branches[0].shape[1]
    tm = min(1024, n)
    tn = min(512, d)
    nj = d // tn
    return pl.pallas_call(
        _merge_body,
        grid=(n // tm, nj, N_BRANCH),
        in_specs=[pl.BlockSpec((tm, d), lambda i, j, b: (i, 0))]
                 + [pl.BlockSpec((tm, bw), lambda i, j, b: (i, 0))] * N_BRANCH
                 + [pl.BlockSpec((d, tn), lambda i, j, b: (0, b * nj + j)),
                    pl.BlockSpec((1, bw, tn), lambda i, j, b: (b, 0, j))],
        out_specs=pl.BlockSpec((tm, tn), lambda i, j, b: (i, j)),
        out_shape=jax.ShapeDtypeStruct((n, d), BF16),
        scratch_shapes=[pltpu.VMEM((tm, tn), F32)],
        compiler_params=_cp(("parallel", "parallel", "arbitrary")),
        name="gate_merge",
    )(h, *branches, w_gate, w_branch)


def _mla_prep_body(cq_ref, ckv_ref, misc_ref, qnw_ref, kvnw_ref, wqa_ref, wqb_ref, wk_ref, wv_ref,
                   ska_ref, skb_ref, cq_tab, sq_tab, q_out, k_out, v_out):
    def norm(v, w_ref):
        vf = v.astype(F32)
        return (vf * lax.rsqrt(jnp.mean(vf * vf, axis=-1, keepdims=True) + NORM_EPS) * w_ref[...]).astype(BF16)

    xq = norm(cq_ref[...], qnw_ref)
    xkv = norm(ckv_ref[...], kvnw_ref)
    cos, sin = cq_tab[...], sq_tab[...]
    misc = misc_ref[...].astype(BF16)
    kr = (jnp.dot(misc, ska_ref[...], preferred_element_type=F32) * cos
          + jnp.dot(misc, skb_ref[...], preferred_element_type=F32) * sin)
    for h in range(MLA_HEADS):
        qa = jnp.dot(xq, wqa_ref[h], preferred_element_type=F32)
        qb = jnp.dot(xq, wqb_ref[h], preferred_element_type=F32)
        q_out[0, h] = (qa * cos + qb * sin).astype(BF16)
        k_out[0, h] = (jnp.dot(xkv, wk_ref[h], preferred_element_type=F32) + kr).astype(BF16)
        v_out[0, h] = jnp.dot(xkv, wv_ref[h], preferred_element_type=F32).astype(BF16)


def _mla_weights(q_norm_w, kv_norm_w, w_uq, w_ukv):
    dk = MLA_NOPE + MLA_ROPE
    half = MLA_ROPE // 2
    scale = dk ** -0.5
    wq = jnp.transpose(w_uq, (1, 0, 2)) * scale
    nope0 = jnp.zeros(wq.shape[:2] + (MLA_NOPE,), F32)
    wq_rot = jnp.concatenate([nope0, -wq[..., MLA_NOPE + half:], wq[..., MLA_NOPE:MLA_NOPE + half]], axis=-1)
    wkv = jnp.transpose(w_ukv, (1, 0, 2))
    wk = jnp.concatenate([wkv[..., :MLA_NOPE], jnp.zeros(wkv.shape[:2] + (MLA_ROPE,), F32)], axis=-1)
    wv = wkv[..., MLA_NOPE:]
    eye = jnp.eye(MLA_ROPE, dtype=F32)
    rot = jnp.concatenate([-eye[:, half:], eye[:, :half]], axis=-1)
    pad_r = MISC_W - MLA_ROPE
    ska = jnp.pad(eye, ((0, pad_r), (MLA_NOPE, 0)))
    skb = jnp.pad(rot, ((0, pad_r), (MLA_NOPE, 0)))
    return tuple(t.astype(BF16) for t in (wq, wq_rot, wk, wv, ska, skb))


def _mla_tables(rope, length):
    dk = MLA_NOPE + MLA_ROPE
    if rope is None:
        return jnp.ones((length, dk), F32), jnp.zeros((length, dk), F32)
    cos, sin = rope
    ones = jnp.ones((length, MLA_NOPE), F32)
    return (jnp.concatenate([ones, cos, cos], axis=-1),
            jnp.concatenate([0.0 * ones, sin, sin], axis=-1))


def mla_prepare(main, misc, b, length, weights, q_norm_w, kv_norm_w, tables):
    wq, wq_rot, wk, wv, ska, skb = weights
    cos, sin = tables
    dk = MLA_NOPE + MLA_ROPE
    tm = min(512, length)
    nt = length // tm
    full = lambda a: pl.BlockSpec(a.shape, lambda bi, i: (0,) * a.ndim)
    qnw = q_norm_w.reshape(1, -1)
    kvnw = kv_norm_w.reshape(1, -1)
    outs = pl.pallas_call(
        _mla_prep_body,
        grid=(b, nt),
        in_specs=[pl.BlockSpec((tm, MLA_Q_LORA), lambda bi, i: (bi * nt + i, OFF_CQ // MLA_Q_LORA)),
                  pl.BlockSpec((tm, MLA_KV_LORA), lambda bi, i: (bi * nt + i, OFF_CKV // MLA_KV_LORA)),
                  pl.BlockSpec((tm, MISC_W), lambda bi, i: (bi * nt + i, 0)),
                  full(qnw), full(kvnw), full(wq), full(wq_rot), full(wk), full(wv), full(ska), full(skb),
                  pl.BlockSpec((tm, dk), lambda bi, i: (i, 0)),
                  pl.BlockSpec((tm, dk), lambda bi, i: (i, 0))],
        out_specs=[pl.BlockSpec((1, MLA_HEADS, tm, dk), lambda bi, i: (bi, 0, i, 0)),
                   pl.BlockSpec((1, MLA_HEADS, tm, dk), lambda bi, i: (bi, 0, i, 0)),
                   pl.BlockSpec((1, MLA_HEADS, tm, MLA_V), lambda bi, i: (bi, 0, i, 0))],
        out_shape=[jax.ShapeDtypeStruct((b, MLA_HEADS, length, dk), BF16),
                   jax.ShapeDtypeStruct((b, MLA_HEADS, length, dk), BF16),
                   jax.ShapeDtypeStruct((b, MLA_HEADS, length, MLA_V), BF16)],
        compiler_params=_cp(("parallel", "parallel")),
        name="mla_prep",
    )(main, main, misc, qnw, kvnw, wq, wq_rot, wk, wv, ska, skb, cos, sin)
    return outs


def _gqa_prep_body(q_ref, k_ref, v_ref, qnw_ref, knw_ref, gsum_ref, rot_ref, cos_ref, sin_ref,
                   q_out, k_out, v_out):
    cos, sin = cos_ref[...], sin_ref[...]

    def prep(v, nw, width):
        vf = v.astype(F32)
        sq = vf * vf
        hi = sq.astype(BF16)
        lo = (sq - hi.astype(F32)).astype(BF16)
        g = gsum_ref[:width, :width]
        ss = jnp.dot(hi, g, preferred_element_type=F32) + jnp.dot(lo, g, preferred_element_type=F32)
        xn = vf * lax.rsqrt(ss * (1.0 / GQA_HEAD_DIM) + NORM_EPS) * nw
        xr = jnp.dot(xn.astype(BF16), rot_ref[:width, :width], preferred_element_type=F32)
        return xn * cos[:, :width] + xr * sin[:, :width]

    qf = prep(q_ref[...], qnw_ref[...], GQA_HEADS * GQA_HEAD_DIM) * GQA_HEAD_DIM ** -0.5
    kf = prep(k_ref[...], knw_ref[...], LANES)
    q_out[...] = qf.astype(BF16)
    k_out[...] = kf.astype(BF16)
    v_out[...] = v_ref[...]


def gqa_prepare(main, b, length, q_norm_w, k_norm_w, rope):
    n = b * length
    qw = GQA_HEADS * GQA_HEAD_DIM
    kw = GQA_KV_HEADS * GQA_HEAD_DIM
    half = GQA_HEAD_DIM // 2
    if rope is None:
        cos = jnp.ones((length, qw), F32)
        sin = jnp.zeros((length, qw), F32)
    else:
        cos = jnp.tile(jnp.concatenate([rope[0], rope[0]], axis=-1), (1, GQA_HEADS))
        sin = jnp.tile(jnp.concatenate([rope[1], rope[1]], axis=-1), (1, GQA_HEADS))
    head = np.arange(qw) // GQA_HEAD_DIM
    gsum = jnp.asarray(head[:, None] == head[None, :], BF16)
    eye = np.eye(GQA_HEAD_DIM, dtype=np.float32)
    rot1 = np.concatenate([-eye[:, half:], eye[:, :half]], axis=-1)
    rot = jnp.asarray(np.kron(np.eye(GQA_HEADS, dtype=np.float32), rot1), BF16)
    tm = min(512, length)
    nt = length // tm
    full = lambda a: pl.BlockSpec(a.shape, lambda bi, i: (0,) * a.ndim)
    qnw = jnp.tile(q_norm_w, GQA_HEADS).reshape(1, qw)
    knw = jnp.tile(k_norm_w, GQA_KV_HEADS).reshape(1, kw)
    return pl.pallas_call(
        _gqa_prep_body,
        grid=(b, nt),
        in_specs=[pl.BlockSpec((tm, qw), lambda bi, i: (bi * nt + i, OFF_GQ // qw)),
                  pl.BlockSpec((tm, kw), lambda bi, i: (bi * nt + i, OFF_GK // kw)),
                  pl.BlockSpec((tm, kw), lambda bi, i: (bi * nt + i, OFF_GV // kw)),
                  full(qnw), full(knw), full(gsum), full(rot),
                  pl.BlockSpec((tm, qw), lambda bi, i: (i, 0)),
                  pl.BlockSpec((tm, qw), lambda bi, i: (i, 0))],
        out_specs=[pl.BlockSpec((tm, qw), lambda bi, i: (bi * nt + i, 0)),
                   pl.BlockSpec((tm, kw), lambda bi, i: (bi * nt + i, 0)),
                   pl.BlockSpec((tm, kw), lambda bi, i: (bi * nt + i, 0))],
        out_shape=[jax.ShapeDtypeStruct((n, qw), BF16),
                   jax.ShapeDtypeStruct((n, kw), BF16),
                   jax.ShapeDtypeStruct((n, kw), BF16)],
        compiler_params=_cp(("parallel", "parallel")),
        name="gqa_prep",
    )(main, main, main, qnw, knw, gsum, rot, cos, sin)


def _flash_body(q_ref, k_ref, v_ref, kc_ref, vc_ref, o_ref, m_ref, l_ref, acc_ref, *, nk, has_ctx, groups):
    ki = pl.program_id(3)

    @pl.when(ki == 0)
    def _():
        m_ref[...] = jnp.full_like(m_ref, -jnp.inf)
        l_ref[...] = jnp.zeros_like(l_ref)
        acc_ref[...] = jnp.zeros_like(acc_ref)

    q = q_ref[0, 0]
    q = q.reshape(q.shape[0] * q.shape[1], q.shape[2])

    def step(k, v):
        s = lax.dot_general(q, k, (((1,), (1,)), ((), ())), preferred_element_type=F32)
        m_prev = m_ref[...]
        m_new = jnp.maximum(m_prev, jnp.max(s, axis=-1, keepdims=True))
        p = jnp.exp(s - m_new)
        alpha = jnp.exp(m_prev - m_new)
        l_ref[...] = alpha * l_ref[...] + jnp.sum(p, axis=-1, keepdims=True)
        acc_ref[...] = alpha * acc_ref[...] + jnp.dot(p.astype(BF16), v, preferred_element_type=F32)
        m_ref[...] = m_new

    @pl.when(ki < nk)
    def _():
        step(k_ref[0, 0], v_ref[0, 0])

    if has_ctx:
        @pl.when(ki == nk)
        def _():
            step(kc_ref[0, 0], vc_ref[0, 0])

    @pl.when(ki == nk - 1 + int(has_ctx))
    def _():
        out = acc_ref[...] / l_ref[...]
        o_ref[0, 0] = out.reshape(o_ref.shape[2:]).astype(o_ref.dtype)


def flash_attention(q, k, v, kc, vc, tq, tk):
    b, hkv, g, sq, dk = q.shape
    sk = k.shape[2]
    dv = v.shape[3]
    tq = min(tq, sq)
    tk = min(tk, sk)
    nk = sk // tk
    has_ctx = kc is not None
    if not has_ctx:
        kc, vc = k[:, :, :16], v[:, :, :16]
    skc = kc.shape[2]
    rows = g * tq
    body = functools.partial(_flash_body, nk=nk, has_ctx=has_ctx, groups=g)
    return pl.pallas_call(
        body,
        grid=(b, hkv, sq // tq, nk + int(has_ctx)),
        in_specs=[pl.BlockSpec((1, 1, g, tq, dk), lambda bi, h, qi, ki: (bi, h, 0, qi, 0)),
                  pl.BlockSpec((1, 1, tk, dk), lambda bi, h, qi, ki: (bi, h, jnp.minimum(ki, nk - 1), 0)),
                  pl.BlockSpec((1, 1, tk, dv), lambda bi, h, qi, ki: (bi, h, jnp.minimum(ki, nk - 1), 0)),
                  pl.BlockSpec((1, 1, skc, dk), lambda bi, h, qi, ki: (bi, h, 0, 0)),
                  pl.BlockSpec((1, 1, skc, dv), lambda bi, h, qi, ki: (bi, h, 0, 0))],
        out_specs=pl.BlockSpec((1, 1, g, tq, dv), lambda bi, h, qi, ki: (bi, h, 0, qi, 0)),
        out_shape=jax.ShapeDtypeStruct((b, hkv, g, sq, dv), BF16),
        scratch_shapes=[pltpu.VMEM((rows, 1), F32), pltpu.VMEM((rows, 1), F32), pltpu.VMEM((rows, dv), F32)],
        compiler_params=_cp(("parallel", "parallel", "parallel", "arbitrary")),
        name="flash_attention",
    )(q, k, v, kc, vc)


def _router_body(x_ref, nw_ref, sh_ref, sc_ref, wrh_ref, wrl_ref, rb_ref, tri_ref,
                 h_ref, ri_ref, rw_ref, cnt_ref, carry_ref):
    i = pl.program_id(0)

    @pl.when(i == 0)
    def _():
        carry_ref[...] = jnp.zeros_like(carry_ref)

    xf = x_ref[...]
    y = xf * lax.rsqrt(jnp.mean(xf * xf, axis=-1, keepdims=True) + NORM_EPS) * nw_ref[...]
    h = y * (1.0 + sc_ref[0]) + sh_ref[0]
    h_ref[...] = h
    hi = h.astype(BF16)
    lo = (h - hi.astype(F32)).astype(BF16)
    nt = (((1,), (1,)), ((), ()))
    logits = (lax.dot_general(wrh_ref[...], hi, nt, preferred_element_type=F32)
              + lax.dot_general(wrh_ref[...], lo, nt, preferred_element_type=F32)
              + lax.dot_general(wrl_ref[...], hi, nt, preferred_element_type=F32))
    scores = _sigmoid(logits)
    sel = scores + rb_ref[...]
    s = [scores[e:e + 1] for e in range(N_EXPERTS)]
    v = [sel[e:e + 1] for e in range(N_EXPERTS)]
    epg = EXPERTS_PER_GROUP
    gscore = []
    for gi in range(N_GROUPS):
        mem = v[gi * epg:(gi + 1) * epg]
        best = None
        for a in range(epg):
            for c in range(a + 1, epg):
                pair = mem[a] + mem[c]
                best = pair if best is None else jnp.maximum(best, pair)
        gscore.append(best)
    is_best = []
    for gi in range(N_GROUPS):
        ok = None
        for gj in range(N_GROUPS):
            if gj == gi:
                continue
            c = (gscore[gi] > gscore[gj]) if gj < gi else (gscore[gi] >= gscore[gj])
            ok = c if ok is None else (ok & c)
        is_best.append(ok)
    chosen = []
    for e in range(N_EXPERTS):
        gi = e // epg
        rank = jnp.zeros_like(v[e])
        for e2 in range(gi * epg, (gi + 1) * epg):
            if e2 == e:
                continue
            ahead = (v[e2] >= v[e]) if e2 < e else (v[e2] > v[e])
            rank = rank + jnp.where(ahead, 1.0, 0.0)
        chosen.append(is_best[gi] & (rank < TOP_K))
    chosen_f = jnp.concatenate([jnp.where(cm, 1.0, 0.0) for cm in chosen], axis=0)
    total = None
    for e in range(N_EXPERTS):
        t = jnp.where(chosen[e], s[e], 0.0)
        total = t if total is None else total + t
    pos = jnp.dot(chosen_f.astype(BF16), tri_ref[...], preferred_element_type=F32) + carry_ref[...]
    carry_ref[...] += jnp.sum(chosen_f, axis=-1, keepdims=True)
    cnt_ref[...] = jnp.broadcast_to(carry_ref[...], cnt_ref.shape)
    e_lo = jnp.full_like(v[0], float(N_EXPERTS))
    e_hi = jnp.full_like(v[0], -1.0)
    for e in range(N_EXPERTS):
        e_lo = jnp.where(chosen[e], jnp.minimum(e_lo, float(e)), e_lo)
        e_hi = jnp.where(chosen[e], jnp.maximum(e_hi, float(e)), e_hi)
    zero = jnp.zeros_like(v[0])
    w_lo, w_hi, p_lo, p_hi = zero, zero, zero, zero
    for e in range(N_EXPERTS):
        pe = pos[e:e + 1]
        w_lo = jnp.where(e_lo == float(e), s[e], w_lo)
        w_hi = jnp.where(e_hi == float(e), s[e], w_hi)
        p_lo = jnp.where(e_lo == float(e), pe, p_lo)
        p_hi = jnp.where(e_hi == float(e), pe, p_hi)
    ri_ref[...] = jnp.concatenate([e_lo, e_hi, p_lo, p_hi, zero, zero, zero, zero], axis=0).astype(jnp.int32)
    rw_ref[...] = jnp.concatenate([w_lo / total, w_hi / total, zero, zero, zero, zero, zero, zero], axis=0)


def norm_route(x, nw, shift, scale, per_batch, w_router, router_bias):
    n, d = x.shape
    tm = _row_tile(512, n, per_batch)
    rf = _mod_row_fn(n, tm, per_batch)
    wr_t = w_router.T
    wr_hi = wr_t.astype(BF16)
    wr_lo = (wr_t - wr_hi.astype(F32)).astype(BF16)
    tri = jnp.asarray(np.triu(np.ones((tm, tm), np.float32), 1), BF16)
    const = lambda a: pl.BlockSpec(a.shape, lambda i: (0,) * a.ndim)
    rb = router_bias.reshape(N_EXPERTS, 1).astype(F32)
    return pl.pallas_call(
        _router_body,
        grid=(n // tm,),
        in_specs=[pl.BlockSpec((tm, d), lambda i: (i, 0)),
                  pl.BlockSpec((1, d), lambda i: (0, 0)),
                  pl.BlockSpec((1, 1, d), lambda i: (rf(i), 0, 0)),
                  pl.BlockSpec((1, 1, d), lambda i: (rf(i), 0, 0)),
                  const(wr_hi), const(wr_lo), const(rb), const(tri)],
        out_specs=[pl.BlockSpec((tm, d), lambda i: (i, 0)),
                   pl.BlockSpec((8, tm), lambda i: (0, i)),
                   pl.BlockSpec((8, tm), lambda i: (0, i)),
                   pl.BlockSpec((N_EXPERTS, LANES), lambda i: (0, 0))],
        out_shape=[jax.ShapeDtypeStruct((n, d), F32),
                   jax.ShapeDtypeStruct((8, n), jnp.int32),
                   jax.ShapeDtypeStruct((8, n), F32),
                   jax.ShapeDtypeStruct((N_EXPERTS, LANES), F32)],
        scratch_shapes=[pltpu.VMEM((N_EXPERTS, 1), F32)],
        compiler_params=_cp(("arbitrary",)),
        name="norm_route",
    )(x, nw.reshape(1, d), shift.reshape(MOD_ROWS, 1, d), scale.reshape(MOD_ROWS, 1, d),
      wr_hi, wr_lo, rb, tri)


def _dispatch_body(sa_ref, sb_ref, pad_ref, h_ref, xs_ref, zero_ref, sem, *, tm, n_pad):
    i = pl.program_id(0)
    base = i * tm

    def row_copy(src, r, slot):
        return pltpu.make_async_copy(src.at[pl.ds(r, 1)], xs_ref.at[pl.ds(slot, 1)], sem)

    @pl.when(i == 0)
    def _():
        zero_ref[...] = jnp.zeros_like(zero_ref)

        def fill(j, carry):
            row_copy(zero_ref, 0, pad_ref[j]).start()
            return carry
        lax.fori_loop(0, n_pad, fill, 0)

        def drain(j, carry):
            row_copy(zero_ref, 0, 0).wait()
            return carry
        lax.fori_loop(0, n_pad, drain, 0)

    def issue(r, carry):
        row_copy(h_ref, r, sa_ref[base + r]).start()
        row_copy(h_ref, r, sb_ref[base + r]).start()
        return carry
    lax.fori_loop(0, tm, issue, 0)

    def drain2(r, carry):
        row_copy(h_ref, 0, 0).wait()
        row_copy(h_ref, 0, 0).wait()
        return carry
    lax.fori_loop(0, tm, drain2, 0)


def moe_dispatch(h, slot_a, slot_b, pad_slots, n_slots):
    n, d = h.shape
    tm = min(256, n)
    n_pad = pad_slots.shape[0]
    body = functools.partial(_dispatch_body, tm=tm, n_pad=n_pad)
    return pl.pallas_call(
        body,
        grid_spec=pltpu.PrefetchScalarGridSpec(
            num_scalar_prefetch=3,
            grid=(n // tm,),
            in_specs=[pl.BlockSpec((tm, d), lambda i, sa, sb, pd: (i, 0))],
            out_specs=pl.BlockSpec(memory_space=pl.ANY),
            scratch_shapes=[pltpu.VMEM((8, d), F32), pltpu.SemaphoreType.DMA(())]),
        out_shape=jax.ShapeDtypeStruct((n_slots, d), F32),
        compiler_params=_cp(("arbitrary",)),
        name="moe_dispatch",
    )(slot_a, slot_b, pad_slots, h)


def _experts_body(te_ref, nu_ref, xs_ref, wg_ref, wu_ref, wd_ref, y_ref):
    i = pl.program_id(0)

    @pl.when(i < nu_ref[0])
    def _():
        xb = xs_ref[...].astype(BF16)
        hg = jnp.dot(xb, wg_ref[0], preferred_element_type=F32)
        hu = jnp.dot(xb, wu_ref[0], preferred_element_type=F32)
        act = (hg * _sigmoid(hg) * hu).astype(BF16)
        y_ref[...] = jnp.dot(act, wd_ref[0], preferred_element_type=F32)

    @pl.when(i >= nu_ref[0])
    def _():
        y_ref[...] = jnp.zeros_like(y_ref)


def moe_experts(xs, tile_expert, n_used, w_gate, w_up, w_down):
    s, d = xs.shape
    f = w_gate.shape[2]
    tm = MOE_TILE
    return pl.pallas_call(
        _experts_body,
        grid_spec=pltpu.PrefetchScalarGridSpec(
            num_scalar_prefetch=2,
            grid=(s // tm,),
            in_specs=[pl.BlockSpec((tm, d), lambda i, te, nu: (jnp.minimum(i, nu[0] - 1), 0)),
                      pl.BlockSpec((1, d, f), lambda i, te, nu: (te[i], 0, 0)),
                      pl.BlockSpec((1, d, f), lambda i, te, nu: (te[i], 0, 0)),
                      pl.BlockSpec((1, f, d), lambda i, te, nu: (te[i], 0, 0))],
            out_specs=pl.BlockSpec((tm, d), lambda i, te, nu: (i, 0))),
        out_shape=jax.ShapeDtypeStruct((s, d), F32),
        compiler_params=_cp(("arbitrary",)),
        name="moe_experts",
    )(tile_expert, n_used, xs, w_gate, w_up, w_down)


def _combine_body(sa_ref, sb_ref, x_ref, w_ref, g_ref, y_ref, o_ref, ba_ref, bb_ref, sem, *, tm):
    i = pl.program_id(0)
    base = i * tm

    def row_copy(slot, dst, r):
        return pltpu.make_async_copy(y_ref.at[pl.ds(slot, 1)], dst.at[pl.ds(r, 1)], sem)

    def issue(r, carry):
        row_copy(sa_ref[base + r], ba_ref, r).start()
        row_copy(sb_ref[base + r], bb_ref, r).start()
        return carry
    lax.fori_loop(0, tm, issue, 0)

    def drain(r, carry):
        row_copy(0, ba_ref, 0).wait()
        row_copy(0, bb_ref, 0).wait()
        return carry
    lax.fori_loop(0, tm, drain, 0)

    w = w_ref[...]
    mix = w[:, 0:1] * ba_ref[...] + w[:, 1:2] * bb_ref[...]
    o_ref[...] = x_ref[...] + g_ref[0] * mix


def moe_combine(x, y, slot_a, slot_b, wts, gate, per_batch):
    n, d = x.shape
    tm = _row_tile(256, n, per_batch)
    rf = _mod_row_fn(n, tm, per_batch)
    body = functools.partial(_combine_body, tm=tm)
    return pl.pallas_call(
        body,
        grid_spec=pltpu.PrefetchScalarGridSpec(
            num_scalar_prefetch=2,
            grid=(n // tm,),
            in_specs=[pl.BlockSpec((tm, d), lambda i, sa, sb: (i, 0)),
                      pl.BlockSpec((tm, 8), lambda i, sa, sb: (i, 0)),
                      pl.BlockSpec((1, 1, d), lambda i, sa, sb: (rf(i), 0, 0)),
                      pl.BlockSpec(memory_space=pl.ANY)],
            out_specs=pl.BlockSpec((tm, d), lambda i, sa, sb: (i, 0)),
            scratch_shapes=[pltpu.VMEM((tm, d), F32), pltpu.VMEM((tm, d), F32), pltpu.SemaphoreType.DMA(())]),
        out_shape=jax.ShapeDtypeStruct((n, d), F32),
        compiler_params=_cp(("arbitrary",)),
        name="moe_combine",
    )(slot_a, slot_b, x, wts, gate.reshape(MOD_ROWS, 1, d), y)


def moe_layer(x, nw, mod, per_batch, w_router, router_bias, w_gate, w_up, w_down):
    n, d = x.shape
    h, route_i, route_w, counts = norm_route(x, nw, mod[:, 3], mod[:, 4], per_batch, w_router, router_bias)
    cnt = counts[:, 0].astype(jnp.int32)
    seg = ((cnt + MOE_TILE - 1) // MOE_TILE) * MOE_TILE
    off = jnp.concatenate([jnp.zeros((1,), jnp.int32), jnp.cumsum(seg)])
    n_slots = TOP_K * n + N_EXPERTS * MOE_TILE
    slot_a = off[route_i[0]] + route_i[2]
    slot_b = off[route_i[1]] + route_i[3]
    n_pad = n_slots - TOP_K * n
    padcnt = seg - cnt
    padstart = jnp.concatenate([jnp.zeros((1,), jnp.int32), jnp.cumsum(padcnt)])
    j = jnp.arange(n_pad, dtype=jnp.int32)
    e_of = jnp.clip(jnp.searchsorted(padstart, j, side='right') - 1, 0, N_EXPERTS)
    in_seg = off[jnp.minimum(e_of, N_EXPERTS - 1)] + cnt[jnp.minimum(e_of, N_EXPERTS - 1)] + (j - padstart[e_of])
    tail = off[N_EXPERTS] + (j - padstart[N_EXPERTS])
    pad_slots = jnp.where(e_of < N_EXPERTS, in_seg, tail).astype(jnp.int32)
    n_tiles = n_slots // MOE_TILE
    tile_start = jnp.arange(n_tiles, dtype=jnp.int32) * MOE_TILE
    n_used = (off[N_EXPERTS] // MOE_TILE).astype(jnp.int32).reshape(1)
    tile_expert = jnp.clip(jnp.searchsorted(off, tile_start, side='right') - 1, 0, N_EXPERTS - 1).astype(jnp.int32)
    last_used = tile_expert[jnp.maximum(n_used[0] - 1, 0)]
    tile_expert = jnp.where(jnp.arange(n_tiles) < n_used[0], tile_expert, last_used)

    xs = moe_dispatch(h, slot_a, slot_b, pad_slots, n_slots)
    y = moe_experts(xs, tile_expert, n_used, w_gate, w_up, w_down)
    wts = jnp.transpose(route_w)
    return moe_combine(x, y, slot_a, slot_b, wts, mod[:, 5], per_batch)


def _final_norm_body(x_ref, w_ref, o_ref):
    xf = x_ref[...]
    y = xf * lax.rsqrt(jnp.mean(xf * xf, axis=-1, keepdims=True) + NORM_EPS)
    o_ref[...] = y * w_ref[...]


def final_rms_norm(x, w):
    n, d = x.shape
    rows = 512
    return pl.pallas_call(
        _final_norm_body,
        grid=(n // rows,),
        in_specs=[pl.BlockSpec((rows, d), lambda i: (i, 0)), pl.BlockSpec((1, d), lambda i: (0, 0))],
        out_specs=pl.BlockSpec((rows, d), lambda i: (i, 0)),
        out_shape=jax.ShapeDtypeStruct((n, d), x.dtype),
        compiler_params=_cp(("parallel",)),
        name="final_norm",
    )(x, w.reshape(1, d))


def _reorder_w_in(w):
    o = np.cumsum((0,) + IN_WIDTHS)
    seg = lambda i: w[:, o[i]:o[i + 1]]
    main = jnp.concatenate([seg(0), seg(1), seg(2), seg(3), seg(6), seg(9), seg(12), seg(7), seg(10), seg(11)], axis=1)
    misc = jnp.concatenate([seg(8), seg(4), seg(5)], axis=1)
    misc = jnp.pad(misc, ((0, 0), (0, MISC_W - misc.shape[1])))
    return main.astype(BF16), misc.astype(BF16)


def _attention_branches(main_l, misc_l, main_c, misc_c, b, seq, ctx_len, ctx_out, rope_mla, rope_gqa,
                        mla_w, mla_qn, mla_kvn, gqa_qn, gqa_kn):
    g = GQA_HEADS // GQA_KV_HEADS
    hd = GQA_HEAD_DIM

    mq_l, mk_l, mv_l = mla_prepare(main_l, misc_l, b, seq, mla_w, mla_qn, mla_kvn, _mla_tables(rope_mla, seq))
    mq_c, mk_c, mv_c = mla_prepare(main_c, misc_c, b, ctx_len, mla_w, mla_qn, mla_kvn, _mla_tables(None, ctx_len))
    mla_l = flash_attention(mq_l[:, :, None], mk_l, mv_l, mk_c, mv_c, 512, 512)
    mla_l = jnp.transpose(mla_l[:, :, 0], (0, 2, 1, 3)).reshape(b * seq, BRANCH_W)

    def split_heads(t, length, heads):
        return jnp.transpose(t.reshape(b, length, heads, hd), (0, 2, 1, 3))

    gq_l, gk_l, gv_l = gqa_prepare(main_l, b, seq, gqa_qn, gqa_kn, rope_gqa)
    gq_c, gk_c, gv_c = gqa_prepare(main_c, b, ctx_len, gqa_qn, gqa_kn, None)
    gq_l5 = split_heads(gq_l, seq, GQA_HEADS).reshape(b, GQA_KV_HEADS, g, seq, hd)
    gk_l4, gv_l4 = split_heads(gk_l, seq, GQA_KV_HEADS), split_heads(gv_l, seq, GQA_KV_HEADS)
    gk_c4, gv_c4 = split_heads(gk_c, ctx_len, GQA_KV_HEADS), split_heads(gv_c, ctx_len, GQA_KV_HEADS)
    gqa_l = flash_attention(gq_l5, gk_l4, gv_l4, gk_c4, gv_c4, 256, 512)
    gqa_l = jnp.transpose(gqa_l.reshape(b, GQA_HEADS, seq, hd), (0, 2, 1, 3)).reshape(b * seq, BRANCH_W)

    mla_c = gqa_c = None
    if ctx_out:
        mla_c = flash_attention(mq_c[:, :, None], mk_c, mv_c, None, None, 256, 256)
        mla_c = jnp.transpose(mla_c[:, :, 0], (0, 2, 1, 3)).reshape(b * ctx_len, BRANCH_W)
        gq_c5 = split_heads(gq_c, ctx_len, GQA_HEADS).reshape(b, GQA_KV_HEADS, g, ctx_len, hd)
        gqa_c = flash_attention(gq_c5, gk_c4, gv_c4, None, None, 256, 256)
        gqa_c = jnp.transpose(gqa_c.reshape(b, GQA_HEADS, ctx_len, hd), (0, 2, 1, 3)).reshape(b * ctx_len, BRANCH_W)
    return mla_l, gqa_l, mla_c, gqa_c


def kernel(x, c, ctx, c_ctx, w_ada, b_ada, norm1_w, norm2_w, w_in,
           gdn_conv_w, gdn_a_log, gdn_dt_bias, gdn_norm_w,
           mla_q_norm_w, mla_kv_norm_w, mla_w_uq, mla_w_ukv,
           gqa_q_norm_w, gqa_k_norm_w,
           hy_conv_w, hy_w1, hy_b1, hy_w2, hy_b2, hy_w3, hy_sin_freq, hy_bias,
           w_branch, w_out, w_router, router_bias,
           moe_w_gate, moe_w_up, moe_w_down, final_norm_w):
    b, seq, d = x.shape
    ctx_len = ctx.shape[1]
    rows = seq // GRID_W
    rope_mla = axial_rope_tables(rows, MLA_ROPE)
    rope_gqa = axial_rope_tables(rows, GQA_HEAD_DIM)
    mod_all = ada_modulation(c, c_ctx, w_ada, b_ada)
    xl = x.reshape(b * seq, d)
    xc = ctx.reshape(b * ctx_len, d)
    f32 = lambda t: t.astype(F32)
    for layer in range(DEPTH):
        ctx_out = layer < DEPTH - 1
        mod = mod_all[layer]
        w_main, w_misc = _reorder_w_in(w_in[layer][:, :MIX_IN])
        w_gates = w_in[layer][:, MIX_IN:].astype(BF16)
        w_br = w_branch[layer].astype(BF16)
        w_o = w_out[layer].astype(BF16)
        wg, wu, wd = (t[layer].astype(BF16) for t in (moe_w_gate, moe_w_up, moe_w_down))

        hl = norm_modulate(xl, norm1_w[layer], mod[:, 0], mod[:, 1], seq)
        hc = norm_modulate(xc, norm1_w[layer], mod[:, 0], mod[:, 1], None)
        main_l, misc_l = matmul(hl, w_main, BF16, 512), matmul(hl, w_misc, F32, MISC_W)
        main_c, misc_c = matmul(hc, w_main, BF16, 512), matmul(hc, w_misc, F32, MISC_W)

        gdn_l, gdn_c = gdn_branch(main_l, misc_l, main_c, misc_c, b, seq, ctx_len, gdn_conv_w[layer],
                                  gdn_a_log[layer], gdn_dt_bias[layer], gdn_norm_w[layer], ctx_out)

        mla_w = _mla_weights(mla_q_norm_w[layer], mla_kv_norm_w[layer], mla_w_uq[layer], mla_w_ukv[layer])
        mla_l, gqa_l, mla_c, gqa_c = _attention_branches(
            main_l, misc_l, main_c, misc_c, b, seq, ctx_len, ctx_out, rope_mla, rope_gqa,
            mla_w, mla_q_norm_w[layer], mla_kv_norm_w[layer], gqa_q_norm_w[layer], gqa_k_norm_w[layer])

        hy_params = (hy_w1[layer], hy_b1[layer], hy_w2[layer], hy_b2[layer], hy_w3[layer], hy_sin_freq[layer])
        hy_l = hyena_branch(main_l, b, seq, hy_conv_w[layer], hyena_filters(seq, *hy_params), hy_bias[layer])

        branches_l = [gdn_l.reshape(b * seq, BRANCH_W).astype(BF16), mla_l, gqa_l,
                      hy_l.reshape(b * seq, BRANCH_W).astype(BF16)]
        merged_l = merge_branches_gated(hl, branches_l, w_gates, w_br)

        if ctx_out:
            hy_c = hyena_branch(main_c, b, ctx_len, hy_conv_w[layer], hyena_filters(ctx_len, *hy_params),
                                hy_bias[layer])
            branches_c = [gdn_c.reshape(b * ctx_len, BRANCH_W).astype(BF16), mla_c, gqa_c,
                          hy_c.reshape(b * ctx_len, BRANCH_W).astype(BF16)]
            merged_c = merge_branches_gated(hc, branches_c, w_gates, w_br)
            xc = matmul_gated_residual(merged_c, w_o, xc, mod[:, 2], None)
            xc = moe_layer(xc, norm2_w[layer], mod, None, w_router, router_bias, wg, wu, wd)

        xl = matmul_gated_residual(merged_l, w_o, xl, mod[:, 2], seq)
        xl = moe_layer(xl, norm2_w[layer], mod, seq, w_router, router_bias, wg, wu, wd)
    return final_rms_norm(xl, final_norm_w).reshape(b, seq, d)
```

```python
import math, functools
import jax, jax.numpy as jnp
from jax import lax
import numpy as np
from jax.experimental import pallas as pl
from jax.experimental.pallas import tpu as pltpu

D_MODEL = 2048
BATCH = 4
SEQ = 4096
DEPTH = 2

GRID_W = 64
CTX_LEN = 256
N_BRANCH = 4
BRANCH_W = 512
NORM_EPS = 1e-6
Q_BLOCK = 128
ROPE_THETA = 10000.0
SHORT_CONV = 3

GDN_HEADS = 4
GDN_HEAD_DIM = 128
GDN_CHUNK = 64

MLA_HEADS = 4
MLA_Q_LORA = 512
MLA_KV_LORA = 256
MLA_NOPE = 128
MLA_ROPE = 64
MLA_V = 128

GQA_HEADS = 8
GQA_KV_HEADS = 2
GQA_HEAD_DIM = 64

HY_WIDTH = 512
HY_ORDER = 2
HY_EMB = 33
HY_HIDDEN = 64
HY_DECAY_TARGET = 1e-2
HY_FAST_DECAY = 0.3
HY_SLOW_DECAY = 1.5

N_EXPERTS = 16
N_GROUPS = 4
EXPERTS_PER_GROUP = N_EXPERTS // N_GROUPS
TOP_K = 2
D_EXPERT = 512

GDN_W = GDN_HEADS * GDN_HEAD_DIM
IN_WIDTHS = (GDN_W, GDN_W, GDN_W, GDN_W, 2 * GDN_HEADS, 2 * GDN_HEADS,
             MLA_Q_LORA, MLA_KV_LORA, MLA_ROPE,
             GQA_HEADS * GQA_HEAD_DIM, GQA_KV_HEADS * GQA_HEAD_DIM, GQA_KV_HEADS * GQA_HEAD_DIM,
             (HY_ORDER + 1) * HY_WIDTH)
MIX_IN = sum(IN_WIDTHS)
IN_DIM = MIX_IN + N_BRANCH * D_MODEL

F32 = jnp.float32
BF16 = jnp.bfloat16
LANES = 128
MOD_ROWS = 8
CTX_MOD_ROW = BATCH
MOE_TILE = 512
VMEM_LIMIT = 56 << 20

MAIN_W = 5120
OFF_GDN, OFF_CQ, OFF_GQ, OFF_HY, OFF_CKV, OFF_GK, OFF_GV = 0, 2048, 2560, 3072, 4608, 4864, 4992
MISC_W = LANES


def _cp(sem):
    return pltpu.CompilerParams(dimension_semantics=sem, vmem_limit_bytes=VMEM_LIMIT)


def _sigmoid(v):
    return 0.5 * jnp.tanh(0.5 * v) + 0.5


def rms_norm(x, w):
    xf = x.astype(jnp.float32)
    y = xf * lax.rsqrt(jnp.mean(xf * xf, axis=-1, keepdims=True) + NORM_EPS)
    return (y * w.astype(jnp.float32)).astype(x.dtype)


def l2_normalize(x):
    xf = x.astype(jnp.float32)
    return xf * lax.rsqrt(jnp.sum(xf * xf, axis=-1, keepdims=True) + NORM_EPS)


def depthwise_conv_centred(u, w):
    k = w.shape[0]
    return lax.conv_general_dilated(u, w[:, None, :].astype(u.dtype), window_strides=(1,),
                                    padding=[(k // 2, k // 2)],
                                    dimension_numbers=('NWC', 'WIO', 'NWC'),
                                    feature_group_count=u.shape[-1])


def axial_rope_tables(rows, rot_dim):
    n_freq = rot_dim // 4
    freqs = ROPE_THETA ** (-jnp.arange(n_freq, dtype=jnp.float32) / n_freq)
    row = jnp.repeat(jnp.arange(rows, dtype=jnp.float32), GRID_W)
    col = jnp.tile(jnp.arange(GRID_W, dtype=jnp.float32), rows)
    ang = jnp.concatenate([row[:, None] * freqs, col[:, None] * freqs], axis=-1)
    return jnp.cos(ang), jnp.sin(ang)


def gdn_prep(q, k, v, a, bt, conv_w, a_log, dt_bias):
    b, l = q.shape[:2]
    qkv = jax.nn.silu(depthwise_conv_centred(jnp.concatenate([q, k, v], axis=-1), conv_w)).astype(jnp.float32)
    q, k, v = jnp.split(qkv, 3, axis=-1)
    hd = (b, l, GDN_HEADS, GDN_HEAD_DIM)
    q = l2_normalize(q.reshape(hd)) * GDN_HEAD_DIM ** -0.5
    k = l2_normalize(k.reshape(hd))
    v = v.reshape(hd)
    a = a.astype(jnp.float32).reshape(b, l, 2, GDN_HEADS)
    g = -jnp.exp(a_log.astype(jnp.float32)) * jax.nn.softplus(a + dt_bias.astype(jnp.float32))
    beta = jax.nn.sigmoid(bt.astype(jnp.float32).reshape(b, l, 2, GDN_HEADS))
    return q, k, v, g, beta


def gated_delta_rule(q, k, v, g, beta, state, with_out):
    b, l, h, _ = q.shape
    dv = v.shape[-1]
    c = GDN_CHUNK
    n = l // c

    def to_chunks(t):
        t = t.reshape(b, n, c, h, *t.shape[3:])
        return jnp.moveaxis(t, (1, 3), (0, 2))

    qc, kc, vc, bc = to_chunks(q), to_chunks(k), to_chunks(v), to_chunks(beta)
    gc = jnp.cumsum(to_chunks(g), axis=-1)
    idx = jnp.arange(c)
    lower = idx[:, None] >= idx[None, :]
    strict = idx[:, None] > idx[None, :]
    diff = gc[..., :, None] - gc[..., None, :]
    decay = jnp.where(lower, jnp.exp(jnp.where(lower, diff, 0.0)), 0.0)
    kb = kc * bc[..., None]
    a = jnp.where(strict, jnp.einsum('nbhid,nbhjd->nbhij', kb, kc) * decay, 0.0)
    solve = functools.partial(lax.linalg.triangular_solve, left_side=True, lower=True, unit_diagonal=True)
    u = solve(a, vc * bc[..., None])
    w = solve(a, kb * jnp.exp(gc)[..., None])
    g_last = gc[..., -1]
    k_dec = kc * jnp.exp(g_last[..., None] - gc)[..., None]
    xs = (u, w, k_dec, g_last)
    if with_out:
        qk = jnp.where(lower, jnp.einsum('nbhid,nbhjd->nbhij', qc, kc) * decay, 0.0)
        xs = xs + (qc * jnp.exp(gc)[..., None], qk)

    def step(s, inp):
        u_i, w_i, kd_i, gl_i = inp[:4]
        v_new = u_i - jnp.einsum('bhck,bhkv->bhcv', w_i, s)
        s_new = s * jnp.exp(gl_i)[..., None, None] + jnp.einsum('bhck,bhcv->bhkv', kd_i, v_new)
        if not with_out:
            return s_new, None
        qd_i, qk_i = inp[4:]
        o = jnp.einsum('bhck,bhkv->bhcv', qd_i, s) + jnp.einsum('bhij,bhjv->bhiv', qk_i, v_new)
        return s_new, o

    state, o = lax.scan(step, state, xs)
    if not with_out:
        return None, state
    o = jnp.moveaxis(o, (0, 2), (1, 3)).reshape(b, l, h, dv)
    return o, state


def gdn_output(o, z, norm_w):
    b, l = z.shape[:2]
    zh = z.reshape(b, l, GDN_HEADS, GDN_HEAD_DIM).astype(jnp.float32)
    y = rms_norm(o, norm_w) * jax.nn.silu(zh)
    return y.reshape(b, l, GDN_W).astype(z.dtype)


def gdn_mixer(p_lat, p_ctx, conv_w, a_log, dt_bias, norm_w, ctx_out):
    lat = gdn_prep(p_lat[0], p_lat[1], p_lat[2], p_lat[4], p_lat[5], conv_w, a_log, dt_bias)
    ctx = gdn_prep(p_ctx[0], p_ctx[1], p_ctx[2], p_ctx[4], p_ctx[5], conv_w, a_log, dt_bias)
    b = p_lat[0].shape[0]
    s0 = jnp.zeros((b, GDN_HEADS, GDN_HEAD_DIM, GDN_HEAD_DIM), jnp.float32)
    o_lat, o_ctx = 0.0, 0.0
    for direction in range(2):
        flip = (lambda t: t[:, ::-1]) if direction else (lambda t: t)

        def seq_args(s):
            q, k, v, g, beta = s
            return flip(q), flip(k), flip(v), flip(g[:, :, direction]), flip(beta[:, :, direction])

        oc, s_ctx = gated_delta_rule(*seq_args(ctx), s0, ctx_out)
        ol, _ = gated_delta_rule(*seq_args(lat), s_ctx, True)
        o_lat = o_lat + flip(ol)
        if ctx_out:
            o_ctx = o_ctx + flip(oc)
    out_lat = gdn_output(o_lat, p_lat[3], norm_w)
    out_ctx = gdn_output(o_ctx, p_ctx[3], norm_w) if ctx_out else None
    return out_lat, out_ctx


def hyena_filters(length, w1, b1, w2, b2, w3, sin_freq):
    t = jnp.arange(length, dtype=jnp.float32)
    bands = (HY_EMB - 1) // 2
    f = jnp.linspace(1e-4, bands - 1, bands, dtype=jnp.float32)
    phase = (2.0 * math.pi / length) * t[:, None] * f[None, :]
    feats = jnp.concatenate([t[:, None] / (length - 1), jnp.cos(phase), -jnp.sin(phase)], axis=-1)
    hid = jnp.sin(sin_freq[0] * (feats @ w1 + b1))
    hid = jnp.sin(sin_freq[1] * (hid @ w2 + b2))
    filt = (hid @ w3).astype(jnp.float32)
    centre = length // 2
    dist = jnp.abs(t - centre) / centre
    deltas = jnp.abs(jnp.linspace(math.log(HY_DECAY_TARGET) / HY_SLOW_DECAY,
                                  math.log(HY_DECAY_TARGET) / HY_FAST_DECAY,
                                  HY_ORDER * HY_WIDTH, dtype=jnp.float32))
    filt = filt * jnp.exp(-dist[:, None] * deltas[None, :])
    filt = filt / jnp.sum(jnp.abs(filt), axis=0, keepdims=True)
    return filt.reshape(length, HY_ORDER, HY_WIDTH)


def fft_conv_centred(u, h):
    l = u.shape[1]
    n = 2 * l
    uf = jnp.fft.rfft(u.astype(jnp.float32), n=n, axis=1)
    hf = jnp.fft.rfft(h.astype(jnp.float32), n=n, axis=0)
    y = jnp.fft.irfft(uf * hf[None], n=n, axis=1)
    return y[:, l // 2: l // 2 + l]


def hyena_mixer(u, conv_w, filt, bias):
    parts = jnp.split(depthwise_conv_centred(u, conv_w).astype(jnp.float32), HY_ORDER + 1, axis=-1)
    z = parts[0]
    for o in range(HY_ORDER):
        z = parts[o + 1] * (fft_conv_centred(z, filt[:, o]) + bias[o] * z)
    return z


HALO_ROWS = 16
GDN_BATCHES_PER_STEP = 2


def _conv3(x, prev_row, next_row, w_ref):
    tm = x.shape[0]
    rows = lax.broadcasted_iota(jnp.int32, x.shape, 0)
    up = jnp.where(rows == 0, prev_row, pltpu.roll(x, 1, 0))
    dn = jnp.where(rows == tm - 1, next_row, pltpu.roll(x, tm - 1, 0))
    return w_ref[0:1, :] * up + w_ref[1:2, :] * x + w_ref[2:3, :] * dn


def _halo_rows(xp_ref, xn_ref):
    i = pl.program_id(1)
    prev = jnp.where(i == 0, 0.0, xp_ref[HALO_ROWS - 1:HALO_ROWS, :].astype(F32))
    nxt = jnp.where(i == pl.num_programs(1) - 1, 0.0, xn_ref[0:1, :].astype(F32))
    return prev, nxt


def _halo_specs(tm, width, col_block, nt, n_rows):
    per = tm // HALO_ROWS
    last = n_rows // HALO_ROWS - 1
    return [pl.BlockSpec((tm, width), lambda bi, i: (bi * nt + i, col_block)),
            pl.BlockSpec((HALO_ROWS, width), lambda bi, i: (jnp.maximum((bi * nt + i) * per - 1, 0), col_block)),
            pl.BlockSpec((HALO_ROWS, width), lambda bi, i: (jnp.minimum((bi * nt + i + 1) * per, last), col_block))]


def _split3(v):
    hi = v.astype(BF16)
    r1 = v - hi.astype(F32)
    mid = r1.astype(BF16)
    lo = (r1 - mid.astype(F32)).astype(BF16)
    return hi, mid, lo


def _gdn_prep_body(x_ref, xp_ref, xn_ref, misc_ref, cw_ref, alog_ref, dt_ref, gmask_ref, tp_ref, ts_ref,
                   q_out, k_out, v_out, gcf_out, gcb_out, beta_out):
    prev, nxt = _halo_rows(xp_ref, xn_ref)
    y = _conv3(x_ref[...].astype(F32), prev, nxt, cw_ref)
    y = y * _sigmoid(y)
    hd = GDN_HEAD_DIM
    for h in range(GDN_HEADS):
        qh = y[:, h * hd:(h + 1) * hd]
        kh = y[:, GDN_W + h * hd:GDN_W + (h + 1) * hd]
        qn = qh * lax.rsqrt(jnp.sum(qh * qh, axis=-1, keepdims=True) + NORM_EPS) * hd ** -0.5
        kn = kh * lax.rsqrt(jnp.sum(kh * kh, axis=-1, keepdims=True) + NORM_EPS)
        q_out[:, h * hd:(h + 1) * hd] = qn.astype(BF16)
        k_out[:, h * hd:(h + 1) * hd] = kn.astype(BF16)
    v_out[...] = y[:, 2 * GDN_W:3 * GDN_W].astype(BF16)
    m = misc_ref[...]
    a = m + dt_ref[...]
    softplus = jnp.maximum(a, 0.0) + jnp.log(1.0 + jnp.exp(-jnp.abs(a)))
    g = -(jnp.exp(alog_ref[...]) * gmask_ref[...]) * softplus
    beta_out[...] = _sigmoid(m)
    parts = _split3(g)
    gcf_out[...] = sum(jnp.dot(tp_ref[...], p, preferred_element_type=F32) for p in parts)
    gcb_out[...] = sum(jnp.dot(ts_ref[...], p, preferred_element_type=F32) for p in parts)


def gdn_prepare(main, misc, b, length, conv_w, a_log, dt_bias):
    n = b * length
    w3 = 3 * GDN_W
    tm = min(256, length)
    nt = length // tm
    lane0 = MLA_ROPE
    vec = lambda v: jnp.zeros((1, MISC_W), F32).at[0, lane0:lane0 + 2 * GDN_HEADS].set(v.reshape(-1))
    alog, dtb = vec(a_log), vec(dt_bias)
    gmask = vec(jnp.ones((2 * GDN_HEADS,), F32))
    r = np.arange(tm)
    same = (r[:, None] // GDN_CHUNK) == (r[None, :] // GDN_CHUNK)
    tpre = jnp.asarray(same & (r[None, :] <= r[:, None]), BF16)
    tsuf = jnp.asarray(same & (r[None, :] >= r[:, None]), BF16)
    const = lambda a: pl.BlockSpec(a.shape, lambda bi, i: (0,) * a.ndim)
    row = lambda width: pl.BlockSpec((tm, width), lambda bi, i: (bi * nt + i, 0))
    cw = conv_w.astype(F32)
    return pl.pallas_call(
        _gdn_prep_body,
        grid=(b, nt),
        in_specs=_halo_specs(tm, w3, OFF_GDN // w3, nt, n)
                 + [row(MISC_W), const(cw), const(alog), const(dtb), const(gmask), const(tpre), const(tsuf)],
        out_specs=[row(GDN_W), row(GDN_W), row(GDN_W), row(MISC_W), row(MISC_W), row(MISC_W)],
        out_shape=[jax.ShapeDtypeStruct((n, GDN_W), BF16)] * 3 + [jax.ShapeDtypeStruct((n, MISC_W), F32)] * 3,
        compiler_params=_cp(("parallel", "parallel")),
        name="gdn_prep",
    )(main, main, main, misc, cw, alog, dtb, gmask, tpre, tsuf)


def _gdn_chunk_body(qf_ref, kf_ref, vf_ref, qb_ref, kb_ref, vb_ref, gcf_ref, gcb_ref, bcf_ref, bcb_ref,
                    grf_ref, grb_ref, s0_ref, *rest, nc, with_out, bpb):
    if with_out:
        of_ref, ob_ref, sfin_ref, s_ref = rest
    else:
        sfin_ref, s_ref = rest
        of_ref = ob_ref = None
    c = pl.program_id(1)
    nst = 2 * GDN_HEADS

    @pl.when(c == 0)
    def _():
        s_ref[...] = s0_ref[...].reshape(s_ref.shape)

    ch = GDN_CHUNK
    hd = GDN_HEAD_DIM
    ii = lax.broadcasted_iota(jnp.int32, (ch, ch), 0)
    jj = lax.broadcasted_iota(jnp.int32, (ch, ch), 1)
    nt_dims = (((1,), (1,)), ((), ()))
    tn_dims = (((0,), (0,)), ((), ()))
    bdot = lambda a, b_: jnp.dot(a.astype(BF16), b_.astype(BF16), preferred_element_type=F32)
    eye = jnp.where(ii == jj, 1.0, 0.0)
    pair_masks = [((ii >> (l + 1)) == (jj >> (l + 1))) & ((ii >> l) != (jj >> l))
                  for l in range(int(math.log2(ch)))]
    dirs = ((qf_ref, kf_ref, vf_ref, gcf_ref, bcf_ref, grf_ref, of_ref, ii >= jj, ii > jj, ch - 1),
            (qb_ref, kb_ref, vb_ref, gcb_ref, bcb_ref, grb_ref, ob_ref, ii <= jj, ii < jj, 0))
    chains = []
    for bb in range(bpb):
        for d, (q_ref, k_ref, v_ref, gc_ref, bc_ref, gr_ref, o_ref, incl, strict, last_row) in enumerate(dirs):
            for h in range(GDN_HEADS):
                j = d * GDN_HEADS + h
                cols = slice(h * hd, (h + 1) * hd)
                cn = dict(bb=bb, j=j, cols=cols, o_ref=o_ref, incl=incl, strict=strict)
                cn['q'], cn['k'], cn['v'] = q_ref[bb, :, cols], k_ref[bb, :, cols], v_ref[bb, :, cols]
                cn['gc'] = gc_ref[bb, :, j:j + 1]
                cn['gr'] = gr_ref[bb, 0, j:j + 1, :]
                cn['beta'] = bc_ref[bb, :, j:j + 1]
                cn['g_last'] = gc_ref[bb, last_row:last_row + 1, j:j + 1]
                chains.append(cn)
    for cn in chains:
        incl = cn['incl']
        cn['decay'] = jnp.where(incl, jnp.exp(jnp.where(incl, cn['gc'] - cn['gr'], 0.0)), 0.0)
        cn['kf'] = cn['k'].astype(F32)
        cn['kbeta'] = cn['kf'] * cn['beta']
    for cn in chains:
        kk = lax.dot_general(cn['kbeta'].astype(BF16), cn['k'], nt_dims, preferred_element_type=F32)
        cn['a'] = jnp.where(cn['strict'], kk * cn['decay'], 0.0)
    for cn in chains:
        cn['t'] = eye - jnp.where(pair_masks[0], cn['a'], 0.0)
    for pm in pair_masks[1:]:
        for cn in chains:
            cn['tmp'] = bdot(cn['t'], jnp.where(pm, cn['a'], 0.0))
        for cn in chains:
            cn['t'] = cn['t'] - bdot(cn['tmp'], cn['t'])
    for cn in chains:
        rhs = jnp.concatenate([cn['v'].astype(F32) * cn['beta'], cn['kbeta'] * jnp.exp(cn['gc'])], axis=1)
        cn['x'] = bdot(cn['t'], rhs)
    for cn in chains:
        cn['s'] = s_ref[cn['bb'] * nst + cn['j']]
        cn['v_new'] = cn['x'][:, :hd] - bdot(cn['x'][:, hd:], cn['s'])
    if with_out:
        for cn in chains:
            qk = lax.dot_general(cn['q'], cn['k'], nt_dims, preferred_element_type=F32)
            qk = jnp.where(cn['incl'], qk * cn['decay'], 0.0)
            o = bdot(cn['q'].astype(F32) * jnp.exp(cn['gc']), cn['s']) + bdot(qk, cn['v_new'])
            cn['o_ref'][cn['bb'], :, cn['cols']] = o
    for cn in chains:
        kdec = cn['kf'] * jnp.exp(cn['g_last'] - cn['gc'])
        s_ref[cn['bb'] * nst + cn['j']] = cn['s'] * jnp.exp(cn['g_last']) + lax.dot_general(
            kdec.astype(BF16), cn['v_new'].astype(BF16), tn_dims, preferred_element_type=F32)

    @pl.when(c == nc - 1)
    def _():
        sfin_ref[...] = s_ref[...].reshape(sfin_ref.shape)


def gdn_scan(q, k, v, gcol, bcol, grow, s0, b, length, with_out):
    n = b * length
    ch = GDN_CHUNK
    nc = length // ch
    nst = 2 * GDN_HEADS
    bpb = min(GDN_BATCHES_PER_STEP, b)
    fwd = lambda bg, c: (bg, c, 0)
    bwd = lambda bg, c: (bg, nc - 1 - c, 0)
    fwd4 = lambda bg, c: (bg, c, 0, 0)
    bwd4 = lambda bg, c: (bg, nc - 1 - c, 0, 0)
    wide = lambda f: pl.BlockSpec((bpb, ch, GDN_W), f)
    narrow = lambda f: pl.BlockSpec((bpb, ch, nst), f)
    rows = lambda f: pl.BlockSpec((bpb, 1, nst, ch), f)
    state = pl.BlockSpec((bpb, nst, GDN_HEAD_DIM, GDN_HEAD_DIM), lambda bg, c: (bg, 0, 0, 0))
    out_specs = [state]
    out_shape = [jax.ShapeDtypeStruct((b, nst, GDN_HEAD_DIM, GDN_HEAD_DIM), F32)]
    if with_out:
        out_specs = [wide(fwd), wide(bwd)] + out_specs
        out_shape = [jax.ShapeDtypeStruct((b, length, GDN_W), F32)] * 2 + out_shape
    body = functools.partial(_gdn_chunk_body, nc=nc, with_out=with_out, bpb=bpb)
    q3, k3, v3 = (t.reshape(b, length, GDN_W) for t in (q, k, v))
    gcol3, bcol3 = gcol.reshape(b, length, nst), bcol.reshape(b, length, nst)
    grow4 = grow.reshape(b, nc, nst, ch)
    outs = pl.pallas_call(
        body,
        grid=(b // bpb, nc),
        in_specs=[wide(fwd), wide(fwd), wide(fwd), wide(bwd), wide(bwd), wide(bwd),
                  narrow(fwd), narrow(bwd), narrow(fwd), narrow(bwd), rows(fwd4), rows(bwd4), state],
        out_specs=out_specs,
        out_shape=out_shape,
        scratch_shapes=[pltpu.VMEM((bpb * nst, GDN_HEAD_DIM, GDN_HEAD_DIM), F32)],
        compiler_params=_cp(("parallel", "arbitrary")),
        name="gdn_scan",
    )(q3, k3, v3, q3, k3, v3, gcol3, gcol3, bcol3, bcol3, grow4, grow4, s0)
    if with_out:
        return outs[0].reshape(n, GDN_W), outs[1].reshape(n, GDN_W), outs[2]
    return outs


def _gdn_out_body(of_ref, ob_ref, z_ref, nw_ref, y_ref):
    o = of_ref[...] + ob_ref[...]
    z = z_ref[...].astype(F32)
    hd = GDN_HEAD_DIM
    for h in range(GDN_HEADS):
        cols = slice(h * hd, (h + 1) * hd)
        oh = o[:, cols]
        yh = oh * lax.rsqrt(jnp.mean(oh * oh, axis=-1, keepdims=True) + NORM_EPS) * nw_ref[...]
        zh = z[:, cols]
        y_ref[:, cols] = (yh * (zh * _sigmoid(zh))).astype(BF16)


def gdn_output_gate(o_f, o_b, main, norm_w):
    n = o_f.shape[0]
    tm = min(512, n)
    nw = norm_w.reshape(1, GDN_HEAD_DIM).astype(F32)
    return pl.pallas_call(
        _gdn_out_body,
        grid=(n // tm,),
        in_specs=[pl.BlockSpec((tm, GDN_W), lambda i: (i, 0)),
                  pl.BlockSpec((tm, GDN_W), lambda i: (i, 0)),
                  pl.BlockSpec((tm, GDN_W), lambda i: (i, (OFF_GDN + 3 * GDN_W) // GDN_W)),
                  pl.BlockSpec((1, GDN_HEAD_DIM), lambda i: (0, 0))],
        out_specs=pl.BlockSpec((tm, GDN_W), lambda i: (i, 0)),
        out_shape=jax.ShapeDtypeStruct((n, GDN_W), BF16),
        compiler_params=_cp(("parallel",)),
        name="gdn_out",
    )(o_f, o_b, main, nw)


def gdn_branch(main_l, misc_l, main_c, misc_c, b, seq, ctx_len, conv_w, a_log, dt_bias, norm_w, ctx_out):
    nst = 2 * GDN_HEADS
    lane0 = MLA_ROPE

    def gates(gcf, gcb, beta, length):
        gcol = jnp.concatenate([gcf[:, lane0:lane0 + GDN_HEADS], gcb[:, lane0 + GDN_HEADS:lane0 + nst]], axis=1)
        bcol = beta[:, lane0 + nst:lane0 + 2 * nst]
        grow = jnp.transpose(gcol.reshape(-1, GDN_CHUNK, nst), (0, 2, 1))
        return gcol, bcol, grow

    qc, kc, vc, gcf, gcb, beta = gdn_prepare(main_c, misc_c, b, ctx_len, conv_w, a_log, dt_bias)
    gcol_c, bcol_c, grow_c = gates(gcf, gcb, beta, ctx_len)
    ql, kl, vl, gcf, gcb, beta = gdn_prepare(main_l, misc_l, b, seq, conv_w, a_log, dt_bias)
    gcol_l, bcol_l, grow_l = gates(gcf, gcb, beta, seq)
    s0 = jnp.zeros((b, nst, GDN_HEAD_DIM, GDN_HEAD_DIM), F32)
    outs_c = gdn_scan(qc, kc, vc, gcol_c, bcol_c, grow_c, s0, b, ctx_len, ctx_out)
    s_ctx = outs_c[-1]
    of_l, ob_l, _ = gdn_scan(ql, kl, vl, gcol_l, bcol_l, grow_l, s_ctx, b, seq, True)
    out_l = gdn_output_gate(of_l, ob_l, main_l, norm_w)
    out_c = gdn_output_gate(outs_c[0], outs_c[1], main_c, norm_w) if ctx_out else None
    return out_l, out_c


def _hy_conv_body(x_ref, xp_ref, xn_ref, cw_ref, v_out, x1_out, x2_out):
    prev, nxt = _halo_rows(xp_ref, xn_ref)
    y = _conv3(x_ref[...].astype(F32), prev, nxt, cw_ref)
    w = HY_WIDTH
    v_out[...] = y[:, :w].astype(BF16)
    x1_out[...] = y[:, w:2 * w].astype(BF16)
    x2_out[...] = y[:, 2 * w:3 * w].astype(BF16)


def hyena_short_conv(main, b, length, conv_w):
    n = b * length
    w3 = (HY_ORDER + 1) * HY_WIDTH
    tm = min(256, length)
    nt = length // tm
    cw = conv_w.astype(F32)
    row = pl.BlockSpec((tm, HY_WIDTH), lambda bi, i: (bi * nt + i, 0))
    return pl.pallas_call(
        _hy_conv_body,
        grid=(b, nt),
        in_specs=_halo_specs(tm, w3, OFF_HY // w3, nt, n) + [pl.BlockSpec(cw.shape, lambda bi, i: (0, 0))],
        out_specs=[row, row, row],
        out_shape=[jax.ShapeDtypeStruct((n, HY_WIDTH), BF16)] * 3,
        compiler_params=_cp(("parallel", "parallel")),
        name="hyena_conv",
    )(main, main, main, cw)


def _dft_consts(length):
    n = 2 * length
    n2 = 64 if length >= 2048 else 16
    n1 = n // n2
    k1 = np.arange(n1)
    t1 = np.arange(n1 // 2)
    ang1 = 2.0 * np.pi * np.outer(k1, t1) / n1
    f_first = np.concatenate([np.cos(ang1), -np.sin(ang1)], axis=0)
    t2 = np.arange(n2)
    ang2 = 2.0 * np.pi * np.outer(t2, t2) / n2
    c2, s2 = np.cos(ang2), np.sin(ang2)
    g_fwd = np.block([[c2, s2], [-s2, c2]])
    g_inv = g_fwd.T
    angt = 2.0 * np.pi * np.outer(k1, t2) / n
    tw_r, tw_i = np.cos(angt)[:, :, None], -np.sin(angt)[:, :, None]
    tt = np.arange(n1 // 4, 3 * n1 // 4)
    ang3 = 2.0 * np.pi * np.outer(tt, k1) / n1
    f_last = np.concatenate([np.cos(ang3), -np.sin(ang3)], axis=1) / n
    bf = lambda a: jnp.asarray(a, BF16)
    return dict(n1=n1, n2=n2, f_first=bf(f_first), g_fwd=bf(g_fwd), g_inv=bf(g_inv),
                tw_r=jnp.asarray(tw_r, F32), tw_i=jnp.asarray(tw_i, F32), f_last=bf(f_last))


def _hy_first_body(f_ref, z_ref, a_ref):
    a_ref[0] = jnp.dot(f_ref[...], z_ref[0], preferred_element_type=F32).astype(BF16)


def hyena_dft_first(zv, consts):
    b, half, cols = zv.shape
    n1 = consts['n1']
    tn = min(4096, cols)
    f = consts['f_first']
    return pl.pallas_call(
        _hy_first_body,
        grid=(b, cols // tn),
        in_specs=[pl.BlockSpec(f.shape, lambda bi, j: (0, 0)),
                  pl.BlockSpec((1, half, tn), lambda bi, j: (bi, 0, j))],
        out_specs=pl.BlockSpec((1, 2 * n1, tn), lambda bi, j: (bi, 0, j)),
        out_shape=jax.ShapeDtypeStruct((b, 2 * n1, cols), BF16),
        compiler_params=_cp(("parallel", "parallel")),
        name="hyena_dft_first",
    )(f, zv)


def _hy_mid_body(a_ref, twr_ref, twi_ref, gf_ref, *rest, kt, spectrum_only):
    if spectrum_only:
        (o_ref,) = rest
    else:
        gi_ref, h_ref, o_ref = rest
    n2 = gf_ref.shape[0] // 2

    def one(i, carry):
        ar = a_ref[0, 0, i].astype(F32)
        ai = a_ref[0, 1, i].astype(F32)
        twr, twi = twr_ref[i], twi_ref[i]
        br = ar * twr - ai * twi
        bi = ar * twi + ai * twr
        z = jnp.dot(gf_ref[...], jnp.concatenate([br, bi], axis=0).astype(BF16), preferred_element_type=F32)
        zr, zi = z[:n2], z[n2:]
        if spectrum_only:
            o_ref[0, 0, i] = zr
            o_ref[0, 1, i] = zi
            return carry
        hr, hi = h_ref[0, i], h_ref[1, i]
        yr = zr * hr - zi * hi
        yi = zr * hi + zi * hr
        w = jnp.dot(gi_ref[...], jnp.concatenate([yr, yi], axis=0).astype(BF16), preferred_element_type=F32)
        wr, wi = w[:n2], w[n2:]
        o_ref[0, 0, i] = (wr * twr + wi * twi).astype(BF16)
        o_ref[0, 1, i] = (wi * twr - wr * twi).astype(BF16)
        return carry

    lax.fori_loop(0, kt, one, 0)


def hyena_dft_mid(a5, consts, spectrum=None):
    b, _, n1, n2, c = a5.shape
    kt = 8
    only = spectrum is None
    blk = pl.BlockSpec((1, 2, kt, n2, c), lambda bi, j: (bi, 0, j, 0, 0))
    tw = pl.BlockSpec((kt, n2, 1), lambda bi, j: (j, 0, 0))
    g = pl.BlockSpec((2 * n2, 2 * n2), lambda bi, j: (0, 0))
    in_specs = [blk, tw, tw, g]
    args = [a5, consts['tw_r'], consts['tw_i'], consts['g_fwd']]
    if not only:
        in_specs += [g, pl.BlockSpec((2, kt, n2, c), lambda bi, j: (0, j, 0, 0))]
        args += [consts['g_inv'], spectrum]
    body = functools.partial(_hy_mid_body, kt=kt, spectrum_only=only)
    return pl.pallas_call(
        body,
        grid=(b, n1 // kt),
        in_specs=in_specs,
        out_specs=blk,
        out_shape=jax.ShapeDtypeStruct(a5.shape, F32 if only else BF16),
        compiler_params=_cp(("parallel", "parallel")),
        name="hyena_dft_mid",
    )(*args)


def _hy_last_body(f_ref, b_ref, z_ref, x_ref, bias_ref, o_ref):
    y = jnp.dot(f_ref[...], b_ref[0], preferred_element_type=F32)
    z = z_ref[0].astype(F32)
    o_ref[0] = (x_ref[0].astype(F32) * (y + bias_ref[...] * z)).astype(BF16)


def hyena_dft_last(bv, zv, xv, bias_row, consts):
    b, rows2, cols = bv.shape
    half = rows2 // 4
    tn = min(4096, cols)
    f = consts['f_last']
    sig = pl.BlockSpec((1, half, tn), lambda bi, j: (bi, 0, j))
    return pl.pallas_call(
        _hy_last_body,
        grid=(b, cols // tn),
        in_specs=[pl.BlockSpec(f.shape, lambda bi, j: (0, 0)),
                  pl.BlockSpec((1, rows2, tn), lambda bi, j: (bi, 0, j)),
                  sig, sig,
                  pl.BlockSpec((1, tn), lambda bi, j: (0, j))],
        out_specs=sig,
        out_shape=jax.ShapeDtypeStruct((b, half, cols), BF16),
        compiler_params=_cp(("parallel", "parallel")),
        name="hyena_dft_last",
    )(f, bv, zv, xv, bias_row)


def hyena_branch(main, b, length, conv_w, filt, bias):
    consts = _dft_consts(length)
    n1, n2 = consts['n1'], consts['n2']
    c = HY_WIDTH
    cols = n2 * c
    view = lambda t: t.reshape(b, n1 // 2, cols)
    v, x1, x2 = (view(t) for t in hyena_short_conv(main, b, length, conv_w))
    hv = jnp.transpose(filt, (1, 0, 2)).astype(BF16).reshape(HY_ORDER, n1 // 2, cols)
    h_first = hyena_dft_first(hv, consts).reshape(HY_ORDER, 2, n1, n2, c)
    spectra = hyena_dft_mid(h_first, consts)
    z = v
    for o, gate in enumerate((x1, x2)):
        a5 = hyena_dft_first(z, consts).reshape(b, 2, n1, n2, c)
        bm = hyena_dft_mid(a5, consts, spectra[o]).reshape(b, 2 * n1, cols)
        bias_row = jnp.tile(bias[o].astype(F32), n2).reshape(1, cols)
        z = hyena_dft_last(bm, z, gate, bias_row, consts)
    return z.reshape(b * length, c)


def _ada_body(c_ref, w_ref, b_ref, o_ref):
    cv = c_ref[...]
    s = cv * _sigmoid(cv)
    o_ref[0] = jnp.dot(s, w_ref[0], precision=lax.Precision.HIGHEST, preferred_element_type=F32) + b_ref[0]


def ada_modulation(c, c_ctx, w_ada, b_ada):
    depth, d, d6 = w_ada.shape
    c8 = jnp.zeros((MOD_ROWS, d), F32).at[:c.shape[0]].set(c).at[CTX_MOD_ROW].set(c_ctx)
    tn = 512
    out = pl.pallas_call(
        _ada_body,
        grid=(depth, d6 // tn),
        in_specs=[pl.BlockSpec((MOD_ROWS, d), lambda l, j: (0, 0)),
                  pl.BlockSpec((1, d, tn), lambda l, j: (l, 0, j)),
                  pl.BlockSpec((1, 1, tn), lambda l, j: (l, 0, j))],
        out_specs=pl.BlockSpec((1, MOD_ROWS, tn), lambda l, j: (l, 0, j)),
        out_shape=jax.ShapeDtypeStruct((depth, MOD_ROWS, d6), F32),
        compiler_params=_cp(("parallel", "parallel")),
        name="ada_mod",
    )(c8, w_ada, b_ada.reshape(depth, 1, d6))
    return out.reshape(depth, MOD_ROWS, 6, d)


def _row_tile(cap, n_rows, per_batch):
    return min(cap, n_rows if per_batch is None else per_batch)


def _mod_row_fn(n_rows, tm, per_batch):
    if per_batch is None:
        return lambda i: CTX_MOD_ROW
    tiles = per_batch // tm
    return lambda i: i // tiles


def _normmod_body(x_ref, nw_ref, sh_ref, sc_ref, o_ref):
    xf = x_ref[...]
    y = xf * lax.rsqrt(jnp.mean(xf * xf, axis=-1, keepdims=True) + NORM_EPS) * nw_ref[...]
    o_ref[...] = (y * (1.0 + sc_ref[0]) + sh_ref[0]).astype(o_ref.dtype)


def norm_modulate(x, nw, shift, scale, per_batch, out_dtype=BF16):
    n, d = x.shape
    tm = _row_tile(512, n, per_batch)
    rf = _mod_row_fn(n, tm, per_batch)
    return pl.pallas_call(
        _normmod_body,
        grid=(n // tm,),
        in_specs=[pl.BlockSpec((tm, d), lambda i: (i, 0)),
                  pl.BlockSpec((1, d), lambda i: (0, 0)),
                  pl.BlockSpec((1, 1, d), lambda i: (rf(i), 0, 0)),
                  pl.BlockSpec((1, 1, d), lambda i: (rf(i), 0, 0))],
        out_specs=pl.BlockSpec((tm, d), lambda i: (i, 0)),
        out_shape=jax.ShapeDtypeStruct((n, d), out_dtype),
        compiler_params=_cp(("parallel",)),
        name="norm_mod",
    )(x, nw.reshape(1, d), shift.reshape(MOD_ROWS, 1, d), scale.reshape(MOD_ROWS, 1, d))


def _mm_body(a_ref, w_ref, o_ref):
    o_ref[...] = jnp.dot(a_ref[...], w_ref[...], preferred_element_type=F32).astype(o_ref.dtype)


def matmul(a, w, out_dtype, tn):
    n, k = a.shape
    m = w.shape[1]
    tm = min(2048, n)
    return pl.pallas_call(
        _mm_body,
        grid=(n // tm, m // tn),
        in_specs=[pl.BlockSpec((tm, k), lambda i, j: (i, 0)),
                  pl.BlockSpec((k, tn), lambda i, j: (0, j))],
        out_specs=pl.BlockSpec((tm, tn), lambda i, j: (i, j)),
        out_shape=jax.ShapeDtypeStruct((n, m), out_dtype),
        compiler_params=_cp(("parallel", "parallel")),
        name="proj",
    )(a, w)


def _mm_res_body(a_ref, w_ref, x_ref, g_ref, o_ref):
    y = jnp.dot(a_ref[...], w_ref[...], preferred_element_type=F32)
    o_ref[...] = x_ref[...] + g_ref[0] * y


def matmul_gated_residual(a, w, x, gate, per_batch):
    n, k = a.shape
    d = w.shape[1]
    tm = _row_tile(1024, n, per_batch)
    tn = min(512, d)
    rf = _mod_row_fn(n, tm, per_batch)
    return pl.pallas_call(
        _mm_res_body,
        grid=(n // tm, d // tn),
        in_specs=[pl.BlockSpec((tm, k), lambda i, j: (i, 0)),
                  pl.BlockSpec((k, tn), lambda i, j: (0, j)),
                  pl.BlockSpec((tm, tn), lambda i, j: (i, j)),
                  pl.BlockSpec((1, 1, tn), lambda i, j: (rf(i), 0, j))],
        out_specs=pl.BlockSpec((tm, tn), lambda i, j: (i, j)),
        out_shape=jax.ShapeDtypeStruct((n, d), F32),
        compiler_params=_cp(("parallel", "parallel")),
        name="out_proj_residual",
    )(a, w, x, gate.reshape(MOD_ROWS, 1, d))


def _merge_body(h_ref, b0_ref, b1_ref, b2_ref, b3_ref, wg_ref, wb_ref, o_ref, acc_ref):
    n = pl.program_id(2)

    @pl.when(n == 0)
    def _():
        acc_ref[...] = jnp.zeros_like(acc_ref)

    gate = _sigmoid(jnp.dot(h_ref[...], wg_ref[...], preferred_element_type=F32))
    for idx, b_ref in enumerate((b0_ref, b1_ref, b2_ref, b3_ref)):
        @pl.when(n == idx)
        def _(b_ref=b_ref):
            acc_ref[...] += gate * jnp.dot(b_ref[...], wb_ref[0], preferred_element_type=F32)

    @pl.when(n == N_BRANCH - 1)
    def _():
        o_ref[...] = acc_ref[...].astype(o_ref.dtype)


def merge_branches_gated(h, branches, w_gate, w_branch):
    n, d = h.shape
    bw = branches[0].shape[1]
    tm = min(1024, n)
    tn = min(512, d)
    nj = d // tn
    return pl.pallas_call(
        _merge_body,
        grid=(n // tm, nj, N_BRANCH),
        in_specs=[pl.BlockSpec((tm, d), lambda i, j, b: (i, 0))]
                 + [pl.BlockSpec((tm, bw), lambda i, j, b: (i, 0))] * N_BRANCH
                 + [pl.BlockSpec((d, tn), lambda i, j, b: (0, b * nj + j)),
                    pl.BlockSpec((1, bw, tn), lambda i, j, b: (b, 0, j))],
        out_specs=pl.BlockSpec((tm, tn), lambda i, j, b: (i, j)),
        out_shape=jax.ShapeDtypeStruct((n, d), BF16),
        scratch_shapes=[pltpu.VMEM((tm, tn), F32)],
        compiler_params=_cp(("parallel", "parallel", "arbitrary")),
        name="gate_merge",
    )(h, *branches, w_gate, w_branch)


def _mla_prep_body(cq_ref, ckv_ref, misc_ref, qnw_ref, kvnw_ref, wqa_ref, wqb_ref, wk_ref, wv_ref,
                   ska_ref, skb_ref, cq_tab, sq_tab, q_out, k_out, v_out):
    def norm(v, w_ref):
        vf = v.astype(F32)
        return (vf * lax.rsqrt(jnp.mean(vf * vf, axis=-1, keepdims=True) + NORM_EPS) * w_ref[...]).astype(BF16)

    xq = norm(cq_ref[...], qnw_ref)
    xkv = norm(ckv_ref[...], kvnw_ref)
    cos, sin = cq_tab[...], sq_tab[...]
    misc = misc_ref[...].astype(BF16)
    kr = (jnp.dot(misc, ska_ref[...], preferred_element_type=F32) * cos
          + jnp.dot(misc, skb_ref[...], preferred_element_type=F32) * sin)
    for h in range(MLA_HEADS):
        qa = jnp.dot(xq, wqa_ref[h], preferred_element_type=F32)
        qb = jnp.dot(xq, wqb_ref[h], preferred_element_type=F32)
        q_out[0, h] = (qa * cos + qb * sin).astype(BF16)
        k_out[0, h] = (jnp.dot(xkv, wk_ref[h], preferred_element_type=F32) + kr).astype(BF16)
        v_out[0, h] = jnp.dot(xkv, wv_ref[h], preferred_element_type=F32).astype(BF16)


def _mla_weights(q_norm_w, kv_norm_w, w_uq, w_ukv):
    dk = MLA_NOPE + MLA_ROPE
    half = MLA_ROPE // 2
    scale = dk ** -0.5
    wq = jnp.transpose(w_uq, (1, 0, 2)) * scale
    nope0 = jnp.zeros(wq.shape[:2] + (MLA_NOPE,), F32)
    wq_rot = jnp.concatenate([nope0, -wq[..., MLA_NOPE + half:], wq[..., MLA_NOPE:MLA_NOPE + half]], axis=-1)
    wkv = jnp.transpose(w_ukv, (1, 0, 2))
    wk = jnp.concatenate([wkv[..., :MLA_NOPE], jnp.zeros(wkv.shape[:2] + (MLA_ROPE,), F32)], axis=-1)
    wv = wkv[..., MLA_NOPE:]
    eye = jnp.eye(MLA_ROPE, dtype=F32)
    rot = jnp.concatenate([-eye[:, half:], eye[:, :half]], axis=-1)
    pad_r = MISC_W - MLA_ROPE
    ska = jnp.pad(eye, ((0, pad_r), (MLA_NOPE, 0)))
    skb = jnp.pad(rot, ((0, pad_r), (MLA_NOPE, 0)))
    return tuple(t.astype(BF16) for t in (wq, wq_rot, wk, wv, ska, skb))


def _mla_tables(rope, length):
    dk = MLA_NOPE + MLA_ROPE
    if rope is None:
        return jnp.ones((length, dk), F32), jnp.zeros((length, dk), F32)
    cos, sin = rope
    ones = jnp.ones((length, MLA_NOPE), F32)
    return (jnp.concatenate([ones, cos, cos], axis=-1),
            jnp.concatenate([0.0 * ones, sin, sin], axis=-1))


def mla_prepare(main, misc, b, length, weights, q_norm_w, kv_norm_w, tables):
    wq, wq_rot, wk, wv, ska, skb = weights
    cos, sin = tables
    dk = MLA_NOPE + MLA_ROPE
    tm = min(512, length)
    nt = length // tm
    full = lambda a: pl.BlockSpec(a.shape, lambda bi, i: (0,) * a.ndim)
    qnw = q_norm_w.reshape(1, -1)
    kvnw = kv_norm_w.reshape(1, -1)
    outs = pl.pallas_call(
        _mla_prep_body,
        grid=(b, nt),
        in_specs=[pl.BlockSpec((tm, MLA_Q_LORA), lambda bi, i: (bi * nt + i, OFF_CQ // MLA_Q_LORA)),
                  pl.BlockSpec((tm, MLA_KV_LORA), lambda bi, i: (bi * nt + i, OFF_CKV // MLA_KV_LORA)),
                  pl.BlockSpec((tm, MISC_W), lambda bi, i: (bi * nt + i, 0)),
                  full(qnw), full(kvnw), full(wq), full(wq_rot), full(wk), full(wv), full(ska), full(skb),
                  pl.BlockSpec((tm, dk), lambda bi, i: (i, 0)),
                  pl.BlockSpec((tm, dk), lambda bi, i: (i, 0))],
        out_specs=[pl.BlockSpec((1, MLA_HEADS, tm, dk), lambda bi, i: (bi, 0, i, 0)),
                   pl.BlockSpec((1, MLA_HEADS, tm, dk), lambda bi, i: (bi, 0, i, 0)),
                   pl.BlockSpec((1, MLA_HEADS, tm, MLA_V), lambda bi, i: (bi, 0, i, 0))],
        out_shape=[jax.ShapeDtypeStruct((b, MLA_HEADS, length, dk), BF16),
                   jax.ShapeDtypeStruct((b, MLA_HEADS, length, dk), BF16),
                   jax.ShapeDtypeStruct((b, MLA_HEADS, length, MLA_V), BF16)],
        compiler_params=_cp(("parallel", "parallel")),
        name="mla_prep",
    )(main, main, misc, qnw, kvnw, wq, wq_rot, wk, wv, ska, skb, cos, sin)
    return outs


def _gqa_prep_body(q_ref, k_ref, v_ref, qnw_ref, knw_ref, gsum_ref, rot_ref, cos_ref, sin_ref,
                   q_out, k_out, v_out):
    cos, sin = cos_ref[...], sin_ref[...]

    def prep(v, nw, width):
        vf = v.astype(F32)
        sq = vf * vf
        hi = sq.astype(BF16)
        lo = (sq - hi.astype(F32)).astype(BF16)
        g = gsum_ref[:width, :width]
        ss = jnp.dot(hi, g, preferred_element_type=F32) + jnp.dot(lo, g, preferred_element_type=F32)
        xn = vf * lax.rsqrt(ss * (1.0 / GQA_HEAD_DIM) + NORM_EPS) * nw
        xr = jnp.dot(xn.astype(BF16), rot_ref[:width, :width], preferred_element_type=F32)
        return xn * cos[:, :width] + xr * sin[:, :width]

    qf = prep(q_ref[...], qnw_ref[...], GQA_HEADS * GQA_HEAD_DIM) * GQA_HEAD_DIM ** -0.5
    kf = prep(k_ref[...], knw_ref[...], LANES)
    q_out[...] = qf.astype(BF16)
    k_out[...] = kf.astype(BF16)
    v_out[...] = v_ref[...]


def gqa_prepare(main, b, length, q_norm_w, k_norm_w, rope):
    n = b * length
    qw = GQA_HEADS * GQA_HEAD_DIM
    kw = GQA_KV_HEADS * GQA_HEAD_DIM
    half = GQA_HEAD_DIM // 2
    if rope is None:
        cos = jnp.ones((length, qw), F32)
        sin = jnp.zeros((length, qw), F32)
    else:
        cos = jnp.tile(jnp.concatenate([rope[0], rope[0]], axis=-1), (1, GQA_HEADS))
        sin = jnp.tile(jnp.concatenate([rope[1], rope[1]], axis=-1), (1, GQA_HEADS))
    head = np.arange(qw) // GQA_HEAD_DIM
    gsum = jnp.asarray(head[:, None] == head[None, :], BF16)
    eye = np.eye(GQA_HEAD_DIM, dtype=np.float32)
    rot1 = np.concatenate([-eye[:, half:], eye[:, :half]], axis=-1)
    rot = jnp.asarray(np.kron(np.eye(GQA_HEADS, dtype=np.float32), rot1), BF16)
    tm = min(512, length)
    nt = length // tm
    full = lambda a: pl.BlockSpec(a.shape, lambda bi, i: (0,) * a.ndim)
    qnw = jnp.tile(q_norm_w, GQA_HEADS).reshape(1, qw)
    knw = jnp.tile(k_norm_w, GQA_KV_HEADS).reshape(1, kw)
    return pl.pallas_call(
        _gqa_prep_body,
        grid=(b, nt),
        in_specs=[pl.BlockSpec((tm, qw), lambda bi, i: (bi * nt + i, OFF_GQ // qw)),
                  pl.BlockSpec((tm, kw), lambda bi, i: (bi * nt + i, OFF_GK // kw)),
                  pl.BlockSpec((tm, kw), lambda bi, i: (bi * nt + i, OFF_GV // kw)),
                  full(qnw), full(knw), full(gsum), full(rot),
                  pl.BlockSpec((tm, qw), lambda bi, i: (i, 0)),
                  pl.BlockSpec((tm, qw), lambda bi, i: (i, 0))],
        out_specs=[pl.BlockSpec((tm, qw), lambda bi, i: (bi * nt + i, 0)),
                   pl.BlockSpec((tm, kw), lambda bi, i: (bi * nt + i, 0)),
                   pl.BlockSpec((tm, kw), lambda bi, i: (bi * nt + i, 0))],
        out_shape=[jax.ShapeDtypeStruct((n, qw), BF16),
                   jax.ShapeDtypeStruct((n, kw), BF16),
                   jax.ShapeDtypeStruct((n, kw), BF16)],
        compiler_params=_cp(("parallel", "parallel")),
        name="gqa_prep",
    )(main, main, main, qnw, knw, gsum, rot, cos, sin)


FLASH_CHAIN_ROWS = 256


def _flash_body(q_ref, k_ref, v_ref, kc_ref, vc_ref, o_ref, m_ref, l_ref, acc_ref, *, nk, has_ctx, chains):
    ki = pl.program_id(3)

    @pl.when(ki == 0)
    def _():
        m_ref[...] = jnp.full_like(m_ref, -jnp.inf)
        l_ref[...] = jnp.zeros_like(l_ref)
        acc_ref[...] = jnp.zeros_like(acc_ref)

    def step(k, vt):
        ss = [jnp.dot(k, q_ref[0, 0, gi, :, r0:r0 + rc], preferred_element_type=F32)
              for gi, r0, rc in chains]
        m_prev = [m_ref[ci] for ci in range(len(chains))]
        m_new = [jnp.maximum(mp, jnp.max(s, axis=0, keepdims=True)) for mp, s in zip(m_prev, ss)]
        ps = [jnp.exp(s - mn) for s, mn in zip(ss, m_new)]
        alphas = [jnp.exp(mp - mn) for mp, mn in zip(m_prev, m_new)]
        pv = [jnp.dot(vt, p.astype(BF16), preferred_element_type=F32) for p in ps]
        for ci in range(len(chains)):
            l_ref[ci] = alphas[ci] * l_ref[ci] + jnp.sum(ps[ci], axis=0, keepdims=True)
            acc_ref[ci] = alphas[ci] * acc_ref[ci] + pv[ci]
            m_ref[ci] = m_new[ci]

    @pl.when(ki < nk)
    def _():
        step(k_ref[0, 0], v_ref[0, 0])

    if has_ctx:
        @pl.when(ki == nk)
        def _():
            step(kc_ref[0, 0], vc_ref[0, 0])

    @pl.when(ki == nk - 1 + int(has_ctx))
    def _():
        for ci, (gi, r0, rc) in enumerate(chains):
            o_ref[0, 0, gi, :, r0:r0 + rc] = (acc_ref[ci] / l_ref[ci]).astype(o_ref.dtype)


def flash_attention(q, k, v, kc, vc, tq, tk):
    b, hkv, g, sq, dk = q.shape
    sk = k.shape[2]
    dv = v.shape[3]
    tq = min(tq, sq)
    tk = min(tk, sk)
    nk = sk // tk
    has_ctx = kc is not None
    if not has_ctx:
        kc, vc = k[:, :, :LANES], v[:, :, :LANES]
    skc = kc.shape[2]
    rc = min(FLASH_CHAIN_ROWS, tq)
    chains = tuple((gi, r0, rc) for gi in range(g) for r0 in range(0, tq, rc))
    nch = len(chains)
    body = functools.partial(_flash_body, nk=nk, has_ctx=has_ctx, chains=chains)
    qt = jnp.swapaxes(q, 3, 4)
    vt = jnp.swapaxes(v, 2, 3)
    vct = jnp.swapaxes(vc, 2, 3)
    out_t = pl.pallas_call(
        body,
        grid=(b, hkv, sq // tq, nk + int(has_ctx)),
        in_specs=[pl.BlockSpec((1, 1, g, dk, tq), lambda bi, h, qi, ki: (bi, h, 0, 0, qi)),
                  pl.BlockSpec((1, 1, tk, dk), lambda bi, h, qi, ki: (bi, h, jnp.minimum(ki, nk - 1), 0)),
                  pl.BlockSpec((1, 1, dv, tk), lambda bi, h, qi, ki: (bi, h, 0, jnp.minimum(ki, nk - 1))),
                  pl.BlockSpec((1, 1, skc, dk), lambda bi, h, qi, ki: (bi, h, 0, 0)),
                  pl.BlockSpec((1, 1, dv, skc), lambda bi, h, qi, ki: (bi, h, 0, 0))],
        out_specs=pl.BlockSpec((1, 1, g, dv, tq), lambda bi, h, qi, ki: (bi, h, 0, 0, qi)),
        out_shape=jax.ShapeDtypeStruct((b, hkv, g, dv, sq), BF16),
        scratch_shapes=[pltpu.VMEM((nch, 1, rc), F32), pltpu.VMEM((nch, 1, rc), F32),
                        pltpu.VMEM((nch, dv, rc), F32)],
        compiler_params=_cp(("parallel", "parallel", "parallel", "arbitrary")),
        name="flash_attention",
    )(qt, k, vt, kc, vct)
    return jnp.swapaxes(out_t, 3, 4)


def _router_body(x_ref, nw_ref, sh_ref, sc_ref, wrh_ref, wrl_ref, rb_ref, tri_ref,
                 h_ref, ri_ref, rw_ref, cnt_ref, carry_ref):
    i = pl.program_id(0)

    @pl.when(i == 0)
    def _():
        carry_ref[...] = jnp.zeros_like(carry_ref)

    xf = x_ref[...]
    y = xf * lax.rsqrt(jnp.mean(xf * xf, axis=-1, keepdims=True) + NORM_EPS) * nw_ref[...]
    h = y * (1.0 + sc_ref[0]) + sh_ref[0]
    h_ref[...] = h
    hi = h.astype(BF16)
    lo = (h - hi.astype(F32)).astype(BF16)
    nt = (((1,), (1,)), ((), ()))
    logits = (lax.dot_general(wrh_ref[...], hi, nt, preferred_element_type=F32)
              + lax.dot_general(wrh_ref[...], lo, nt, preferred_element_type=F32)
              + lax.dot_general(wrl_ref[...], hi, nt, preferred_element_type=F32))
    scores = _sigmoid(logits)
    sel = scores + rb_ref[...]
    s = [scores[e:e + 1] for e in range(N_EXPERTS)]
    v = [sel[e:e + 1] for e in range(N_EXPERTS)]
    epg = EXPERTS_PER_GROUP
    gscore = []
    for gi in range(N_GROUPS):
        mem = v[gi * epg:(gi + 1) * epg]
        best = None
        for a in range(epg):
            for c in range(a + 1, epg):
                pair = mem[a] + mem[c]
                best = pair if best is None else jnp.maximum(best, pair)
        gscore.append(best)
    is_best = []
    for gi in range(N_GROUPS):
        ok = None
        for gj in range(N_GROUPS):
            if gj == gi:
                continue
            c = (gscore[gi] > gscore[gj]) if gj < gi else (gscore[gi] >= gscore[gj])
            ok = c if ok is None else (ok & c)
        is_best.append(ok)
    chosen = []
    for e in range(N_EXPERTS):
        gi = e // epg
        rank = jnp.zeros_like(v[e])
        for e2 in range(gi * epg, (gi + 1) * epg):
            if e2 == e:
                continue
            ahead = (v[e2] >= v[e]) if e2 < e else (v[e2] > v[e])
            rank = rank + jnp.where(ahead, 1.0, 0.0)
        chosen.append(is_best[gi] & (rank < TOP_K))
    chosen_f = jnp.concatenate([jnp.where(cm, 1.0, 0.0) for cm in chosen], axis=0)
    total = None
    for e in range(N_EXPERTS):
        t = jnp.where(chosen[e], s[e], 0.0)
        total = t if total is None else total + t
    pos = jnp.dot(chosen_f.astype(BF16), tri_ref[...], preferred_element_type=F32) + carry_ref[...]
    carry_ref[...] += jnp.sum(chosen_f, axis=-1, keepdims=True)
    cnt_ref[...] = jnp.broadcast_to(carry_ref[...], cnt_ref.shape)
    e_lo = jnp.full_like(v[0], float(N_EXPERTS))
    e_hi = jnp.full_like(v[0], -1.0)
    for e in range(N_EXPERTS):
        e_lo = jnp.where(chosen[e], jnp.minimum(e_lo, float(e)), e_lo)
        e_hi = jnp.where(chosen[e], jnp.maximum(e_hi, float(e)), e_hi)
    zero = jnp.zeros_like(v[0])
    w_lo, w_hi, p_lo, p_hi = zero, zero, zero, zero
    for e in range(N_EXPERTS):
        pe = pos[e:e + 1]
        w_lo = jnp.where(e_lo == float(e), s[e], w_lo)
        w_hi = jnp.where(e_hi == float(e), s[e], w_hi)
        p_lo = jnp.where(e_lo == float(e), pe, p_lo)
        p_hi = jnp.where(e_hi == float(e), pe, p_hi)
    ri_ref[...] = jnp.concatenate([e_lo, e_hi, p_lo, p_hi, zero, zero, zero, zero], axis=0).astype(jnp.int32)
    rw_ref[...] = jnp.concatenate([w_lo / total, w_hi / total, zero, zero, zero, zero, zero, zero], axis=0)


def norm_route(x, nw, shift, scale, per_batch, w_router, router_bias):
    n, d = x.shape
    tm = _row_tile(512, n, per_batch)
    rf = _mod_row_fn(n, tm, per_batch)
    wr_t = w_router.T
    wr_hi = wr_t.astype(BF16)
    wr_lo = (wr_t - wr_hi.astype(F32)).astype(BF16)
    tri = jnp.asarray(np.triu(np.ones((tm, tm), np.float32), 1), BF16)
    const = lambda a: pl.BlockSpec(a.shape, lambda i: (0,) * a.ndim)
    rb = router_bias.reshape(N_EXPERTS, 1).astype(F32)
    return pl.pallas_call(
        _router_body,
        grid=(n // tm,),
        in_specs=[pl.BlockSpec((tm, d), lambda i: (i, 0)),
                  pl.BlockSpec((1, d), lambda i: (0, 0)),
                  pl.BlockSpec((1, 1, d), lambda i: (rf(i), 0, 0)),
                  pl.BlockSpec((1, 1, d), lambda i: (rf(i), 0, 0)),
                  const(wr_hi), const(wr_lo), const(rb), const(tri)],
        out_specs=[pl.BlockSpec((tm, d), lambda i: (i, 0)),
                   pl.BlockSpec((8, tm), lambda i: (0, i)),
                   pl.BlockSpec((8, tm), lambda i: (0, i)),
                   pl.BlockSpec((N_EXPERTS, LANES), lambda i: (0, 0))],
        out_shape=[jax.ShapeDtypeStruct((n, d), F32),
                   jax.ShapeDtypeStruct((8, n), jnp.int32),
                   jax.ShapeDtypeStruct((8, n), F32),
                   jax.ShapeDtypeStruct((N_EXPERTS, LANES), F32)],
        scratch_shapes=[pltpu.VMEM((N_EXPERTS, 1), F32)],
        compiler_params=_cp(("arbitrary",)),
        name="norm_route",
    )(x, nw.reshape(1, d), shift.reshape(MOD_ROWS, 1, d), scale.reshape(MOD_ROWS, 1, d),
      wr_hi, wr_lo, rb, tri)


def _dispatch_body(sa_ref, sb_ref, pad_ref, h_ref, xs_ref, zero_ref, sem, *, tm, n_pad):
    i = pl.program_id(0)
    base = i * tm

    def row_copy(src, r, slot):
        return pltpu.make_async_copy(src.at[pl.ds(r, 1)], xs_ref.at[pl.ds(slot, 1)], sem)

    @pl.when(i == 0)
    def _():
        zero_ref[...] = jnp.zeros_like(zero_ref)

        def fill(j, carry):
            row_copy(zero_ref, 0, pad_ref[j]).start()
            return carry
        lax.fori_loop(0, n_pad, fill, 0)

        def drain(j, carry):
            row_copy(zero_ref, 0, 0).wait()
            return carry
        lax.fori_loop(0, n_pad, drain, 0)

    def issue(r, carry):
        row_copy(h_ref, r, sa_ref[base + r]).start()
        row_copy(h_ref, r, sb_ref[base + r]).start()
        return carry
    lax.fori_loop(0, tm, issue, 0)

    def drain2(r, carry):
        row_copy(h_ref, 0, 0).wait()
        row_copy(h_ref, 0, 0).wait()
        return carry
    lax.fori_loop(0, tm, drain2, 0)


def moe_dispatch(h, slot_a, slot_b, pad_slots, n_slots):
    n, d = h.shape
    tm = min(256, n)
    n_pad = pad_slots.shape[0]
    body = functools.partial(_dispatch_body, tm=tm, n_pad=n_pad)
    return pl.pallas_call(
        body,
        grid_spec=pltpu.PrefetchScalarGridSpec(
            num_scalar_prefetch=3,
            grid=(n // tm,),
            in_specs=[pl.BlockSpec((tm, d), lambda i, sa, sb, pd: (i, 0))],
            out_specs=pl.BlockSpec(memory_space=pl.ANY),
            scratch_shapes=[pltpu.VMEM((8, d), F32), pltpu.SemaphoreType.DMA(())]),
        out_shape=jax.ShapeDtypeStruct((n_slots, d), F32),
        compiler_params=_cp(("arbitrary",)),
        name="moe_dispatch",
    )(slot_a, slot_b, pad_slots, h)


def _experts_body(te_ref, nu_ref, xs_ref, wg_ref, wu_ref, wd_ref, y_ref):
    i = pl.program_id(0)

    @pl.when(i < nu_ref[0])
    def _():
        xb = xs_ref[...].astype(BF16)
        hg = jnp.dot(xb, wg_ref[0], preferred_element_type=F32)
        hu = jnp.dot(xb, wu_ref[0], preferred_element_type=F32)
        act = (hg * _sigmoid(hg) * hu).astype(BF16)
        y_ref[...] = jnp.dot(act, wd_ref[0], preferred_element_type=F32)

    @pl.when(i >= nu_ref[0])
    def _():
        y_ref[...] = jnp.zeros_like(y_ref)


def moe_experts(xs, tile_expert, n_used, w_gate, w_up, w_down):
    s, d = xs.shape
    f = w_gate.shape[2]
    tm = MOE_TILE
    return pl.pallas_call(
        _experts_body,
        grid_spec=pltpu.PrefetchScalarGridSpec(
            num_scalar_prefetch=2,
            grid=(s // tm,),
            in_specs=[pl.BlockSpec((tm, d), lambda i, te, nu: (jnp.minimum(i, nu[0] - 1), 0)),
                      pl.BlockSpec((1, d, f), lambda i, te, nu: (te[i], 0, 0)),
                      pl.BlockSpec((1, d, f), lambda i, te, nu: (te[i], 0, 0)),
                      pl.BlockSpec((1, f, d), lambda i, te, nu: (te[i], 0, 0))],
            out_specs=pl.BlockSpec((tm, d), lambda i, te, nu: (i, 0))),
        out_shape=jax.ShapeDtypeStruct((s, d), F32),
        compiler_params=_cp(("arbitrary",)),
        name="moe_experts",
    )(tile_expert, n_used, xs, w_gate, w_up, w_down)


def _combine_body(sa_ref, sb_ref, x_ref, w_ref, g_ref, y_ref, o_ref, ba_ref, bb_ref, sem, *, tm):
    i = pl.program_id(0)
    base = i * tm

    def row_copy(slot, dst, r):
        return pltpu.make_async_copy(y_ref.at[pl.ds(slot, 1)], dst.at[pl.ds(r, 1)], sem)

    def issue(r, carry):
        row_copy(sa_ref[base + r], ba_ref, r).start()
        row_copy(sb_ref[base + r], bb_ref, r).start()
        return carry
    lax.fori_loop(0, tm, issue, 0)

    def drain(r, carry):
        row_copy(0, ba_ref, 0).wait()
        row_copy(0, bb_ref, 0).wait()
        return carry
    lax.fori_loop(0, tm, drain, 0)

    w = w_ref[...]
    mix = w[:, 0:1] * ba_ref[...] + w[:, 1:2] * bb_ref[...]
    o_ref[...] = x_ref[...] + g_ref[0] * mix


def moe_combine(x, y, slot_a, slot_b, wts, gate, per_batch):
    n, d = x.shape
    tm = _row_tile(256, n, per_batch)
    rf = _mod_row_fn(n, tm, per_batch)
    body = functools.partial(_combine_body, tm=tm)
    return pl.pallas_call(
        body,
        grid_spec=pltpu.PrefetchScalarGridSpec(
            num_scalar_prefetch=2,
            grid=(n // tm,),
            in_specs=[pl.BlockSpec((tm, d), lambda i, sa, sb: (i, 0)),
                      pl.BlockSpec((tm, 8), lambda i, sa, sb: (i, 0)),
                      pl.BlockSpec((1, 1, d), lambda i, sa, sb: (rf(i), 0, 0)),
                      pl.BlockSpec(memory_space=pl.ANY)],
            out_specs=pl.BlockSpec((tm, d), lambda i, sa, sb: (i, 0)),
            scratch_shapes=[pltpu.VMEM((tm, d), F32), pltpu.VMEM((tm, d), F32), pltpu.SemaphoreType.DMA(())]),
        out_shape=jax.ShapeDtypeStruct((n, d), F32),
        compiler_params=_cp(("arbitrary",)),
        name="moe_combine",
    )(slot_a, slot_b, x, wts, gate.reshape(MOD_ROWS, 1, d), y)


def moe_layer(x, nw, mod, per_batch, w_router, router_bias, w_gate, w_up, w_down):
    n, d = x.shape
    h, route_i, route_w, counts = norm_route(x, nw, mod[:, 3], mod[:, 4], per_batch, w_router, router_bias)
    cnt = counts[:, 0].astype(jnp.int32)
    seg = ((cnt + MOE_TILE - 1) // MOE_TILE) * MOE_TILE
    off = jnp.concatenate([jnp.zeros((1,), jnp.int32), jnp.cumsum(seg)])
    n_slots = TOP_K * n + N_EXPERTS * MOE_TILE
    slot_a = off[route_i[0]] + route_i[2]
    slot_b = off[route_i[1]] + route_i[3]
    n_pad = n_slots - TOP_K * n
    padcnt = seg - cnt
    padstart = jnp.concatenate([jnp.zeros((1,), jnp.int32), jnp.cumsum(padcnt)])
    j = jnp.arange(n_pad, dtype=jnp.int32)
    e_of = jnp.clip(jnp.searchsorted(padstart, j, side='right') - 1, 0, N_EXPERTS)
    in_seg = off[jnp.minimum(e_of, N_EXPERTS - 1)] + cnt[jnp.minimum(e_of, N_EXPERTS - 1)] + (j - padstart[e_of])
    tail = off[N_EXPERTS] + (j - padstart[N_EXPERTS])
    pad_slots = jnp.where(e_of < N_EXPERTS, in_seg, tail).astype(jnp.int32)
    n_tiles = n_slots // MOE_TILE
    tile_start = jnp.arange(n_tiles, dtype=jnp.int32) * MOE_TILE
    n_used = (off[N_EXPERTS] // MOE_TILE).astype(jnp.int32).reshape(1)
    tile_expert = jnp.clip(jnp.searchsorted(off, tile_start, side='right') - 1, 0, N_EXPERTS - 1).astype(jnp.int32)
    last_used = tile_expert[jnp.maximum(n_used[0] - 1, 0)]
    tile_expert = jnp.where(jnp.arange(n_tiles) < n_used[0], tile_expert, last_used)

    xs = moe_dispatch(h, slot_a, slot_b, pad_slots, n_slots)
    y = moe_experts(xs, tile_expert, n_used, w_gate, w_up, w_down)
    wts = jnp.transpose(route_w)
    return moe_combine(x, y, slot_a, slot_b, wts, mod[:, 5], per_batch)


def _final_norm_body(x_ref, w_ref, o_ref):
    xf = x_ref[...]
    y = xf * lax.rsqrt(jnp.mean(xf * xf, axis=-1, keepdims=True) + NORM_EPS)
    o_ref[...] = y * w_ref[...]


def final_rms_norm(x, w):
    n, d = x.shape
    rows = 512
    return pl.pallas_call(
        _final_norm_body,
        grid=(n // rows,),
        in_specs=[pl.BlockSpec((rows, d), lambda i: (i, 0)), pl.BlockSpec((1, d), lambda i: (0, 0))],
        out_specs=pl.BlockSpec((rows, d), lambda i: (i, 0)),
        out_shape=jax.ShapeDtypeStruct((n, d), x.dtype),
        compiler_params=_cp(("parallel",)),
        name="final_norm",
    )(x, w.reshape(1, d))


def _reorder_w_in(w):
    o = np.cumsum((0,) + IN_WIDTHS)
    seg = lambda i: w[:, o[i]:o[i + 1]]
    main = jnp.concatenate([seg(0), seg(1), seg(2), seg(3), seg(6), seg(9), seg(12), seg(7), seg(10), seg(11)], axis=1)
    misc = jnp.concatenate([seg(8), seg(4), seg(5)], axis=1)
    misc = jnp.pad(misc, ((0, 0), (0, MISC_W - misc.shape[1])))
    return main.astype(BF16), misc.astype(BF16)


def _attention_branches(main_l, misc_l, main_c, misc_c, b, seq, ctx_len, ctx_out, rope_mla, rope_gqa,
                        mla_w, mla_qn, mla_kvn, gqa_qn, gqa_kn):
    g = GQA_HEADS // GQA_KV_HEADS
    hd = GQA_HEAD_DIM

    mq_l, mk_l, mv_l = mla_prepare(main_l, misc_l, b, seq, mla_w, mla_qn, mla_kvn, _mla_tables(rope_mla, seq))
    mq_c, mk_c, mv_c = mla_prepare(main_c, misc_c, b, ctx_len, mla_w, mla_qn, mla_kvn, _mla_tables(None, ctx_len))
    mla_l = flash_attention(mq_l[:, :, None], mk_l, mv_l, mk_c, mv_c, 1024, 512)
    mla_l = jnp.transpose(mla_l[:, :, 0], (0, 2, 1, 3)).reshape(b * seq, BRANCH_W)

    def split_heads(t, length, heads):
        return jnp.transpose(t.reshape(b, length, heads, hd), (0, 2, 1, 3))

    gq_l, gk_l, gv_l = gqa_prepare(main_l, b, seq, gqa_qn, gqa_kn, rope_gqa)
    gq_c, gk_c, gv_c = gqa_prepare(main_c, b, ctx_len, gqa_qn, gqa_kn, None)
    gq_l5 = split_heads(gq_l, seq, GQA_HEADS).reshape(b, GQA_KV_HEADS, g, seq, hd)
    gk_l4, gv_l4 = split_heads(gk_l, seq, GQA_KV_HEADS), split_heads(gv_l, seq, GQA_KV_HEADS)
    gk_c4, gv_c4 = split_heads(gk_c, ctx_len, GQA_KV_HEADS), split_heads(gv_c, ctx_len, GQA_KV_HEADS)
    gqa_l = flash_attention(gq_l5, gk_l4, gv_l4, gk_c4, gv_c4, 256, 512)
    gqa_l = jnp.transpose(gqa_l.reshape(b, GQA_HEADS, seq, hd), (0, 2, 1, 3)).reshape(b * seq, BRANCH_W)

    mla_c = gqa_c = None
    if ctx_out:
        mla_c = flash_attention(mq_c[:, :, None], mk_c, mv_c, None, None, 256, 256)
        mla_c = jnp.transpose(mla_c[:, :, 0], (0, 2, 1, 3)).reshape(b * ctx_len, BRANCH_W)
        gq_c5 = split_heads(gq_c, ctx_len, GQA_HEADS).reshape(b, GQA_KV_HEADS, g, ctx_len, hd)
        gqa_c = flash_attention(gq_c5, gk_c4, gv_c4, None, None, 256, 256)
        gqa_c = jnp.transpose(gqa_c.reshape(b, GQA_HEADS, ctx_len, hd), (0, 2, 1, 3)).reshape(b * ctx_len, BRANCH_W)
    return mla_l, gqa_l, mla_c, gqa_c


def kernel(x, c, ctx, c_ctx, w_ada, b_ada, norm1_w, norm2_w, w_in,
           gdn_conv_w, gdn_a_log, gdn_dt_bias, gdn_norm_w,
           mla_q_norm_w, mla_kv_norm_w, mla_w_uq, mla_w_ukv,
           gqa_q_norm_w, gqa_k_norm_w,
           hy_conv_w, hy_w1, hy_b1, hy_w2, hy_b2, hy_w3, hy_sin_freq, hy_bias,
           w_branch, w_out, w_router, router_bias,
           moe_w_gate, moe_w_up, moe_w_down, final_norm_w):
    b, seq, d = x.shape
    ctx_len = ctx.shape[1]
    rows = seq // GRID_W
    rope_mla = axial_rope_tables(rows, MLA_ROPE)
    rope_gqa = axial_rope_tables(rows, GQA_HEAD_DIM)
    mod_all = ada_modulation(c, c_ctx, w_ada, b_ada)
    xl = x.reshape(b * seq, d)
    xc = ctx.reshape(b * ctx_len, d)
    f32 = lambda t: t.astype(F32)
    for layer in range(DEPTH):
        ctx_out = layer < DEPTH - 1
        mod = mod_all[layer]
        w_main, w_misc = _reorder_w_in(w_in[layer][:, :MIX_IN])
        w_gates = w_in[layer][:, MIX_IN:].astype(BF16)
        w_br = w_branch[layer].astype(BF16)
        w_o = w_out[layer].astype(BF16)
        wg, wu, wd = (t[layer].astype(BF16) for t in (moe_w_gate, moe_w_up, moe_w_down))

        hl = norm_modulate(xl, norm1_w[layer], mod[:, 0], mod[:, 1], seq)
        hc = norm_modulate(xc, norm1_w[layer], mod[:, 0], mod[:, 1], None)
        main_l, misc_l = matmul(hl, w_main, BF16, 512), matmul(hl, w_misc, F32, MISC_W)
        main_c, misc_c = matmul(hc, w_main, BF16, 512), matmul(hc, w_misc, F32, MISC_W)

        gdn_l, gdn_c = gdn_branch(main_l, misc_l, main_c, misc_c, b, seq, ctx_len, gdn_conv_w[layer],
                                  gdn_a_log[layer], gdn_dt_bias[layer], gdn_norm_w[layer], ctx_out)

        mla_w = _mla_weights(mla_q_norm_w[layer], mla_kv_norm_w[layer], mla_w_uq[layer], mla_w_ukv[layer])
        mla_l, gqa_l, mla_c, gqa_c = _attention_branches(
            main_l, misc_l, main_c, misc_c, b, seq, ctx_len, ctx_out, rope_mla, rope_gqa,
            mla_w, mla_q_norm_w[layer], mla_kv_norm_w[layer], gqa_q_norm_w[layer], gqa_k_norm_w[layer])

        hy_params = (hy_w1[layer], hy_b1[layer], hy_w2[layer], hy_b2[layer], hy_w3[layer], hy_sin_freq[layer])
        hy_l = hyena_branch(main_l, b, seq, hy_conv_w[layer], hyena_filters(seq, *hy_params), hy_bias[layer])

        branches_l = [gdn_l.reshape(b * seq, BRANCH_W).astype(BF16), mla_l, gqa_l,
                      hy_l.reshape(b * seq, BRANCH_W).astype(BF16)]
        merged_l = merge_branches_gated(hl, branches_l, w_gates, w_br)

        if ctx_out:
            hy_c = hyena_branch(main_c, b, ctx_len, hy_conv_w[layer], hyena_filters(ctx_len, *hy_params),
                                hy_bias[layer])
            branches_c = [gdn_c.reshape(b * ctx_len, BRANCH_W).astype(BF16), mla_c, gqa_c,
                          hy_c.reshape(b * ctx_len, BRANCH_W).astype(BF16)]
            merged_c = merge_branches_gated(hc, branches_c, w_gates, w_br)
            xc = matmul_gated_residual(merged_c, w_o, xc, mod[:, 2], None)
            xc = moe_layer(xc, norm2_w[layer], mod, None, w_router, router_bias, wg, wu, wd)

        xl = matmul_gated_residual(merged_l, w_o, xl, mod[:, 2], seq)
        xl = moe_layer(xl, norm2_w[layer], mod, seq, w_router, router_bias, wg, wu, wd)
    return final_rms_norm(xl, final_norm_w).reshape(b, seq, d)
```

```python
import math, functools
import jax, jax.numpy as jnp
from jax import lax
import numpy as np
from jax.experimental import pallas as pl
from jax.experimental.pallas import tpu as pltpu

D_MODEL = 2048
BATCH = 4
SEQ = 4096
DEPTH = 2

GRID_W = 64
CTX_LEN = 256
N_BRANCH = 4
BRANCH_W = 512
NORM_EPS = 1e-6
Q_BLOCK = 128
ROPE_THETA = 10000.0
SHORT_CONV = 3

GDN_HEADS = 4
GDN_HEAD_DIM = 128
GDN_CHUNK = 64

MLA_HEADS = 4
MLA_Q_LORA = 512
MLA_KV_LORA = 256
MLA_NOPE = 128
MLA_ROPE = 64
MLA_V = 128

GQA_HEADS = 8
GQA_KV_HEADS = 2
GQA_HEAD_DIM = 64

HY_WIDTH = 512
HY_ORDER = 2
HY_EMB = 33
HY_HIDDEN = 64
HY_DECAY_TARGET = 1e-2
HY_FAST_DECAY = 0.3
HY_SLOW_DECAY = 1.5

N_EXPERTS = 16
N_GROUPS = 4
EXPERTS_PER_GROUP = N_EXPERTS // N_GROUPS
TOP_K = 2
D_EXPERT = 512

GDN_W = GDN_HEADS * GDN_HEAD_DIM
IN_WIDTHS = (GDN_W, GDN_W, GDN_W, GDN_W, 2 * GDN_HEADS, 2 * GDN_HEADS,
             MLA_Q_LORA, MLA_KV_LORA, MLA_ROPE,
             GQA_HEADS * GQA_HEAD_DIM, GQA_KV_HEADS * GQA_HEAD_DIM, GQA_KV_HEADS * GQA_HEAD_DIM,
             (HY_ORDER + 1) * HY_WIDTH)
MIX_IN = sum(IN_WIDTHS)
IN_DIM = MIX_IN + N_BRANCH * D_MODEL

F32 = jnp.float32
BF16 = jnp.bfloat16
LANES = 128
MOD_ROWS = 8
CTX_MOD_ROW = BATCH
MOE_TILE = 512
VMEM_LIMIT = 56 << 20

MAIN_W = 5120
OFF_GDN, OFF_CQ, OFF_GQ, OFF_HY, OFF_CKV, OFF_GK, OFF_GV = 0, 2048, 2560, 3072, 4608, 4864, 4992
MISC_W = LANES


def _cp(sem):
    return pltpu.CompilerParams(dimension_semantics=sem, vmem_limit_bytes=VMEM_LIMIT)


def _sigmoid(v):
    return 0.5 * jnp.tanh(0.5 * v) + 0.5


def rms_norm(x, w):
    xf = x.astype(jnp.float32)
    y = xf * lax.rsqrt(jnp.mean(xf * xf, axis=-1, keepdims=True) + NORM_EPS)
    return (y * w.astype(jnp.float32)).astype(x.dtype)


def l2_normalize(x):
    xf = x.astype(jnp.float32)
    return xf * lax.rsqrt(jnp.sum(xf * xf, axis=-1, keepdims=True) + NORM_EPS)


def depthwise_conv_centred(u, w):
    k = w.shape[0]
    return lax.conv_general_dilated(u, w[:, None, :].astype(u.dtype), window_strides=(1,),
                                    padding=[(k // 2, k // 2)],
                                    dimension_numbers=('NWC', 'WIO', 'NWC'),
                                    feature_group_count=u.shape[-1])


def axial_rope_tables(rows, rot_dim):
    n_freq = rot_dim // 4
    freqs = ROPE_THETA ** (-jnp.arange(n_freq, dtype=jnp.float32) / n_freq)
    row = jnp.repeat(jnp.arange(rows, dtype=jnp.float32), GRID_W)
    col = jnp.tile(jnp.arange(GRID_W, dtype=jnp.float32), rows)
    ang = jnp.concatenate([row[:, None] * freqs, col[:, None] * freqs], axis=-1)
    return jnp.cos(ang), jnp.sin(ang)


def gdn_prep(q, k, v, a, bt, conv_w, a_log, dt_bias):
    b, l = q.shape[:2]
    qkv = jax.nn.silu(depthwise_conv_centred(jnp.concatenate([q, k, v], axis=-1), conv_w)).astype(jnp.float32)
    q, k, v = jnp.split(qkv, 3, axis=-1)
    hd = (b, l, GDN_HEADS, GDN_HEAD_DIM)
    q = l2_normalize(q.reshape(hd)) * GDN_HEAD_DIM ** -0.5
    k = l2_normalize(k.reshape(hd))
    v = v.reshape(hd)
    a = a.astype(jnp.float32).reshape(b, l, 2, GDN_HEADS)
    g = -jnp.exp(a_log.astype(jnp.float32)) * jax.nn.softplus(a + dt_bias.astype(jnp.float32))
    beta = jax.nn.sigmoid(bt.astype(jnp.float32).reshape(b, l, 2, GDN_HEADS))
    return q, k, v, g, beta


def gated_delta_rule(q, k, v, g, beta, state, with_out):
    b, l, h, _ = q.shape
    dv = v.shape[-1]
    c = GDN_CHUNK
    n = l // c

    def to_chunks(t):
        t = t.reshape(b, n, c, h, *t.shape[3:])
        return jnp.moveaxis(t, (1, 3), (0, 2))

    qc, kc, vc, bc = to_chunks(q), to_chunks(k), to_chunks(v), to_chunks(beta)
    gc = jnp.cumsum(to_chunks(g), axis=-1)
    idx = jnp.arange(c)
    lower = idx[:, None] >= idx[None, :]
    strict = idx[:, None] > idx[None, :]
    diff = gc[..., :, None] - gc[..., None, :]
    decay = jnp.where(lower, jnp.exp(jnp.where(lower, diff, 0.0)), 0.0)
    kb = kc * bc[..., None]
    a = jnp.where(strict, jnp.einsum('nbhid,nbhjd->nbhij', kb, kc) * decay, 0.0)
    solve = functools.partial(lax.linalg.triangular_solve, left_side=True, lower=True, unit_diagonal=True)
    u = solve(a, vc * bc[..., None])
    w = solve(a, kb * jnp.exp(gc)[..., None])
    g_last = gc[..., -1]
    k_dec = kc * jnp.exp(g_last[..., None] - gc)[..., None]
    xs = (u, w, k_dec, g_last)
    if with_out:
        qk = jnp.where(lower, jnp.einsum('nbhid,nbhjd->nbhij', qc, kc) * decay, 0.0)
        xs = xs + (qc * jnp.exp(gc)[..., None], qk)

    def step(s, inp):
        u_i, w_i, kd_i, gl_i = inp[:4]
        v_new = u_i - jnp.einsum('bhck,bhkv->bhcv', w_i, s)
        s_new = s * jnp.exp(gl_i)[..., None, None] + jnp.einsum('bhck,bhcv->bhkv', kd_i, v_new)
        if not with_out:
            return s_new, None
        qd_i, qk_i = inp[4:]
        o = jnp.einsum('bhck,bhkv->bhcv', qd_i, s) + jnp.einsum('bhij,bhjv->bhiv', qk_i, v_new)
        return s_new, o

    state, o = lax.scan(step, state, xs)
    if not with_out:
        return None, state
    o = jnp.moveaxis(o, (0, 2), (1, 3)).reshape(b, l, h, dv)
    return o, state


def gdn_output(o, z, norm_w):
    b, l = z.shape[:2]
    zh = z.reshape(b, l, GDN_HEADS, GDN_HEAD_DIM).astype(jnp.float32)
    y = rms_norm(o, norm_w) * jax.nn.silu(zh)
    return y.reshape(b, l, GDN_W).astype(z.dtype)


def gdn_mixer(p_lat, p_ctx, conv_w, a_log, dt_bias, norm_w, ctx_out):
    lat = gdn_prep(p_lat[0], p_lat[1], p_lat[2], p_lat[4], p_lat[5], conv_w, a_log, dt_bias)
    ctx = gdn_prep(p_ctx[0], p_ctx[1], p_ctx[2], p_ctx[4], p_ctx[5], conv_w, a_log, dt_bias)
    b = p_lat[0].shape[0]
    s0 = jnp.zeros((b, GDN_HEADS, GDN_HEAD_DIM, GDN_HEAD_DIM), jnp.float32)
    o_lat, o_ctx = 0.0, 0.0
    for direction in range(2):
        flip = (lambda t: t[:, ::-1]) if direction else (lambda t: t)

        def seq_args(s):
            q, k, v, g, beta = s
            return flip(q), flip(k), flip(v), flip(g[:, :, direction]), flip(beta[:, :, direction])

        oc, s_ctx = gated_delta_rule(*seq_args(ctx), s0, ctx_out)
        ol, _ = gated_delta_rule(*seq_args(lat), s_ctx, True)
        o_lat = o_lat + flip(ol)
        if ctx_out:
            o_ctx = o_ctx + flip(oc)
    out_lat = gdn_output(o_lat, p_lat[3], norm_w)
    out_ctx = gdn_output(o_ctx, p_ctx[3], norm_w) if ctx_out else None
    return out_lat, out_ctx


def hyena_filters(length, w1, b1, w2, b2, w3, sin_freq):
    t = jnp.arange(length, dtype=jnp.float32)
    bands = (HY_EMB - 1) // 2
    f = jnp.linspace(1e-4, bands - 1, bands, dtype=jnp.float32)
    phase = (2.0 * math.pi / length) * t[:, None] * f[None, :]
    feats = jnp.concatenate([t[:, None] / (length - 1), jnp.cos(phase), -jnp.sin(phase)], axis=-1)
    hid = jnp.sin(sin_freq[0] * (feats @ w1 + b1))
    hid = jnp.sin(sin_freq[1] * (hid @ w2 + b2))
    filt = (hid @ w3).astype(jnp.float32)
    centre = length // 2
    dist = jnp.abs(t - centre) / centre
    deltas = jnp.abs(jnp.linspace(math.log(HY_DECAY_TARGET) / HY_SLOW_DECAY,
                                  math.log(HY_DECAY_TARGET) / HY_FAST_DECAY,
                                  HY_ORDER * HY_WIDTH, dtype=jnp.float32))
    filt = filt * jnp.exp(-dist[:, None] * deltas[None, :])
    filt = filt / jnp.sum(jnp.abs(filt), axis=0, keepdims=True)
    return filt.reshape(length, HY_ORDER, HY_WIDTH)


def fft_conv_centred(u, h):
    l = u.shape[1]
    n = 2 * l
    uf = jnp.fft.rfft(u.astype(jnp.float32), n=n, axis=1)
    hf = jnp.fft.rfft(h.astype(jnp.float32), n=n, axis=0)
    y = jnp.fft.irfft(uf * hf[None], n=n, axis=1)
    return y[:, l // 2: l // 2 + l]


def hyena_mixer(u, conv_w, filt, bias):
    parts = jnp.split(depthwise_conv_centred(u, conv_w).astype(jnp.float32), HY_ORDER + 1, axis=-1)
    z = parts[0]
    for o in range(HY_ORDER):
        z = parts[o + 1] * (fft_conv_centred(z, filt[:, o]) + bias[o] * z)
    return z


HALO_ROWS = 16
GDN_BATCHES_PER_STEP = 4


def _conv3(x, prev_row, next_row, w_ref):
    tm = x.shape[0]
    rows = lax.broadcasted_iota(jnp.int32, x.shape, 0)
    up = jnp.where(rows == 0, prev_row, pltpu.roll(x, 1, 0))
    dn = jnp.where(rows == tm - 1, next_row, pltpu.roll(x, tm - 1, 0))
    return w_ref[0:1, :] * up + w_ref[1:2, :] * x + w_ref[2:3, :] * dn


def _halo_rows(xp_ref, xn_ref):
    i = pl.program_id(1)
    prev = jnp.where(i == 0, 0.0, xp_ref[HALO_ROWS - 1:HALO_ROWS, :].astype(F32))
    nxt = jnp.where(i == pl.num_programs(1) - 1, 0.0, xn_ref[0:1, :].astype(F32))
    return prev, nxt


def _halo_specs(tm, width, col_block, nt, n_rows):
    per = tm // HALO_ROWS
    last = n_rows // HALO_ROWS - 1
    return [pl.BlockSpec((tm, width), lambda bi, i: (bi * nt + i, col_block)),
            pl.BlockSpec((HALO_ROWS, width), lambda bi, i: (jnp.maximum((bi * nt + i) * per - 1, 0), col_block)),
            pl.BlockSpec((HALO_ROWS, width), lambda bi, i: (jnp.minimum((bi * nt + i + 1) * per, last), col_block))]


def _split3(v):
    hi = v.astype(BF16)
    r1 = v - hi.astype(F32)
    mid = r1.astype(BF16)
    lo = (r1 - mid.astype(F32)).astype(BF16)
    return hi, mid, lo


def _gdn_prep_body(x_ref, xp_ref, xn_ref, misc_ref, cw_ref, alog_ref, dt_ref, gmask_ref, tp_ref, ts_ref,
                   q_out, k_out, v_out, gcf_out, gcb_out, beta_out):
    prev, nxt = _halo_rows(xp_ref, xn_ref)
    y = _conv3(x_ref[...].astype(F32), prev, nxt, cw_ref)
    y = y * _sigmoid(y)
    hd = GDN_HEAD_DIM
    for h in range(GDN_HEADS):
        qh = y[:, h * hd:(h + 1) * hd]
        kh = y[:, GDN_W + h * hd:GDN_W + (h + 1) * hd]
        qn = qh * lax.rsqrt(jnp.sum(qh * qh, axis=-1, keepdims=True) + NORM_EPS) * hd ** -0.5
        kn = kh * lax.rsqrt(jnp.sum(kh * kh, axis=-1, keepdims=True) + NORM_EPS)
        q_out[:, h * hd:(h + 1) * hd] = qn.astype(BF16)
        k_out[:, h * hd:(h + 1) * hd] = kn.astype(BF16)
    v_out[...] = y[:, 2 * GDN_W:3 * GDN_W].astype(BF16)
    m = misc_ref[...]
    a = m + dt_ref[...]
    softplus = jnp.maximum(a, 0.0) + jnp.log(1.0 + jnp.exp(-jnp.abs(a)))
    g = -(jnp.exp(alog_ref[...]) * gmask_ref[...]) * softplus
    beta_out[...] = _sigmoid(m)
    parts = _split3(g)
    gcf_out[...] = sum(jnp.dot(tp_ref[...], p, preferred_element_type=F32) for p in parts)
    gcb_out[...] = sum(jnp.dot(ts_ref[...], p, preferred_element_type=F32) for p in parts)


def gdn_prepare(main, misc, b, length, conv_w, a_log, dt_bias):
    n = b * length
    w3 = 3 * GDN_W
    tm = min(256, length)
    nt = length // tm
    lane0 = MLA_ROPE
    vec = lambda v: jnp.zeros((1, MISC_W), F32).at[0, lane0:lane0 + 2 * GDN_HEADS].set(v.reshape(-1))
    alog, dtb = vec(a_log), vec(dt_bias)
    gmask = vec(jnp.ones((2 * GDN_HEADS,), F32))
    r = np.arange(tm)
    same = (r[:, None] // GDN_CHUNK) == (r[None, :] // GDN_CHUNK)
    tpre = jnp.asarray(same & (r[None, :] <= r[:, None]), BF16)
    tsuf = jnp.asarray(same & (r[None, :] >= r[:, None]), BF16)
    const = lambda a: pl.BlockSpec(a.shape, lambda bi, i: (0,) * a.ndim)
    row = lambda width: pl.BlockSpec((tm, width), lambda bi, i: (bi * nt + i, 0))
    cw = conv_w.astype(F32)
    return pl.pallas_call(
        _gdn_prep_body,
        grid=(b, nt),
        in_specs=_halo_specs(tm, w3, OFF_GDN // w3, nt, n)
                 + [row(MISC_W), const(cw), const(alog), const(dtb), const(gmask), const(tpre), const(tsuf)],
        out_specs=[row(GDN_W), row(GDN_W), row(GDN_W), row(MISC_W), row(MISC_W), row(MISC_W)],
        out_shape=[jax.ShapeDtypeStruct((n, GDN_W), BF16)] * 3 + [jax.ShapeDtypeStruct((n, MISC_W), F32)] * 3,
        compiler_params=_cp(("parallel", "parallel")),
        name="gdn_prep",
    )(main, main, main, misc, cw, alog, dtb, gmask, tpre, tsuf)


def _gdn_chunk_body(qf_ref, kf_ref, vf_ref, qb_ref, kb_ref, vb_ref, gcf_ref, gcb_ref, bcf_ref, bcb_ref,
                    grf_ref, grb_ref, s0_ref, *rest, nc, with_out, bpb):
    if with_out:
        of_ref, ob_ref, sfin_ref, s_ref = rest
    else:
        sfin_ref, s_ref = rest
        of_ref = ob_ref = None
    c = pl.program_id(1)
    nst = 2 * GDN_HEADS

    @pl.when(c == 0)
    def _():
        s_ref[...] = s0_ref[...].reshape(s_ref.shape)

    ch = GDN_CHUNK
    hd = GDN_HEAD_DIM
    ii = lax.broadcasted_iota(jnp.int32, (ch, ch), 0)
    jj = lax.broadcasted_iota(jnp.int32, (ch, ch), 1)
    nt_dims = (((1,), (1,)), ((), ()))
    tn_dims = (((0,), (0,)), ((), ()))
    bdot = lambda a, b_: jnp.dot(a.astype(BF16), b_.astype(BF16), preferred_element_type=F32)
    eye = jnp.where(ii == jj, 1.0, 0.0)
    pair_masks = [((ii >> (l + 1)) == (jj >> (l + 1))) & ((ii >> l) != (jj >> l))
                  for l in range(int(math.log2(ch)))]
    dirs = ((qf_ref, kf_ref, vf_ref, gcf_ref, bcf_ref, grf_ref, of_ref, ii >= jj, ii > jj, ch - 1),
            (qb_ref, kb_ref, vb_ref, gcb_ref, bcb_ref, grb_ref, ob_ref, ii <= jj, ii < jj, 0))
    chains = []
    for bb in range(bpb):
        for d, (q_ref, k_ref, v_ref, gc_ref, bc_ref, gr_ref, o_ref, incl, strict, last_row) in enumerate(dirs):
            for h in range(GDN_HEADS):
                j = d * GDN_HEADS + h
                cols = slice(h * hd, (h + 1) * hd)
                cn = dict(bb=bb, j=j, cols=cols, o_ref=o_ref, incl=incl, strict=strict)
                cn['q'], cn['k'], cn['v'] = q_ref[bb, :, cols], k_ref[bb, :, cols], v_ref[bb, :, cols]
                cn['gc'] = gc_ref[bb, :, j:j + 1]
                cn['gr'] = gr_ref[bb, 0, j:j + 1, :]
                cn['beta'] = bc_ref[bb, :, j:j + 1]
                cn['g_last'] = gc_ref[bb, last_row:last_row + 1, j:j + 1]
                chains.append(cn)
    for cn in chains:
        incl = cn['incl']
        cn['decay'] = jnp.where(incl, jnp.exp(jnp.where(incl, cn['gc'] - cn['gr'], 0.0)), 0.0)
        cn['kf'] = cn['k'].astype(F32)
        cn['kbeta'] = cn['kf'] * cn['beta']
    for cn in chains:
        kk = lax.dot_general(cn['kbeta'].astype(BF16), cn['k'], nt_dims, preferred_element_type=F32)
        cn['a'] = jnp.where(cn['strict'], kk * cn['decay'], 0.0)
    for cn in chains:
        cn['t'] = eye - jnp.where(pair_masks[0], cn['a'], 0.0)
    for pm in pair_masks[1:]:
        for cn in chains:
            cn['tmp'] = bdot(cn['t'], jnp.where(pm, cn['a'], 0.0))
        for cn in chains:
            cn['t'] = cn['t'] - bdot(cn['tmp'], cn['t'])
    for cn in chains:
        rhs = jnp.concatenate([cn['v'].astype(F32) * cn['beta'], cn['kbeta'] * jnp.exp(cn['gc'])], axis=1)
        cn['x'] = bdot(cn['t'], rhs)
    for cn in chains:
        cn['s'] = s_ref[cn['bb'] * nst + cn['j']]
        cn['v_new'] = cn['x'][:, :hd] - bdot(cn['x'][:, hd:], cn['s'])
    if with_out:
        for cn in chains:
            qk = lax.dot_general(cn['q'], cn['k'], nt_dims, preferred_element_type=F32)
            qk = jnp.where(cn['incl'], qk * cn['decay'], 0.0)
            o = bdot(cn['q'].astype(F32) * jnp.exp(cn['gc']), cn['s']) + bdot(qk, cn['v_new'])
            cn['o_ref'][cn['bb'], :, cn['cols']] = o
    for cn in chains:
        kdec = cn['kf'] * jnp.exp(cn['g_last'] - cn['gc'])
        s_ref[cn['bb'] * nst + cn['j']] = cn['s'] * jnp.exp(cn['g_last']) + lax.dot_general(
            kdec.astype(BF16), cn['v_new'].astype(BF16), tn_dims, preferred_element_type=F32)

    @pl.when(c == nc - 1)
    def _():
        sfin_ref[...] = s_ref[...].reshape(sfin_ref.shape)


def gdn_scan(q, k, v, gcol, bcol, grow, s0, b, length, with_out):
    n = b * length
    ch = GDN_CHUNK
    nc = length // ch
    nst = 2 * GDN_HEADS
    bpb = min(GDN_BATCHES_PER_STEP, b)
    fwd = lambda bg, c: (bg, c, 0)
    bwd = lambda bg, c: (bg, nc - 1 - c, 0)
    fwd4 = lambda bg, c: (bg, c, 0, 0)
    bwd4 = lambda bg, c: (bg, nc - 1 - c, 0, 0)
    wide = lambda f: pl.BlockSpec((bpb, ch, GDN_W), f)
    narrow = lambda f: pl.BlockSpec((bpb, ch, nst), f)
    rows = lambda f: pl.BlockSpec((bpb, 1, nst, ch), f)
    state = pl.BlockSpec((bpb, nst, GDN_HEAD_DIM, GDN_HEAD_DIM), lambda bg, c: (bg, 0, 0, 0))
    out_specs = [state]
    out_shape = [jax.ShapeDtypeStruct((b, nst, GDN_HEAD_DIM, GDN_HEAD_DIM), F32)]
    if with_out:
        out_specs = [wide(fwd), wide(bwd)] + out_specs
        out_shape = [jax.ShapeDtypeStruct((b, length, GDN_W), F32)] * 2 + out_shape
    body = functools.partial(_gdn_chunk_body, nc=nc, with_out=with_out, bpb=bpb)
    q3, k3, v3 = (t.reshape(b, length, GDN_W) for t in (q, k, v))
    gcol3, bcol3 = gcol.reshape(b, length, nst), bcol.reshape(b, length, nst)
    grow4 = grow.reshape(b, nc, nst, ch)
    outs = pl.pallas_call(
        body,
        grid=(b // bpb, nc),
        in_specs=[wide(fwd), wide(fwd), wide(fwd), wide(bwd), wide(bwd), wide(bwd),
                  narrow(fwd), narrow(bwd), narrow(fwd), narrow(bwd), rows(fwd4), rows(bwd4), state],
        out_specs=out_specs,
        out_shape=out_shape,
        scratch_shapes=[pltpu.VMEM((bpb * nst, GDN_HEAD_DIM, GDN_HEAD_DIM), F32)],
        compiler_params=_cp(("parallel", "arbitrary")),
        name="gdn_scan",
    )(q3, k3, v3, q3, k3, v3, gcol3, gcol3, bcol3, bcol3, grow4, grow4, s0)
    if with_out:
        return outs[0].reshape(n, GDN_W), outs[1].reshape(n, GDN_W), outs[2]
    return outs


def _gdn_out_body(of_ref, ob_ref, z_ref, nw_ref, y_ref):
    o = of_ref[...] + ob_ref[...]
    z = z_ref[...].astype(F32)
    hd = GDN_HEAD_DIM
    for h in range(GDN_HEADS):
        cols = slice(h * hd, (h + 1) * hd)
        oh = o[:, cols]
        yh = oh * lax.rsqrt(jnp.mean(oh * oh, axis=-1, keepdims=True) + NORM_EPS) * nw_ref[...]
        zh = z[:, cols]
        y_ref[:, cols] = (yh * (zh * _sigmoid(zh))).astype(BF16)


def gdn_output_gate(o_f, o_b, main, norm_w):
    n = o_f.shape[0]
    tm = min(512, n)
    nw = norm_w.reshape(1, GDN_HEAD_DIM).astype(F32)
    return pl.pallas_call(
        _gdn_out_body,
        grid=(n // tm,),
        in_specs=[pl.BlockSpec((tm, GDN_W), lambda i: (i, 0)),
                  pl.BlockSpec((tm, GDN_W), lambda i: (i, 0)),
                  pl.BlockSpec((tm, GDN_W), lambda i: (i, (OFF_GDN + 3 * GDN_W) // GDN_W)),
                  pl.BlockSpec((1, GDN_HEAD_DIM), lambda i: (0, 0))],
        out_specs=pl.BlockSpec((tm, GDN_W), lambda i: (i, 0)),
        out_shape=jax.ShapeDtypeStruct((n, GDN_W), BF16),
        compiler_params=_cp(("parallel",)),
        name="gdn_out",
    )(o_f, o_b, main, nw)


def gdn_branch(main_l, misc_l, main_c, misc_c, b, seq, ctx_len, conv_w, a_log, dt_bias, norm_w, ctx_out):
    nst = 2 * GDN_HEADS
    lane0 = MLA_ROPE

    def gates(gcf, gcb, beta, length):
        gcol = jnp.concatenate([gcf[:, lane0:lane0 + GDN_HEADS], gcb[:, lane0 + GDN_HEADS:lane0 + nst]], axis=1)
        bcol = beta[:, lane0 + nst:lane0 + 2 * nst]
        grow = jnp.transpose(gcol.reshape(-1, GDN_CHUNK, nst), (0, 2, 1))
        return gcol, bcol, grow

    qc, kc, vc, gcf, gcb, beta = gdn_prepare(main_c, misc_c, b, ctx_len, conv_w, a_log, dt_bias)
    gcol_c, bcol_c, grow_c = gates(gcf, gcb, beta, ctx_len)
    ql, kl, vl, gcf, gcb, beta = gdn_prepare(main_l, misc_l, b, seq, conv_w, a_log, dt_bias)
    gcol_l, bcol_l, grow_l = gates(gcf, gcb, beta, seq)
    s0 = jnp.zeros((b, nst, GDN_HEAD_DIM, GDN_HEAD_DIM), F32)
    outs_c = gdn_scan(qc, kc, vc, gcol_c, bcol_c, grow_c, s0, b, ctx_len, ctx_out)
    s_ctx = outs_c[-1]
    of_l, ob_l, _ = gdn_scan(ql, kl, vl, gcol_l, bcol_l, grow_l, s_ctx, b, seq, True)
    out_l = gdn_output_gate(of_l, ob_l, main_l, norm_w)
    out_c = gdn_output_gate(outs_c[0], outs_c[1], main_c, norm_w) if ctx_out else None
    return out_l, out_c


def _hy_conv_body(x_ref, xp_ref, xn_ref, cw_ref, v_out, x1_out, x2_out):
    prev, nxt = _halo_rows(xp_ref, xn_ref)
    y = _conv3(x_ref[...].astype(F32), prev, nxt, cw_ref)
    w = HY_WIDTH
    v_out[...] = y[:, :w].astype(BF16)
    x1_out[...] = y[:, w:2 * w].astype(BF16)
    x2_out[...] = y[:, 2 * w:3 * w].astype(BF16)


def hyena_short_conv(main, b, length, conv_w):
    n = b * length
    w3 = (HY_ORDER + 1) * HY_WIDTH
    tm = min(256, length)
    nt = length // tm
    cw = conv_w.astype(F32)
    row = pl.BlockSpec((tm, HY_WIDTH), lambda bi, i: (bi * nt + i, 0))
    return pl.pallas_call(
        _hy_conv_body,
        grid=(b, nt),
        in_specs=_halo_specs(tm, w3, OFF_HY // w3, nt, n) + [pl.BlockSpec(cw.shape, lambda bi, i: (0, 0))],
        out_specs=[row, row, row],
        out_shape=[jax.ShapeDtypeStruct((n, HY_WIDTH), BF16)] * 3,
        compiler_params=_cp(("parallel", "parallel")),
        name="hyena_conv",
    )(main, main, main, cw)


def _dft_consts(length):
    n = 2 * length
    n2 = 64 if length >= 2048 else 16
    n1 = n // n2
    nk1 = n1 // 2 + 8
    k1 = np.arange(nk1)
    t1 = np.arange(n1 // 2)
    ang1 = 2.0 * np.pi * np.outer(k1, t1) / n1
    f_first = np.concatenate([np.cos(ang1), -np.sin(ang1)], axis=0)
    t2 = np.arange(n2)
    ang2 = 2.0 * np.pi * np.outer(t2, t2) / n2
    c2, s2 = np.cos(ang2), np.sin(ang2)
    g_fwd = np.block([[c2, s2], [-s2, c2]])
    g_inv = g_fwd.T
    angt = 2.0 * np.pi * np.outer(k1, t2) / n
    tw_r, tw_i = np.cos(angt)[:, :, None], -np.sin(angt)[:, :, None]
    tt = np.arange(n1 // 4, 3 * n1 // 4)
    ang3 = 2.0 * np.pi * np.outer(tt, k1) / n1
    fold = np.where((k1 == 0) | (k1 == n1 // 2), 1.0, np.where(k1 < n1 // 2, 2.0, 0.0))[None, :]
    f_last = np.concatenate([np.cos(ang3) * fold, -np.sin(ang3) * fold], axis=1) / n
    bf = lambda a: jnp.asarray(a, BF16)
    return dict(n1=n1, nk1=nk1, n2=n2, f_first=bf(f_first), g_fwd=bf(g_fwd), g_inv=bf(g_inv),
                tw_r=jnp.asarray(tw_r, F32), tw_i=jnp.asarray(tw_i, F32), f_last=bf(f_last))


def _hy_first_body(f_ref, z_ref, a_ref):
    a_ref[0] = jnp.dot(f_ref[...], z_ref[0], preferred_element_type=F32).astype(BF16)


def hyena_dft_first(zv, consts):
    b, half, cols = zv.shape
    n1 = consts['nk1']
    tn = min(4096, cols)
    f = consts['f_first']
    return pl.pallas_call(
        _hy_first_body,
        grid=(b, cols // tn),
        in_specs=[pl.BlockSpec(f.shape, lambda bi, j: (0, 0)),
                  pl.BlockSpec((1, half, tn), lambda bi, j: (bi, 0, j))],
        out_specs=pl.BlockSpec((1, 2 * n1, tn), lambda bi, j: (bi, 0, j)),
        out_shape=jax.ShapeDtypeStruct((b, 2 * n1, cols), BF16),
        compiler_params=_cp(("parallel", "parallel")),
        name="hyena_dft_first",
    )(f, zv)


def _hy_mid_body(a_ref, twr_ref, twi_ref, gf_ref, *rest, kt, spectrum_only):
    if spectrum_only:
        (o_ref,) = rest
    else:
        gi_ref, h_ref, o_ref = rest
    n2 = gf_ref.shape[0] // 2

    def one(i, carry):
        ar = a_ref[0, 0, i].astype(F32)
        ai = a_ref[0, 1, i].astype(F32)
        twr, twi = twr_ref[i], twi_ref[i]
        br = ar * twr - ai * twi
        bi = ar * twi + ai * twr
        z = jnp.dot(gf_ref[...], jnp.concatenate([br, bi], axis=0).astype(BF16), preferred_element_type=F32)
        zr, zi = z[:n2], z[n2:]
        if spectrum_only:
            o_ref[0, 0, i] = zr
            o_ref[0, 1, i] = zi
            return carry
        hr, hi = h_ref[0, i], h_ref[1, i]
        yr = zr * hr - zi * hi
        yi = zr * hi + zi * hr
        w = jnp.dot(gi_ref[...], jnp.concatenate([yr, yi], axis=0).astype(BF16), preferred_element_type=F32)
        wr, wi = w[:n2], w[n2:]
        o_ref[0, 0, i] = (wr * twr + wi * twi).astype(BF16)
        o_ref[0, 1, i] = (wi * twr - wr * twi).astype(BF16)
        return carry

    lax.fori_loop(0, kt, one, 0)


def hyena_dft_mid(a5, consts, spectrum=None):
    b, _, n1, n2, c = a5.shape
    kt = 8
    only = spectrum is None
    blk = pl.BlockSpec((1, 2, kt, n2, c), lambda bi, j: (bi, 0, j, 0, 0))
    tw = pl.BlockSpec((kt, n2, 1), lambda bi, j: (j, 0, 0))
    g = pl.BlockSpec((2 * n2, 2 * n2), lambda bi, j: (0, 0))
    in_specs = [blk, tw, tw, g]
    args = [a5, consts['tw_r'], consts['tw_i'], consts['g_fwd']]
    if not only:
        in_specs += [g, pl.BlockSpec((2, kt, n2, c), lambda bi, j: (0, j, 0, 0))]
        args += [consts['g_inv'], spectrum]
    body = functools.partial(_hy_mid_body, kt=kt, spectrum_only=only)
    return pl.pallas_call(
        body,
        grid=(b, n1 // kt),
        in_specs=in_specs,
        out_specs=blk,
        out_shape=jax.ShapeDtypeStruct(a5.shape, F32 if only else BF16),
        compiler_params=_cp(("parallel", "parallel")),
        name="hyena_dft_mid",
    )(*args)


def _hy_last_body(f_ref, b_ref, z_ref, x_ref, bias_ref, o_ref):
    y = jnp.dot(f_ref[...], b_ref[0], preferred_element_type=F32)
    z = z_ref[0].astype(F32)
    o_ref[0] = (x_ref[0].astype(F32) * (y + bias_ref[...] * z)).astype(BF16)


def hyena_dft_last(bv, zv, xv, bias_row, consts):
    b, rows2, cols = bv.shape
    half = consts['n1'] // 2
    tn = min(4096, cols)
    f = consts['f_last']
    sig = pl.BlockSpec((1, half, tn), lambda bi, j: (bi, 0, j))
    return pl.pallas_call(
        _hy_last_body,
        grid=(b, cols // tn),
        in_specs=[pl.BlockSpec(f.shape, lambda bi, j: (0, 0)),
                  pl.BlockSpec((1, rows2, tn), lambda bi, j: (bi, 0, j)),
                  sig, sig,
                  pl.BlockSpec((1, tn), lambda bi, j: (0, j))],
        out_specs=sig,
        out_shape=jax.ShapeDtypeStruct((b, half, cols), BF16),
        compiler_params=_cp(("parallel", "parallel")),
        name="hyena_dft_last",
    )(f, bv, zv, xv, bias_row)


def hyena_branch(main, b, length, conv_w, filt, bias):
    consts = _dft_consts(length)
    n1, nk1, n2 = consts['n1'], consts['nk1'], consts['n2']
    c = HY_WIDTH
    cols = n2 * c
    view = lambda t: t.reshape(b, n1 // 2, cols)
    v, x1, x2 = (view(t) for t in hyena_short_conv(main, b, length, conv_w))
    hv = jnp.transpose(filt, (1, 0, 2)).astype(BF16).reshape(HY_ORDER, n1 // 2, cols)
    h_first = hyena_dft_first(hv, consts).reshape(HY_ORDER, 2, nk1, n2, c)
    spectra = hyena_dft_mid(h_first, consts)
    z = v
    for o, gate in enumerate((x1, x2)):
        a5 = hyena_dft_first(z, consts).reshape(b, 2, nk1, n2, c)
        bm = hyena_dft_mid(a5, consts, spectra[o]).reshape(b, 2 * nk1, cols)
        bias_row = jnp.tile(bias[o].astype(F32), n2).reshape(1, cols)
        z = hyena_dft_last(bm, z, gate, bias_row, consts)
    return z.reshape(b * length, c)


def _ada_body(c_ref, w_ref, b_ref, o_ref):
    cv = c_ref[...]
    s = cv * _sigmoid(cv)
    o_ref[0] = jnp.dot(s, w_ref[0], precision=lax.Precision.HIGHEST, preferred_element_type=F32) + b_ref[0]


def ada_modulation(c, c_ctx, w_ada, b_ada):
    depth, d, d6 = w_ada.shape
    c8 = jnp.zeros((MOD_ROWS, d), F32).at[:c.shape[0]].set(c).at[CTX_MOD_ROW].set(c_ctx)
    tn = 512
    out = pl.pallas_call(
        _ada_body,
        grid=(depth, d6 // tn),
        in_specs=[pl.BlockSpec((MOD_ROWS, d), lambda l, j: (0, 0)),
                  pl.BlockSpec((1, d, tn), lambda l, j: (l, 0, j)),
                  pl.BlockSpec((1, 1, tn), lambda l, j: (l, 0, j))],
        out_specs=pl.BlockSpec((1, MOD_ROWS, tn), lambda l, j: (l, 0, j)),
        out_shape=jax.ShapeDtypeStruct((depth, MOD_ROWS, d6), F32),
        compiler_params=_cp(("parallel", "parallel")),
        name="ada_mod",
    )(c8, w_ada, b_ada.reshape(depth, 1, d6))
    return out.reshape(depth, MOD_ROWS, 6, d)


def _row_tile(cap, n_rows, per_batch):
    return min(cap, n_rows if per_batch is None else per_batch)


def _mod_row_fn(n_rows, tm, per_batch):
    if per_batch is None:
        return lambda i: CTX_MOD_ROW
    tiles = per_batch // tm
    return lambda i: i // tiles


def _normmod_body(x_ref, nw_ref, sh_ref, sc_ref, o_ref):
    xf = x_ref[...]
    y = xf * lax.rsqrt(jnp.mean(xf * xf, axis=-1, keepdims=True) + NORM_EPS) * nw_ref[...]
    o_ref[...] = (y * (1.0 + sc_ref[0]) + sh_ref[0]).astype(o_ref.dtype)


def norm_modulate(x, nw, shift, scale, per_batch, out_dtype=BF16):
    n, d = x.shape
    tm = _row_tile(512, n, per_batch)
    rf = _mod_row_fn(n, tm, per_batch)
    return pl.pallas_call(
        _normmod_body,
        grid=(n // tm,),
        in_specs=[pl.BlockSpec((tm, d), lambda i: (i, 0)),
                  pl.BlockSpec((1, d), lambda i: (0, 0)),
                  pl.BlockSpec((1, 1, d), lambda i: (rf(i), 0, 0)),
                  pl.BlockSpec((1, 1, d), lambda i: (rf(i), 0, 0))],
        out_specs=pl.BlockSpec((tm, d), lambda i: (i, 0)),
        out_shape=jax.ShapeDtypeStruct((n, d), out_dtype),
        compiler_params=_cp(("parallel",)),
        name="norm_mod",
    )(x, nw.reshape(1, d), shift.reshape(MOD_ROWS, 1, d), scale.reshape(MOD_ROWS, 1, d))


def _mm_body(a_ref, w_ref, o_ref):
    o_ref[...] = jnp.dot(a_ref[...], w_ref[...], preferred_element_type=F32).astype(o_ref.dtype)


def matmul(a, w, out_dtype, tn):
    n, k = a.shape
    m = w.shape[1]
    tm = min(2048, n)
    return pl.pallas_call(
        _mm_body,
        grid=(n // tm, m // tn),
        in_specs=[pl.BlockSpec((tm, k), lambda i, j: (i, 0)),
                  pl.BlockSpec((k, tn), lambda i, j: (0, j))],
        out_specs=pl.BlockSpec((tm, tn), lambda i, j: (i, j)),
        out_shape=jax.ShapeDtypeStruct((n, m), out_dtype),
        compiler_params=_cp(("parallel", "parallel")),
        name="proj",
    )(a, w)


def _mm_res_body(a_ref, w_ref, x_ref, g_ref, o_ref):
    y = jnp.dot(a_ref[...], w_ref[...], preferred_element_type=F32)
    o_ref[...] = x_ref[...] + g_ref[0] * y


def matmul_gated_residual(a, w, x, gate, per_batch):
    n, k = a.shape
    d = w.shape[1]
    tm = _row_tile(1024, n, per_batch)
    tn = min(512, d)
    rf = _mod_row_fn(n, tm, per_batch)
    return pl.pallas_call(
        _mm_res_body,
        grid=(n // tm, d // tn),
        in_specs=[pl.BlockSpec((tm, k), lambda i, j: (i, 0)),
                  pl.BlockSpec((k, tn), lambda i, j: (0, j)),
                  pl.BlockSpec((tm, tn), lambda i, j: (i, j)),
                  pl.BlockSpec((1, 1, tn), lambda i, j: (rf(i), 0, j))],
        out_specs=pl.BlockSpec((tm, tn), lambda i, j: (i, j)),
        out_shape=jax.ShapeDtypeStruct((n, d), F32),
        compiler_params=_cp(("parallel", "parallel")),
        name="out_proj_residual",
    )(a, w, x, gate.reshape(MOD_ROWS, 1, d))


def _merge_body(h_ref, b0_ref, b1_ref, b2_ref, b3_ref, wg_ref, wb_ref, o_ref, acc_ref):
    n = pl.program_id(2)

    @pl.when(n == 0)
    def _():
        acc_ref[...] = jnp.zeros_like(acc_ref)

    gate = _sigmoid(jnp.dot(h_ref[...], wg_ref[...], preferred_element_type=F32))
    for idx, b_ref in enumerate((b0_ref, b1_ref, b2_ref, b3_ref)):
        @pl.when(n == idx)
        def _(b_ref=b_ref):
            acc_ref[...] += gate * jnp.dot(b_ref[...], wb_ref[0], preferred_element_type=F32)

    @pl.when(n == N_BRANCH - 1)
    def _():
        o_ref[...] = acc_ref[...].astype(o_ref.dtype)


def merge_branches_gated(h, branches, w_gate, w_branch):
    n, d = h.shape
    bw = branches[0].shape[1]
    tm = min(1024, n)
    tn = min(512, d)
    nj = d // tn
    return pl.pallas_call(
        _merge_body,
        grid=(n // tm, nj, N_BRANCH),
        in_specs=[pl.BlockSpec((tm, d), lambda i, j, b: (i, 0))]
                 + [pl.BlockSpec((tm, bw), lambda i, j, b: (i, 0))] * N_BRANCH
                 + [pl.BlockSpec((d, tn), lambda i, j, b: (0, b * nj + j)),
                    pl.BlockSpec((1, bw, tn), lambda i, j, b: (b, 0, j))],
        out_specs=pl.BlockSpec((tm, tn), lambda i, j, b: (i, j)),
        out_shape=jax.ShapeDtypeStruct((n, d), BF16),
        scratch_shapes=[pltpu.VMEM((tm, tn), F32)],
        compiler_params=_cp(("parallel", "parallel", "arbitrary")),
        name="gate_merge",
    )(h, *branches, w_gate, w_branch)


def _mla_prep_body(cq_ref, ckv_ref, misc_ref, qnw_ref, kvnw_ref, wqa_ref, wqb_ref, wk_ref, wv_ref,
                   ska_ref, skb_ref, cq_tab, sq_tab, q_out, k_out, v_out):
    def norm(v, w_ref):
        vf = v.astype(F32)
        return (vf * lax.rsqrt(jnp.mean(vf * vf, axis=-1, keepdims=True) + NORM_EPS) * w_ref[...]).astype(BF16)

    xq = norm(cq_ref[...], qnw_ref)
    xkv = norm(ckv_ref[...], kvnw_ref)
    cos, sin = cq_tab[...], sq_tab[...]
    misc = misc_ref[...].astype(BF16)
    kr = (jnp.dot(misc, ska_ref[...], preferred_element_type=F32) * cos
          + jnp.dot(misc, skb_ref[...], preferred_element_type=F32) * sin)
    for h in range(MLA_HEADS):
        qa = jnp.dot(xq, wqa_ref[h], preferred_element_type=F32)
        qb = jnp.dot(xq, wqb_ref[h], preferred_element_type=F32)
        q_out[0, h] = (qa * cos + qb * sin).astype(BF16)
        k_out[0, h] = (jnp.dot(xkv, wk_ref[h], preferred_element_type=F32) + kr).astype(BF16)
        v_out[0, h] = jnp.dot(xkv, wv_ref[h], preferred_element_type=F32).astype(BF16)


def _mla_weights(q_norm_w, kv_norm_w, w_uq, w_ukv):
    dk = MLA_NOPE + MLA_ROPE
    half = MLA_ROPE // 2
    scale = dk ** -0.5
    wq = jnp.transpose(w_uq, (1, 0, 2)) * scale
    nope0 = jnp.zeros(wq.shape[:2] + (MLA_NOPE,), F32)
    wq_rot = jnp.concatenate([nope0, -wq[..., MLA_NOPE + half:], wq[..., MLA_NOPE:MLA_NOPE + half]], axis=-1)
    wkv = jnp.transpose(w_ukv, (1, 0, 2))
    wk = jnp.concatenate([wkv[..., :MLA_NOPE], jnp.zeros(wkv.shape[:2] + (MLA_ROPE,), F32)], axis=-1)
    wv = wkv[..., MLA_NOPE:]
    eye = jnp.eye(MLA_ROPE, dtype=F32)
    rot = jnp.concatenate([-eye[:, half:], eye[:, :half]], axis=-1)
    pad_r = MISC_W - MLA_ROPE
    ska = jnp.pad(eye, ((0, pad_r), (MLA_NOPE, 0)))
    skb = jnp.pad(rot, ((0, pad_r), (MLA_NOPE, 0)))
    return tuple(t.astype(BF16) for t in (wq, wq_rot, wk, wv, ska, skb))


def _mla_tables(rope, length):
    dk = MLA_NOPE + MLA_ROPE
    if rope is None:
        return jnp.ones((length, dk), F32), jnp.zeros((length, dk), F32)
    cos, sin = rope
    ones = jnp.ones((length, MLA_NOPE), F32)
    return (jnp.concatenate([ones, cos, cos], axis=-1),
            jnp.concatenate([0.0 * ones, sin, sin], axis=-1))


def mla_prepare(main, misc, b, length, weights, q_norm_w, kv_norm_w, tables):
    wq, wq_rot, wk, wv, ska, skb = weights
    cos, sin = tables
    dk = MLA_NOPE + MLA_ROPE
    tm = min(512, length)
    nt = length // tm
    full = lambda a: pl.BlockSpec(a.shape, lambda bi, i: (0,) * a.ndim)
    qnw = q_norm_w.reshape(1, -1)
    kvnw = kv_norm_w.reshape(1, -1)
    outs = pl.pallas_call(
        _mla_prep_body,
        grid=(b, nt),
        in_specs=[pl.BlockSpec((tm, MLA_Q_LORA), lambda bi, i: (bi * nt + i, OFF_CQ // MLA_Q_LORA)),
                  pl.BlockSpec((tm, MLA_KV_LORA), lambda bi, i: (bi * nt + i, OFF_CKV // MLA_KV_LORA)),
                  pl.BlockSpec((tm, MISC_W), lambda bi, i: (bi * nt + i, 0)),
                  full(qnw), full(kvnw), full(wq), full(wq_rot), full(wk), full(wv), full(ska), full(skb),
                  pl.BlockSpec((tm, dk), lambda bi, i: (i, 0)),
                  pl.BlockSpec((tm, dk), lambda bi, i: (i, 0))],
        out_specs=[pl.BlockSpec((1, MLA_HEADS, tm, dk), lambda bi, i: (bi, 0, i, 0)),
                   pl.BlockSpec((1, MLA_HEADS, tm, dk), lambda bi, i: (bi, 0, i, 0)),
                   pl.BlockSpec((1, MLA_HEADS, tm, MLA_V), lambda bi, i: (bi, 0, i, 0))],
        out_shape=[jax.ShapeDtypeStruct((b, MLA_HEADS, length, dk), BF16),
                   jax.ShapeDtypeStruct((b, MLA_HEADS, length, dk), BF16),
                   jax.ShapeDtypeStruct((b, MLA_HEADS, length, MLA_V), BF16)],
        compiler_params=_cp(("parallel", "parallel")),
        name="mla_prep",
    )(main, main, misc, qnw, kvnw, wq, wq_rot, wk, wv, ska, skb, cos, sin)
    return outs


def _gqa_prep_body(q_ref, k_ref, v_ref, qnw_ref, knw_ref, gsum_ref, rot_ref, cos_ref, sin_ref,
                   q_out, k_out, v_out):
    cos, sin = cos_ref[...], sin_ref[...]

    def prep(v, nw, width):
        vf = v.astype(F32)
        sq = vf * vf
        hi = sq.astype(BF16)
        lo = (sq - hi.astype(F32)).astype(BF16)
        g = gsum_ref[:width, :width]
        ss = jnp.dot(hi, g, preferred_element_type=F32) + jnp.dot(lo, g, preferred_element_type=F32)
        xn = vf * lax.rsqrt(ss * (1.0 / GQA_HEAD_DIM) + NORM_EPS) * nw
        xr = jnp.dot(xn.astype(BF16), rot_ref[:width, :width], preferred_element_type=F32)
        return xn * cos[:, :width] + xr * sin[:, :width]

    qf = prep(q_ref[...], qnw_ref[...], GQA_HEADS * GQA_HEAD_DIM) * GQA_HEAD_DIM ** -0.5
    kf = prep(k_ref[...], knw_ref[...], LANES)
    q_out[...] = qf.astype(BF16)
    k_out[...] = kf.astype(BF16)
    v_out[...] = v_ref[...]


def gqa_prepare(main, b, length, q_norm_w, k_norm_w, rope):
    n = b * length
    qw = GQA_HEADS * GQA_HEAD_DIM
    kw = GQA_KV_HEADS * GQA_HEAD_DIM
    half = GQA_HEAD_DIM // 2
    if rope is None:
        cos = jnp.ones((length, qw), F32)
        sin = jnp.zeros((length, qw), F32)
    else:
        cos = jnp.tile(jnp.concatenate([rope[0], rope[0]], axis=-1), (1, GQA_HEADS))
        sin = jnp.tile(jnp.concatenate([rope[1], rope[1]], axis=-1), (1, GQA_HEADS))
    head = np.arange(qw) // GQA_HEAD_DIM
    gsum = jnp.asarray(head[:, None] == head[None, :], BF16)
    eye = np.eye(GQA_HEAD_DIM, dtype=np.float32)
    rot1 = np.concatenate([-eye[:, half:], eye[:, :half]], axis=-1)
    rot = jnp.asarray(np.kron(np.eye(GQA_HEADS, dtype=np.float32), rot1), BF16)
    tm = min(512, length)
    nt = length // tm
    full = lambda a: pl.BlockSpec(a.shape, lambda bi, i: (0,) * a.ndim)
    qnw = jnp.tile(q_norm_w, GQA_HEADS).reshape(1, qw)
    knw = jnp.tile(k_norm_w, GQA_KV_HEADS).reshape(1, kw)
    return pl.pallas_call(
        _gqa_prep_body,
        grid=(b, nt),
        in_specs=[pl.BlockSpec((tm, qw), lambda bi, i: (bi * nt + i, OFF_GQ // qw)),
                  pl.BlockSpec((tm, kw), lambda bi, i: (bi * nt + i, OFF_GK // kw)),
                  pl.BlockSpec((tm, kw), lambda bi, i: (bi * nt + i, OFF_GV // kw)),
                  full(qnw), full(knw), full(gsum), full(rot),
                  pl.BlockSpec((tm, qw), lambda bi, i: (i, 0)),
                  pl.BlockSpec((tm, qw), lambda bi, i: (i, 0))],
        out_specs=[pl.BlockSpec((tm, qw), lambda bi, i: (bi * nt + i, 0)),
                   pl.BlockSpec((tm, kw), lambda bi, i: (bi * nt + i, 0)),
                   pl.BlockSpec((tm, kw), lambda bi, i: (bi * nt + i, 0))],
        out_shape=[jax.ShapeDtypeStruct((n, qw), BF16),
                   jax.ShapeDtypeStruct((n, kw), BF16),
                   jax.ShapeDtypeStruct((n, kw), BF16)],
        compiler_params=_cp(("parallel", "parallel")),
        name="gqa_prep",
    )(main, main, main, qnw, knw, gsum, rot, cos, sin)


FLASH_CHAIN_ROWS = 256


def _flash_body(q_ref, k_ref, v_ref, kc_ref, vc_ref, o_ref, m_ref, l_ref, acc_ref, *, nk, has_ctx, chains):
    ki = pl.program_id(3)

    @pl.when(ki == 0)
    def _():
        m_ref[...] = jnp.full_like(m_ref, -jnp.inf)
        l_ref[...] = jnp.zeros_like(l_ref)
        acc_ref[...] = jnp.zeros_like(acc_ref)

    def step(k, vt):
        ss = [jnp.dot(k, q_ref[0, 0, gi, :, r0:r0 + rc], preferred_element_type=F32)
              for gi, r0, rc in chains]
        m_prev = [m_ref[ci] for ci in range(len(chains))]
        m_new = [jnp.maximum(mp, jnp.max(s, axis=0, keepdims=True)) for mp, s in zip(m_prev, ss)]
        ps = [jnp.exp(s - mn) for s, mn in zip(ss, m_new)]
        alphas = [jnp.exp(mp - mn) for mp, mn in zip(m_prev, m_new)]
        pv = [jnp.dot(vt, p.astype(BF16), preferred_element_type=F32) for p in ps]
        for ci in range(len(chains)):
            l_ref[ci] = alphas[ci] * l_ref[ci] + jnp.sum(ps[ci], axis=0, keepdims=True)
            acc_ref[ci] = alphas[ci] * acc_ref[ci] + pv[ci]
            m_ref[ci] = m_new[ci]

    @pl.when(ki < nk)
    def _():
        step(k_ref[0, 0], v_ref[0, 0])

    if has_ctx:
        @pl.when(ki == nk)
        def _():
            step(kc_ref[0, 0], vc_ref[0, 0])

    @pl.when(ki == nk - 1 + int(has_ctx))
    def _():
        for ci, (gi, r0, rc) in enumerate(chains):
            o_ref[0, 0, gi, :, r0:r0 + rc] = (acc_ref[ci] / l_ref[ci]).astype(o_ref.dtype)


def flash_attention(q, k, v, kc, vc, tq, tk):
    b, hkv, g, sq, dk = q.shape
    sk = k.shape[2]
    dv = v.shape[3]
    tq = min(tq, sq)
    tk = min(tk, sk)
    nk = sk // tk
    has_ctx = kc is not None
    if not has_ctx:
        kc, vc = k[:, :, :LANES], v[:, :, :LANES]
    skc = kc.shape[2]
    rc = min(FLASH_CHAIN_ROWS, tq)
    chains = tuple((gi, r0, rc) for gi in range(g) for r0 in range(0, tq, rc))
    nch = len(chains)
    body = functools.partial(_flash_body, nk=nk, has_ctx=has_ctx, chains=chains)
    qt = jnp.swapaxes(q, 3, 4)
    vt = jnp.swapaxes(v, 2, 3)
    vct = jnp.swapaxes(vc, 2, 3)
    out_t = pl.pallas_call(
        body,
        grid=(b, hkv, sq // tq, nk + int(has_ctx)),
        in_specs=[pl.BlockSpec((1, 1, g, dk, tq), lambda bi, h, qi, ki: (bi, h, 0, 0, qi)),
                  pl.BlockSpec((1, 1, tk, dk), lambda bi, h, qi, ki: (bi, h, jnp.minimum(ki, nk - 1), 0)),
                  pl.BlockSpec((1, 1, dv, tk), lambda bi, h, qi, ki: (bi, h, 0, jnp.minimum(ki, nk - 1))),
                  pl.BlockSpec((1, 1, skc, dk), lambda bi, h, qi, ki: (bi, h, 0, 0)),
                  pl.BlockSpec((1, 1, dv, skc), lambda bi, h, qi, ki: (bi, h, 0, 0))],
        out_specs=pl.BlockSpec((1, 1, g, dv, tq), lambda bi, h, qi, ki: (bi, h, 0, 0, qi)),
        out_shape=jax.ShapeDtypeStruct((b, hkv, g, dv, sq), BF16),
        scratch_shapes=[pltpu.VMEM((nch, 1, rc), F32), pltpu.VMEM((nch, 1, rc), F32),
                        pltpu.VMEM((nch, dv, rc), F32)],
        compiler_params=_cp(("parallel", "parallel", "parallel", "arbitrary")),
        name="flash_attention",
    )(qt, k, vt, kc, vct)
    return jnp.swapaxes(out_t, 3, 4)


def _router_body(x_ref, nw_ref, sh_ref, sc_ref, wrh_ref, wrl_ref, rb_ref, tri_ref,
                 h_ref, ri_ref, rw_ref, cnt_ref, carry_ref):
    i = pl.program_id(0)

    @pl.when(i == 0)
    def _():
        carry_ref[...] = jnp.zeros_like(carry_ref)

    xf = x_ref[...]
    y = xf * lax.rsqrt(jnp.mean(xf * xf, axis=-1, keepdims=True) + NORM_EPS) * nw_ref[...]
    h = y * (1.0 + sc_ref[0]) + sh_ref[0]
    h_ref[...] = h
    hi = h.astype(BF16)
    lo = (h - hi.astype(F32)).astype(BF16)
    nt = (((1,), (1,)), ((), ()))
    logits = (lax.dot_general(wrh_ref[...], hi, nt, preferred_element_type=F32)
              + lax.dot_general(wrh_ref[...], lo, nt, preferred_element_type=F32)
              + lax.dot_general(wrl_ref[...], hi, nt, preferred_element_type=F32))
    scores = _sigmoid(logits)
    sel = scores + rb_ref[...]
    s = [scores[e:e + 1] for e in range(N_EXPERTS)]
    v = [sel[e:e + 1] for e in range(N_EXPERTS)]
    epg = EXPERTS_PER_GROUP
    gscore = []
    for gi in range(N_GROUPS):
        mem = v[gi * epg:(gi + 1) * epg]
        best = None
        for a in range(epg):
            for c in range(a + 1, epg):
                pair = mem[a] + mem[c]
                best = pair if best is None else jnp.maximum(best, pair)
        gscore.append(best)
    is_best = []
    for gi in range(N_GROUPS):
        ok = None
        for gj in range(N_GROUPS):
            if gj == gi:
                continue
            c = (gscore[gi] > gscore[gj]) if gj < gi else (gscore[gi] >= gscore[gj])
            ok = c if ok is None else (ok & c)
        is_best.append(ok)
    chosen = []
    for e in range(N_EXPERTS):
        gi = e // epg
        rank = jnp.zeros_like(v[e])
        for e2 in range(gi * epg, (gi + 1) * epg):
            if e2 == e:
                continue
            ahead = (v[e2] >= v[e]) if e2 < e else (v[e2] > v[e])
            rank = rank + jnp.where(ahead, 1.0, 0.0)
        chosen.append(is_best[gi] & (rank < TOP_K))
    chosen_f = jnp.concatenate([jnp.where(cm, 1.0, 0.0) for cm in chosen], axis=0)
    total = None
    for e in range(N_EXPERTS):
        t = jnp.where(chosen[e], s[e], 0.0)
        total = t if total is None else total + t
    pos = jnp.dot(chosen_f.astype(BF16), tri_ref[...], preferred_element_type=F32) + carry_ref[...]
    carry_ref[...] += jnp.sum(chosen_f, axis=-1, keepdims=True)
    cnt_ref[...] = jnp.broadcast_to(carry_ref[...], cnt_ref.shape)
    e_lo = jnp.full_like(v[0], float(N_EXPERTS))
    e_hi = jnp.full_like(v[0], -1.0)
    for e in range(N_EXPERTS):
        e_lo = jnp.where(chosen[e], jnp.minimum(e_lo, float(e)), e_lo)
        e_hi = jnp.where(chosen[e], jnp.maximum(e_hi, float(e)), e_hi)
    zero = jnp.zeros_like(v[0])
    w_lo, w_hi, p_lo, p_hi = zero, zero, zero, zero
    for e in range(N_EXPERTS):
        pe = pos[e:e + 1]
        w_lo = jnp.where(e_lo == float(e), s[e], w_lo)
        w_hi = jnp.where(e_hi == float(e), s[e], w_hi)
        p_lo = jnp.where(e_lo == float(e), pe, p_lo)
        p_hi = jnp.where(e_hi == float(e), pe, p_hi)
    ri_ref[...] = jnp.concatenate([e_lo, e_hi, p_lo, p_hi, zero, zero, zero, zero], axis=0).astype(jnp.int32)
    rw_ref[...] = jnp.concatenate([w_lo / total, w_hi / total, zero, zero, zero, zero, zero, zero], axis=0)


def norm_route(x, nw, shift, scale, per_batch, w_router, router_bias):
    n, d = x.shape
    tm = _row_tile(512, n, per_batch)
    rf = _mod_row_fn(n, tm, per_batch)
    wr_t = w_router.T
    wr_hi = wr_t.astype(BF16)
    wr_lo = (wr_t - wr_hi.astype(F32)).astype(BF16)
    tri = jnp.asarray(np.triu(np.ones((tm, tm), np.float32), 1), BF16)
    const = lambda a: pl.BlockSpec(a.shape, lambda i: (0,) * a.ndim)
    rb = router_bias.reshape(N_EXPERTS, 1).astype(F32)
    return pl.pallas_call(
        _router_body,
        grid=(n // tm,),
        in_specs=[pl.BlockSpec((tm, d), lambda i: (i, 0)),
                  pl.BlockSpec((1, d), lambda i: (0, 0)),
                  pl.BlockSpec((1, 1, d), lambda i: (rf(i), 0, 0)),
                  pl.BlockSpec((1, 1, d), lambda i: (rf(i), 0, 0)),
                  const(wr_hi), const(wr_lo), const(rb), const(tri)],
        out_specs=[pl.BlockSpec((tm, d), lambda i: (i, 0)),
                   pl.BlockSpec((8, tm), lambda i: (0, i)),
                   pl.BlockSpec((8, tm), lambda i: (0, i)),
                   pl.BlockSpec((N_EXPERTS, LANES), lambda i: (0, 0))],
        out_shape=[jax.ShapeDtypeStruct((n, d), F32),
                   jax.ShapeDtypeStruct((8, n), jnp.int32),
                   jax.ShapeDtypeStruct((8, n), F32),
                   jax.ShapeDtypeStruct((N_EXPERTS, LANES), F32)],
        scratch_shapes=[pltpu.VMEM((N_EXPERTS, 1), F32)],
        compiler_params=_cp(("arbitrary",)),
        name="norm_route",
    )(x, nw.reshape(1, d), shift.reshape(MOD_ROWS, 1, d), scale.reshape(MOD_ROWS, 1, d),
      wr_hi, wr_lo, rb, tri)


def _dispatch_body(sa_ref, sb_ref, pad_ref, h_ref, xs_ref, zero_ref, sem, *, tm, n_pad):
    i = pl.program_id(0)
    base = i * tm

    def row_copy(src, r, slot):
        return pltpu.make_async_copy(src.at[pl.ds(r, 1)], xs_ref.at[pl.ds(slot, 1)], sem)

    @pl.when(i == 0)
    def _():
        zero_ref[...] = jnp.zeros_like(zero_ref)

        def fill(j, carry):
            row_copy(zero_ref, 0, pad_ref[j]).start()
            return carry
        lax.fori_loop(0, n_pad, fill, 0)

        def drain(j, carry):
            row_copy(zero_ref, 0, 0).wait()
            return carry
        lax.fori_loop(0, n_pad, drain, 0)

    def issue(r, carry):
        row_copy(h_ref, r, sa_ref[base + r]).start()
        row_copy(h_ref, r, sb_ref[base + r]).start()
        return carry
    lax.fori_loop(0, tm, issue, 0)

    def drain2(r, carry):
        row_copy(h_ref, 0, 0).wait()
        row_copy(h_ref, 0, 0).wait()
        return carry
    lax.fori_loop(0, tm, drain2, 0)


def moe_dispatch(h, slot_a, slot_b, pad_slots, n_slots):
    n, d = h.shape
    tm = min(256, n)
    n_pad = pad_slots.shape[0]
    body = functools.partial(_dispatch_body, tm=tm, n_pad=n_pad)
    return pl.pallas_call(
        body,
        grid_spec=pltpu.PrefetchScalarGridSpec(
            num_scalar_prefetch=3,
            grid=(n // tm,),
            in_specs=[pl.BlockSpec((tm, d), lambda i, sa, sb, pd: (i, 0))],
            out_specs=pl.BlockSpec(memory_space=pl.ANY),
            scratch_shapes=[pltpu.VMEM((8, d), F32), pltpu.SemaphoreType.DMA(())]),
        out_shape=jax.ShapeDtypeStruct((n_slots, d), F32),
        compiler_params=_cp(("arbitrary",)),
        name="moe_dispatch",
    )(slot_a, slot_b, pad_slots, h)


def _experts_body(te_ref, nu_ref, xs_ref, wg_ref, wu_ref, wd_ref, y_ref):
    i = pl.program_id(0)

    @pl.when(i < nu_ref[0])
    def _():
        xb = xs_ref[...].astype(BF16)
        hg = jnp.dot(xb, wg_ref[0], preferred_element_type=F32)
        hu = jnp.dot(xb, wu_ref[0], preferred_element_type=F32)
        act = (hg * _sigmoid(hg) * hu).astype(BF16)
        y_ref[...] = jnp.dot(act, wd_ref[0], preferred_element_type=F32)

    @pl.when(i >= nu_ref[0])
    def _():
        y_ref[...] = jnp.zeros_like(y_ref)


def moe_experts(xs, tile_expert, n_used, w_gate, w_up, w_down):
    s, d = xs.shape
    f = w_gate.shape[2]
    tm = MOE_TILE
    return pl.pallas_call(
        _experts_body,
        grid_spec=pltpu.PrefetchScalarGridSpec(
            num_scalar_prefetch=2,
            grid=(s // tm,),
            in_specs=[pl.BlockSpec((tm, d), lambda i, te, nu: (jnp.minimum(i, nu[0] - 1), 0)),
                      pl.BlockSpec((1, d, f), lambda i, te, nu: (te[i], 0, 0)),
                      pl.BlockSpec((1, d, f), lambda i, te, nu: (te[i], 0, 0)),
                      pl.BlockSpec((1, f, d), lambda i, te, nu: (te[i], 0, 0))],
            out_specs=pl.BlockSpec((tm, d), lambda i, te, nu: (i, 0))),
        out_shape=jax.ShapeDtypeStruct((s, d), F32),
        compiler_params=_cp(("arbitrary",)),
        name="moe_experts",
    )(tile_expert, n_used, xs, w_gate, w_up, w_down)


def _combine_body(sa_ref, sb_ref, x_ref, w_ref, g_ref, y_ref, o_ref, ba_ref, bb_ref, sem, *, tm):
    i = pl.program_id(0)
    base = i * tm

    def row_copy(slot, dst, r):
        return pltpu.make_async_copy(y_ref.at[pl.ds(slot, 1)], dst.at[pl.ds(r, 1)], sem)

    def issue(r, carry):
        row_copy(sa_ref[base + r], ba_ref, r).start()
        row_copy(sb_ref[base + r], bb_ref, r).start()
        return carry
    lax.fori_loop(0, tm, issue, 0)

    def drain(r, carry):
        row_copy(0, ba_ref, 0).wait()
        row_copy(0, bb_ref, 0).wait()
        return carry
    lax.fori_loop(0, tm, drain, 0)

    w = w_ref[...]
    mix = w[:, 0:1] * ba_ref[...] + w[:, 1:2] * bb_ref[...]
    o_ref[...] = x_ref[...] + g_ref[0] * mix


def moe_combine(x, y, slot_a, slot_b, wts, gate, per_batch):
    n, d = x.shape
    tm = _row_tile(256, n, per_batch)
    rf = _mod_row_fn(n, tm, per_batch)
    body = functools.partial(_combine_body, tm=tm)
    return pl.pallas_call(
        body,
        grid_spec=pltpu.PrefetchScalarGridSpec(
            num_scalar_prefetch=2,
            grid=(n // tm,),
            in_specs=[pl.BlockSpec((tm, d), lambda i, sa, sb: (i, 0)),
                      pl.BlockSpec((tm, 8), lambda i, sa, sb: (i, 0)),
                      pl.BlockSpec((1, 1, d), lambda i, sa, sb: (rf(i), 0, 0)),
                      pl.BlockSpec(memory_space=pl.ANY)],
            out_specs=pl.BlockSpec((tm, d), lambda i, sa, sb: (i, 0)),
            scratch_shapes=[pltpu.VMEM((tm, d), F32), pltpu.VMEM((tm, d), F32), pltpu.SemaphoreType.DMA(())]),
        out_shape=jax.ShapeDtypeStruct((n, d), F32),
        compiler_params=_cp(("arbitrary",)),
        name="moe_combine",
    )(slot_a, slot_b, x, wts, gate.reshape(MOD_ROWS, 1, d), y)


def moe_layer(x, nw, mod, per_batch, w_router, router_bias, w_gate, w_up, w_down):
    n, d = x.shape
    h, route_i, route_w, counts = norm_route(x, nw, mod[:, 3], mod[:, 4], per_batch, w_router, router_bias)
    cnt = counts[:, 0].astype(jnp.int32)
    seg = ((cnt + MOE_TILE - 1) // MOE_TILE) * MOE_TILE
    off = jnp.concatenate([jnp.zeros((1,), jnp.int32), jnp.cumsum(seg)])
    n_slots = TOP_K * n + N_EXPERTS * MOE_TILE
    slot_a = off[route_i[0]] + route_i[2]
    slot_b = off[route_i[1]] + route_i[3]
    n_pad = n_slots - TOP_K * n
    padcnt = seg - cnt
    padstart = jnp.concatenate([jnp.zeros((1,), jnp.int32), jnp.cumsum(padcnt)])
    j = jnp.arange(n_pad, dtype=jnp.int32)
    count_le = lambda edges, v: jnp.sum((edges[None, :] <= v[:, None]).astype(jnp.int32), axis=1)
    e_of = jnp.clip(count_le(padstart, j) - 1, 0, N_EXPERTS)
    in_seg = off[jnp.minimum(e_of, N_EXPERTS - 1)] + cnt[jnp.minimum(e_of, N_EXPERTS - 1)] + (j - padstart[e_of])
    tail = off[N_EXPERTS] + (j - padstart[N_EXPERTS])
    pad_slots = jnp.where(e_of < N_EXPERTS, in_seg, tail).astype(jnp.int32)
    n_tiles = n_slots // MOE_TILE
    tile_start = jnp.arange(n_tiles, dtype=jnp.int32) * MOE_TILE
    n_used = (off[N_EXPERTS] // MOE_TILE).astype(jnp.int32).reshape(1)
    tile_expert = jnp.clip(count_le(off, tile_start) - 1, 0, N_EXPERTS - 1).astype(jnp.int32)
    last_used = tile_expert[jnp.maximum(n_used[0] - 1, 0)]
    tile_expert = jnp.where(jnp.arange(n_tiles) < n_used[0], tile_expert, last_used)

    xs = moe_dispatch(h, slot_a, slot_b, pad_slots, n_slots)
    y = moe_experts(xs, tile_expert, n_used, w_gate, w_up, w_down)
    wts = jnp.transpose(route_w)
    return moe_combine(x, y, slot_a, slot_b, wts, mod[:, 5], per_batch)


def _final_norm_body(x_ref, w_ref, o_ref):
    xf = x_ref[...]
    y = xf * lax.rsqrt(jnp.mean(xf * xf, axis=-1, keepdims=True) + NORM_EPS)
    o_ref[...] = y * w_ref[...]


def final_rms_norm(x, w):
    n, d = x.shape
    rows = 512
    return pl.pallas_call(
        _final_norm_body,
        grid=(n // rows,),
        in_specs=[pl.BlockSpec((rows, d), lambda i: (i, 0)), pl.BlockSpec((1, d), lambda i: (0, 0))],
        out_specs=pl.BlockSpec((rows, d), lambda i: (i, 0)),
        out_shape=jax.ShapeDtypeStruct((n, d), x.dtype),
        compiler_params=_cp(("parallel",)),
        name="final_norm",
    )(x, w.reshape(1, d))


def _reorder_w_in(w):
    o = np.cumsum((0,) + IN_WIDTHS)
    seg = lambda i: w[:, o[i]:o[i + 1]]
    main = jnp.concatenate([seg(0), seg(1), seg(2), seg(3), seg(6), seg(9), seg(12), seg(7), seg(10), seg(11)], axis=1)
    misc = jnp.concatenate([seg(8), seg(4), seg(5)], axis=1)
    misc = jnp.pad(misc, ((0, 0), (0, MISC_W - misc.shape[1])))
    return main.astype(BF16), misc.astype(BF16)


def _attention_branches(main_l, misc_l, main_c, misc_c, b, seq, ctx_len, ctx_out, rope_mla, rope_gqa,
                        mla_w, mla_qn, mla_kvn, gqa_qn, gqa_kn):
    g = GQA_HEADS // GQA_KV_HEADS
    hd = GQA_HEAD_DIM

    mq_l, mk_l, mv_l = mla_prepare(main_l, misc_l, b, seq, mla_w, mla_qn, mla_kvn, _mla_tables(rope_mla, seq))
    mq_c, mk_c, mv_c = mla_prepare(main_c, misc_c, b, ctx_len, mla_w, mla_qn, mla_kvn, _mla_tables(None, ctx_len))
    tk_all = (seq + ctx_len) // 2
    cat = lambda lat, ctx_: jnp.concatenate([lat, ctx_], axis=2)
    mla_l = flash_attention(mq_l[:, :, None], cat(mk_l, mk_c), cat(mv_l, mv_c), None, None, 1024, tk_all)
    mla_l = jnp.transpose(mla_l[:, :, 0], (0, 2, 1, 3)).reshape(b * seq, BRANCH_W)

    def split_heads(t, length, heads):
        return jnp.transpose(t.reshape(b, length, heads, hd), (0, 2, 1, 3))

    gq_l, gk_l, gv_l = gqa_prepare(main_l, b, seq, gqa_qn, gqa_kn, rope_gqa)
    gq_c, gk_c, gv_c = gqa_prepare(main_c, b, ctx_len, gqa_qn, gqa_kn, None)
    gq_l5 = split_heads(gq_l, seq, GQA_HEADS).reshape(b, GQA_KV_HEADS, g, seq, hd)
    gk_l4, gv_l4 = split_heads(gk_l, seq, GQA_KV_HEADS), split_heads(gv_l, seq, GQA_KV_HEADS)
    gk_c4, gv_c4 = split_heads(gk_c, ctx_len, GQA_KV_HEADS), split_heads(gv_c, ctx_len, GQA_KV_HEADS)
    gqa_l = flash_attention(gq_l5, cat(gk_l4, gk_c4), cat(gv_l4, gv_c4), None, None, 256, tk_all)
    gqa_l = jnp.transpose(gqa_l.reshape(b, GQA_HEADS, seq, hd), (0, 2, 1, 3)).reshape(b * seq, BRANCH_W)

    mla_c = gqa_c = None
    if ctx_out:
        mla_c = flash_attention(mq_c[:, :, None], mk_c, mv_c, None, None, 256, 256)
        mla_c = jnp.transpose(mla_c[:, :, 0], (0, 2, 1, 3)).reshape(b * ctx_len, BRANCH_W)
        gq_c5 = split_heads(gq_c, ctx_len, GQA_HEADS).reshape(b, GQA_KV_HEADS, g, ctx_len, hd)
        gqa_c = flash_attention(gq_c5, gk_c4, gv_c4, None, None, 256, 256)
        gqa_c = jnp.transpose(gqa_c.reshape(b, GQA_HEADS, ctx_len, hd), (0, 2, 1, 3)).reshape(b * ctx_len, BRANCH_W)
    return mla_l, gqa_l, mla_c, gqa_c


def kernel(x, c, ctx, c_ctx, w_ada, b_ada, norm1_w, norm2_w, w_in,
           gdn_conv_w, gdn_a_log, gdn_dt_bias, gdn_norm_w,
           mla_q_norm_w, mla_kv_norm_w, mla_w_uq, mla_w_ukv,
           gqa_q_norm_w, gqa_k_norm_w,
           hy_conv_w, hy_w1, hy_b1, hy_w2, hy_b2, hy_w3, hy_sin_freq, hy_bias,
           w_branch, w_out, w_router, router_bias,
           moe_w_gate, moe_w_up, moe_w_down, final_norm_w):
    b, seq, d = x.shape
    ctx_len = ctx.shape[1]
    rows = seq // GRID_W
    rope_mla = axial_rope_tables(rows, MLA_ROPE)
    rope_gqa = axial_rope_tables(rows, GQA_HEAD_DIM)
    mod_all = ada_modulation(c, c_ctx, w_ada, b_ada)
    xl = x.reshape(b * seq, d)
    xc = ctx.reshape(b * ctx_len, d)
    f32 = lambda t: t.astype(F32)
    for layer in range(DEPTH):
        ctx_out = layer < DEPTH - 1
        mod = mod_all[layer]
        w_main, w_misc = _reorder_w_in(w_in[layer][:, :MIX_IN])
        w_gates = w_in[layer][:, MIX_IN:].astype(BF16)
        w_br = w_branch[layer].astype(BF16)
        w_o = w_out[layer].astype(BF16)
        wg, wu, wd = (t[layer].astype(BF16) for t in (moe_w_gate, moe_w_up, moe_w_down))

        hl = norm_modulate(xl, norm1_w[layer], mod[:, 0], mod[:, 1], seq)
        hc = norm_modulate(xc, norm1_w[layer], mod[:, 0], mod[:, 1], None)
        main_l, misc_l = matmul(hl, w_main, BF16, 512), matmul(hl, w_misc, F32, MISC_W)
        main_c, misc_c = matmul(hc, w_main, BF16, 512), matmul(hc, w_misc, F32, MISC_W)

        gdn_l, gdn_c = gdn_branch(main_l, misc_l, main_c, misc_c, b, seq, ctx_len, gdn_conv_w[layer],
                                  gdn_a_log[layer], gdn_dt_bias[layer], gdn_norm_w[layer], ctx_out)

        mla_w = _mla_weights(mla_q_norm_w[layer], mla_kv_norm_w[layer], mla_w_uq[layer], mla_w_ukv[layer])
        mla_l, gqa_l, mla_c, gqa_c = _attention_branches(
            main_l, misc_l, main_c, misc_c, b, seq, ctx_len, ctx_out, rope_mla, rope_gqa,
            mla_w, mla_q_norm_w[layer], mla_kv_norm_w[layer], gqa_q_norm_w[layer], gqa_k_norm_w[layer])

        hy_params = (hy_w1[layer], hy_b1[layer], hy_w2[layer], hy_b2[layer], hy_w3[layer], hy_sin_freq[layer])
        hy_l = hyena_branch(main_l, b, seq, hy_conv_w[layer], hyena_filters(seq, *hy_params), hy_bias[layer])

        branches_l = [gdn_l.reshape(b * seq, BRANCH_W).astype(BF16), mla_l, gqa_l,
                      hy_l.reshape(b * seq, BRANCH_W).astype(BF16)]
        merged_l = merge_branches_gated(hl, branches_l, w_gates, w_br)

        if ctx_out:
            hy_c = hyena_branch(main_c, b, ctx_len, hy_conv_w[layer], hyena_filters(ctx_len, *hy_params),
                                hy_bias[layer])
            branches_c = [gdn_c.reshape(b * ctx_len, BRANCH_W).astype(BF16), mla_c, gqa_c,
                          hy_c.reshape(b * ctx_len, BRANCH_W).astype(BF16)]
            merged_c = merge_branches_gated(hc, branches_c, w_gates, w_br)
            xc = matmul_gated_residual(merged_c, w_o, xc, mod[:, 2], None)
            xc = moe_layer(xc, norm2_w[layer], mod, None, w_router, router_bias, wg, wu, wd)

        xl = matmul_gated_residual(merged_l, w_o, xl, mod[:, 2], seq)
        xl = moe_layer(xl, norm2_w[layer], mod, seq, w_router, router_bias, wg, wu, wd)
    return final_rms_norm(xl, final_norm_w).reshape(b, seq, d)
```

```python
import math, functools
import jax, jax.numpy as jnp
from jax import lax
import numpy as np
from jax.experimental import pallas as pl
from jax.experimental.pallas import tpu as pltpu

D_MODEL = 2048
BATCH = 4
SEQ = 4096
DEPTH = 2

GRID_W = 64
CTX_LEN = 256
N_BRANCH = 4
BRANCH_W = 512
NORM_EPS = 1e-6
Q_BLOCK = 128
ROPE_THETA = 10000.0
SHORT_CONV = 3

GDN_HEADS = 4
GDN_HEAD_DIM = 128
GDN_CHUNK = 64

MLA_HEADS = 4
MLA_Q_LORA = 512
MLA_KV_LORA = 256
MLA_NOPE = 128
MLA_ROPE = 64
MLA_V = 128

GQA_HEADS = 8
GQA_KV_HEADS = 2
GQA_HEAD_DIM = 64

HY_WIDTH = 512
HY_ORDER = 2
HY_EMB = 33
HY_HIDDEN = 64
HY_DECAY_TARGET = 1e-2
HY_FAST_DECAY = 0.3
HY_SLOW_DECAY = 1.5

N_EXPERTS = 16
N_GROUPS = 4
EXPERTS_PER_GROUP = N_EXPERTS // N_GROUPS
TOP_K = 2
D_EXPERT = 512

GDN_W = GDN_HEADS * GDN_HEAD_DIM
IN_WIDTHS = (GDN_W, GDN_W, GDN_W, GDN_W, 2 * GDN_HEADS, 2 * GDN_HEADS,
             MLA_Q_LORA, MLA_KV_LORA, MLA_ROPE,
             GQA_HEADS * GQA_HEAD_DIM, GQA_KV_HEADS * GQA_HEAD_DIM, GQA_KV_HEADS * GQA_HEAD_DIM,
             (HY_ORDER + 1) * HY_WIDTH)
MIX_IN = sum(IN_WIDTHS)
IN_DIM = MIX_IN + N_BRANCH * D_MODEL

F32 = jnp.float32
BF16 = jnp.bfloat16
LANES = 128
MOD_ROWS = 8
CTX_MOD_ROW = BATCH
MOE_TILE = 512
VMEM_LIMIT = 56 << 20

MAIN_W = 5120
OFF_GDN, OFF_CQ, OFF_GQ, OFF_HY, OFF_CKV, OFF_GK, OFF_GV = 0, 2048, 2560, 3072, 4608, 4864, 4992
MISC_W = LANES


def _cp(sem):
    return pltpu.CompilerParams(dimension_semantics=sem, vmem_limit_bytes=VMEM_LIMIT)


def _sigmoid(v):
    return 0.5 * jnp.tanh(0.5 * v) + 0.5


def rms_norm(x, w):
    xf = x.astype(jnp.float32)
    y = xf * lax.rsqrt(jnp.mean(xf * xf, axis=-1, keepdims=True) + NORM_EPS)
    return (y * w.astype(jnp.float32)).astype(x.dtype)


def l2_normalize(x):
    xf = x.astype(jnp.float32)
    return xf * lax.rsqrt(jnp.sum(xf * xf, axis=-1, keepdims=True) + NORM_EPS)


def depthwise_conv_centred(u, w):
    k = w.shape[0]
    return lax.conv_general_dilated(u, w[:, None, :].astype(u.dtype), window_strides=(1,),
                                    padding=[(k // 2, k // 2)],
                                    dimension_numbers=('NWC', 'WIO', 'NWC'),
                                    feature_group_count=u.shape[-1])


def axial_rope_tables(rows, rot_dim):
    n_freq = rot_dim // 4
    freqs = ROPE_THETA ** (-jnp.arange(n_freq, dtype=jnp.float32) / n_freq)
    row = jnp.repeat(jnp.arange(rows, dtype=jnp.float32), GRID_W)
    col = jnp.tile(jnp.arange(GRID_W, dtype=jnp.float32), rows)
    ang = jnp.concatenate([row[:, None] * freqs, col[:, None] * freqs], axis=-1)
    return jnp.cos(ang), jnp.sin(ang)


def gdn_prep(q, k, v, a, bt, conv_w, a_log, dt_bias):
    b, l = q.shape[:2]
    qkv = jax.nn.silu(depthwise_conv_centred(jnp.concatenate([q, k, v], axis=-1), conv_w)).astype(jnp.float32)
    q, k, v = jnp.split(qkv, 3, axis=-1)
    hd = (b, l, GDN_HEADS, GDN_HEAD_DIM)
    q = l2_normalize(q.reshape(hd)) * GDN_HEAD_DIM ** -0.5
    k = l2_normalize(k.reshape(hd))
    v = v.reshape(hd)
    a = a.astype(jnp.float32).reshape(b, l, 2, GDN_HEADS)
    g = -jnp.exp(a_log.astype(jnp.float32)) * jax.nn.softplus(a + dt_bias.astype(jnp.float32))
    beta = jax.nn.sigmoid(bt.astype(jnp.float32).reshape(b, l, 2, GDN_HEADS))
    return q, k, v, g, beta


def gated_delta_rule(q, k, v, g, beta, state, with_out):
    b, l, h, _ = q.shape
    dv = v.shape[-1]
    c = GDN_CHUNK
    n = l // c

    def to_chunks(t):
        t = t.reshape(b, n, c, h, *t.shape[3:])
        return jnp.moveaxis(t, (1, 3), (0, 2))

    qc, kc, vc, bc = to_chunks(q), to_chunks(k), to_chunks(v), to_chunks(beta)
    gc = jnp.cumsum(to_chunks(g), axis=-1)
    idx = jnp.arange(c)
    lower = idx[:, None] >= idx[None, :]
    strict = idx[:, None] > idx[None, :]
    diff = gc[..., :, None] - gc[..., None, :]
    decay = jnp.where(lower, jnp.exp(jnp.where(lower, diff, 0.0)), 0.0)
    kb = kc * bc[..., None]
    a = jnp.where(strict, jnp.einsum('nbhid,nbhjd->nbhij', kb, kc) * decay, 0.0)
    solve = functools.partial(lax.linalg.triangular_solve, left_side=True, lower=True, unit_diagonal=True)
    u = solve(a, vc * bc[..., None])
    w = solve(a, kb * jnp.exp(gc)[..., None])
    g_last = gc[..., -1]
    k_dec = kc * jnp.exp(g_last[..., None] - gc)[..., None]
    xs = (u, w, k_dec, g_last)
    if with_out:
        qk = jnp.where(lower, jnp.einsum('nbhid,nbhjd->nbhij', qc, kc) * decay, 0.0)
        xs = xs + (qc * jnp.exp(gc)[..., None], qk)

    def step(s, inp):
        u_i, w_i, kd_i, gl_i = inp[:4]
        v_new = u_i - jnp.einsum('bhck,bhkv->bhcv', w_i, s)
        s_new = s * jnp.exp(gl_i)[..., None, None] + jnp.einsum('bhck,bhcv->bhkv', kd_i, v_new)
        if not with_out:
            return s_new, None
        qd_i, qk_i = inp[4:]
        o = jnp.einsum('bhck,bhkv->bhcv', qd_i, s) + jnp.einsum('bhij,bhjv->bhiv', qk_i, v_new)
        return s_new, o

    state, o = lax.scan(step, state, xs)
    if not with_out:
        return None, state
    o = jnp.moveaxis(o, (0, 2), (1, 3)).reshape(b, l, h, dv)
    return o, state


def gdn_output(o, z, norm_w):
    b, l = z.shape[:2]
    zh = z.reshape(b, l, GDN_HEADS, GDN_HEAD_DIM).astype(jnp.float32)
    y = rms_norm(o, norm_w) * jax.nn.silu(zh)
    return y.reshape(b, l, GDN_W).astype(z.dtype)


def gdn_mixer(p_lat, p_ctx, conv_w, a_log, dt_bias, norm_w, ctx_out):
    lat = gdn_prep(p_lat[0], p_lat[1], p_lat[2], p_lat[4], p_lat[5], conv_w, a_log, dt_bias)
    ctx = gdn_prep(p_ctx[0], p_ctx[1], p_ctx[2], p_ctx[4], p_ctx[5], conv_w, a_log, dt_bias)
    b = p_lat[0].shape[0]
    s0 = jnp.zeros((b, GDN_HEADS, GDN_HEAD_DIM, GDN_HEAD_DIM), jnp.float32)
    o_lat, o_ctx = 0.0, 0.0
    for direction in range(2):
        flip = (lambda t: t[:, ::-1]) if direction else (lambda t: t)

        def seq_args(s):
            q, k, v, g, beta = s
            return flip(q), flip(k), flip(v), flip(g[:, :, direction]), flip(beta[:, :, direction])

        oc, s_ctx = gated_delta_rule(*seq_args(ctx), s0, ctx_out)
        ol, _ = gated_delta_rule(*seq_args(lat), s_ctx, True)
        o_lat = o_lat + flip(ol)
        if ctx_out:
            o_ctx = o_ctx + flip(oc)
    out_lat = gdn_output(o_lat, p_lat[3], norm_w)
    out_ctx = gdn_output(o_ctx, p_ctx[3], norm_w) if ctx_out else None
    return out_lat, out_ctx


def hyena_filters(length, w1, b1, w2, b2, w3, sin_freq):
    t = jnp.arange(length, dtype=jnp.float32)
    bands = (HY_EMB - 1) // 2
    f = jnp.linspace(1e-4, bands - 1, bands, dtype=jnp.float32)
    phase = (2.0 * math.pi / length) * t[:, None] * f[None, :]
    feats = jnp.concatenate([t[:, None] / (length - 1), jnp.cos(phase), -jnp.sin(phase)], axis=-1)
    hid = jnp.sin(sin_freq[0] * (feats @ w1 + b1))
    hid = jnp.sin(sin_freq[1] * (hid @ w2 + b2))
    filt = (hid @ w3).astype(jnp.float32)
    centre = length // 2
    dist = jnp.abs(t - centre) / centre
    deltas = jnp.abs(jnp.linspace(math.log(HY_DECAY_TARGET) / HY_SLOW_DECAY,
                                  math.log(HY_DECAY_TARGET) / HY_FAST_DECAY,
                                  HY_ORDER * HY_WIDTH, dtype=jnp.float32))
    filt = filt * jnp.exp(-dist[:, None] * deltas[None, :])
    filt = filt / jnp.sum(jnp.abs(filt), axis=0, keepdims=True)
    return filt.reshape(length, HY_ORDER, HY_WIDTH)


def fft_conv_centred(u, h):
    l = u.shape[1]
    n = 2 * l
    uf = jnp.fft.rfft(u.astype(jnp.float32), n=n, axis=1)
    hf = jnp.fft.rfft(h.astype(jnp.float32), n=n, axis=0)
    y = jnp.fft.irfft(uf * hf[None], n=n, axis=1)
    return y[:, l // 2: l // 2 + l]


def hyena_mixer(u, conv_w, filt, bias):
    parts = jnp.split(depthwise_conv_centred(u, conv_w).astype(jnp.float32), HY_ORDER + 1, axis=-1)
    z = parts[0]
    for o in range(HY_ORDER):
        z = parts[o + 1] * (fft_conv_centred(z, filt[:, o]) + bias[o] * z)
    return z


HALO_ROWS = 16
GDN_BATCHES_PER_STEP = 4


def _conv3(x, prev_row, next_row, w_ref):
    tm = x.shape[0]
    rows = lax.broadcasted_iota(jnp.int32, x.shape, 0)
    up = jnp.where(rows == 0, prev_row, pltpu.roll(x, 1, 0))
    dn = jnp.where(rows == tm - 1, next_row, pltpu.roll(x, tm - 1, 0))
    return w_ref[0:1, :] * up + w_ref[1:2, :] * x + w_ref[2:3, :] * dn


def _halo_rows(xp_ref, xn_ref):
    i = pl.program_id(1)
    prev = jnp.where(i == 0, 0.0, xp_ref[HALO_ROWS - 1:HALO_ROWS, :].astype(F32))
    nxt = jnp.where(i == pl.num_programs(1) - 1, 0.0, xn_ref[0:1, :].astype(F32))
    return prev, nxt


def _halo_specs(tm, width, col_block, nt, n_rows):
    per = tm // HALO_ROWS
    last = n_rows // HALO_ROWS - 1
    return [pl.BlockSpec((tm, width), lambda bi, i: (bi * nt + i, col_block)),
            pl.BlockSpec((HALO_ROWS, width), lambda bi, i: (jnp.maximum((bi * nt + i) * per - 1, 0), col_block)),
            pl.BlockSpec((HALO_ROWS, width), lambda bi, i: (jnp.minimum((bi * nt + i + 1) * per, last), col_block))]


def _split3(v):
    hi = v.astype(BF16)
    r1 = v - hi.astype(F32)
    mid = r1.astype(BF16)
    lo = (r1 - mid.astype(F32)).astype(BF16)
    return hi, mid, lo


def _gdn_prep_body(x_ref, xp_ref, xn_ref, misc_ref, cw_ref, alog_ref, dt_ref, gmask_ref, tp_ref, ts_ref,
                   q_out, k_out, v_out, gcf_out, gcb_out, beta_out):
    prev, nxt = _halo_rows(xp_ref, xn_ref)
    y = _conv3(x_ref[...].astype(F32), prev, nxt, cw_ref)
    y = y * _sigmoid(y)
    hd = GDN_HEAD_DIM
    for h in range(GDN_HEADS):
        qh = y[:, h * hd:(h + 1) * hd]
        kh = y[:, GDN_W + h * hd:GDN_W + (h + 1) * hd]
        qn = qh * lax.rsqrt(jnp.sum(qh * qh, axis=-1, keepdims=True) + NORM_EPS) * hd ** -0.5
        kn = kh * lax.rsqrt(jnp.sum(kh * kh, axis=-1, keepdims=True) + NORM_EPS)
        q_out[:, h * hd:(h + 1) * hd] = qn.astype(BF16)
        k_out[:, h * hd:(h + 1) * hd] = kn.astype(BF16)
    v_out[...] = y[:, 2 * GDN_W:3 * GDN_W].astype(BF16)
    m = misc_ref[...]
    a = m + dt_ref[...]
    softplus = jnp.maximum(a, 0.0) + jnp.log(1.0 + jnp.exp(-jnp.abs(a)))
    g = -(jnp.exp(alog_ref[...]) * gmask_ref[...]) * softplus
    beta_out[...] = _sigmoid(m)
    parts = _split3(g)
    gcf_out[...] = sum(jnp.dot(tp_ref[...], p, preferred_element_type=F32) for p in parts)
    gcb_out[...] = sum(jnp.dot(ts_ref[...], p, preferred_element_type=F32) for p in parts)


def gdn_prepare(main, misc, b, length, conv_w, a_log, dt_bias):
    n = b * length
    w3 = 3 * GDN_W
    tm = min(256, length)
    nt = length // tm
    lane0 = MLA_ROPE
    vec = lambda v: jnp.zeros((1, MISC_W), F32).at[0, lane0:lane0 + 2 * GDN_HEADS].set(v.reshape(-1))
    alog, dtb = vec(a_log), vec(dt_bias)
    gmask = vec(jnp.ones((2 * GDN_HEADS,), F32))
    r = np.arange(tm)
    same = (r[:, None] // GDN_CHUNK) == (r[None, :] // GDN_CHUNK)
    tpre = jnp.asarray(same & (r[None, :] <= r[:, None]), BF16)
    tsuf = jnp.asarray(same & (r[None, :] >= r[:, None]), BF16)
    const = lambda a: pl.BlockSpec(a.shape, lambda bi, i: (0,) * a.ndim)
    row = lambda width: pl.BlockSpec((tm, width), lambda bi, i: (bi * nt + i, 0))
    cw = conv_w.astype(F32)
    return pl.pallas_call(
        _gdn_prep_body,
        grid=(b, nt),
        in_specs=_halo_specs(tm, w3, OFF_GDN // w3, nt, n)
                 + [row(MISC_W), const(cw), const(alog), const(dtb), const(gmask), const(tpre), const(tsuf)],
        out_specs=[row(GDN_W), row(GDN_W), row(GDN_W), row(MISC_W), row(MISC_W), row(MISC_W)],
        out_shape=[jax.ShapeDtypeStruct((n, GDN_W), BF16)] * 3 + [jax.ShapeDtypeStruct((n, MISC_W), F32)] * 3,
        compiler_params=_cp(("parallel", "parallel")),
        name="gdn_prep",
    )(main, main, main, misc, cw, alog, dtb, gmask, tpre, tsuf)


def _gdn_chunk_body(qf_ref, kf_ref, vf_ref, qb_ref, kb_ref, vb_ref, gcf_ref, gcb_ref, bcf_ref, bcb_ref,
                    grf_ref, grb_ref, s0_ref, *rest, nc, with_out, bpb):
    if with_out:
        of_ref, ob_ref, sfin_ref, s_ref = rest
    else:
        sfin_ref, s_ref = rest
        of_ref = ob_ref = None
    c = pl.program_id(1)
    nst = 2 * GDN_HEADS

    @pl.when(c == 0)
    def _():
        s_ref[...] = s0_ref[...].reshape(s_ref.shape)

    ch = GDN_CHUNK
    hd = GDN_HEAD_DIM
    ii = lax.broadcasted_iota(jnp.int32, (ch, ch), 0)
    jj = lax.broadcasted_iota(jnp.int32, (ch, ch), 1)
    nt_dims = (((1,), (1,)), ((), ()))
    tn_dims = (((0,), (0,)), ((), ()))
    bdot = lambda a, b_: jnp.dot(a.astype(BF16), b_.astype(BF16), preferred_element_type=F32)
    eye = jnp.where(ii == jj, 1.0, 0.0)
    pair_masks = [((ii >> (l + 1)) == (jj >> (l + 1))) & ((ii >> l) != (jj >> l))
                  for l in range(int(math.log2(ch)))]
    dirs = ((qf_ref, kf_ref, vf_ref, gcf_ref, bcf_ref, grf_ref, of_ref, ii >= jj, ii > jj, ch - 1),
            (qb_ref, kb_ref, vb_ref, gcb_ref, bcb_ref, grb_ref, ob_ref, ii <= jj, ii < jj, 0))
    chains = []
    for bb in range(bpb):
        for d, (q_ref, k_ref, v_ref, gc_ref, bc_ref, gr_ref, o_ref, incl, strict, last_row) in enumerate(dirs):
            for h in range(GDN_HEADS):
                j = d * GDN_HEADS + h
                cols = slice(h * hd, (h + 1) * hd)
                cn = dict(bb=bb, j=j, cols=cols, o_ref=o_ref, incl=incl, strict=strict)
                cn['q'], cn['k'], cn['v'] = q_ref[bb, :, cols], k_ref[bb, :, cols], v_ref[bb, :, cols]
                cn['gc'] = gc_ref[bb, :, j:j + 1]
                cn['gr'] = gr_ref[bb, 0, j:j + 1, :]
                cn['beta'] = bc_ref[bb, :, j:j + 1]
                cn['g_last'] = gc_ref[bb, last_row:last_row + 1, j:j + 1]
                chains.append(cn)
    for cn in chains:
        incl = cn['incl']
        cn['decay'] = jnp.where(incl, jnp.exp(jnp.where(incl, cn['gc'] - cn['gr'], 0.0)), 0.0)
        cn['kf'] = cn['k'].astype(F32)
        cn['kbeta'] = cn['kf'] * cn['beta']
    for cn in chains:
        kk = lax.dot_general(cn['kbeta'].astype(BF16), cn['k'], nt_dims, preferred_element_type=F32)
        cn['a'] = jnp.where(cn['strict'], kk * cn['decay'], 0.0)
    for cn in chains:
        cn['t'] = eye - jnp.where(pair_masks[0], cn['a'], 0.0)
    for pm in pair_masks[1:]:
        for cn in chains:
            cn['tmp'] = bdot(cn['t'], jnp.where(pm, cn['a'], 0.0))
        for cn in chains:
            cn['t'] = cn['t'] - bdot(cn['tmp'], cn['t'])
    for cn in chains:
        rhs = jnp.concatenate([cn['v'].astype(F32) * cn['beta'], cn['kbeta'] * jnp.exp(cn['gc'])], axis=1)
        cn['x'] = bdot(cn['t'], rhs)
    for cn in chains:
        cn['s'] = s_ref[cn['bb'] * nst + cn['j']]
        cn['v_new'] = cn['x'][:, :hd] - bdot(cn['x'][:, hd:], cn['s'])
    if with_out:
        for cn in chains:
            qk = lax.dot_general(cn['q'], cn['k'], nt_dims, preferred_element_type=F32)
            qk = jnp.where(cn['incl'], qk * cn['decay'], 0.0)
            o = bdot(cn['q'].astype(F32) * jnp.exp(cn['gc']), cn['s']) + bdot(qk, cn['v_new'])
            cn['o_ref'][cn['bb'], :, cn['cols']] = o
    for cn in chains:
        kdec = cn['kf'] * jnp.exp(cn['g_last'] - cn['gc'])
        s_ref[cn['bb'] * nst + cn['j']] = cn['s'] * jnp.exp(cn['g_last']) + lax.dot_general(
            kdec.astype(BF16), cn['v_new'].astype(BF16), tn_dims, preferred_element_type=F32)

    @pl.when(c == nc - 1)
    def _():
        sfin_ref[...] = s_ref[...].reshape(sfin_ref.shape)


def gdn_scan(q, k, v, gcol, bcol, grow, s0, b, length, with_out):
    n = b * length
    ch = GDN_CHUNK
    nc = length // ch
    nst = 2 * GDN_HEADS
    bpb = min(GDN_BATCHES_PER_STEP, b)
    fwd = lambda bg, c: (bg, c, 0)
    bwd = lambda bg, c: (bg, nc - 1 - c, 0)
    fwd4 = lambda bg, c: (bg, c, 0, 0)
    bwd4 = lambda bg, c: (bg, nc - 1 - c, 0, 0)
    wide = lambda f: pl.BlockSpec((bpb, ch, GDN_W), f)
    narrow = lambda f: pl.BlockSpec((bpb, ch, nst), f)
    rows = lambda f: pl.BlockSpec((bpb, 1, nst, ch), f)
    state = pl.BlockSpec((bpb, nst, GDN_HEAD_DIM, GDN_HEAD_DIM), lambda bg, c: (bg, 0, 0, 0))
    out_specs = [state]
    out_shape = [jax.ShapeDtypeStruct((b, nst, GDN_HEAD_DIM, GDN_HEAD_DIM), F32)]
    if with_out:
        out_specs = [wide(fwd), wide(bwd)] + out_specs
        out_shape = [jax.ShapeDtypeStruct((b, length, GDN_W), F32)] * 2 + out_shape
    body = functools.partial(_gdn_chunk_body, nc=nc, with_out=with_out, bpb=bpb)
    q3, k3, v3 = (t.reshape(b, length, GDN_W) for t in (q, k, v))
    gcol3, bcol3 = gcol.reshape(b, length, nst), bcol.reshape(b, length, nst)
    grow4 = grow.reshape(b, nc, nst, ch)
    outs = pl.pallas_call(
        body,
        grid=(b // bpb, nc),
        in_specs=[wide(fwd), wide(fwd), wide(fwd), wide(bwd), wide(bwd), wide(bwd),
                  narrow(fwd), narrow(bwd), narrow(fwd), narrow(bwd), rows(fwd4), rows(bwd4), state],
        out_specs=out_specs,
        out_shape=out_shape,
        scratch_shapes=[pltpu.VMEM((bpb * nst, GDN_HEAD_DIM, GDN_HEAD_DIM), F32)],
        compiler_params=_cp(("parallel", "arbitrary")),
        name="gdn_scan",
    )(q3, k3, v3, q3, k3, v3, gcol3, gcol3, bcol3, bcol3, grow4, grow4, s0)
    if with_out:
        return outs[0].reshape(n, GDN_W), outs[1].reshape(n, GDN_W), outs[2]
    return outs


def _gdn_out_body(of_ref, ob_ref, z_ref, nw_ref, y_ref):
    o = of_ref[...] + ob_ref[...]
    z = z_ref[...].astype(F32)
    hd = GDN_HEAD_DIM
    for h in range(GDN_HEADS):
        cols = slice(h * hd, (h + 1) * hd)
        oh = o[:, cols]
        yh = oh * lax.rsqrt(jnp.mean(oh * oh, axis=-1, keepdims=True) + NORM_EPS) * nw_ref[...]
        zh = z[:, cols]
        y_ref[:, cols] = (yh * (zh * _sigmoid(zh))).astype(BF16)


def gdn_output_gate(o_f, o_b, main, norm_w):
    n = o_f.shape[0]
    tm = min(512, n)
    nw = norm_w.reshape(1, GDN_HEAD_DIM).astype(F32)
    return pl.pallas_call(
        _gdn_out_body,
        grid=(n // tm,),
        in_specs=[pl.BlockSpec((tm, GDN_W), lambda i: (i, 0)),
                  pl.BlockSpec((tm, GDN_W), lambda i: (i, 0)),
                  pl.BlockSpec((tm, GDN_W), lambda i: (i, (OFF_GDN + 3 * GDN_W) // GDN_W)),
                  pl.BlockSpec((1, GDN_HEAD_DIM), lambda i: (0, 0))],
        out_specs=pl.BlockSpec((tm, GDN_W), lambda i: (i, 0)),
        out_shape=jax.ShapeDtypeStruct((n, GDN_W), BF16),
        compiler_params=_cp(("parallel",)),
        name="gdn_out",
    )(o_f, o_b, main, nw)


def gdn_branch(main_l, misc_l, main_c, misc_c, b, seq, ctx_len, conv_w, a_log, dt_bias, norm_w, ctx_out):
    nst = 2 * GDN_HEADS
    lane0 = MLA_ROPE

    def gates(gcf, gcb, beta, length):
        gcol = jnp.concatenate([gcf[:, lane0:lane0 + GDN_HEADS], gcb[:, lane0 + GDN_HEADS:lane0 + nst]], axis=1)
        bcol = beta[:, lane0 + nst:lane0 + 2 * nst]
        grow = jnp.transpose(gcol.reshape(-1, GDN_CHUNK, nst), (0, 2, 1))
        return gcol, bcol, grow

    qc, kc, vc, gcf, gcb, beta = gdn_prepare(main_c, misc_c, b, ctx_len, conv_w, a_log, dt_bias)
    gcol_c, bcol_c, grow_c = gates(gcf, gcb, beta, ctx_len)
    ql, kl, vl, gcf, gcb, beta = gdn_prepare(main_l, misc_l, b, seq, conv_w, a_log, dt_bias)
    gcol_l, bcol_l, grow_l = gates(gcf, gcb, beta, seq)
    s0 = jnp.zeros((b, nst, GDN_HEAD_DIM, GDN_HEAD_DIM), F32)
    outs_c = gdn_scan(qc, kc, vc, gcol_c, bcol_c, grow_c, s0, b, ctx_len, ctx_out)
    s_ctx = outs_c[-1]
    of_l, ob_l, _ = gdn_scan(ql, kl, vl, gcol_l, bcol_l, grow_l, s_ctx, b, seq, True)
    out_l = gdn_output_gate(of_l, ob_l, main_l, norm_w)
    out_c = gdn_output_gate(outs_c[0], outs_c[1], main_c, norm_w) if ctx_out else None
    return out_l, out_c


def _hy_conv_body(x_ref, xp_ref, xn_ref, cw_ref, v_out, x1_out, x2_out):
    prev, nxt = _halo_rows(xp_ref, xn_ref)
    y = _conv3(x_ref[...].astype(F32), prev, nxt, cw_ref)
    w = HY_WIDTH
    v_out[...] = y[:, :w].astype(BF16)
    x1_out[...] = y[:, w:2 * w].astype(BF16)
    x2_out[...] = y[:, 2 * w:3 * w].astype(BF16)


def hyena_short_conv(main, b, length, conv_w):
    n = b * length
    w3 = (HY_ORDER + 1) * HY_WIDTH
    tm = min(256, length)
    nt = length // tm
    cw = conv_w.astype(F32)
    row = pl.BlockSpec((tm, HY_WIDTH), lambda bi, i: (bi * nt + i, 0))
    return pl.pallas_call(
        _hy_conv_body,
        grid=(b, nt),
        in_specs=_halo_specs(tm, w3, OFF_HY // w3, nt, n) + [pl.BlockSpec(cw.shape, lambda bi, i: (0, 0))],
        out_specs=[row, row, row],
        out_shape=[jax.ShapeDtypeStruct((n, HY_WIDTH), BF16)] * 3,
        compiler_params=_cp(("parallel", "parallel")),
        name="hyena_conv",
    )(main, main, main, cw)


def _dft_consts(length):
    n = 2 * length
    n2 = 64 if length >= 2048 else 16
    n1 = n // n2
    nk1 = n1 // 2 + 8
    k1 = np.arange(nk1)
    t1 = np.arange(n1 // 2)
    ang1 = 2.0 * np.pi * np.outer(k1, t1) / n1
    f_first = np.concatenate([np.cos(ang1), -np.sin(ang1)], axis=0)
    t2 = np.arange(n2)
    ang2 = 2.0 * np.pi * np.outer(t2, t2) / n2
    c2, s2 = np.cos(ang2), np.sin(ang2)
    g_fwd = np.block([[c2, s2], [-s2, c2]])
    g_inv = g_fwd.T
    angt = 2.0 * np.pi * np.outer(k1, t2) / n
    tw_r, tw_i = np.cos(angt)[:, :, None], -np.sin(angt)[:, :, None]
    tt = np.arange(n1 // 4, 3 * n1 // 4)
    ang3 = 2.0 * np.pi * np.outer(tt, k1) / n1
    fold = np.where((k1 == 0) | (k1 == n1 // 2), 1.0, np.where(k1 < n1 // 2, 2.0, 0.0))[None, :]
    f_last = np.concatenate([np.cos(ang3) * fold, -np.sin(ang3) * fold], axis=1) / n
    bf = lambda a: jnp.asarray(a, BF16)
    return dict(n1=n1, nk1=nk1, n2=n2, f_first=bf(f_first), g_fwd=bf(g_fwd), g_inv=bf(g_inv),
                tw_r=jnp.asarray(tw_r, F32), tw_i=jnp.asarray(tw_i, F32), f_last=bf(f_last))


def _hy_first_body(f_ref, z_ref, a_ref):
    a_ref[0] = jnp.dot(f_ref[...], z_ref[0], preferred_element_type=F32).astype(BF16)


def hyena_dft_first(zv, consts):
    b, half, cols = zv.shape
    n1 = consts['nk1']
    tn = min(4096, cols)
    f = consts['f_first']
    return pl.pallas_call(
        _hy_first_body,
        grid=(b, cols // tn),
        in_specs=[pl.BlockSpec(f.shape, lambda bi, j: (0, 0)),
                  pl.BlockSpec((1, half, tn), lambda bi, j: (bi, 0, j))],
        out_specs=pl.BlockSpec((1, 2 * n1, tn), lambda bi, j: (bi, 0, j)),
        out_shape=jax.ShapeDtypeStruct((b, 2 * n1, cols), BF16),
        compiler_params=_cp(("parallel", "parallel")),
        name="hyena_dft_first",
    )(f, zv)


def _hy_mid_body(a_ref, twr_ref, twi_ref, gf_ref, *rest, kt, spectrum_only):
    if spectrum_only:
        (o_ref,) = rest
    else:
        gi_ref, h_ref, o_ref = rest
    n2 = gf_ref.shape[0] // 2

    def one(i, carry):
        ar = a_ref[0, 0, i].astype(F32)
        ai = a_ref[0, 1, i].astype(F32)
        twr, twi = twr_ref[i], twi_ref[i]
        br = ar * twr - ai * twi
        bi = ar * twi + ai * twr
        z = jnp.dot(gf_ref[...], jnp.concatenate([br, bi], axis=0).astype(BF16), preferred_element_type=F32)
        zr, zi = z[:n2], z[n2:]
        if spectrum_only:
            o_ref[0, 0, i] = zr
            o_ref[0, 1, i] = zi
            return carry
        hr, hi = h_ref[0, i], h_ref[1, i]
        yr = zr * hr - zi * hi
        yi = zr * hi + zi * hr
        w = jnp.dot(gi_ref[...], jnp.concatenate([yr, yi], axis=0).astype(BF16), preferred_element_type=F32)
        wr, wi = w[:n2], w[n2:]
        o_ref[0, 0, i] = (wr * twr + wi * twi).astype(BF16)
        o_ref[0, 1, i] = (wi * twr - wr * twi).astype(BF16)
        return carry

    lax.fori_loop(0, kt, one, 0)


def hyena_dft_mid(a5, consts, spectrum=None):
    b, _, n1, n2, c = a5.shape
    kt = 8
    only = spectrum is None
    blk = pl.BlockSpec((1, 2, kt, n2, c), lambda bi, j: (bi, 0, j, 0, 0))
    tw = pl.BlockSpec((kt, n2, 1), lambda bi, j: (j, 0, 0))
    g = pl.BlockSpec((2 * n2, 2 * n2), lambda bi, j: (0, 0))
    in_specs = [blk, tw, tw, g]
    args = [a5, consts['tw_r'], consts['tw_i'], consts['g_fwd']]
    if not only:
        in_specs += [g, pl.BlockSpec((2, kt, n2, c), lambda bi, j: (0, j, 0, 0))]
        args += [consts['g_inv'], spectrum]
    body = functools.partial(_hy_mid_body, kt=kt, spectrum_only=only)
    return pl.pallas_call(
        body,
        grid=(b, n1 // kt),
        in_specs=in_specs,
        out_specs=blk,
        out_shape=jax.ShapeDtypeStruct(a5.shape, F32 if only else BF16),
        compiler_params=_cp(("parallel", "parallel")),
        name="hyena_dft_mid",
    )(*args)


def _hy_last_body(f_ref, b_ref, z_ref, x_ref, bias_ref, o_ref):
    y = jnp.dot(f_ref[...], b_ref[0], preferred_element_type=F32)
    z = z_ref[0].astype(F32)
    o_ref[0] = (x_ref[0].astype(F32) * (y + bias_ref[...] * z)).astype(BF16)


def hyena_dft_last(bv, zv, xv, bias_row, consts):
    b, rows2, cols = bv.shape
    half = consts['n1'] // 2
    tn = min(4096, cols)
    f = consts['f_last']
    sig = pl.BlockSpec((1, half, tn), lambda bi, j: (bi, 0, j))
    return pl.pallas_call(
        _hy_last_body,
        grid=(b, cols // tn),
        in_specs=[pl.BlockSpec(f.shape, lambda bi, j: (0, 0)),
                  pl.BlockSpec((1, rows2, tn), lambda bi, j: (bi, 0, j)),
                  sig, sig,
                  pl.BlockSpec((1, tn), lambda bi, j: (0, j))],
        out_specs=sig,
        out_shape=jax.ShapeDtypeStruct((b, half, cols), BF16),
        compiler_params=_cp(("parallel", "parallel")),
        name="hyena_dft_last",
    )(f, bv, zv, xv, bias_row)


def hyena_branch(main, b, length, conv_w, filt, bias):
    consts = _dft_consts(length)
    n1, nk1, n2 = consts['n1'], consts['nk1'], consts['n2']
    c = HY_WIDTH
    cols = n2 * c
    view = lambda t: t.reshape(b, n1 // 2, cols)
    v, x1, x2 = (view(t) for t in hyena_short_conv(main, b, length, conv_w))
    hv = jnp.transpose(filt, (1, 0, 2)).astype(BF16).reshape(HY_ORDER, n1 // 2, cols)
    h_first = hyena_dft_first(hv, consts).reshape(HY_ORDER, 2, nk1, n2, c)
    spectra = hyena_dft_mid(h_first, consts)
    z = v
    for o, gate in enumerate((x1, x2)):
        a5 = hyena_dft_first(z, consts).reshape(b, 2, nk1, n2, c)
        bm = hyena_dft_mid(a5, consts, spectra[o]).reshape(b, 2 * nk1, cols)
        bias_row = jnp.tile(bias[o].astype(F32), n2).reshape(1, cols)
        z = hyena_dft_last(bm, z, gate, bias_row, consts)
    return z.reshape(b * length, c)


def _ada_body(c_ref, w_ref, b_ref, o_ref):
    cv = c_ref[...]
    s = cv * _sigmoid(cv)
    o_ref[0] = jnp.dot(s, w_ref[0], precision=lax.Precision.HIGHEST, preferred_element_type=F32) + b_ref[0]


def ada_modulation(c, c_ctx, w_ada, b_ada):
    depth, d, d6 = w_ada.shape
    c8 = jnp.zeros((MOD_ROWS, d), F32).at[:c.shape[0]].set(c).at[CTX_MOD_ROW].set(c_ctx)
    tn = 512
    out = pl.pallas_call(
        _ada_body,
        grid=(depth, d6 // tn),
        in_specs=[pl.BlockSpec((MOD_ROWS, d), lambda l, j: (0, 0)),
                  pl.BlockSpec((1, d, tn), lambda l, j: (l, 0, j)),
                  pl.BlockSpec((1, 1, tn), lambda l, j: (l, 0, j))],
        out_specs=pl.BlockSpec((1, MOD_ROWS, tn), lambda l, j: (l, 0, j)),
        out_shape=jax.ShapeDtypeStruct((depth, MOD_ROWS, d6), F32),
        compiler_params=_cp(("parallel", "parallel")),
        name="ada_mod",
    )(c8, w_ada, b_ada.reshape(depth, 1, d6))
    return out.reshape(depth, MOD_ROWS, 6, d)


def _row_tile(cap, n_rows, per_batch):
    return min(cap, n_rows if per_batch is None else per_batch)


def _mod_row_fn(n_rows, tm, per_batch):
    if per_batch is None:
        return lambda i: CTX_MOD_ROW
    tiles = per_batch // tm
    return lambda i: i // tiles


def _normmod_body(x_ref, nw_ref, sh_ref, sc_ref, o_ref):
    xf = x_ref[...]
    y = xf * lax.rsqrt(jnp.mean(xf * xf, axis=-1, keepdims=True) + NORM_EPS) * nw_ref[...]
    o_ref[...] = (y * (1.0 + sc_ref[0]) + sh_ref[0]).astype(o_ref.dtype)


def norm_modulate(x, nw, shift, scale, per_batch, out_dtype=BF16):
    n, d = x.shape
    tm = _row_tile(512, n, per_batch)
    rf = _mod_row_fn(n, tm, per_batch)
    return pl.pallas_call(
        _normmod_body,
        grid=(n // tm,),
        in_specs=[pl.BlockSpec((tm, d), lambda i: (i, 0)),
                  pl.BlockSpec((1, d), lambda i: (0, 0)),
                  pl.BlockSpec((1, 1, d), lambda i: (rf(i), 0, 0)),
                  pl.BlockSpec((1, 1, d), lambda i: (rf(i), 0, 0))],
        out_specs=pl.BlockSpec((tm, d), lambda i: (i, 0)),
        out_shape=jax.ShapeDtypeStruct((n, d), out_dtype),
        compiler_params=_cp(("parallel",)),
        name="norm_mod",
    )(x, nw.reshape(1, d), shift.reshape(MOD_ROWS, 1, d), scale.reshape(MOD_ROWS, 1, d))


def _mm_body(a_ref, w_ref, o_ref):
    o_ref[...] = jnp.dot(a_ref[...], w_ref[...], preferred_element_type=F32).astype(o_ref.dtype)


def matmul(a, w, out_dtype, tn):
    n, k = a.shape
    m = w.shape[1]
    tm = min(2048, n)
    return pl.pallas_call(
        _mm_body,
        grid=(n // tm, m // tn),
        in_specs=[pl.BlockSpec((tm, k), lambda i, j: (i, 0)),
                  pl.BlockSpec((k, tn), lambda i, j: (0, j))],
        out_specs=pl.BlockSpec((tm, tn), lambda i, j: (i, j)),
        out_shape=jax.ShapeDtypeStruct((n, m), out_dtype),
        compiler_params=_cp(("parallel", "parallel")),
        name="proj",
    )(a, w)


def _mm_res_body(a_ref, w_ref, x_ref, g_ref, o_ref):
    y = jnp.dot(a_ref[...], w_ref[...], preferred_element_type=F32)
    o_ref[...] = x_ref[...] + g_ref[0] * y


def matmul_gated_residual(a, w, x, gate, per_batch):
    n, k = a.shape
    d = w.shape[1]
    tm = _row_tile(1024, n, per_batch)
    tn = min(512, d)
    rf = _mod_row_fn(n, tm, per_batch)
    return pl.pallas_call(
        _mm_res_body,
        grid=(n // tm, d // tn),
        in_specs=[pl.BlockSpec((tm, k), lambda i, j: (i, 0)),
                  pl.BlockSpec((k, tn), lambda i, j: (0, j)),
                  pl.BlockSpec((tm, tn), lambda i, j: (i, j)),
                  pl.BlockSpec((1, 1, tn), lambda i, j: (rf(i), 0, j))],
        out_specs=pl.BlockSpec((tm, tn), lambda i, j: (i, j)),
        out_shape=jax.ShapeDtypeStruct((n, d), F32),
        compiler_params=_cp(("parallel", "parallel")),
        name="out_proj_residual",
    )(a, w, x, gate.reshape(MOD_ROWS, 1, d))


def _merge_body(h_ref, b0_ref, b1_ref, b2_ref, b3_ref, wg_ref, wb_ref, o_ref, acc_ref):
    n = pl.program_id(2)

    @pl.when(n == 0)
    def _():
        acc_ref[...] = jnp.zeros_like(acc_ref)

    gate = _sigmoid(jnp.dot(h_ref[...], wg_ref[...], preferred_element_type=F32))
    for idx, b_ref in enumerate((b0_ref, b1_ref, b2_ref, b3_ref)):
        @pl.when(n == idx)
        def _(b_ref=b_ref):
            acc_ref[...] += gate * jnp.dot(b_ref[...], wb_ref[0], preferred_element_type=F32)

    @pl.when(n == N_BRANCH - 1)
    def _():
        o_ref[...] = acc_ref[...].astype(o_ref.dtype)


def merge_branches_gated(h, branches, w_gate, w_branch):
    n, d = h.shape
    bw = branches[0].shape[1]
    tm = min(1024, n)
    tn = min(512, d)
    nj = d // tn
    return pl.pallas_call(
        _merge_body,
        grid=(n // tm, nj, N_BRANCH),
        in_specs=[pl.BlockSpec((tm, d), lambda i, j, b: (i, 0))]
                 + [pl.BlockSpec((tm, bw), lambda i, j, b: (i, 0))] * N_BRANCH
                 + [pl.BlockSpec((d, tn), lambda i, j, b: (0, b * nj + j)),
                    pl.BlockSpec((1, bw, tn), lambda i, j, b: (b, 0, j))],
        out_specs=pl.BlockSpec((tm, tn), lambda i, j, b: (i, j)),
        out_shape=jax.ShapeDtypeStruct((n, d), BF16),
        scratch_shapes=[pltpu.VMEM((tm, tn), F32)],
        compiler_params=_cp(("parallel", "parallel", "arbitrary")),
        name="gate_merge",
    )(h, *branches, w_gate, w_branch)


def _mla_prep_body(cq_ref, ckv_ref, misc_ref, qnw_ref, kvnw_ref, wqa_ref, wqb_ref, wk_ref, wv_ref,
                   ska_ref, skb_ref, cq_tab, sq_tab, q_out, k_out, v_out):
    def norm(v, w_ref):
        vf = v.astype(F32)
        return (vf * lax.rsqrt(jnp.mean(vf * vf, axis=-1, keepdims=True) + NORM_EPS) * w_ref[...]).astype(BF16)

    xq = norm(cq_ref[...], qnw_ref)
    xkv = norm(ckv_ref[...], kvnw_ref)
    cos, sin = cq_tab[...], sq_tab[...]
    misc = misc_ref[...].astype(BF16)
    kr = (jnp.dot(misc, ska_ref[...], preferred_element_type=F32) * cos
          + jnp.dot(misc, skb_ref[...], preferred_element_type=F32) * sin)
    for h in range(MLA_HEADS):
        qa = jnp.dot(xq, wqa_ref[h], preferred_element_type=F32)
        qb = jnp.dot(xq, wqb_ref[h], preferred_element_type=F32)
        q_out[0, h] = (qa * cos + qb * sin).astype(BF16)
        k_out[0, h] = (jnp.dot(xkv, wk_ref[h], preferred_element_type=F32) + kr).astype(BF16)
        v_out[0, h] = jnp.dot(xkv, wv_ref[h], preferred_element_type=F32).astype(BF16)


def _mla_weights(q_norm_w, kv_norm_w, w_uq, w_ukv):
    dk = MLA_NOPE + MLA_ROPE
    half = MLA_ROPE // 2
    scale = dk ** -0.5
    wq = jnp.transpose(w_uq, (1, 0, 2)) * scale
    nope0 = jnp.zeros(wq.shape[:2] + (MLA_NOPE,), F32)
    wq_rot = jnp.concatenate([nope0, -wq[..., MLA_NOPE + half:], wq[..., MLA_NOPE:MLA_NOPE + half]], axis=-1)
    wkv = jnp.transpose(w_ukv, (1, 0, 2))
    wk = jnp.concatenate([wkv[..., :MLA_NOPE], jnp.zeros(wkv.shape[:2] + (MLA_ROPE,), F32)], axis=-1)
    wv = wkv[..., MLA_NOPE:]
    eye = jnp.eye(MLA_ROPE, dtype=F32)
    rot = jnp.concatenate([-eye[:, half:], eye[:, :half]], axis=-1)
    pad_r = MISC_W - MLA_ROPE
    ska = jnp.pad(eye, ((0, pad_r), (MLA_NOPE, 0)))
    skb = jnp.pad(rot, ((0, pad_r), (MLA_NOPE, 0)))
    return tuple(t.astype(BF16) for t in (wq, wq_rot, wk, wv, ska, skb))


def _mla_tables(rope, length):
    dk = MLA_NOPE + MLA_ROPE
    if rope is None:
        return jnp.ones((length, dk), F32), jnp.zeros((length, dk), F32)
    cos, sin = rope
    ones = jnp.ones((length, MLA_NOPE), F32)
    return (jnp.concatenate([ones, cos, cos], axis=-1),
            jnp.concatenate([0.0 * ones, sin, sin], axis=-1))


def mla_prepare(main, misc, b, length, weights, q_norm_w, kv_norm_w, tables):
    wq, wq_rot, wk, wv, ska, skb = weights
    cos, sin = tables
    dk = MLA_NOPE + MLA_ROPE
    tm = min(512, length)
    nt = length // tm
    full = lambda a: pl.BlockSpec(a.shape, lambda bi, i: (0,) * a.ndim)
    qnw = q_norm_w.reshape(1, -1)
    kvnw = kv_norm_w.reshape(1, -1)
    outs = pl.pallas_call(
        _mla_prep_body,
        grid=(b, nt),
        in_specs=[pl.BlockSpec((tm, MLA_Q_LORA), lambda bi, i: (bi * nt + i, OFF_CQ // MLA_Q_LORA)),
                  pl.BlockSpec((tm, MLA_KV_LORA), lambda bi, i: (bi * nt + i, OFF_CKV // MLA_KV_LORA)),
                  pl.BlockSpec((tm, MISC_W), lambda bi, i: (bi * nt + i, 0)),
                  full(qnw), full(kvnw), full(wq), full(wq_rot), full(wk), full(wv), full(ska), full(skb),
                  pl.BlockSpec((tm, dk), lambda bi, i: (i, 0)),
                  pl.BlockSpec((tm, dk), lambda bi, i: (i, 0))],
        out_specs=[pl.BlockSpec((1, MLA_HEADS, tm, dk), lambda bi, i: (bi, 0, i, 0)),
                   pl.BlockSpec((1, MLA_HEADS, tm, dk), lambda bi, i: (bi, 0, i, 0)),
                   pl.BlockSpec((1, MLA_HEADS, tm, MLA_V), lambda bi, i: (bi, 0, i, 0))],
        out_shape=[jax.ShapeDtypeStruct((b, MLA_HEADS, length, dk), BF16),
                   jax.ShapeDtypeStruct((b, MLA_HEADS, length, dk), BF16),
                   jax.ShapeDtypeStruct((b, MLA_HEADS, length, MLA_V), BF16)],
        compiler_params=_cp(("parallel", "parallel")),
        name="mla_prep",
    )(main, main, misc, qnw, kvnw, wq, wq_rot, wk, wv, ska, skb, cos, sin)
    return outs


def _gqa_prep_body(q_ref, k_ref, v_ref, qnw_ref, knw_ref, gsum_ref, rot_ref, cos_ref, sin_ref,
                   q_out, k_out, v_out):
    cos, sin = cos_ref[...], sin_ref[...]

    def prep(v, nw, width):
        vf = v.astype(F32)
        sq = vf * vf
        hi = sq.astype(BF16)
        lo = (sq - hi.astype(F32)).astype(BF16)
        g = gsum_ref[:width, :width]
        ss = jnp.dot(hi, g, preferred_element_type=F32) + jnp.dot(lo, g, preferred_element_type=F32)
        xn = vf * lax.rsqrt(ss * (1.0 / GQA_HEAD_DIM) + NORM_EPS) * nw
        xr = jnp.dot(xn.astype(BF16), rot_ref[:width, :width], preferred_element_type=F32)
        return xn * cos[:, :width] + xr * sin[:, :width]

    qf = prep(q_ref[...], qnw_ref[...], GQA_HEADS * GQA_HEAD_DIM) * GQA_HEAD_DIM ** -0.5
    kf = prep(k_ref[...], knw_ref[...], LANES)
    q_out[...] = qf.astype(BF16)
    k_out[...] = kf.astype(BF16)
    v_out[...] = v_ref[...]


def gqa_prepare(main, b, length, q_norm_w, k_norm_w, rope):
    n = b * length
    qw = GQA_HEADS * GQA_HEAD_DIM
    kw = GQA_KV_HEADS * GQA_HEAD_DIM
    half = GQA_HEAD_DIM // 2
    if rope is None:
        cos = jnp.ones((length, qw), F32)
        sin = jnp.zeros((length, qw), F32)
    else:
        cos = jnp.tile(jnp.concatenate([rope[0], rope[0]], axis=-1), (1, GQA_HEADS))
        sin = jnp.tile(jnp.concatenate([rope[1], rope[1]], axis=-1), (1, GQA_HEADS))
    head = np.arange(qw) // GQA_HEAD_DIM
    gsum = jnp.asarray(head[:, None] == head[None, :], BF16)
    eye = np.eye(GQA_HEAD_DIM, dtype=np.float32)
    rot1 = np.concatenate([-eye[:, half:], eye[:, :half]], axis=-1)
    rot = jnp.asarray(np.kron(np.eye(GQA_HEADS, dtype=np.float32), rot1), BF16)
    tm = min(512, length)
    nt = length // tm
    full = lambda a: pl.BlockSpec(a.shape, lambda bi, i: (0,) * a.ndim)
    qnw = jnp.tile(q_norm_w, GQA_HEADS).reshape(1, qw)
    knw = jnp.tile(k_norm_w, GQA_KV_HEADS).reshape(1, kw)
    return pl.pallas_call(
        _gqa_prep_body,
        grid=(b, nt),
        in_specs=[pl.BlockSpec((tm, qw), lambda bi, i: (bi * nt + i, OFF_GQ // qw)),
                  pl.BlockSpec((tm, kw), lambda bi, i: (bi * nt + i, OFF_GK // kw)),
                  pl.BlockSpec((tm, kw), lambda bi, i: (bi * nt + i, OFF_GV // kw)),
                  full(qnw), full(knw), full(gsum), full(rot),
                  pl.BlockSpec((tm, qw), lambda bi, i: (i, 0)),
                  pl.BlockSpec((tm, qw), lambda bi, i: (i, 0))],
        out_specs=[pl.BlockSpec((tm, qw), lambda bi, i: (bi * nt + i, 0)),
                   pl.BlockSpec((tm, kw), lambda bi, i: (bi * nt + i, 0)),
                   pl.BlockSpec((tm, kw), lambda bi, i: (bi * nt + i, 0))],
        out_shape=[jax.ShapeDtypeStruct((n, qw), BF16),
                   jax.ShapeDtypeStruct((n, kw), BF16),
                   jax.ShapeDtypeStruct((n, kw), BF16)],
        compiler_params=_cp(("parallel", "parallel")),
        name="gqa_prep",
    )(main, main, main, qnw, knw, gsum, rot, cos, sin)


FLASH_CHAIN_ROWS = 256


def _flash_body(q_ref, k_ref, v_ref, kc_ref, vc_ref, o_ref, m_ref, l_ref, acc_ref, *, nk, has_ctx, chains):
    ki = pl.program_id(3)

    @pl.when(ki == 0)
    def _():
        m_ref[...] = jnp.full_like(m_ref, -jnp.inf)
        l_ref[...] = jnp.zeros_like(l_ref)
        acc_ref[...] = jnp.zeros_like(acc_ref)

    def step(k, vt):
        ss = [jnp.dot(k, q_ref[0, 0, gi, :, r0:r0 + rc], preferred_element_type=F32)
              for gi, r0, rc in chains]
        m_prev = [m_ref[ci] for ci in range(len(chains))]
        m_new = [jnp.maximum(mp, jnp.max(s, axis=0, keepdims=True)) for mp, s in zip(m_prev, ss)]
        ps = [jnp.exp(s - mn) for s, mn in zip(ss, m_new)]
        alphas = [jnp.exp(mp - mn) for mp, mn in zip(m_prev, m_new)]
        pv = [jnp.dot(vt, p.astype(BF16), preferred_element_type=F32) for p in ps]
        for ci in range(len(chains)):
            l_ref[ci] = alphas[ci] * l_ref[ci] + jnp.sum(ps[ci], axis=0, keepdims=True)
            acc_ref[ci] = alphas[ci] * acc_ref[ci] + pv[ci]
            m_ref[ci] = m_new[ci]

    @pl.when(ki < nk)
    def _():
        step(k_ref[0, 0], v_ref[0, 0])

    if has_ctx:
        @pl.when(ki == nk)
        def _():
            step(kc_ref[0, 0], vc_ref[0, 0])

    @pl.when(ki == nk - 1 + int(has_ctx))
    def _():
        for ci, (gi, r0, rc) in enumerate(chains):
            o_ref[0, 0, gi, :, r0:r0 + rc] = (acc_ref[ci] / l_ref[ci]).astype(o_ref.dtype)


def flash_attention(q, k, v, kc, vc, tq, tk):
    b, hkv, g, sq, dk = q.shape
    sk = k.shape[2]
    dv = v.shape[3]
    tq = min(tq, sq)
    tk = min(tk, sk)
    nk = sk // tk
    has_ctx = kc is not None
    if not has_ctx:
        kc, vc = k[:, :, :LANES], v[:, :, :LANES]
    skc = kc.shape[2]
    rc = min(FLASH_CHAIN_ROWS, tq)
    chains = tuple((gi, r0, rc) for gi in range(g) for r0 in range(0, tq, rc))
    nch = len(chains)
    body = functools.partial(_flash_body, nk=nk, has_ctx=has_ctx, chains=chains)
    qt = jnp.swapaxes(q, 3, 4)
    vt = jnp.swapaxes(v, 2, 3)
    vct = jnp.swapaxes(vc, 2, 3)
    out_t = pl.pallas_call(
        body,
        grid=(b, hkv, sq // tq, nk + int(has_ctx)),
        in_specs=[pl.BlockSpec((1, 1, g, dk, tq), lambda bi, h, qi, ki: (bi, h, 0, 0, qi)),
                  pl.BlockSpec((1, 1, tk, dk), lambda bi, h, qi, ki: (bi, h, jnp.minimum(ki, nk - 1), 0)),
                  pl.BlockSpec((1, 1, dv, tk), lambda bi, h, qi, ki: (bi, h, 0, jnp.minimum(ki, nk - 1))),
                  pl.BlockSpec((1, 1, skc, dk), lambda bi, h, qi, ki: (bi, h, 0, 0)),
                  pl.BlockSpec((1, 1, dv, skc), lambda bi, h, qi, ki: (bi, h, 0, 0))],
        out_specs=pl.BlockSpec((1, 1, g, dv, tq), lambda bi, h, qi, ki: (bi, h, 0, 0, qi)),
        out_shape=jax.ShapeDtypeStruct((b, hkv, g, dv, sq), BF16),
        scratch_shapes=[pltpu.VMEM((nch, 1, rc), F32), pltpu.VMEM((nch, 1, rc), F32),
                        pltpu.VMEM((nch, dv, rc), F32)],
        compiler_params=_cp(("parallel", "parallel", "parallel", "arbitrary")),
        name="flash_attention",
    )(qt, k, vt, kc, vct)
    return jnp.swapaxes(out_t, 3, 4)


def _router_body(x_ref, nw_ref, sh_ref, sc_ref, wrh_ref, wrl_ref, rb_ref, tri_ref,
                 h_ref, ri_ref, rw_ref, cnt_ref, carry_ref):
    i = pl.program_id(0)

    @pl.when(i == 0)
    def _():
        carry_ref[...] = jnp.zeros_like(carry_ref)

    xf = x_ref[...]
    y = xf * lax.rsqrt(jnp.mean(xf * xf, axis=-1, keepdims=True) + NORM_EPS) * nw_ref[...]
    h = y * (1.0 + sc_ref[0]) + sh_ref[0]
    h_ref[...] = h
    hi = h.astype(BF16)
    lo = (h - hi.astype(F32)).astype(BF16)
    nt = (((1,), (1,)), ((), ()))
    logits = (lax.dot_general(wrh_ref[...], hi, nt, preferred_element_type=F32)
              + lax.dot_general(wrh_ref[...], lo, nt, preferred_element_type=F32)
              + lax.dot_general(wrl_ref[...], hi, nt, preferred_element_type=F32))
    scores = _sigmoid(logits)
    sel = scores + rb_ref[...]
    s = [scores[e:e + 1] for e in range(N_EXPERTS)]
    v = [sel[e:e + 1] for e in range(N_EXPERTS)]
    epg = EXPERTS_PER_GROUP
    gscore = []
    for gi in range(N_GROUPS):
        mem = v[gi * epg:(gi + 1) * epg]
        best = None
        for a in range(epg):
            for c in range(a + 1, epg):
                pair = mem[a] + mem[c]
                best = pair if best is None else jnp.maximum(best, pair)
        gscore.append(best)
    is_best = []
    for gi in range(N_GROUPS):
        ok = None
        for gj in range(N_GROUPS):
            if gj == gi:
                continue
            c = (gscore[gi] > gscore[gj]) if gj < gi else (gscore[gi] >= gscore[gj])
            ok = c if ok is None else (ok & c)
        is_best.append(ok)
    chosen = []
    for e in range(N_EXPERTS):
        gi = e // epg
        rank = jnp.zeros_like(v[e])
        for e2 in range(gi * epg, (gi + 1) * epg):
            if e2 == e:
                continue
            ahead = (v[e2] >= v[e]) if e2 < e else (v[e2] > v[e])
            rank = rank + jnp.where(ahead, 1.0, 0.0)
        chosen.append(is_best[gi] & (rank < TOP_K))
    chosen_f = jnp.concatenate([jnp.where(cm, 1.0, 0.0) for cm in chosen], axis=0)
    total = None
    for e in range(N_EXPERTS):
        t = jnp.where(chosen[e], s[e], 0.0)
        total = t if total is None else total + t
    pos = jnp.dot(chosen_f.astype(BF16), tri_ref[...], preferred_element_type=F32) + carry_ref[...]
    carry_ref[...] += jnp.sum(chosen_f, axis=-1, keepdims=True)
    cnt_ref[...] = jnp.broadcast_to(carry_ref[...], cnt_ref.shape)
    e_lo = jnp.full_like(v[0], float(N_EXPERTS))
    e_hi = jnp.full_like(v[0], -1.0)
    for e in range(N_EXPERTS):
        e_lo = jnp.where(chosen[e], jnp.minimum(e_lo, float(e)), e_lo)
        e_hi = jnp.where(chosen[e], jnp.maximum(e_hi, float(e)), e_hi)
    zero = jnp.zeros_like(v[0])
    w_lo, w_hi, p_lo, p_hi = zero, zero, zero, zero
    for e in range(N_EXPERTS):
        pe = pos[e:e + 1]
        w_lo = jnp.where(e_lo == float(e), s[e], w_lo)
        w_hi = jnp.where(e_hi == float(e), s[e], w_hi)
        p_lo = jnp.where(e_lo == float(e), pe, p_lo)
        p_hi = jnp.where(e_hi == float(e), pe, p_hi)
    ri_ref[...] = jnp.concatenate([e_lo, e_hi, p_lo, p_hi, zero, zero, zero, zero], axis=0).astype(jnp.int32)
    rw_ref[...] = jnp.concatenate([w_lo / total, w_hi / total, zero, zero, zero, zero, zero, zero], axis=0)


def norm_route(x, nw, shift, scale, per_batch, w_router, router_bias):
    n, d = x.shape
    tm = _row_tile(512, n, per_batch)
    rf = _mod_row_fn(n, tm, per_batch)
    wr_t = w_router.T
    wr_hi = wr_t.astype(BF16)
    wr_lo = (wr_t - wr_hi.astype(F32)).astype(BF16)
    tri = jnp.asarray(np.triu(np.ones((tm, tm), np.float32), 1), BF16)
    const = lambda a: pl.BlockSpec(a.shape, lambda i: (0,) * a.ndim)
    rb = router_bias.reshape(N_EXPERTS, 1).astype(F32)
    return pl.pallas_call(
        _router_body,
        grid=(n // tm,),
        in_specs=[pl.BlockSpec((tm, d), lambda i: (i, 0)),
                  pl.BlockSpec((1, d), lambda i: (0, 0)),
                  pl.BlockSpec((1, 1, d), lambda i: (rf(i), 0, 0)),
                  pl.BlockSpec((1, 1, d), lambda i: (rf(i), 0, 0)),
                  const(wr_hi), const(wr_lo), const(rb), const(tri)],
        out_specs=[pl.BlockSpec((tm, d), lambda i: (i, 0)),
                   pl.BlockSpec((8, tm), lambda i: (0, i)),
                   pl.BlockSpec((8, tm), lambda i: (0, i)),
                   pl.BlockSpec((N_EXPERTS, LANES), lambda i: (0, 0))],
        out_shape=[jax.ShapeDtypeStruct((n, d), F32),
                   jax.ShapeDtypeStruct((8, n), jnp.int32),
                   jax.ShapeDtypeStruct((8, n), F32),
                   jax.ShapeDtypeStruct((N_EXPERTS, LANES), F32)],
        scratch_shapes=[pltpu.VMEM((N_EXPERTS, 1), F32)],
        compiler_params=_cp(("arbitrary",)),
        name="norm_route",
    )(x, nw.reshape(1, d), shift.reshape(MOD_ROWS, 1, d), scale.reshape(MOD_ROWS, 1, d),
      wr_hi, wr_lo, rb, tri)


def _dispatch_body(sa_ref, sb_ref, pad_ref, h_ref, xs_ref, zero_ref, sem, *, tm, n_pad):
    i = pl.program_id(0)
    base = i * tm

    def row_copy(src, r, slot):
        return pltpu.make_async_copy(src.at[pl.ds(r, 1)], xs_ref.at[pl.ds(slot, 1)], sem)

    @pl.when(i == 0)
    def _():
        zero_ref[...] = jnp.zeros_like(zero_ref)

        def fill(j, carry):
            row_copy(zero_ref, 0, pad_ref[2 * j]).start(priority=0)
            row_copy(zero_ref, 1, pad_ref[2 * j + 1]).start(priority=1)
            return carry
        lax.fori_loop(0, n_pad // 2, fill, 0, unroll=DMA_UNROLL)

        def drain(j, carry):
            row_copy(zero_ref, 0, 0).wait()
            return carry
        lax.fori_loop(0, n_pad, drain, 0, unroll=DMA_UNROLL)

    def issue(r, carry):
        row_copy(h_ref, r, sa_ref[base + r]).start(priority=0)
        row_copy(h_ref, r, sb_ref[base + r]).start(priority=1)
        return carry
    lax.fori_loop(0, tm, issue, 0, unroll=DMA_UNROLL)

    def drain2(r, carry):
        row_copy(h_ref, 0, 0).wait()
        row_copy(h_ref, 0, 0).wait()
        return carry
    lax.fori_loop(0, tm, drain2, 0, unroll=DMA_UNROLL)


def moe_dispatch(h, slot_a, slot_b, pad_slots, n_slots):
    n, d = h.shape
    tm = min(256, n)
    n_pad = pad_slots.shape[0]
    body = functools.partial(_dispatch_body, tm=tm, n_pad=n_pad)
    return pl.pallas_call(
        body,
        grid_spec=pltpu.PrefetchScalarGridSpec(
            num_scalar_prefetch=3,
            grid=(n // tm,),
            in_specs=[pl.BlockSpec((tm, d), lambda i, sa, sb, pd: (i, 0))],
            out_specs=pl.BlockSpec(memory_space=pl.ANY),
            scratch_shapes=[pltpu.VMEM((8, d), F32), pltpu.SemaphoreType.DMA(())]),
        out_shape=jax.ShapeDtypeStruct((n_slots, d), F32),
        compiler_params=_cp(("arbitrary",)),
        name="moe_dispatch",
    )(slot_a, slot_b, pad_slots, h)


def _experts_body(te_ref, nu_ref, xs_ref, wg_ref, wu_ref, wd_ref, y_ref):
    i = pl.program_id(0)

    @pl.when(i < nu_ref[0])
    def _():
        xb = xs_ref[...].astype(BF16)
        hg = jnp.dot(xb, wg_ref[0], preferred_element_type=F32)
        hu = jnp.dot(xb, wu_ref[0], preferred_element_type=F32)
        act = (hg * _sigmoid(hg) * hu).astype(BF16)
        y_ref[...] = jnp.dot(act, wd_ref[0], preferred_element_type=F32)

    @pl.when(i >= nu_ref[0])
    def _():
        y_ref[...] = jnp.zeros_like(y_ref)


def moe_experts(xs, tile_expert, n_used, w_gate, w_up, w_down):
    s, d = xs.shape
    f = w_gate.shape[2]
    tm = MOE_TILE
    return pl.pallas_call(
        _experts_body,
        grid_spec=pltpu.PrefetchScalarGridSpec(
            num_scalar_prefetch=2,
            grid=(s // tm,),
            in_specs=[pl.BlockSpec((tm, d), lambda i, te, nu: (jnp.minimum(i, nu[0] - 1), 0)),
                      pl.BlockSpec((1, d, f), lambda i, te, nu: (te[i], 0, 0)),
                      pl.BlockSpec((1, d, f), lambda i, te, nu: (te[i], 0, 0)),
                      pl.BlockSpec((1, f, d), lambda i, te, nu: (te[i], 0, 0))],
            out_specs=pl.BlockSpec((tm, d), lambda i, te, nu: (i, 0))),
        out_shape=jax.ShapeDtypeStruct((s, d), F32),
        compiler_params=_cp(("arbitrary",)),
        name="moe_experts",
    )(tile_expert, n_used, xs, w_gate, w_up, w_down)


DMA_UNROLL = 8


def _combine_body(sa_ref, sb_ref, x_ref, w_ref, g_ref, y_ref, o_ref, ba_ref, bb_ref, sem, *, tm):
    i = pl.program_id(0)
    n_tiles = pl.num_programs(0)

    def row_copy(slot, dst, buf, r):
        return pltpu.make_async_copy(y_ref.at[pl.ds(slot, 1)], dst.at[buf, pl.ds(r, 1)], sem.at[buf])

    def issue_tile(tile, buf):
        base = tile * tm

        def issue(r, carry):
            row_copy(sa_ref[base + r], ba_ref, buf, r).start(priority=0)
            row_copy(sb_ref[base + r], bb_ref, buf, r).start(priority=1)
            return carry
        lax.fori_loop(0, tm, issue, 0, unroll=DMA_UNROLL)

    @pl.when(i == 0)
    def _():
        issue_tile(0, 0)

    @pl.when(i + 1 < n_tiles)
    def _():
        issue_tile(i + 1, (i + 1) % 2)

    buf = i % 2

    def drain(r, carry):
        row_copy(0, ba_ref, buf, 0).wait()
        row_copy(0, bb_ref, buf, 0).wait()
        return carry
    lax.fori_loop(0, tm, drain, 0, unroll=DMA_UNROLL)

    w = w_ref[...]
    mix = w[:, 0:1] * ba_ref[buf] + w[:, 1:2] * bb_ref[buf]
    o_ref[...] = x_ref[...] + g_ref[0] * mix


def moe_combine(x, y, slot_a, slot_b, wts, gate, per_batch):
    n, d = x.shape
    tm = _row_tile(256, n, per_batch)
    rf = _mod_row_fn(n, tm, per_batch)
    body = functools.partial(_combine_body, tm=tm)
    return pl.pallas_call(
        body,
        grid_spec=pltpu.PrefetchScalarGridSpec(
            num_scalar_prefetch=2,
            grid=(n // tm,),
            in_specs=[pl.BlockSpec((tm, d), lambda i, sa, sb: (i, 0)),
                      pl.BlockSpec((tm, 8), lambda i, sa, sb: (i, 0)),
                      pl.BlockSpec((1, 1, d), lambda i, sa, sb: (rf(i), 0, 0)),
                      pl.BlockSpec(memory_space=pl.ANY)],
            out_specs=pl.BlockSpec((tm, d), lambda i, sa, sb: (i, 0)),
            scratch_shapes=[pltpu.VMEM((2, tm, d), F32), pltpu.VMEM((2, tm, d), F32),
                            pltpu.SemaphoreType.DMA((2,))]),
        out_shape=jax.ShapeDtypeStruct((n, d), F32),
        compiler_params=_cp(("arbitrary",)),
        name="moe_combine",
    )(slot_a, slot_b, x, wts, gate.reshape(MOD_ROWS, 1, d), y)


def moe_layer(x, nw, mod, per_batch, w_router, router_bias, w_gate, w_up, w_down):
    n, d = x.shape
    h, route_i, route_w, counts = norm_route(x, nw, mod[:, 3], mod[:, 4], per_batch, w_router, router_bias)
    cnt = counts[:, 0].astype(jnp.int32)
    seg = ((cnt + MOE_TILE - 1) // MOE_TILE) * MOE_TILE
    off = jnp.concatenate([jnp.zeros((1,), jnp.int32), jnp.cumsum(seg)])
    n_slots = TOP_K * n + N_EXPERTS * MOE_TILE
    slot_a = off[route_i[0]] + route_i[2]
    slot_b = off[route_i[1]] + route_i[3]
    n_pad = n_slots - TOP_K * n
    padcnt = seg - cnt
    padstart = jnp.concatenate([jnp.zeros((1,), jnp.int32), jnp.cumsum(padcnt)])
    j = jnp.arange(n_pad, dtype=jnp.int32)
    count_le = lambda edges, v: jnp.sum((edges[None, :] <= v[:, None]).astype(jnp.int32), axis=1)
    e_of = jnp.clip(count_le(padstart, j) - 1, 0, N_EXPERTS)
    in_seg = off[jnp.minimum(e_of, N_EXPERTS - 1)] + cnt[jnp.minimum(e_of, N_EXPERTS - 1)] + (j - padstart[e_of])
    tail = off[N_EXPERTS] + (j - padstart[N_EXPERTS])
    pad_slots = jnp.where(e_of < N_EXPERTS, in_seg, tail).astype(jnp.int32)
    n_tiles = n_slots // MOE_TILE
    tile_start = jnp.arange(n_tiles, dtype=jnp.int32) * MOE_TILE
    n_used = (off[N_EXPERTS] // MOE_TILE).astype(jnp.int32).reshape(1)
    tile_expert = jnp.clip(count_le(off, tile_start) - 1, 0, N_EXPERTS - 1).astype(jnp.int32)
    last_used = tile_expert[jnp.maximum(n_used[0] - 1, 0)]
    tile_expert = jnp.where(jnp.arange(n_tiles) < n_used[0], tile_expert, last_used)

    xs = moe_dispatch(h, slot_a, slot_b, pad_slots, n_slots)
    y = moe_experts(xs, tile_expert, n_used, w_gate, w_up, w_down)
    wts = jnp.transpose(route_w)
    return moe_combine(x, y, slot_a, slot_b, wts, mod[:, 5], per_batch)


def _final_norm_body(x_ref, w_ref, o_ref):
    xf = x_ref[...]
    y = xf * lax.rsqrt(jnp.mean(xf * xf, axis=-1, keepdims=True) + NORM_EPS)
    o_ref[...] = y * w_ref[...]


def final_rms_norm(x, w):
    n, d = x.shape
    rows = 512
    return pl.pallas_call(
        _final_norm_body,
        grid=(n // rows,),
        in_specs=[pl.BlockSpec((rows, d), lambda i: (i, 0)), pl.BlockSpec((1, d), lambda i: (0, 0))],
        out_specs=pl.BlockSpec((rows, d), lambda i: (i, 0)),
        out_shape=jax.ShapeDtypeStruct((n, d), x.dtype),
        compiler_params=_cp(("parallel",)),
        name="final_norm",
    )(x, w.reshape(1, d))


def _reorder_w_in(w):
    o = np.cumsum((0,) + IN_WIDTHS)
    seg = lambda i: w[:, o[i]:o[i + 1]]
    main = jnp.concatenate([seg(0), seg(1), seg(2), seg(3), seg(6), seg(9), seg(12), seg(7), seg(10), seg(11)], axis=1)
    misc = jnp.concatenate([seg(8), seg(4), seg(5)], axis=1)
    misc = jnp.pad(misc, ((0, 0), (0, MISC_W - misc.shape[1])))
    return main.astype(BF16), misc.astype(BF16)


def _attention_branches(main_l, misc_l, main_c, misc_c, b, seq, ctx_len, ctx_out, rope_mla, rope_gqa,
                        mla_w, mla_qn, mla_kvn, gqa_qn, gqa_kn):
    g = GQA_HEADS // GQA_KV_HEADS
    hd = GQA_HEAD_DIM

    mq_l, mk_l, mv_l = mla_prepare(main_l, misc_l, b, seq, mla_w, mla_qn, mla_kvn, _mla_tables(rope_mla, seq))
    mq_c, mk_c, mv_c = mla_prepare(main_c, misc_c, b, ctx_len, mla_w, mla_qn, mla_kvn, _mla_tables(None, ctx_len))
    tk_all = (seq + ctx_len) // 2
    cat = lambda lat, ctx_: jnp.concatenate([lat, ctx_], axis=2)
    mla_l = flash_attention(mq_l[:, :, None], cat(mk_l, mk_c), cat(mv_l, mv_c), None, None, 1024, tk_all)
    mla_l = jnp.transpose(mla_l[:, :, 0], (0, 2, 1, 3)).reshape(b * seq, BRANCH_W)

    def split_heads(t, length, heads):
        return jnp.transpose(t.reshape(b, length, heads, hd), (0, 2, 1, 3))

    gq_l, gk_l, gv_l = gqa_prepare(main_l, b, seq, gqa_qn, gqa_kn, rope_gqa)
    gq_c, gk_c, gv_c = gqa_prepare(main_c, b, ctx_len, gqa_qn, gqa_kn, None)
    gq_l5 = split_heads(gq_l, seq, GQA_HEADS).reshape(b, GQA_KV_HEADS, g, seq, hd)
    gk_l4, gv_l4 = split_heads(gk_l, seq, GQA_KV_HEADS), split_heads(gv_l, seq, GQA_KV_HEADS)
    gk_c4, gv_c4 = split_heads(gk_c, ctx_len, GQA_KV_HEADS), split_heads(gv_c, ctx_len, GQA_KV_HEADS)
    gqa_l = flash_attention(gq_l5, cat(gk_l4, gk_c4), cat(gv_l4, gv_c4), None, None, 256, tk_all)
    gqa_l = jnp.transpose(gqa_l.reshape(b, GQA_HEADS, seq, hd), (0, 2, 1, 3)).reshape(b * seq, BRANCH_W)

    mla_c = gqa_c = None
    if ctx_out:
        mla_c = flash_attention(mq_c[:, :, None], mk_c, mv_c, None, None, 256, 256)
        mla_c = jnp.transpose(mla_c[:, :, 0], (0, 2, 1, 3)).reshape(b * ctx_len, BRANCH_W)
        gq_c5 = split_heads(gq_c, ctx_len, GQA_HEADS).reshape(b, GQA_KV_HEADS, g, ctx_len, hd)
        gqa_c = flash_attention(gq_c5, gk_c4, gv_c4, None, None, 256, 256)
        gqa_c = jnp.transpose(gqa_c.reshape(b, GQA_HEADS, ctx_len, hd), (0, 2, 1, 3)).reshape(b * ctx_len, BRANCH_W)
    return mla_l, gqa_l, mla_c, gqa_c


def kernel(x, c, ctx, c_ctx, w_ada, b_ada, norm1_w, norm2_w, w_in,
           gdn_conv_w, gdn_a_log, gdn_dt_bias, gdn_norm_w,
           mla_q_norm_w, mla_kv_norm_w, mla_w_uq, mla_w_ukv,
           gqa_q_norm_w, gqa_k_norm_w,
           hy_conv_w, hy_w1, hy_b1, hy_w2, hy_b2, hy_w3, hy_sin_freq, hy_bias,
           w_branch, w_out, w_router, router_bias,
           moe_w_gate, moe_w_up, moe_w_down, final_norm_w):
    b, seq, d = x.shape
    ctx_len = ctx.shape[1]
    rows = seq // GRID_W
    rope_mla = axial_rope_tables(rows, MLA_ROPE)
    rope_gqa = axial_rope_tables(rows, GQA_HEAD_DIM)
    mod_all = ada_modulation(c, c_ctx, w_ada, b_ada)
    xl = x.reshape(b * seq, d)
    xc = ctx.reshape(b * ctx_len, d)
    f32 = lambda t: t.astype(F32)
    for layer in range(DEPTH):
        ctx_out = layer < DEPTH - 1
        mod = mod_all[layer]
        w_main, w_misc = _reorder_w_in(w_in[layer][:, :MIX_IN])
        w_gates = w_in[layer][:, MIX_IN:].astype(BF16)
        w_br = w_branch[layer].astype(BF16)
        w_o = w_out[layer].astype(BF16)
        wg, wu, wd = (t[layer].astype(BF16) for t in (moe_w_gate, moe_w_up, moe_w_down))

        hl = norm_modulate(xl, norm1_w[layer], mod[:, 0], mod[:, 1], seq)
        hc = norm_modulate(xc, norm1_w[layer], mod[:, 0], mod[:, 1], None)
        main_l, misc_l = matmul(hl, w_main, BF16, 512), matmul(hl, w_misc, F32, MISC_W)
        main_c, misc_c = matmul(hc, w_main, BF16, 512), matmul(hc, w_misc, F32, MISC_W)

        gdn_l, gdn_c = gdn_branch(main_l, misc_l, main_c, misc_c, b, seq, ctx_len, gdn_conv_w[layer],
                                  gdn_a_log[layer], gdn_dt_bias[layer], gdn_norm_w[layer], ctx_out)

        mla_w = _mla_weights(mla_q_norm_w[layer], mla_kv_norm_w[layer], mla_w_uq[layer], mla_w_ukv[layer])
        mla_l, gqa_l, mla_c, gqa_c = _attention_branches(
            main_l, misc_l, main_c, misc_c, b, seq, ctx_len, ctx_out, rope_mla, rope_gqa,
            mla_w, mla_q_norm_w[layer], mla_kv_norm_w[layer], gqa_q_norm_w[layer], gqa_k_norm_w[layer])

        hy_params = (hy_w1[layer], hy_b1[layer], hy_w2[layer], hy_b2[layer], hy_w3[layer], hy_sin_freq[layer])
        hy_l = hyena_branch(main_l, b, seq, hy_conv_w[layer], hyena_filters(seq, *hy_params), hy_bias[layer])

        branches_l = [gdn_l.reshape(b * seq, BRANCH_W).astype(BF16), mla_l, gqa_l,
                      hy_l.reshape(b * seq, BRANCH_W).astype(BF16)]
        merged_l = merge_branches_gated(hl, branches_l, w_gates, w_br)

        if ctx_out:
            hy_c = hyena_branch(main_c, b, ctx_len, hy_conv_w[layer], hyena_filters(ctx_len, *hy_params),
                                hy_bias[layer])
            branches_c = [gdn_c.reshape(b * ctx_len, BRANCH_W).astype(BF16), mla_c, gqa_c,
                          hy_c.reshape(b * ctx_len, BRANCH_W).astype(BF16)]
            merged_c = merge_branches_gated(hc, branches_c, w_gates, w_br)
            xc = matmul_gated_residual(merged_c, w_o, xc, mod[:, 2], None)
            xc = moe_layer(xc, norm2_w[layer], mod, None, w_router, router_bias, wg, wu, wd)

        xl = matmul_gated_residual(merged_l, w_o, xl, mod[:, 2], seq)
        xl = moe_layer(xl, norm2_w[layer], mod, seq, w_router, router_bias, wg, wu, wd)
    return final_rms_norm(xl, final_norm_w).reshape(b, seq, d)
```

```python
import math, functools
import jax, jax.numpy as jnp
from jax import lax
import numpy as np
from jax.experimental import pallas as pl
from jax.experimental.pallas import tpu as pltpu

D_MODEL = 2048
BATCH = 4
SEQ = 4096
DEPTH = 2

GRID_W = 64
CTX_LEN = 256
N_BRANCH = 4
BRANCH_W = 512
NORM_EPS = 1e-6
Q_BLOCK = 128
ROPE_THETA = 10000.0
SHORT_CONV = 3

GDN_HEADS = 4
GDN_HEAD_DIM = 128
GDN_CHUNK = 64

MLA_HEADS = 4
MLA_Q_LORA = 512
MLA_KV_LORA = 256
MLA_NOPE = 128
MLA_ROPE = 64
MLA_V = 128

GQA_HEADS = 8
GQA_KV_HEADS = 2
GQA_HEAD_DIM = 64

HY_WIDTH = 512
HY_ORDER = 2
HY_EMB = 33
HY_HIDDEN = 64
HY_DECAY_TARGET = 1e-2
HY_FAST_DECAY = 0.3
HY_SLOW_DECAY = 1.5

N_EXPERTS = 16
N_GROUPS = 4
EXPERTS_PER_GROUP = N_EXPERTS // N_GROUPS
TOP_K = 2
D_EXPERT = 512

GDN_W = GDN_HEADS * GDN_HEAD_DIM
IN_WIDTHS = (GDN_W, GDN_W, GDN_W, GDN_W, 2 * GDN_HEADS, 2 * GDN_HEADS,
             MLA_Q_LORA, MLA_KV_LORA, MLA_ROPE,
             GQA_HEADS * GQA_HEAD_DIM, GQA_KV_HEADS * GQA_HEAD_DIM, GQA_KV_HEADS * GQA_HEAD_DIM,
             (HY_ORDER + 1) * HY_WIDTH)
MIX_IN = sum(IN_WIDTHS)
IN_DIM = MIX_IN + N_BRANCH * D_MODEL

F32 = jnp.float32
BF16 = jnp.bfloat16
LANES = 128
MOD_ROWS = 8
CTX_MOD_ROW = BATCH
MOE_TILE = 512
VMEM_LIMIT = 56 << 20

MAIN_W = 5120
OFF_GDN, OFF_CQ, OFF_GQ, OFF_HY, OFF_CKV, OFF_GK, OFF_GV = 0, 2048, 2560, 3072, 4608, 4864, 4992
MISC_W = LANES


def _cp(sem):
    return pltpu.CompilerParams(dimension_semantics=sem, vmem_limit_bytes=VMEM_LIMIT)


def _sigmoid(v):
    return 0.5 * jnp.tanh(0.5 * v) + 0.5


def rms_norm(x, w):
    xf = x.astype(jnp.float32)
    y = xf * lax.rsqrt(jnp.mean(xf * xf, axis=-1, keepdims=True) + NORM_EPS)
    return (y * w.astype(jnp.float32)).astype(x.dtype)


def l2_normalize(x):
    xf = x.astype(jnp.float32)
    return xf * lax.rsqrt(jnp.sum(xf * xf, axis=-1, keepdims=True) + NORM_EPS)


def depthwise_conv_centred(u, w):
    k = w.shape[0]
    return lax.conv_general_dilated(u, w[:, None, :].astype(u.dtype), window_strides=(1,),
                                    padding=[(k // 2, k // 2)],
                                    dimension_numbers=('NWC', 'WIO', 'NWC'),
                                    feature_group_count=u.shape[-1])


def axial_rope_tables(rows, rot_dim):
    n_freq = rot_dim // 4
    freqs = ROPE_THETA ** (-jnp.arange(n_freq, dtype=jnp.float32) / n_freq)
    row = jnp.repeat(jnp.arange(rows, dtype=jnp.float32), GRID_W)
    col = jnp.tile(jnp.arange(GRID_W, dtype=jnp.float32), rows)
    ang = jnp.concatenate([row[:, None] * freqs, col[:, None] * freqs], axis=-1)
    return jnp.cos(ang), jnp.sin(ang)


def gdn_prep(q, k, v, a, bt, conv_w, a_log, dt_bias):
    b, l = q.shape[:2]
    qkv = jax.nn.silu(depthwise_conv_centred(jnp.concatenate([q, k, v], axis=-1), conv_w)).astype(jnp.float32)
    q, k, v = jnp.split(qkv, 3, axis=-1)
    hd = (b, l, GDN_HEADS, GDN_HEAD_DIM)
    q = l2_normalize(q.reshape(hd)) * GDN_HEAD_DIM ** -0.5
    k = l2_normalize(k.reshape(hd))
    v = v.reshape(hd)
    a = a.astype(jnp.float32).reshape(b, l, 2, GDN_HEADS)
    g = -jnp.exp(a_log.astype(jnp.float32)) * jax.nn.softplus(a + dt_bias.astype(jnp.float32))
    beta = jax.nn.sigmoid(bt.astype(jnp.float32).reshape(b, l, 2, GDN_HEADS))
    return q, k, v, g, beta


def gated_delta_rule(q, k, v, g, beta, state, with_out):
    b, l, h, _ = q.shape
    dv = v.shape[-1]
    c = GDN_CHUNK
    n = l // c

    def to_chunks(t):
        t = t.reshape(b, n, c, h, *t.shape[3:])
        return jnp.moveaxis(t, (1, 3), (0, 2))

    qc, kc, vc, bc = to_chunks(q), to_chunks(k), to_chunks(v), to_chunks(beta)
    gc = jnp.cumsum(to_chunks(g), axis=-1)
    idx = jnp.arange(c)
    lower = idx[:, None] >= idx[None, :]
    strict = idx[:, None] > idx[None, :]
    diff = gc[..., :, None] - gc[..., None, :]
    decay = jnp.where(lower, jnp.exp(jnp.where(lower, diff, 0.0)), 0.0)
    kb = kc * bc[..., None]
    a = jnp.where(strict, jnp.einsum('nbhid,nbhjd->nbhij', kb, kc) * decay, 0.0)
    solve = functools.partial(lax.linalg.triangular_solve, left_side=True, lower=True, unit_diagonal=True)
    u = solve(a, vc * bc[..., None])
    w = solve(a, kb * jnp.exp(gc)[..., None])
    g_last = gc[..., -1]
    k_dec = kc * jnp.exp(g_last[..., None] - gc)[..., None]
    xs = (u, w, k_dec, g_last)
    if with_out:
        qk = jnp.where(lower, jnp.einsum('nbhid,nbhjd->nbhij', qc, kc) * decay, 0.0)
        xs = xs + (qc * jnp.exp(gc)[..., None], qk)

    def step(s, inp):
        u_i, w_i, kd_i, gl_i = inp[:4]
        v_new = u_i - jnp.einsum('bhck,bhkv->bhcv', w_i, s)
        s_new = s * jnp.exp(gl_i)[..., None, None] + jnp.einsum('bhck,bhcv->bhkv', kd_i, v_new)
        if not with_out:
            return s_new, None
        qd_i, qk_i = inp[4:]
        o = jnp.einsum('bhck,bhkv->bhcv', qd_i, s) + jnp.einsum('bhij,bhjv->bhiv', qk_i, v_new)
        return s_new, o

    state, o = lax.scan(step, state, xs)
    if not with_out:
        return None, state
    o = jnp.moveaxis(o, (0, 2), (1, 3)).reshape(b, l, h, dv)
    return o, state


def gdn_output(o, z, norm_w):
    b, l = z.shape[:2]
    zh = z.reshape(b, l, GDN_HEADS, GDN_HEAD_DIM).astype(jnp.float32)
    y = rms_norm(o, norm_w) * jax.nn.silu(zh)
    return y.reshape(b, l, GDN_W).astype(z.dtype)


def gdn_mixer(p_lat, p_ctx, conv_w, a_log, dt_bias, norm_w, ctx_out):
    lat = gdn_prep(p_lat[0], p_lat[1], p_lat[2], p_lat[4], p_lat[5], conv_w, a_log, dt_bias)
    ctx = gdn_prep(p_ctx[0], p_ctx[1], p_ctx[2], p_ctx[4], p_ctx[5], conv_w, a_log, dt_bias)
    b = p_lat[0].shape[0]
    s0 = jnp.zeros((b, GDN_HEADS, GDN_HEAD_DIM, GDN_HEAD_DIM), jnp.float32)
    o_lat, o_ctx = 0.0, 0.0
    for direction in range(2):
        flip = (lambda t: t[:, ::-1]) if direction else (lambda t: t)

        def seq_args(s):
            q, k, v, g, beta = s
            return flip(q), flip(k), flip(v), flip(g[:, :, direction]), flip(beta[:, :, direction])

        oc, s_ctx = gated_delta_rule(*seq_args(ctx), s0, ctx_out)
        ol, _ = gated_delta_rule(*seq_args(lat), s_ctx, True)
        o_lat = o_lat + flip(ol)
        if ctx_out:
            o_ctx = o_ctx + flip(oc)
    out_lat = gdn_output(o_lat, p_lat[3], norm_w)
    out_ctx = gdn_output(o_ctx, p_ctx[3], norm_w) if ctx_out else None
    return out_lat, out_ctx


def hyena_filters(length, w1, b1, w2, b2, w3, sin_freq):
    t = jnp.arange(length, dtype=jnp.float32)
    bands = (HY_EMB - 1) // 2
    f = jnp.linspace(1e-4, bands - 1, bands, dtype=jnp.float32)
    phase = (2.0 * math.pi / length) * t[:, None] * f[None, :]
    feats = jnp.concatenate([t[:, None] / (length - 1), jnp.cos(phase), -jnp.sin(phase)], axis=-1)
    hid = jnp.sin(sin_freq[0] * (feats @ w1 + b1))
    hid = jnp.sin(sin_freq[1] * (hid @ w2 + b2))
    filt = (hid @ w3).astype(jnp.float32)
    centre = length // 2
    dist = jnp.abs(t - centre) / centre
    deltas = jnp.abs(jnp.linspace(math.log(HY_DECAY_TARGET) / HY_SLOW_DECAY,
                                  math.log(HY_DECAY_TARGET) / HY_FAST_DECAY,
                                  HY_ORDER * HY_WIDTH, dtype=jnp.float32))
    filt = filt * jnp.exp(-dist[:, None] * deltas[None, :])
    filt = filt / jnp.sum(jnp.abs(filt), axis=0, keepdims=True)
    return filt.reshape(length, HY_ORDER, HY_WIDTH)


def fft_conv_centred(u, h):
    l = u.shape[1]
    n = 2 * l
    uf = jnp.fft.rfft(u.astype(jnp.float32), n=n, axis=1)
    hf = jnp.fft.rfft(h.astype(jnp.float32), n=n, axis=0)
    y = jnp.fft.irfft(uf * hf[None], n=n, axis=1)
    return y[:, l // 2: l // 2 + l]


def hyena_mixer(u, conv_w, filt, bias):
    parts = jnp.split(depthwise_conv_centred(u, conv_w).astype(jnp.float32), HY_ORDER + 1, axis=-1)
    z = parts[0]
    for o in range(HY_ORDER):
        z = parts[o + 1] * (fft_conv_centred(z, filt[:, o]) + bias[o] * z)
    return z


HALO_ROWS = 16
GDN_BATCHES_PER_STEP = 4


def _conv3(x, prev_row, next_row, w_ref):
    tm = x.shape[0]
    rows = lax.broadcasted_iota(jnp.int32, x.shape, 0)
    up = jnp.where(rows == 0, prev_row, pltpu.roll(x, 1, 0))
    dn = jnp.where(rows == tm - 1, next_row, pltpu.roll(x, tm - 1, 0))
    return w_ref[0:1, :] * up + w_ref[1:2, :] * x + w_ref[2:3, :] * dn


def _halo_rows(xp_ref, xn_ref):
    i = pl.program_id(1)
    prev = jnp.where(i == 0, 0.0, xp_ref[HALO_ROWS - 1:HALO_ROWS, :].astype(F32))
    nxt = jnp.where(i == pl.num_programs(1) - 1, 0.0, xn_ref[0:1, :].astype(F32))
    return prev, nxt


def _halo_specs(tm, width, col_block, nt, n_rows):
    per = tm // HALO_ROWS
    last = n_rows // HALO_ROWS - 1
    return [pl.BlockSpec((tm, width), lambda bi, i: (bi * nt + i, col_block)),
            pl.BlockSpec((HALO_ROWS, width), lambda bi, i: (jnp.maximum((bi * nt + i) * per - 1, 0), col_block)),
            pl.BlockSpec((HALO_ROWS, width), lambda bi, i: (jnp.minimum((bi * nt + i + 1) * per, last), col_block))]


def _split3(v):
    hi = v.astype(BF16)
    r1 = v - hi.astype(F32)
    mid = r1.astype(BF16)
    lo = (r1 - mid.astype(F32)).astype(BF16)
    return hi, mid, lo


def _gdn_prep_body(x_ref, xp_ref, xn_ref, misc_ref, cw_ref, alog_ref, dt_ref, gmask_ref, tp_ref, ts_ref,
                   q_out, k_out, v_out, gcf_out, gcb_out, beta_out):
    prev, nxt = _halo_rows(xp_ref, xn_ref)
    y = _conv3(x_ref[...].astype(F32), prev, nxt, cw_ref)
    y = y * _sigmoid(y)
    hd = GDN_HEAD_DIM
    for h in range(GDN_HEADS):
        qh = y[:, h * hd:(h + 1) * hd]
        kh = y[:, GDN_W + h * hd:GDN_W + (h + 1) * hd]
        qn = qh * lax.rsqrt(jnp.sum(qh * qh, axis=-1, keepdims=True) + NORM_EPS) * hd ** -0.5
        kn = kh * lax.rsqrt(jnp.sum(kh * kh, axis=-1, keepdims=True) + NORM_EPS)
        q_out[:, h * hd:(h + 1) * hd] = qn.astype(BF16)
        k_out[:, h * hd:(h + 1) * hd] = kn.astype(BF16)
    v_out[...] = y[:, 2 * GDN_W:3 * GDN_W].astype(BF16)
    m = misc_ref[...]
    a = m + dt_ref[...]
    softplus = jnp.maximum(a, 0.0) + jnp.log(1.0 + jnp.exp(-jnp.abs(a)))
    g = -(jnp.exp(alog_ref[...]) * gmask_ref[...]) * softplus
    beta_out[...] = _sigmoid(m)
    parts = _split3(g)
    gcf_out[...] = sum(jnp.dot(tp_ref[...], p, preferred_element_type=F32) for p in parts)
    gcb_out[...] = sum(jnp.dot(ts_ref[...], p, preferred_element_type=F32) for p in parts)


def gdn_prepare(main, misc, b, length, conv_w, a_log, dt_bias):
    n = b * length
    w3 = 3 * GDN_W
    tm = min(256, length)
    nt = length // tm
    lane0 = MLA_ROPE
    vec = lambda v: jnp.zeros((1, MISC_W), F32).at[0, lane0:lane0 + 2 * GDN_HEADS].set(v.reshape(-1))
    alog, dtb = vec(a_log), vec(dt_bias)
    gmask = vec(jnp.ones((2 * GDN_HEADS,), F32))
    r = np.arange(tm)
    same = (r[:, None] // GDN_CHUNK) == (r[None, :] // GDN_CHUNK)
    tpre = jnp.asarray(same & (r[None, :] <= r[:, None]), BF16)
    tsuf = jnp.asarray(same & (r[None, :] >= r[:, None]), BF16)
    const = lambda a: pl.BlockSpec(a.shape, lambda bi, i: (0,) * a.ndim)
    row = lambda width: pl.BlockSpec((tm, width), lambda bi, i: (bi * nt + i, 0))
    cw = conv_w.astype(F32)
    return pl.pallas_call(
        _gdn_prep_body,
        grid=(b, nt),
        in_specs=_halo_specs(tm, w3, OFF_GDN // w3, nt, n)
                 + [row(MISC_W), const(cw), const(alog), const(dtb), const(gmask), const(tpre), const(tsuf)],
        out_specs=[row(GDN_W), row(GDN_W), row(GDN_W), row(MISC_W), row(MISC_W), row(MISC_W)],
        out_shape=[jax.ShapeDtypeStruct((n, GDN_W), BF16)] * 3 + [jax.ShapeDtypeStruct((n, MISC_W), F32)] * 3,
        compiler_params=_cp(("parallel", "parallel")),
        name="gdn_prep",
    )(main, main, main, misc, cw, alog, dtb, gmask, tpre, tsuf)


def _gdn_chunk_body(qf_ref, kf_ref, vf_ref, qb_ref, kb_ref, vb_ref, gcf_ref, gcb_ref, bcf_ref, bcb_ref,
                    grf_ref, grb_ref, s0_ref, *rest, nc, with_out, bpb):
    if with_out:
        of_ref, ob_ref, sfin_ref, s_ref = rest
    else:
        sfin_ref, s_ref = rest
        of_ref = ob_ref = None
    c = pl.program_id(1)
    nst = 2 * GDN_HEADS

    @pl.when(c == 0)
    def _():
        s_ref[...] = s0_ref[...].reshape(s_ref.shape)

    ch = GDN_CHUNK
    hd = GDN_HEAD_DIM
    ii = lax.broadcasted_iota(jnp.int32, (ch, ch), 0)
    jj = lax.broadcasted_iota(jnp.int32, (ch, ch), 1)
    nt_dims = (((1,), (1,)), ((), ()))
    tn_dims = (((0,), (0,)), ((), ()))
    bdot = lambda a, b_: jnp.dot(a.astype(BF16), b_.astype(BF16), preferred_element_type=F32)
    eye = jnp.where(ii == jj, 1.0, 0.0)
    pair_masks = [((ii >> (l + 1)) == (jj >> (l + 1))) & ((ii >> l) != (jj >> l))
                  for l in range(int(math.log2(ch)))]
    dirs = ((qf_ref, kf_ref, vf_ref, gcf_ref, bcf_ref, grf_ref, of_ref, ii >= jj, ii > jj, ch - 1),
            (qb_ref, kb_ref, vb_ref, gcb_ref, bcb_ref, grb_ref, ob_ref, ii <= jj, ii < jj, 0))
    chains = []
    for bb in range(bpb):
        for d, (q_ref, k_ref, v_ref, gc_ref, bc_ref, gr_ref, o_ref, incl, strict, last_row) in enumerate(dirs):
            for h in range(GDN_HEADS):
                j = d * GDN_HEADS + h
                cols = slice(h * hd, (h + 1) * hd)
                cn = dict(bb=bb, j=j, cols=cols, o_ref=o_ref, incl=incl, strict=strict)
                cn['q'], cn['k'], cn['v'] = q_ref[bb, :, cols], k_ref[bb, :, cols], v_ref[bb, :, cols]
                cn['gc'] = gc_ref[bb, :, j:j + 1]
                cn['gr'] = gr_ref[bb, 0, j:j + 1, :]
                cn['beta'] = bc_ref[bb, :, j:j + 1]
                cn['g_last'] = gc_ref[bb, last_row:last_row + 1, j:j + 1]
                chains.append(cn)
    for cn in chains:
        incl = cn['incl']
        cn['decay'] = jnp.where(incl, jnp.exp(jnp.where(incl, cn['gc'] - cn['gr'], 0.0)), 0.0)
        cn['kf'] = cn['k'].astype(F32)
        cn['kbeta'] = cn['kf'] * cn['beta']
    for cn in chains:
        kk = lax.dot_general(cn['kbeta'].astype(BF16), cn['k'], nt_dims, preferred_element_type=F32)
        cn['a'] = jnp.where(cn['strict'], kk * cn['decay'], 0.0)
    for cn in chains:
        cn['t'] = eye - jnp.where(pair_masks[0], cn['a'], 0.0)
    for pm in pair_masks[1:]:
        for cn in chains:
            cn['tmp'] = bdot(cn['t'], jnp.where(pm, cn['a'], 0.0))
        for cn in chains:
            cn['t'] = cn['t'] - bdot(cn['tmp'], cn['t'])
    for cn in chains:
        rhs = jnp.concatenate([cn['v'].astype(F32) * cn['beta'], cn['kbeta'] * jnp.exp(cn['gc'])], axis=1)
        cn['x'] = bdot(cn['t'], rhs)
    for cn in chains:
        cn['s'] = s_ref[cn['bb'] * nst + cn['j']]
        cn['v_new'] = cn['x'][:, :hd] - bdot(cn['x'][:, hd:], cn['s'])
    if with_out:
        for cn in chains:
            qk = lax.dot_general(cn['q'], cn['k'], nt_dims, preferred_element_type=F32)
            qk = jnp.where(cn['incl'], qk * cn['decay'], 0.0)
            o = bdot(cn['q'].astype(F32) * jnp.exp(cn['gc']), cn['s']) + bdot(qk, cn['v_new'])
            cn['o_ref'][cn['bb'], :, cn['cols']] = o
    for cn in chains:
        kdec = cn['kf'] * jnp.exp(cn['g_last'] - cn['gc'])
        s_ref[cn['bb'] * nst + cn['j']] = cn['s'] * jnp.exp(cn['g_last']) + lax.dot_general(
            kdec.astype(BF16), cn['v_new'].astype(BF16), tn_dims, preferred_element_type=F32)

    @pl.when(c == nc - 1)
    def _():
        sfin_ref[...] = s_ref[...].reshape(sfin_ref.shape)


def gdn_scan(q, k, v, gcol, bcol, grow, s0, b, length, with_out):
    n = b * length
    ch = GDN_CHUNK
    nc = length // ch
    nst = 2 * GDN_HEADS
    bpb = min(GDN_BATCHES_PER_STEP, b)
    fwd = lambda bg, c: (bg, c, 0)
    bwd = lambda bg, c: (bg, nc - 1 - c, 0)
    fwd4 = lambda bg, c: (bg, c, 0, 0)
    bwd4 = lambda bg, c: (bg, nc - 1 - c, 0, 0)
    wide = lambda f: pl.BlockSpec((bpb, ch, GDN_W), f)
    narrow = lambda f: pl.BlockSpec((bpb, ch, nst), f)
    rows = lambda f: pl.BlockSpec((bpb, 1, nst, ch), f)
    state = pl.BlockSpec((bpb, nst, GDN_HEAD_DIM, GDN_HEAD_DIM), lambda bg, c: (bg, 0, 0, 0))
    out_specs = [state]
    out_shape = [jax.ShapeDtypeStruct((b, nst, GDN_HEAD_DIM, GDN_HEAD_DIM), F32)]
    if with_out:
        out_specs = [wide(fwd), wide(bwd)] + out_specs
        out_shape = [jax.ShapeDtypeStruct((b, length, GDN_W), F32)] * 2 + out_shape
    body = functools.partial(_gdn_chunk_body, nc=nc, with_out=with_out, bpb=bpb)
    q3, k3, v3 = (t.reshape(b, length, GDN_W) for t in (q, k, v))
    gcol3, bcol3 = gcol.reshape(b, length, nst), bcol.reshape(b, length, nst)
    grow4 = grow.reshape(b, nc, nst, ch)
    outs = pl.pallas_call(
        body,
        grid=(b // bpb, nc),
        in_specs=[wide(fwd), wide(fwd), wide(fwd), wide(bwd), wide(bwd), wide(bwd),
                  narrow(fwd), narrow(bwd), narrow(fwd), narrow(bwd), rows(fwd4), rows(bwd4), state],
        out_specs=out_specs,
        out_shape=out_shape,
        scratch_shapes=[pltpu.VMEM((bpb * nst, GDN_HEAD_DIM, GDN_HEAD_DIM), F32)],
        compiler_params=_cp(("parallel", "arbitrary")),
        name="gdn_scan",
    )(q3, k3, v3, q3, k3, v3, gcol3, gcol3, bcol3, bcol3, grow4, grow4, s0)
    if with_out:
        return outs[0].reshape(n, GDN_W), outs[1].reshape(n, GDN_W), outs[2]
    return outs


def _gdn_out_body(of_ref, ob_ref, z_ref, nw_ref, y_ref):
    o = of_ref[...] + ob_ref[...]
    z = z_ref[...].astype(F32)
    hd = GDN_HEAD_DIM
    for h in range(GDN_HEADS):
        cols = slice(h * hd, (h + 1) * hd)
        oh = o[:, cols]
        yh = oh * lax.rsqrt(jnp.mean(oh * oh, axis=-1, keepdims=True) + NORM_EPS) * nw_ref[...]
        zh = z[:, cols]
        y_ref[:, cols] = (yh * (zh * _sigmoid(zh))).astype(BF16)


def gdn_output_gate(o_f, o_b, main, norm_w):
    n = o_f.shape[0]
    tm = min(512, n)
    nw = norm_w.reshape(1, GDN_HEAD_DIM).astype(F32)
    return pl.pallas_call(
        _gdn_out_body,
        grid=(n // tm,),
        in_specs=[pl.BlockSpec((tm, GDN_W), lambda i: (i, 0)),
                  pl.BlockSpec((tm, GDN_W), lambda i: (i, 0)),
                  pl.BlockSpec((tm, GDN_W), lambda i: (i, (OFF_GDN + 3 * GDN_W) // GDN_W)),
                  pl.BlockSpec((1, GDN_HEAD_DIM), lambda i: (0, 0))],
        out_specs=pl.BlockSpec((tm, GDN_W), lambda i: (i, 0)),
        out_shape=jax.ShapeDtypeStruct((n, GDN_W), BF16),
        compiler_params=_cp(("parallel",)),
        name="gdn_out",
    )(o_f, o_b, main, nw)


def gdn_branch(main_l, misc_l, main_c, misc_c, b, seq, ctx_len, conv_w, a_log, dt_bias, norm_w, ctx_out):
    nst = 2 * GDN_HEADS
    lane0 = MLA_ROPE

    def gates(gcf, gcb, beta, length):
        gcol = jnp.concatenate([gcf[:, lane0:lane0 + GDN_HEADS], gcb[:, lane0 + GDN_HEADS:lane0 + nst]], axis=1)
        bcol = beta[:, lane0 + nst:lane0 + 2 * nst]
        grow = jnp.transpose(gcol.reshape(-1, GDN_CHUNK, nst), (0, 2, 1))
        return gcol, bcol, grow

    qc, kc, vc, gcf, gcb, beta = gdn_prepare(main_c, misc_c, b, ctx_len, conv_w, a_log, dt_bias)
    gcol_c, bcol_c, grow_c = gates(gcf, gcb, beta, ctx_len)
    ql, kl, vl, gcf, gcb, beta = gdn_prepare(main_l, misc_l, b, seq, conv_w, a_log, dt_bias)
    gcol_l, bcol_l, grow_l = gates(gcf, gcb, beta, seq)
    s0 = jnp.zeros((b, nst, GDN_HEAD_DIM, GDN_HEAD_DIM), F32)
    outs_c = gdn_scan(qc, kc, vc, gcol_c, bcol_c, grow_c, s0, b, ctx_len, ctx_out)
    s_ctx = outs_c[-1]
    of_l, ob_l, _ = gdn_scan(ql, kl, vl, gcol_l, bcol_l, grow_l, s_ctx, b, seq, True)
    out_l = gdn_output_gate(of_l, ob_l, main_l, norm_w)
    out_c = gdn_output_gate(outs_c[0], outs_c[1], main_c, norm_w) if ctx_out else None
    return out_l, out_c


def _hy_conv_body(x_ref, xp_ref, xn_ref, cw_ref, v_out, x1_out, x2_out):
    prev, nxt = _halo_rows(xp_ref, xn_ref)
    y = _conv3(x_ref[...].astype(F32), prev, nxt, cw_ref)
    w = HY_WIDTH
    v_out[...] = y[:, :w].astype(BF16)
    x1_out[...] = y[:, w:2 * w].astype(BF16)
    x2_out[...] = y[:, 2 * w:3 * w].astype(BF16)


def hyena_short_conv(main, b, length, conv_w):
    n = b * length
    w3 = (HY_ORDER + 1) * HY_WIDTH
    tm = min(256, length)
    nt = length // tm
    cw = conv_w.astype(F32)
    row = pl.BlockSpec((tm, HY_WIDTH), lambda bi, i: (bi * nt + i, 0))
    return pl.pallas_call(
        _hy_conv_body,
        grid=(b, nt),
        in_specs=_halo_specs(tm, w3, OFF_HY // w3, nt, n) + [pl.BlockSpec(cw.shape, lambda bi, i: (0, 0))],
        out_specs=[row, row, row],
        out_shape=[jax.ShapeDtypeStruct((n, HY_WIDTH), BF16)] * 3,
        compiler_params=_cp(("parallel", "parallel")),
        name="hyena_conv",
    )(main, main, main, cw)


def _dft_consts(length):
    n = 2 * length
    n2 = 64 if length >= 2048 else 16
    n1 = n // n2
    nk1 = n1 // 2 + 8
    k1 = np.arange(nk1)
    t1 = np.arange(n1 // 2)
    ang1 = 2.0 * np.pi * np.outer(k1, t1) / n1
    f_first = np.concatenate([np.cos(ang1), -np.sin(ang1)], axis=0)
    t2 = np.arange(n2)
    ang2 = 2.0 * np.pi * np.outer(t2, t2) / n2
    c2, s2 = np.cos(ang2), np.sin(ang2)
    g_fwd = np.block([[c2, s2], [-s2, c2]])
    g_inv = g_fwd.T
    angt = 2.0 * np.pi * np.outer(k1, t2) / n
    tw_r, tw_i = np.cos(angt)[:, :, None], -np.sin(angt)[:, :, None]
    tt = np.arange(n1 // 4, 3 * n1 // 4)
    ang3 = 2.0 * np.pi * np.outer(tt, k1) / n1
    fold = np.where((k1 == 0) | (k1 == n1 // 2), 1.0, np.where(k1 < n1 // 2, 2.0, 0.0))[None, :]
    f_last = np.concatenate([np.cos(ang3) * fold, -np.sin(ang3) * fold], axis=1) / n
    bf = lambda a: jnp.asarray(a, BF16)
    return dict(n1=n1, nk1=nk1, n2=n2, f_first=bf(f_first), g_fwd=bf(g_fwd), g_inv=bf(g_inv),
                tw_r=jnp.asarray(tw_r, F32), tw_i=jnp.asarray(tw_i, F32), f_last=bf(f_last))


def _hy_first_body(f_ref, z_ref, a_ref):
    a_ref[0] = jnp.dot(f_ref[...], z_ref[0], preferred_element_type=F32).astype(BF16)


def hyena_dft_first(zv, consts):
    b, half, cols = zv.shape
    n1 = consts['nk1']
    tn = min(4096, cols)
    f = consts['f_first']
    return pl.pallas_call(
        _hy_first_body,
        grid=(b, cols // tn),
        in_specs=[pl.BlockSpec(f.shape, lambda bi, j: (0, 0)),
                  pl.BlockSpec((1, half, tn), lambda bi, j: (bi, 0, j))],
        out_specs=pl.BlockSpec((1, 2 * n1, tn), lambda bi, j: (bi, 0, j)),
        out_shape=jax.ShapeDtypeStruct((b, 2 * n1, cols), BF16),
        compiler_params=_cp(("parallel", "parallel")),
        name="hyena_dft_first",
    )(f, zv)


def _hy_mid_body(a_ref, twr_ref, twi_ref, gf_ref, *rest, kt, spectrum_only):
    if spectrum_only:
        (o_ref,) = rest
    else:
        gi_ref, h_ref, o_ref = rest
    n2 = gf_ref.shape[0] // 2

    def one(i, carry):
        ar = a_ref[0, 0, i].astype(F32)
        ai = a_ref[0, 1, i].astype(F32)
        twr, twi = twr_ref[i], twi_ref[i]
        br = ar * twr - ai * twi
        bi = ar * twi + ai * twr
        z = jnp.dot(gf_ref[...], jnp.concatenate([br, bi], axis=0).astype(BF16), preferred_element_type=F32)
        zr, zi = z[:n2], z[n2:]
        if spectrum_only:
            o_ref[0, 0, i] = zr
            o_ref[0, 1, i] = zi
            return carry
        hr, hi = h_ref[0, i], h_ref[1, i]
        yr = zr * hr - zi * hi
        yi = zr * hi + zi * hr
        w = jnp.dot(gi_ref[...], jnp.concatenate([yr, yi], axis=0).astype(BF16), preferred_element_type=F32)
        wr, wi = w[:n2], w[n2:]
        o_ref[0, 0, i] = (wr * twr + wi * twi).astype(BF16)
        o_ref[0, 1, i] = (wi * twr - wr * twi).astype(BF16)
        return carry

    lax.fori_loop(0, kt, one, 0)


def hyena_dft_mid(a5, consts, spectrum=None):
    b, _, n1, n2, c = a5.shape
    kt = 8
    only = spectrum is None
    blk = pl.BlockSpec((1, 2, kt, n2, c), lambda bi, j: (bi, 0, j, 0, 0))
    tw = pl.BlockSpec((kt, n2, 1), lambda bi, j: (j, 0, 0))
    g = pl.BlockSpec((2 * n2, 2 * n2), lambda bi, j: (0, 0))
    in_specs = [blk, tw, tw, g]
    args = [a5, consts['tw_r'], consts['tw_i'], consts['g_fwd']]
    if not only:
        in_specs += [g, pl.BlockSpec((2, kt, n2, c), lambda bi, j: (0, j, 0, 0))]
        args += [consts['g_inv'], spectrum]
    body = functools.partial(_hy_mid_body, kt=kt, spectrum_only=only)
    return pl.pallas_call(
        body,
        grid=(b, n1 // kt),
        in_specs=in_specs,
        out_specs=blk,
        out_shape=jax.ShapeDtypeStruct(a5.shape, F32 if only else BF16),
        compiler_params=_cp(("parallel", "parallel")),
        name="hyena_dft_mid",
    )(*args)


def _hy_last_body(f_ref, b_ref, z_ref, x_ref, bias_ref, o_ref):
    y = jnp.dot(f_ref[...], b_ref[0], preferred_element_type=F32)
    z = z_ref[0].astype(F32)
    o_ref[0] = (x_ref[0].astype(F32) * (y + bias_ref[...] * z)).astype(BF16)


def hyena_dft_last(bv, zv, xv, bias_row, consts):
    b, rows2, cols = bv.shape
    half = consts['n1'] // 2
    tn = min(4096, cols)
    f = consts['f_last']
    sig = pl.BlockSpec((1, half, tn), lambda bi, j: (bi, 0, j))
    return pl.pallas_call(
        _hy_last_body,
        grid=(b, cols // tn),
        in_specs=[pl.BlockSpec(f.shape, lambda bi, j: (0, 0)),
                  pl.BlockSpec((1, rows2, tn), lambda bi, j: (bi, 0, j)),
                  sig, sig,
                  pl.BlockSpec((1, tn), lambda bi, j: (0, j))],
        out_specs=sig,
        out_shape=jax.ShapeDtypeStruct((b, half, cols), BF16),
        compiler_params=_cp(("parallel", "parallel")),
        name="hyena_dft_last",
    )(f, bv, zv, xv, bias_row)


def hyena_branch(main, b, length, conv_w, filt, bias):
    consts = _dft_consts(length)
    n1, nk1, n2 = consts['n1'], consts['nk1'], consts['n2']
    c = HY_WIDTH
    cols = n2 * c
    view = lambda t: t.reshape(b, n1 // 2, cols)
    v, x1, x2 = (view(t) for t in hyena_short_conv(main, b, length, conv_w))
    hv = jnp.transpose(filt, (1, 0, 2)).astype(BF16).reshape(HY_ORDER, n1 // 2, cols)
    h_first = hyena_dft_first(hv, consts).reshape(HY_ORDER, 2, nk1, n2, c)
    spectra = hyena_dft_mid(h_first, consts)
    z = v
    for o, gate in enumerate((x1, x2)):
        a5 = hyena_dft_first(z, consts).reshape(b, 2, nk1, n2, c)
        bm = hyena_dft_mid(a5, consts, spectra[o]).reshape(b, 2 * nk1, cols)
        bias_row = jnp.tile(bias[o].astype(F32), n2).reshape(1, cols)
        z = hyena_dft_last(bm, z, gate, bias_row, consts)
    return z.reshape(b * length, c)


def _ada_body(c_ref, w_ref, b_ref, o_ref):
    cv = c_ref[...]
    s = cv * _sigmoid(cv)
    o_ref[0] = jnp.dot(s, w_ref[0], precision=lax.Precision.HIGHEST, preferred_element_type=F32) + b_ref[0]


def ada_modulation(c, c_ctx, w_ada, b_ada):
    depth, d, d6 = w_ada.shape
    c8 = jnp.zeros((MOD_ROWS, d), F32).at[:c.shape[0]].set(c).at[CTX_MOD_ROW].set(c_ctx)
    tn = 512
    out = pl.pallas_call(
        _ada_body,
        grid=(depth, d6 // tn),
        in_specs=[pl.BlockSpec((MOD_ROWS, d), lambda l, j: (0, 0)),
                  pl.BlockSpec((1, d, tn), lambda l, j: (l, 0, j)),
                  pl.BlockSpec((1, 1, tn), lambda l, j: (l, 0, j))],
        out_specs=pl.BlockSpec((1, MOD_ROWS, tn), lambda l, j: (l, 0, j)),
        out_shape=jax.ShapeDtypeStruct((depth, MOD_ROWS, d6), F32),
        compiler_params=_cp(("parallel", "parallel")),
        name="ada_mod",
    )(c8, w_ada, b_ada.reshape(depth, 1, d6))
    return out.reshape(depth, MOD_ROWS, 6, d)


def _row_tile(cap, n_rows, per_batch):
    return min(cap, n_rows if per_batch is None else per_batch)


def _mod_row_fn(n_rows, tm, per_batch):
    if per_batch is None:
        return lambda i: CTX_MOD_ROW
    tiles = per_batch // tm
    return lambda i: i // tiles


def _normmod_body(x_ref, nw_ref, sh_ref, sc_ref, o_ref):
    xf = x_ref[...]
    y = xf * lax.rsqrt(jnp.mean(xf * xf, axis=-1, keepdims=True) + NORM_EPS) * nw_ref[...]
    o_ref[...] = (y * (1.0 + sc_ref[0]) + sh_ref[0]).astype(o_ref.dtype)


def norm_modulate(x, nw, shift, scale, per_batch, out_dtype=BF16):
    n, d = x.shape
    tm = _row_tile(512, n, per_batch)
    rf = _mod_row_fn(n, tm, per_batch)
    return pl.pallas_call(
        _normmod_body,
        grid=(n // tm,),
        in_specs=[pl.BlockSpec((tm, d), lambda i: (i, 0)),
                  pl.BlockSpec((1, d), lambda i: (0, 0)),
                  pl.BlockSpec((1, 1, d), lambda i: (rf(i), 0, 0)),
                  pl.BlockSpec((1, 1, d), lambda i: (rf(i), 0, 0))],
        out_specs=pl.BlockSpec((tm, d), lambda i: (i, 0)),
        out_shape=jax.ShapeDtypeStruct((n, d), out_dtype),
        compiler_params=_cp(("parallel",)),
        name="norm_mod",
    )(x, nw.reshape(1, d), shift.reshape(MOD_ROWS, 1, d), scale.reshape(MOD_ROWS, 1, d))


def _mm_body(a_ref, w_ref, o_ref):
    o_ref[...] = jnp.dot(a_ref[...], w_ref[...], preferred_element_type=F32).astype(o_ref.dtype)


def matmul(a, w, out_dtype, tn):
    n, k = a.shape
    m = w.shape[1]
    tm = min(2048, n)
    return pl.pallas_call(
        _mm_body,
        grid=(n // tm, m // tn),
        in_specs=[pl.BlockSpec((tm, k), lambda i, j: (i, 0)),
                  pl.BlockSpec((k, tn), lambda i, j: (0, j))],
        out_specs=pl.BlockSpec((tm, tn), lambda i, j: (i, j)),
        out_shape=jax.ShapeDtypeStruct((n, m), out_dtype),
        compiler_params=_cp(("parallel", "parallel")),
        name="proj",
    )(a, w)


def _mm_res_body(a_ref, w_ref, x_ref, g_ref, o_ref):
    y = jnp.dot(a_ref[...], w_ref[...], preferred_element_type=F32)
    o_ref[...] = x_ref[...] + g_ref[0] * y


def matmul_gated_residual(a, w, x, gate, per_batch):
    n, k = a.shape
    d = w.shape[1]
    tm = _row_tile(1024, n, per_batch)
    tn = min(1024, d)
    rf = _mod_row_fn(n, tm, per_batch)
    return pl.pallas_call(
        _mm_res_body,
        grid=(n // tm, d // tn),
        in_specs=[pl.BlockSpec((tm, k), lambda i, j: (i, 0)),
                  pl.BlockSpec((k, tn), lambda i, j: (0, j)),
                  pl.BlockSpec((tm, tn), lambda i, j: (i, j)),
                  pl.BlockSpec((1, 1, tn), lambda i, j: (rf(i), 0, j))],
        out_specs=pl.BlockSpec((tm, tn), lambda i, j: (i, j)),
        out_shape=jax.ShapeDtypeStruct((n, d), F32),
        compiler_params=_cp(("parallel", "parallel")),
        name="out_proj_residual",
    )(a, w, x, gate.reshape(MOD_ROWS, 1, d))


def _merge_body(h_ref, *refs):
    b_refs, wg_refs = refs[:N_BRANCH], refs[N_BRANCH:2 * N_BRANCH]
    wb_ref, o_ref = refs[2 * N_BRANCH:]
    h = h_ref[...]
    acc = None
    for n in range(N_BRANCH):
        gate = jnp.dot(h, wg_refs[n][...], preferred_element_type=F32)
        proj = jnp.dot(b_refs[n][...], wb_ref[n], preferred_element_type=F32)
        term = _sigmoid(gate) * proj
        acc = term if acc is None else acc + term
    o_ref[...] = acc.astype(o_ref.dtype)


def merge_branches_gated(h, branches, w_gate, w_branch):
    n, d = h.shape
    bw = branches[0].shape[1]
    tm = min(1024, n)
    tn = min(512, d)
    nj = d // tn
    gate_spec = lambda b: pl.BlockSpec((d, tn), lambda i, j: (0, b * nj + j))
    return pl.pallas_call(
        _merge_body,
        grid=(n // tm, nj),
        in_specs=[pl.BlockSpec((tm, d), lambda i, j: (i, 0))]
                 + [pl.BlockSpec((tm, bw), lambda i, j: (i, 0))] * N_BRANCH
                 + [gate_spec(b) for b in range(N_BRANCH)]
                 + [pl.BlockSpec((N_BRANCH, bw, tn), lambda i, j: (0, 0, j))],
        out_specs=pl.BlockSpec((tm, tn), lambda i, j: (i, j)),
        out_shape=jax.ShapeDtypeStruct((n, d), BF16),
        compiler_params=_cp(("parallel", "parallel")),
        name="gate_merge",
    )(h, *branches, *([w_gate] * N_BRANCH), w_branch)


def _mla_prep_body(cq_ref, ckv_ref, misc_ref, qnw_ref, kvnw_ref, wqa_ref, wqb_ref, wk_ref, wv_ref,
                   ska_ref, skb_ref, cq_tab, sq_tab, q_out, k_out, v_out):
    def norm(v, w_ref):
        vf = v.astype(F32)
        return (vf * lax.rsqrt(jnp.mean(vf * vf, axis=-1, keepdims=True) + NORM_EPS) * w_ref[...]).astype(BF16)

    xq = norm(cq_ref[...], qnw_ref)
    xkv = norm(ckv_ref[...], kvnw_ref)
    cos, sin = cq_tab[...], sq_tab[...]
    misc = misc_ref[...].astype(BF16)
    kr = (jnp.dot(misc, ska_ref[...], preferred_element_type=F32) * cos
          + jnp.dot(misc, skb_ref[...], preferred_element_type=F32) * sin)
    for h in range(MLA_HEADS):
        qa = jnp.dot(xq, wqa_ref[h], preferred_element_type=F32)
        qb = jnp.dot(xq, wqb_ref[h], preferred_element_type=F32)
        q_out[0, h] = (qa * cos + qb * sin).astype(BF16)
        k_out[0, h] = (jnp.dot(xkv, wk_ref[h], preferred_element_type=F32) + kr).astype(BF16)
        v_out[0, h] = jnp.dot(xkv, wv_ref[h], preferred_element_type=F32).astype(BF16)


def _mla_weights(q_norm_w, kv_norm_w, w_uq, w_ukv):
    dk = MLA_NOPE + MLA_ROPE
    half = MLA_ROPE // 2
    scale = dk ** -0.5
    wq = jnp.transpose(w_uq, (1, 0, 2)) * scale
    nope0 = jnp.zeros(wq.shape[:2] + (MLA_NOPE,), F32)
    wq_rot = jnp.concatenate([nope0, -wq[..., MLA_NOPE + half:], wq[..., MLA_NOPE:MLA_NOPE + half]], axis=-1)
    wkv = jnp.transpose(w_ukv, (1, 0, 2))
    wk = jnp.concatenate([wkv[..., :MLA_NOPE], jnp.zeros(wkv.shape[:2] + (MLA_ROPE,), F32)], axis=-1)
    wv = wkv[..., MLA_NOPE:]
    eye = jnp.eye(MLA_ROPE, dtype=F32)
    rot = jnp.concatenate([-eye[:, half:], eye[:, :half]], axis=-1)
    pad_r = MISC_W - MLA_ROPE
    ska = jnp.pad(eye, ((0, pad_r), (MLA_NOPE, 0)))
    skb = jnp.pad(rot, ((0, pad_r), (MLA_NOPE, 0)))
    return tuple(t.astype(BF16) for t in (wq, wq_rot, wk, wv, ska, skb))


def _mla_tables(rope, length):
    dk = MLA_NOPE + MLA_ROPE
    if rope is None:
        return jnp.ones((length, dk), F32), jnp.zeros((length, dk), F32)
    cos, sin = rope
    ones = jnp.ones((length, MLA_NOPE), F32)
    return (jnp.concatenate([ones, cos, cos], axis=-1),
            jnp.concatenate([0.0 * ones, sin, sin], axis=-1))


def mla_prepare(main, misc, b, length, weights, q_norm_w, kv_norm_w, tables):
    wq, wq_rot, wk, wv, ska, skb = weights
    cos, sin = tables
    dk = MLA_NOPE + MLA_ROPE
    tm = min(512, length)
    nt = length // tm
    full = lambda a: pl.BlockSpec(a.shape, lambda bi, i: (0,) * a.ndim)
    qnw = q_norm_w.reshape(1, -1)
    kvnw = kv_norm_w.reshape(1, -1)
    outs = pl.pallas_call(
        _mla_prep_body,
        grid=(b, nt),
        in_specs=[pl.BlockSpec((tm, MLA_Q_LORA), lambda bi, i: (bi * nt + i, OFF_CQ // MLA_Q_LORA)),
                  pl.BlockSpec((tm, MLA_KV_LORA), lambda bi, i: (bi * nt + i, OFF_CKV // MLA_KV_LORA)),
                  pl.BlockSpec((tm, MISC_W), lambda bi, i: (bi * nt + i, 0)),
                  full(qnw), full(kvnw), full(wq), full(wq_rot), full(wk), full(wv), full(ska), full(skb),
                  pl.BlockSpec((tm, dk), lambda bi, i: (i, 0)),
                  pl.BlockSpec((tm, dk), lambda bi, i: (i, 0))],
        out_specs=[pl.BlockSpec((1, MLA_HEADS, tm, dk), lambda bi, i: (bi, 0, i, 0)),
                   pl.BlockSpec((1, MLA_HEADS, tm, dk), lambda bi, i: (bi, 0, i, 0)),
                   pl.BlockSpec((1, MLA_HEADS, tm, MLA_V), lambda bi, i: (bi, 0, i, 0))],
        out_shape=[jax.ShapeDtypeStruct((b, MLA_HEADS, length, dk), BF16),
                   jax.ShapeDtypeStruct((b, MLA_HEADS, length, dk), BF16),
                   jax.ShapeDtypeStruct((b, MLA_HEADS, length, MLA_V), BF16)],
        compiler_params=_cp(("parallel", "parallel")),
        name="mla_prep",
    )(main, main, misc, qnw, kvnw, wq, wq_rot, wk, wv, ska, skb, cos, sin)
    return outs


def _gqa_prep_body(q_ref, k_ref, v_ref, qnw_ref, knw_ref, gsum_ref, rot_ref, cos_ref, sin_ref,
                   q_out, k_out, v_out):
    cos, sin = cos_ref[...], sin_ref[...]

    def prep(v, nw, width):
        vf = v.astype(F32)
        sq = vf * vf
        hi = sq.astype(BF16)
        lo = (sq - hi.astype(F32)).astype(BF16)
        g = gsum_ref[:width, :width]
        ss = jnp.dot(hi, g, preferred_element_type=F32) + jnp.dot(lo, g, preferred_element_type=F32)
        xn = vf * lax.rsqrt(ss * (1.0 / GQA_HEAD_DIM) + NORM_EPS) * nw
        xr = jnp.dot(xn.astype(BF16), rot_ref[:width, :width], preferred_element_type=F32)
        return xn * cos[:, :width] + xr * sin[:, :width]

    qf = prep(q_ref[...], qnw_ref[...], GQA_HEADS * GQA_HEAD_DIM) * GQA_HEAD_DIM ** -0.5
    kf = prep(k_ref[...], knw_ref[...], LANES)
    q_out[...] = qf.astype(BF16)
    k_out[...] = kf.astype(BF16)
    v_out[...] = v_ref[...]


def gqa_prepare(main, b, length, q_norm_w, k_norm_w, rope):
    n = b * length
    qw = GQA_HEADS * GQA_HEAD_DIM
    kw = GQA_KV_HEADS * GQA_HEAD_DIM
    half = GQA_HEAD_DIM // 2
    if rope is None:
        cos = jnp.ones((length, qw), F32)
        sin = jnp.zeros((length, qw), F32)
    else:
        cos = jnp.tile(jnp.concatenate([rope[0], rope[0]], axis=-1), (1, GQA_HEADS))
        sin = jnp.tile(jnp.concatenate([rope[1], rope[1]], axis=-1), (1, GQA_HEADS))
    head = np.arange(qw) // GQA_HEAD_DIM
    gsum = jnp.asarray(head[:, None] == head[None, :], BF16)
    eye = np.eye(GQA_HEAD_DIM, dtype=np.float32)
    rot1 = np.concatenate([-eye[:, half:], eye[:, :half]], axis=-1)
    rot = jnp.asarray(np.kron(np.eye(GQA_HEADS, dtype=np.float32), rot1), BF16)
    tm = min(512, length)
    nt = length // tm
    full = lambda a: pl.BlockSpec(a.shape, lambda bi, i: (0,) * a.ndim)
    qnw = jnp.tile(q_norm_w, GQA_HEADS).reshape(1, qw)
    knw = jnp.tile(k_norm_w, GQA_KV_HEADS).reshape(1, kw)
    return pl.pallas_call(
        _gqa_prep_body,
        grid=(b, nt),
        in_specs=[pl.BlockSpec((tm, qw), lambda bi, i: (bi * nt + i, OFF_GQ // qw)),
                  pl.BlockSpec((tm, kw), lambda bi, i: (bi * nt + i, OFF_GK // kw)),
                  pl.BlockSpec((tm, kw), lambda bi, i: (bi * nt + i, OFF_GV // kw)),
                  full(qnw), full(knw), full(gsum), full(rot),
                  pl.BlockSpec((tm, qw), lambda bi, i: (i, 0)),
                  pl.BlockSpec((tm, qw), lambda bi, i: (i, 0))],
        out_specs=[pl.BlockSpec((tm, qw), lambda bi, i: (bi * nt + i, 0)),
                   pl.BlockSpec((tm, kw), lambda bi, i: (bi * nt + i, 0)),
                   pl.BlockSpec((tm, kw), lambda bi, i: (bi * nt + i, 0))],
        out_shape=[jax.ShapeDtypeStruct((n, qw), BF16),
                   jax.ShapeDtypeStruct((n, kw), BF16),
                   jax.ShapeDtypeStruct((n, kw), BF16)],
        compiler_params=_cp(("parallel", "parallel")),
        name="gqa_prep",
    )(main, main, main, qnw, knw, gsum, rot, cos, sin)


FLASH_CHAIN_ROWS = 256


def _flash_body(q_ref, k_ref, v_ref, kc_ref, vc_ref, o_ref, m_ref, l_ref, acc_ref, *, nk, has_ctx, chains):
    ki = pl.program_id(3)

    @pl.when(ki == 0)
    def _():
        m_ref[...] = jnp.full_like(m_ref, -jnp.inf)
        l_ref[...] = jnp.zeros_like(l_ref)
        acc_ref[...] = jnp.zeros_like(acc_ref)

    def step(k, vt):
        ss = [jnp.dot(k, q_ref[0, 0, gi, :, r0:r0 + rc], preferred_element_type=F32)
              for gi, r0, rc in chains]
        m_prev = [m_ref[ci] for ci in range(len(chains))]
        m_new = [jnp.maximum(mp, jnp.max(s, axis=0, keepdims=True)) for mp, s in zip(m_prev, ss)]
        ps = [jnp.exp(s - mn) for s, mn in zip(ss, m_new)]
        alphas = [jnp.exp(mp - mn) for mp, mn in zip(m_prev, m_new)]
        pv = [jnp.dot(vt, p.astype(BF16), preferred_element_type=F32) for p in ps]
        for ci in range(len(chains)):
            l_ref[ci] = alphas[ci] * l_ref[ci] + jnp.sum(ps[ci], axis=0, keepdims=True)
            acc_ref[ci] = alphas[ci] * acc_ref[ci] + pv[ci]
            m_ref[ci] = m_new[ci]

    @pl.when(ki < nk)
    def _():
        step(k_ref[0, 0], v_ref[0, 0])

    if has_ctx:
        @pl.when(ki == nk)
        def _():
            step(kc_ref[0, 0], vc_ref[0, 0])

    @pl.when(ki == nk - 1 + int(has_ctx))
    def _():
        for ci, (gi, r0, rc) in enumerate(chains):
            o_ref[0, 0, gi, :, r0:r0 + rc] = (acc_ref[ci] / l_ref[ci]).astype(o_ref.dtype)


def flash_attention(q, k, v, kc, vc, tq, tk):
    b, hkv, g, sq, dk = q.shape
    sk = k.shape[2]
    dv = v.shape[3]
    tq = min(tq, sq)
    tk = min(tk, sk)
    nk = sk // tk
    has_ctx = kc is not None
    if not has_ctx:
        kc, vc = k[:, :, :LANES], v[:, :, :LANES]
    skc = kc.shape[2]
    rc = min(FLASH_CHAIN_ROWS, tq)
    chains = tuple((gi, r0, rc) for gi in range(g) for r0 in range(0, tq, rc))
    nch = len(chains)
    body = functools.partial(_flash_body, nk=nk, has_ctx=has_ctx, chains=chains)
    qt = jnp.swapaxes(q, 3, 4)
    vt = jnp.swapaxes(v, 2, 3)
    vct = jnp.swapaxes(vc, 2, 3)
    out_t = pl.pallas_call(
        body,
        grid=(b, hkv, sq // tq, nk + int(has_ctx)),
        in_specs=[pl.BlockSpec((1, 1, g, dk, tq), lambda bi, h, qi, ki: (bi, h, 0, 0, qi)),
                  pl.BlockSpec((1, 1, tk, dk), lambda bi, h, qi, ki: (bi, h, jnp.minimum(ki, nk - 1), 0)),
                  pl.BlockSpec((1, 1, dv, tk), lambda bi, h, qi, ki: (bi, h, 0, jnp.minimum(ki, nk - 1))),
                  pl.BlockSpec((1, 1, skc, dk), lambda bi, h, qi, ki: (bi, h, 0, 0)),
                  pl.BlockSpec((1, 1, dv, skc), lambda bi, h, qi, ki: (bi, h, 0, 0))],
        out_specs=pl.BlockSpec((1, 1, g, dv, tq), lambda bi, h, qi, ki: (bi, h, 0, 0, qi)),
        out_shape=jax.ShapeDtypeStruct((b, hkv, g, dv, sq), BF16),
        scratch_shapes=[pltpu.VMEM((nch, 1, rc), F32), pltpu.VMEM((nch, 1, rc), F32),
                        pltpu.VMEM((nch, dv, rc), F32)],
        compiler_params=_cp(("parallel", "parallel", "parallel", "arbitrary")),
        name="flash_attention",
    )(qt, k, vt, kc, vct)
    return jnp.swapaxes(out_t, 3, 4)


def _router_body(x_ref, nw_ref, sh_ref, sc_ref, wrh_ref, wrl_ref, rb_ref, tri_ref,
                 h_ref, ri_ref, rw_ref, cnt_ref, carry_ref):
    i = pl.program_id(0)

    @pl.when(i == 0)
    def _():
        carry_ref[...] = jnp.zeros_like(carry_ref)

    xf = x_ref[...]
    y = xf * lax.rsqrt(jnp.mean(xf * xf, axis=-1, keepdims=True) + NORM_EPS) * nw_ref[...]
    h = y * (1.0 + sc_ref[0]) + sh_ref[0]
    h_ref[...] = h
    hi = h.astype(BF16)
    lo = (h - hi.astype(F32)).astype(BF16)
    nt = (((1,), (1,)), ((), ()))
    logits = (lax.dot_general(wrh_ref[...], hi, nt, preferred_element_type=F32)
              + lax.dot_general(wrh_ref[...], lo, nt, preferred_element_type=F32)
              + lax.dot_general(wrl_ref[...], hi, nt, preferred_element_type=F32))
    scores = _sigmoid(logits)
    sel = scores + rb_ref[...]
    s = [scores[e:e + 1] for e in range(N_EXPERTS)]
    v = [sel[e:e + 1] for e in range(N_EXPERTS)]
    epg = EXPERTS_PER_GROUP
    gscore = []
    for gi in range(N_GROUPS):
        mem = v[gi * epg:(gi + 1) * epg]
        best = None
        for a in range(epg):
            for c in range(a + 1, epg):
                pair = mem[a] + mem[c]
                best = pair if best is None else jnp.maximum(best, pair)
        gscore.append(best)
    is_best = []
    for gi in range(N_GROUPS):
        ok = None
        for gj in range(N_GROUPS):
            if gj == gi:
                continue
            c = (gscore[gi] > gscore[gj]) if gj < gi else (gscore[gi] >= gscore[gj])
            ok = c if ok is None else (ok & c)
        is_best.append(ok)
    chosen = []
    for e in range(N_EXPERTS):
        gi = e // epg
        rank = jnp.zeros_like(v[e])
        for e2 in range(gi * epg, (gi + 1) * epg):
            if e2 == e:
                continue
            ahead = (v[e2] >= v[e]) if e2 < e else (v[e2] > v[e])
            rank = rank + jnp.where(ahead, 1.0, 0.0)
        chosen.append(is_best[gi] & (rank < TOP_K))
    chosen_f = jnp.concatenate([jnp.where(cm, 1.0, 0.0) for cm in chosen], axis=0)
    total = None
    for e in range(N_EXPERTS):
        t = jnp.where(chosen[e], s[e], 0.0)
        total = t if total is None else total + t
    pos = jnp.dot(chosen_f.astype(BF16), tri_ref[...], preferred_element_type=F32) + carry_ref[...]
    carry_ref[...] += jnp.sum(chosen_f, axis=-1, keepdims=True)
    cnt_ref[...] = jnp.broadcast_to(carry_ref[...], cnt_ref.shape)
    e_lo = jnp.full_like(v[0], float(N_EXPERTS))
    e_hi = jnp.full_like(v[0], -1.0)
    for e in range(N_EXPERTS):
        e_lo = jnp.where(chosen[e], jnp.minimum(e_lo, float(e)), e_lo)
        e_hi = jnp.where(chosen[e], jnp.maximum(e_hi, float(e)), e_hi)
    zero = jnp.zeros_like(v[0])
    w_lo, w_hi, p_lo, p_hi = zero, zero, zero, zero
    for e in range(N_EXPERTS):
        pe = pos[e:e + 1]
        w_lo = jnp.where(e_lo == float(e), s[e], w_lo)
        w_hi = jnp.where(e_hi == float(e), s[e], w_hi)
        p_lo = jnp.where(e_lo == float(e), pe, p_lo)
        p_hi = jnp.where(e_hi == float(e), pe, p_hi)
    ri_ref[...] = jnp.concatenate([e_lo, e_hi, p_lo, p_hi, zero, zero, zero, zero], axis=0).astype(jnp.int32)
    rw_ref[...] = jnp.concatenate([w_lo / total, w_hi / total, zero, zero, zero, zero, zero, zero], axis=0)


def norm_route(x, nw, shift, scale, per_batch, w_router, router_bias):
    n, d = x.shape
    tm = _row_tile(512, n, per_batch)
    rf = _mod_row_fn(n, tm, per_batch)
    wr_t = w_router.T
    wr_hi = wr_t.astype(BF16)
    wr_lo = (wr_t - wr_hi.astype(F32)).astype(BF16)
    tri = jnp.asarray(np.triu(np.ones((tm, tm), np.float32), 1), BF16)
    const = lambda a: pl.BlockSpec(a.shape, lambda i: (0,) * a.ndim)
    rb = router_bias.reshape(N_EXPERTS, 1).astype(F32)
    return pl.pallas_call(
        _router_body,
        grid=(n // tm,),
        in_specs=[pl.BlockSpec((tm, d), lambda i: (i, 0)),
                  pl.BlockSpec((1, d), lambda i: (0, 0)),
                  pl.BlockSpec((1, 1, d), lambda i: (rf(i), 0, 0)),
                  pl.BlockSpec((1, 1, d), lambda i: (rf(i), 0, 0)),
                  const(wr_hi), const(wr_lo), const(rb), const(tri)],
        out_specs=[pl.BlockSpec((tm, d), lambda i: (i, 0)),
                   pl.BlockSpec((8, tm), lambda i: (0, i)),
                   pl.BlockSpec((8, tm), lambda i: (0, i)),
                   pl.BlockSpec((N_EXPERTS, LANES), lambda i: (0, 0))],
        out_shape=[jax.ShapeDtypeStruct((n, d), F32),
                   jax.ShapeDtypeStruct((8, n), jnp.int32),
                   jax.ShapeDtypeStruct((8, n), F32),
                   jax.ShapeDtypeStruct((N_EXPERTS, LANES), F32)],
        scratch_shapes=[pltpu.VMEM((N_EXPERTS, 1), F32)],
        compiler_params=_cp(("arbitrary",)),
        name="norm_route",
    )(x, nw.reshape(1, d), shift.reshape(MOD_ROWS, 1, d), scale.reshape(MOD_ROWS, 1, d),
      wr_hi, wr_lo, rb, tri)


def _dispatch_body(sa_ref, sb_ref, pad_ref, h_ref, xs_ref, zero_ref, sem, *, tm, n_pad):
    i = pl.program_id(0)
    base = i * tm

    def row_copy(src, r, slot):
        return pltpu.make_async_copy(src.at[pl.ds(r, 1)], xs_ref.at[pl.ds(slot, 1)], sem)

    @pl.when(i == 0)
    def _():
        zero_ref[...] = jnp.zeros_like(zero_ref)

        def fill(j, carry):
            row_copy(zero_ref, 0, pad_ref[2 * j]).start(priority=0)
            row_copy(zero_ref, 1, pad_ref[2 * j + 1]).start(priority=1)
            return carry
        lax.fori_loop(0, n_pad // 2, fill, 0, unroll=DMA_UNROLL)

        def drain(j, carry):
            row_copy(zero_ref, 0, 0).wait()
            return carry
        lax.fori_loop(0, n_pad, drain, 0, unroll=DMA_UNROLL)

    def issue(r, carry):
        row_copy(h_ref, r, sa_ref[base + r]).start(priority=0)
        row_copy(h_ref, r, sb_ref[base + r]).start(priority=1)
        return carry
    lax.fori_loop(0, tm, issue, 0, unroll=DMA_UNROLL)

    def drain2(r, carry):
        row_copy(h_ref, 0, 0).wait()
        row_copy(h_ref, 0, 0).wait()
        return carry
    lax.fori_loop(0, tm, drain2, 0, unroll=DMA_UNROLL)


def moe_dispatch(h, slot_a, slot_b, pad_slots, n_slots):
    n, d = h.shape
    tm = min(256, n)
    n_pad = pad_slots.shape[0]
    body = functools.partial(_dispatch_body, tm=tm, n_pad=n_pad)
    return pl.pallas_call(
        body,
        grid_spec=pltpu.PrefetchScalarGridSpec(
            num_scalar_prefetch=3,
            grid=(n // tm,),
            in_specs=[pl.BlockSpec((tm, d), lambda i, sa, sb, pd: (i, 0))],
            out_specs=pl.BlockSpec(memory_space=pl.ANY),
            scratch_shapes=[pltpu.VMEM((8, d), F32), pltpu.SemaphoreType.DMA(())]),
        out_shape=jax.ShapeDtypeStruct((n_slots, d), F32),
        compiler_params=_cp(("arbitrary",)),
        name="moe_dispatch",
    )(slot_a, slot_b, pad_slots, h)


def _experts_body(te_ref, nu_ref, xs_ref, wg_ref, wu_ref, wd_ref, y_ref):
    i = pl.program_id(0)

    @pl.when(i < nu_ref[0])
    def _():
        xb = xs_ref[...].astype(BF16)
        hg = jnp.dot(xb, wg_ref[0].astype(BF16), preferred_element_type=F32)
        hu = jnp.dot(xb, wu_ref[0].astype(BF16), preferred_element_type=F32)
        act = (hg * _sigmoid(hg) * hu).astype(BF16)
        y_ref[...] = jnp.dot(act, wd_ref[0].astype(BF16), preferred_element_type=F32)

    @pl.when(i >= nu_ref[0])
    def _():
        y_ref[...] = jnp.zeros_like(y_ref)


def moe_experts(xs, tile_expert, n_used, w_gate, w_up, w_down, base):
    s, d = xs.shape
    f = w_gate.shape[2]
    tm = MOE_TILE
    return pl.pallas_call(
        _experts_body,
        grid_spec=pltpu.PrefetchScalarGridSpec(
            num_scalar_prefetch=2,
            grid=(s // tm,),
            in_specs=[pl.BlockSpec((tm, d), lambda i, te, nu: (jnp.minimum(i, nu[0] - 1), 0)),
                      pl.BlockSpec((1, d, f), lambda i, te, nu: (te[i] + base, 0, 0)),
                      pl.BlockSpec((1, d, f), lambda i, te, nu: (te[i] + base, 0, 0)),
                      pl.BlockSpec((1, f, d), lambda i, te, nu: (te[i] + base, 0, 0))],
            out_specs=pl.BlockSpec((tm, d), lambda i, te, nu: (i, 0))),
        out_shape=jax.ShapeDtypeStruct((s, d), F32),
        compiler_params=_cp(("arbitrary",)),
        name="moe_experts",
    )(tile_expert, n_used, xs, w_gate, w_up, w_down)


DMA_UNROLL = 8


def _combine_body(sa_ref, sb_ref, x_ref, w_ref, g_ref, y_ref, o_ref, ba_ref, bb_ref, sem, *, tm):
    i = pl.program_id(0)
    n_tiles = pl.num_programs(0)

    def row_copy(slot, dst, buf, r):
        return pltpu.make_async_copy(y_ref.at[pl.ds(slot, 1)], dst.at[buf, pl.ds(r, 1)], sem.at[buf])

    def issue_tile(tile, buf):
        base = tile * tm

        def issue(r, carry):
            row_copy(sa_ref[base + r], ba_ref, buf, r).start(priority=0)
            row_copy(sb_ref[base + r], bb_ref, buf, r).start(priority=1)
            return carry
        lax.fori_loop(0, tm, issue, 0, unroll=DMA_UNROLL)

    @pl.when(i == 0)
    def _():
        issue_tile(0, 0)

    @pl.when(i + 1 < n_tiles)
    def _():
        issue_tile(i + 1, (i + 1) % 2)

    buf = i % 2

    def drain(r, carry):
        row_copy(0, ba_ref, buf, 0).wait()
        row_copy(0, bb_ref, buf, 0).wait()
        return carry
    lax.fori_loop(0, tm, drain, 0, unroll=DMA_UNROLL)

    w = w_ref[...]
    mix = w[:, 0:1] * ba_ref[buf] + w[:, 1:2] * bb_ref[buf]
    o_ref[...] = x_ref[...] + g_ref[0] * mix


def moe_combine(x, y, slot_a, slot_b, wts, gate, per_batch):
    n, d = x.shape
    tm = _row_tile(256, n, per_batch)
    rf = _mod_row_fn(n, tm, per_batch)
    body = functools.partial(_combine_body, tm=tm)
    return pl.pallas_call(
        body,
        grid_spec=pltpu.PrefetchScalarGridSpec(
            num_scalar_prefetch=2,
            grid=(n // tm,),
            in_specs=[pl.BlockSpec((tm, d), lambda i, sa, sb: (i, 0)),
                      pl.BlockSpec((tm, 8), lambda i, sa, sb: (i, 0)),
                      pl.BlockSpec((1, 1, d), lambda i, sa, sb: (rf(i), 0, 0)),
                      pl.BlockSpec(memory_space=pl.ANY)],
            out_specs=pl.BlockSpec((tm, d), lambda i, sa, sb: (i, 0)),
            scratch_shapes=[pltpu.VMEM((2, tm, d), F32), pltpu.VMEM((2, tm, d), F32),
                            pltpu.SemaphoreType.DMA((2,))]),
        out_shape=jax.ShapeDtypeStruct((n, d), F32),
        compiler_params=_cp(("arbitrary",)),
        name="moe_combine",
    )(slot_a, slot_b, x, wts, gate.reshape(MOD_ROWS, 1, d), y)


def moe_layer(x, nw, mod, per_batch, w_router, router_bias, w_gate, w_up, w_down, base):
    n, d = x.shape
    h, route_i, route_w, counts = norm_route(x, nw, mod[:, 3], mod[:, 4], per_batch, w_router, router_bias)
    cnt = counts[:, 0].astype(jnp.int32)
    seg = ((cnt + MOE_TILE - 1) // MOE_TILE) * MOE_TILE
    off = jnp.concatenate([jnp.zeros((1,), jnp.int32), jnp.cumsum(seg)])
    n_slots = TOP_K * n + N_EXPERTS * MOE_TILE
    slot_a = off[route_i[0]] + route_i[2]
    slot_b = off[route_i[1]] + route_i[3]
    n_pad = n_slots - TOP_K * n
    padcnt = seg - cnt
    padstart = jnp.concatenate([jnp.zeros((1,), jnp.int32), jnp.cumsum(padcnt)])
    j = jnp.arange(n_pad, dtype=jnp.int32)
    count_le = lambda edges, v: jnp.sum((edges[None, :] <= v[:, None]).astype(jnp.int32), axis=1)
    e_of = jnp.clip(count_le(padstart, j) - 1, 0, N_EXPERTS)
    in_seg = off[jnp.minimum(e_of, N_EXPERTS - 1)] + cnt[jnp.minimum(e_of, N_EXPERTS - 1)] + (j - padstart[e_of])
    tail = off[N_EXPERTS] + (j - padstart[N_EXPERTS])
    pad_slots = jnp.where(e_of < N_EXPERTS, in_seg, tail).astype(jnp.int32)
    n_tiles = n_slots // MOE_TILE
    tile_start = jnp.arange(n_tiles, dtype=jnp.int32) * MOE_TILE
    n_used = (off[N_EXPERTS] // MOE_TILE).astype(jnp.int32).reshape(1)
    tile_expert = jnp.clip(count_le(off, tile_start) - 1, 0, N_EXPERTS - 1).astype(jnp.int32)
    last_used = tile_expert[jnp.maximum(n_used[0] - 1, 0)]
    tile_expert = jnp.where(jnp.arange(n_tiles) < n_used[0], tile_expert, last_used)

    xs = moe_dispatch(h, slot_a, slot_b, pad_slots, n_slots)
    y = moe_experts(xs, tile_expert, n_used, w_gate, w_up, w_down, base)
    wts = jnp.transpose(route_w)
    return moe_combine(x, y, slot_a, slot_b, wts, mod[:, 5], per_batch)


def _final_norm_body(x_ref, w_ref, o_ref):
    xf = x_ref[...]
    y = xf * lax.rsqrt(jnp.mean(xf * xf, axis=-1, keepdims=True) + NORM_EPS)
    o_ref[...] = y * w_ref[...]


def final_rms_norm(x, w):
    n, d = x.shape
    rows = 512
    return pl.pallas_call(
        _final_norm_body,
        grid=(n // rows,),
        in_specs=[pl.BlockSpec((rows, d), lambda i: (i, 0)), pl.BlockSpec((1, d), lambda i: (0, 0))],
        out_specs=pl.BlockSpec((rows, d), lambda i: (i, 0)),
        out_shape=jax.ShapeDtypeStruct((n, d), x.dtype),
        compiler_params=_cp(("parallel",)),
        name="final_norm",
    )(x, w.reshape(1, d))


def _reorder_w_in(w):
    o = np.cumsum((0,) + IN_WIDTHS)
    seg = lambda i: w[:, o[i]:o[i + 1]]
    main = jnp.concatenate([seg(0), seg(1), seg(2), seg(3), seg(6), seg(9), seg(12), seg(7), seg(10), seg(11)], axis=1)
    misc = jnp.concatenate([seg(8), seg(4), seg(5)], axis=1)
    misc = jnp.pad(misc, ((0, 0), (0, MISC_W - misc.shape[1])))
    return main.astype(BF16), misc.astype(BF16)


def _attention_branches(main_l, misc_l, main_c, misc_c, b, seq, ctx_len, ctx_out, rope_mla, rope_gqa,
                        mla_w, mla_qn, mla_kvn, gqa_qn, gqa_kn):
    g = GQA_HEADS // GQA_KV_HEADS
    hd = GQA_HEAD_DIM

    mq_l, mk_l, mv_l = mla_prepare(main_l, misc_l, b, seq, mla_w, mla_qn, mla_kvn, _mla_tables(rope_mla, seq))
    mq_c, mk_c, mv_c = mla_prepare(main_c, misc_c, b, ctx_len, mla_w, mla_qn, mla_kvn, _mla_tables(None, ctx_len))
    tk_all = (seq + ctx_len) // 2
    cat = lambda lat, ctx_: jnp.concatenate([lat, ctx_], axis=2)
    mla_l = flash_attention(mq_l[:, :, None], cat(mk_l, mk_c), cat(mv_l, mv_c), None, None, 1024, tk_all)
    mla_l = jnp.transpose(mla_l[:, :, 0], (0, 2, 1, 3)).reshape(b * seq, BRANCH_W)

    def split_heads(t, length, heads):
        return jnp.transpose(t.reshape(b, length, heads, hd), (0, 2, 1, 3))

    gq_l, gk_l, gv_l = gqa_prepare(main_l, b, seq, gqa_qn, gqa_kn, rope_gqa)
    gq_c, gk_c, gv_c = gqa_prepare(main_c, b, ctx_len, gqa_qn, gqa_kn, None)
    gq_l5 = split_heads(gq_l, seq, GQA_HEADS).reshape(b, GQA_KV_HEADS, g, seq, hd)
    gk_l4, gv_l4 = split_heads(gk_l, seq, GQA_KV_HEADS), split_heads(gv_l, seq, GQA_KV_HEADS)
    gk_c4, gv_c4 = split_heads(gk_c, ctx_len, GQA_KV_HEADS), split_heads(gv_c, ctx_len, GQA_KV_HEADS)
    gqa_l = flash_attention(gq_l5, cat(gk_l4, gk_c4), cat(gv_l4, gv_c4), None, None, 256, tk_all)
    gqa_l = jnp.transpose(gqa_l.reshape(b, GQA_HEADS, seq, hd), (0, 2, 1, 3)).reshape(b * seq, BRANCH_W)

    mla_c = gqa_c = None
    if ctx_out:
        mla_c = flash_attention(mq_c[:, :, None], mk_c, mv_c, None, None, 256, 256)
        mla_c = jnp.transpose(mla_c[:, :, 0], (0, 2, 1, 3)).reshape(b * ctx_len, BRANCH_W)
        gq_c5 = split_heads(gq_c, ctx_len, GQA_HEADS).reshape(b, GQA_KV_HEADS, g, ctx_len, hd)
        gqa_c = flash_attention(gq_c5, gk_c4, gv_c4, None, None, 256, 256)
        gqa_c = jnp.transpose(gqa_c.reshape(b, GQA_HEADS, ctx_len, hd), (0, 2, 1, 3)).reshape(b * ctx_len, BRANCH_W)
    return mla_l, gqa_l, mla_c, gqa_c


def kernel(x, c, ctx, c_ctx, w_ada, b_ada, norm1_w, norm2_w, w_in,
           gdn_conv_w, gdn_a_log, gdn_dt_bias, gdn_norm_w,
           mla_q_norm_w, mla_kv_norm_w, mla_w_uq, mla_w_ukv,
           gqa_q_norm_w, gqa_k_norm_w,
           hy_conv_w, hy_w1, hy_b1, hy_w2, hy_b2, hy_w3, hy_sin_freq, hy_bias,
           w_branch, w_out, w_router, router_bias,
           moe_w_gate, moe_w_up, moe_w_down, final_norm_w):
    b, seq, d = x.shape
    ctx_len = ctx.shape[1]
    rows = seq // GRID_W
    rope_mla = axial_rope_tables(rows, MLA_ROPE)
    rope_gqa = axial_rope_tables(rows, GQA_HEAD_DIM)
    mod_all = ada_modulation(c, c_ctx, w_ada, b_ada)
    xl = x.reshape(b * seq, d)
    xc = ctx.reshape(b * ctx_len, d)
    f32 = lambda t: t.astype(F32)
    for layer in range(DEPTH):
        ctx_out = layer < DEPTH - 1
        mod = mod_all[layer]
        w_main, w_misc = _reorder_w_in(w_in[layer][:, :MIX_IN])
        w_gates = w_in[layer][:, MIX_IN:].astype(BF16)
        w_br = w_branch[layer].astype(BF16)
        w_o = w_out[layer].astype(BF16)
        wg, wu, wd = (t.reshape((DEPTH * N_EXPERTS,) + t.shape[2:]) for t in (moe_w_gate, moe_w_up, moe_w_down))

        hl = norm_modulate(xl, norm1_w[layer], mod[:, 0], mod[:, 1], seq)
        hc = norm_modulate(xc, norm1_w[layer], mod[:, 0], mod[:, 1], None)
        main_l, misc_l = matmul(hl, w_main, BF16, 1024), matmul(hl, w_misc, F32, MISC_W)
        main_c, misc_c = matmul(hc, w_main, BF16, 1024), matmul(hc, w_misc, F32, MISC_W)

        gdn_l, gdn_c = gdn_branch(main_l, misc_l, main_c, misc_c, b, seq, ctx_len, gdn_conv_w[layer],
                                  gdn_a_log[layer], gdn_dt_bias[layer], gdn_norm_w[layer], ctx_out)

        mla_w = _mla_weights(mla_q_norm_w[layer], mla_kv_norm_w[layer], mla_w_uq[layer], mla_w_ukv[layer])
        mla_l, gqa_l, mla_c, gqa_c = _attention_branches(
            main_l, misc_l, main_c, misc_c, b, seq, ctx_len, ctx_out, rope_mla, rope_gqa,
            mla_w, mla_q_norm_w[layer], mla_kv_norm_w[layer], gqa_q_norm_w[layer], gqa_k_norm_w[layer])

        hy_params = (hy_w1[layer], hy_b1[layer], hy_w2[layer], hy_b2[layer], hy_w3[layer], hy_sin_freq[layer])
        hy_l = hyena_branch(main_l, b, seq, hy_conv_w[layer], hyena_filters(seq, *hy_params), hy_bias[layer])

        branches_l = [gdn_l.reshape(b * seq, BRANCH_W).astype(BF16), mla_l, gqa_l,
                      hy_l.reshape(b * seq, BRANCH_W).astype(BF16)]
        merged_l = merge_branches_gated(hl, branches_l, w_gates, w_br)

        if ctx_out:
            hy_c = hyena_branch(main_c, b, ctx_len, hy_conv_w[layer], hyena_filters(ctx_len, *hy_params),
                                hy_bias[layer])
            branches_c = [gdn_c.reshape(b * ctx_len, BRANCH_W).astype(BF16), mla_c, gqa_c,
                          hy_c.reshape(b * ctx_len, BRANCH_W).astype(BF16)]
            merged_c = merge_branches_gated(hc, branches_c, w_gates, w_br)
            xc = matmul_gated_residual(merged_c, w_o, xc, mod[:, 2], None)
            xc = moe_layer(xc, norm2_w[layer], mod, None, w_router, router_bias, wg, wu, wd,
                           layer * N_EXPERTS)

        xl = matmul_gated_residual(merged_l, w_o, xl, mod[:, 2], seq)
        xl = moe_layer(xl, norm2_w[layer], mod, seq, w_router, router_bias, wg, wu, wd,
                           layer * N_EXPERTS)
    return final_rms_norm(xl, final_norm_w).reshape(b, seq, d)
```

```python
import math, functools
import jax, jax.numpy as jnp
from jax import lax
import numpy as np
from jax.experimental import pallas as pl
from jax.experimental.pallas import tpu as pltpu

D_MODEL = 2048
BATCH = 4
SEQ = 4096
DEPTH = 2

GRID_W = 64
CTX_LEN = 256
N_BRANCH = 4
BRANCH_W = 512
NORM_EPS = 1e-6
Q_BLOCK = 128
ROPE_THETA = 10000.0
SHORT_CONV = 3

GDN_HEADS = 4
GDN_HEAD_DIM = 128
GDN_CHUNK = 64

MLA_HEADS = 4
MLA_Q_LORA = 512
MLA_KV_LORA = 256
MLA_NOPE = 128
MLA_ROPE = 64
MLA_V = 128

GQA_HEADS = 8
GQA_KV_HEADS = 2
GQA_HEAD_DIM = 64

HY_WIDTH = 512
HY_ORDER = 2
HY_EMB = 33
HY_HIDDEN = 64
HY_DECAY_TARGET = 1e-2
HY_FAST_DECAY = 0.3
HY_SLOW_DECAY = 1.5

N_EXPERTS = 16
N_GROUPS = 4
EXPERTS_PER_GROUP = N_EXPERTS // N_GROUPS
TOP_K = 2
D_EXPERT = 512

GDN_W = GDN_HEADS * GDN_HEAD_DIM
IN_WIDTHS = (GDN_W, GDN_W, GDN_W, GDN_W, 2 * GDN_HEADS, 2 * GDN_HEADS,
             MLA_Q_LORA, MLA_KV_LORA, MLA_ROPE,
             GQA_HEADS * GQA_HEAD_DIM, GQA_KV_HEADS * GQA_HEAD_DIM, GQA_KV_HEADS * GQA_HEAD_DIM,
             (HY_ORDER + 1) * HY_WIDTH)
MIX_IN = sum(IN_WIDTHS)
IN_DIM = MIX_IN + N_BRANCH * D_MODEL

F32 = jnp.float32
BF16 = jnp.bfloat16
LANES = 128
MOD_ROWS = 8
CTX_MOD_ROW = BATCH
MOE_TILE = 512
VMEM_LIMIT = 56 << 20

MAIN_W = 5120
OFF_GDN, OFF_CQ, OFF_GQ, OFF_HY, OFF_CKV, OFF_GK, OFF_GV = 0, 2048, 2560, 3072, 4608, 4864, 4992
MISC_W = LANES


def _cp(sem):
    return pltpu.CompilerParams(dimension_semantics=sem, vmem_limit_bytes=VMEM_LIMIT)


def _sigmoid(v):
    return 0.5 * jnp.tanh(0.5 * v) + 0.5


def rms_norm(x, w):
    xf = x.astype(jnp.float32)
    y = xf * lax.rsqrt(jnp.mean(xf * xf, axis=-1, keepdims=True) + NORM_EPS)
    return (y * w.astype(jnp.float32)).astype(x.dtype)


def l2_normalize(x):
    xf = x.astype(jnp.float32)
    return xf * lax.rsqrt(jnp.sum(xf * xf, axis=-1, keepdims=True) + NORM_EPS)


def depthwise_conv_centred(u, w):
    k = w.shape[0]
    return lax.conv_general_dilated(u, w[:, None, :].astype(u.dtype), window_strides=(1,),
                                    padding=[(k // 2, k // 2)],
                                    dimension_numbers=('NWC', 'WIO', 'NWC'),
                                    feature_group_count=u.shape[-1])


def axial_rope_tables(rows, rot_dim):
    n_freq = rot_dim // 4
    freqs = ROPE_THETA ** (-jnp.arange(n_freq, dtype=jnp.float32) / n_freq)
    row = jnp.repeat(jnp.arange(rows, dtype=jnp.float32), GRID_W)
    col = jnp.tile(jnp.arange(GRID_W, dtype=jnp.float32), rows)
    ang = jnp.concatenate([row[:, None] * freqs, col[:, None] * freqs], axis=-1)
    return jnp.cos(ang), jnp.sin(ang)


def gdn_prep(q, k, v, a, bt, conv_w, a_log, dt_bias):
    b, l = q.shape[:2]
    qkv = jax.nn.silu(depthwise_conv_centred(jnp.concatenate([q, k, v], axis=-1), conv_w)).astype(jnp.float32)
    q, k, v = jnp.split(qkv, 3, axis=-1)
    hd = (b, l, GDN_HEADS, GDN_HEAD_DIM)
    q = l2_normalize(q.reshape(hd)) * GDN_HEAD_DIM ** -0.5
    k = l2_normalize(k.reshape(hd))
    v = v.reshape(hd)
    a = a.astype(jnp.float32).reshape(b, l, 2, GDN_HEADS)
    g = -jnp.exp(a_log.astype(jnp.float32)) * jax.nn.softplus(a + dt_bias.astype(jnp.float32))
    beta = jax.nn.sigmoid(bt.astype(jnp.float32).reshape(b, l, 2, GDN_HEADS))
    return q, k, v, g, beta


def gated_delta_rule(q, k, v, g, beta, state, with_out):
    b, l, h, _ = q.shape
    dv = v.shape[-1]
    c = GDN_CHUNK
    n = l // c

    def to_chunks(t):
        t = t.reshape(b, n, c, h, *t.shape[3:])
        return jnp.moveaxis(t, (1, 3), (0, 2))

    qc, kc, vc, bc = to_chunks(q), to_chunks(k), to_chunks(v), to_chunks(beta)
    gc = jnp.cumsum(to_chunks(g), axis=-1)
    idx = jnp.arange(c)
    lower = idx[:, None] >= idx[None, :]
    strict = idx[:, None] > idx[None, :]
    diff = gc[..., :, None] - gc[..., None, :]
    decay = jnp.where(lower, jnp.exp(jnp.where(lower, diff, 0.0)), 0.0)
    kb = kc * bc[..., None]
    a = jnp.where(strict, jnp.einsum('nbhid,nbhjd->nbhij', kb, kc) * decay, 0.0)
    solve = functools.partial(lax.linalg.triangular_solve, left_side=True, lower=True, unit_diagonal=True)
    u = solve(a, vc * bc[..., None])
    w = solve(a, kb * jnp.exp(gc)[..., None])
    g_last = gc[..., -1]
    k_dec = kc * jnp.exp(g_last[..., None] - gc)[..., None]
    xs = (u, w, k_dec, g_last)
    if with_out:
        qk = jnp.where(lower, jnp.einsum('nbhid,nbhjd->nbhij', qc, kc) * decay, 0.0)
        xs = xs + (qc * jnp.exp(gc)[..., None], qk)

    def step(s, inp):
        u_i, w_i, kd_i, gl_i = inp[:4]
        v_new = u_i - jnp.einsum('bhck,bhkv->bhcv', w_i, s)
        s_new = s * jnp.exp(gl_i)[..., None, None] + jnp.einsum('bhck,bhcv->bhkv', kd_i, v_new)
        if not with_out:
            return s_new, None
        qd_i, qk_i = inp[4:]
        o = jnp.einsum('bhck,bhkv->bhcv', qd_i, s) + jnp.einsum('bhij,bhjv->bhiv', qk_i, v_new)
        return s_new, o

    state, o = lax.scan(step, state, xs)
    if not with_out:
        return None, state
    o = jnp.moveaxis(o, (0, 2), (1, 3)).reshape(b, l, h, dv)
    return o, state


def gdn_output(o, z, norm_w):
    b, l = z.shape[:2]
    zh = z.reshape(b, l, GDN_HEADS, GDN_HEAD_DIM).astype(jnp.float32)
    y = rms_norm(o, norm_w) * jax.nn.silu(zh)
    return y.reshape(b, l, GDN_W).astype(z.dtype)


def gdn_mixer(p_lat, p_ctx, conv_w, a_log, dt_bias, norm_w, ctx_out):
    lat = gdn_prep(p_lat[0], p_lat[1], p_lat[2], p_lat[4], p_lat[5], conv_w, a_log, dt_bias)
    ctx = gdn_prep(p_ctx[0], p_ctx[1], p_ctx[2], p_ctx[4], p_ctx[5], conv_w, a_log, dt_bias)
    b = p_lat[0].shape[0]
    s0 = jnp.zeros((b, GDN_HEADS, GDN_HEAD_DIM, GDN_HEAD_DIM), jnp.float32)
    o_lat, o_ctx = 0.0, 0.0
    for direction in range(2):
        flip = (lambda t: t[:, ::-1]) if direction else (lambda t: t)

        def seq_args(s):
            q, k, v, g, beta = s
            return flip(q), flip(k), flip(v), flip(g[:, :, direction]), flip(beta[:, :, direction])

        oc, s_ctx = gated_delta_rule(*seq_args(ctx), s0, ctx_out)
        ol, _ = gated_delta_rule(*seq_args(lat), s_ctx, True)
        o_lat = o_lat + flip(ol)
        if ctx_out:
            o_ctx = o_ctx + flip(oc)
    out_lat = gdn_output(o_lat, p_lat[3], norm_w)
    out_ctx = gdn_output(o_ctx, p_ctx[3], norm_w) if ctx_out else None
    return out_lat, out_ctx


def hyena_filters(length, w1, b1, w2, b2, w3, sin_freq):
    t = jnp.arange(length, dtype=jnp.float32)
    bands = (HY_EMB - 1) // 2
    f = jnp.linspace(1e-4, bands - 1, bands, dtype=jnp.float32)
    phase = (2.0 * math.pi / length) * t[:, None] * f[None, :]
    feats = jnp.concatenate([t[:, None] / (length - 1), jnp.cos(phase), -jnp.sin(phase)], axis=-1)
    hid = jnp.sin(sin_freq[0] * (feats @ w1 + b1))
    hid = jnp.sin(sin_freq[1] * (hid @ w2 + b2))
    filt = (hid @ w3).astype(jnp.float32)
    centre = length // 2
    dist = jnp.abs(t - centre) / centre
    deltas = jnp.abs(jnp.linspace(math.log(HY_DECAY_TARGET) / HY_SLOW_DECAY,
                                  math.log(HY_DECAY_TARGET) / HY_FAST_DECAY,
                                  HY_ORDER * HY_WIDTH, dtype=jnp.float32))
    filt = filt * jnp.exp(-dist[:, None] * deltas[None, :])
    filt = filt / jnp.sum(jnp.abs(filt), axis=0, keepdims=True)
    return filt.reshape(length, HY_ORDER, HY_WIDTH)


def fft_conv_centred(u, h):
    l = u.shape[1]
    n = 2 * l
    uf = jnp.fft.rfft(u.astype(jnp.float32), n=n, axis=1)
    hf = jnp.fft.rfft(h.astype(jnp.float32), n=n, axis=0)
    y = jnp.fft.irfft(uf * hf[None], n=n, axis=1)
    return y[:, l // 2: l // 2 + l]


def hyena_mixer(u, conv_w, filt, bias):
    parts = jnp.split(depthwise_conv_centred(u, conv_w).astype(jnp.float32), HY_ORDER + 1, axis=-1)
    z = parts[0]
    for o in range(HY_ORDER):
        z = parts[o + 1] * (fft_conv_centred(z, filt[:, o]) + bias[o] * z)
    return z


HALO_ROWS = 16
GDN_BATCHES_PER_STEP = 4


def _conv3(x, prev_row, next_row, w_ref):
    tm = x.shape[0]
    rows = lax.broadcasted_iota(jnp.int32, x.shape, 0)
    up = jnp.where(rows == 0, prev_row, pltpu.roll(x, 1, 0))
    dn = jnp.where(rows == tm - 1, next_row, pltpu.roll(x, tm - 1, 0))
    return w_ref[0:1, :] * up + w_ref[1:2, :] * x + w_ref[2:3, :] * dn


def _halo_rows(xp_ref, xn_ref):
    i = pl.program_id(1)
    prev = jnp.where(i == 0, 0.0, xp_ref[HALO_ROWS - 1:HALO_ROWS, :].astype(F32))
    nxt = jnp.where(i == pl.num_programs(1) - 1, 0.0, xn_ref[0:1, :].astype(F32))
    return prev, nxt


def _halo_specs(tm, width, col_block, nt, n_rows):
    per = tm // HALO_ROWS
    last = n_rows // HALO_ROWS - 1
    return [pl.BlockSpec((tm, width), lambda bi, i: (bi * nt + i, col_block)),
            pl.BlockSpec((HALO_ROWS, width), lambda bi, i: (jnp.maximum((bi * nt + i) * per - 1, 0), col_block)),
            pl.BlockSpec((HALO_ROWS, width), lambda bi, i: (jnp.minimum((bi * nt + i + 1) * per, last), col_block))]


def _split3(v):
    hi = v.astype(BF16)
    r1 = v - hi.astype(F32)
    mid = r1.astype(BF16)
    lo = (r1 - mid.astype(F32)).astype(BF16)
    return hi, mid, lo


def _gdn_prep_body(x_ref, xp_ref, xn_ref, misc_ref, cw_ref, alog_ref, dt_ref, gmask_ref, tp_ref, ts_ref,
                   q_out, k_out, v_out, gcf_out, gcb_out, beta_out):
    prev, nxt = _halo_rows(xp_ref, xn_ref)
    y = _conv3(x_ref[...].astype(F32), prev, nxt, cw_ref)
    y = y * _sigmoid(y)
    hd = GDN_HEAD_DIM
    for h in range(GDN_HEADS):
        qh = y[:, h * hd:(h + 1) * hd]
        kh = y[:, GDN_W + h * hd:GDN_W + (h + 1) * hd]
        qn = qh * lax.rsqrt(jnp.sum(qh * qh, axis=-1, keepdims=True) + NORM_EPS) * hd ** -0.5
        kn = kh * lax.rsqrt(jnp.sum(kh * kh, axis=-1, keepdims=True) + NORM_EPS)
        q_out[:, h * hd:(h + 1) * hd] = qn.astype(BF16)
        k_out[:, h * hd:(h + 1) * hd] = kn.astype(BF16)
    v_out[...] = y[:, 2 * GDN_W:3 * GDN_W].astype(BF16)
    m = misc_ref[...]
    a = m + dt_ref[...]
    softplus = jnp.maximum(a, 0.0) + jnp.log(1.0 + jnp.exp(-jnp.abs(a)))
    g = -(jnp.exp(alog_ref[...]) * gmask_ref[...]) * softplus
    beta_out[...] = _sigmoid(m)
    parts = _split3(g)
    gcf_out[...] = sum(jnp.dot(tp_ref[...], p, preferred_element_type=F32) for p in parts)
    gcb_out[...] = sum(jnp.dot(ts_ref[...], p, preferred_element_type=F32) for p in parts)


def gdn_prepare(main, misc, b, length, conv_w, a_log, dt_bias):
    n = b * length
    w3 = 3 * GDN_W
    tm = min(256, length)
    nt = length // tm
    lane0 = MLA_ROPE
    vec = lambda v: jnp.zeros((1, MISC_W), F32).at[0, lane0:lane0 + 2 * GDN_HEADS].set(v.reshape(-1))
    alog, dtb = vec(a_log), vec(dt_bias)
    gmask = vec(jnp.ones((2 * GDN_HEADS,), F32))
    r = np.arange(tm)
    same = (r[:, None] // GDN_CHUNK) == (r[None, :] // GDN_CHUNK)
    tpre = jnp.asarray(same & (r[None, :] <= r[:, None]), BF16)
    tsuf = jnp.asarray(same & (r[None, :] >= r[:, None]), BF16)
    const = lambda a: pl.BlockSpec(a.shape, lambda bi, i: (0,) * a.ndim)
    row = lambda width: pl.BlockSpec((tm, width), lambda bi, i: (bi * nt + i, 0))
    cw = conv_w.astype(F32)
    return pl.pallas_call(
        _gdn_prep_body,
        grid=(b, nt),
        in_specs=_halo_specs(tm, w3, OFF_GDN // w3, nt, n)
                 + [row(MISC_W), const(cw), const(alog), const(dtb), const(gmask), const(tpre), const(tsuf)],
        out_specs=[row(GDN_W), row(GDN_W), row(GDN_W), row(MISC_W), row(MISC_W), row(MISC_W)],
        out_shape=[jax.ShapeDtypeStruct((n, GDN_W), BF16)] * 3 + [jax.ShapeDtypeStruct((n, MISC_W), F32)] * 3,
        compiler_params=_cp(("parallel", "parallel")),
        name="gdn_prep",
    )(main, main, main, misc, cw, alog, dtb, gmask, tpre, tsuf)


def _gdn_chunk_body(qf_ref, kf_ref, vf_ref, qb_ref, kb_ref, vb_ref, gcf_ref, gcb_ref, bcf_ref, bcb_ref,
                    grf_ref, grb_ref, s0_ref, *rest, nc, with_out, bpb):
    if with_out:
        of_ref, ob_ref, sfin_ref, s_ref = rest
    else:
        sfin_ref, s_ref = rest
        of_ref = ob_ref = None
    c = pl.program_id(1)
    nst = 2 * GDN_HEADS

    @pl.when(c == 0)
    def _():
        s_ref[...] = s0_ref[...].reshape(s_ref.shape)

    ch = GDN_CHUNK
    hd = GDN_HEAD_DIM
    ii = lax.broadcasted_iota(jnp.int32, (ch, ch), 0)
    jj = lax.broadcasted_iota(jnp.int32, (ch, ch), 1)
    nt_dims = (((1,), (1,)), ((), ()))
    tn_dims = (((0,), (0,)), ((), ()))
    bdot = lambda a, b_: jnp.dot(a.astype(BF16), b_.astype(BF16), preferred_element_type=F32)
    eye = jnp.where(ii == jj, 1.0, 0.0)
    pair_masks = [((ii >> (l + 1)) == (jj >> (l + 1))) & ((ii >> l) != (jj >> l))
                  for l in range(int(math.log2(ch)))]
    dirs = ((qf_ref, kf_ref, vf_ref, gcf_ref, bcf_ref, grf_ref, of_ref, ii >= jj, ii > jj, ch - 1),
            (qb_ref, kb_ref, vb_ref, gcb_ref, bcb_ref, grb_ref, ob_ref, ii <= jj, ii < jj, 0))
    chains = []
    for bb in range(bpb):
        for d, (q_ref, k_ref, v_ref, gc_ref, bc_ref, gr_ref, o_ref, incl, strict, last_row) in enumerate(dirs):
            for h in range(GDN_HEADS):
                j = d * GDN_HEADS + h
                cols = slice(h * hd, (h + 1) * hd)
                cn = dict(bb=bb, j=j, cols=cols, o_ref=o_ref, incl=incl, strict=strict)
                cn['q'], cn['k'], cn['v'] = q_ref[bb, :, cols], k_ref[bb, :, cols], v_ref[bb, :, cols]
                cn['gc'] = gc_ref[bb, :, j:j + 1]
                cn['gr'] = gr_ref[bb, 0, j:j + 1, :]
                cn['beta'] = bc_ref[bb, :, j:j + 1]
                cn['g_last'] = gc_ref[bb, last_row:last_row + 1, j:j + 1]
                chains.append(cn)
    for cn in chains:
        incl = cn['incl']
        cn['decay'] = jnp.where(incl, jnp.exp(jnp.where(incl, cn['gc'] - cn['gr'], 0.0)), 0.0)
        cn['kf'] = cn['k'].astype(F32)
        cn['kbeta'] = cn['kf'] * cn['beta']
    for cn in chains:
        kk = lax.dot_general(cn['kbeta'].astype(BF16), cn['k'], nt_dims, preferred_element_type=F32)
        cn['a'] = jnp.where(cn['strict'], kk * cn['decay'], 0.0)
    for cn in chains:
        cn['t'] = eye - jnp.where(pair_masks[0], cn['a'], 0.0)
    for pm in pair_masks[1:]:
        for cn in chains:
            cn['tmp'] = bdot(cn['t'], jnp.where(pm, cn['a'], 0.0))
        for cn in chains:
            cn['t'] = cn['t'] - bdot(cn['tmp'], cn['t'])
    for cn in chains:
        rhs = jnp.concatenate([cn['v'].astype(F32) * cn['beta'], cn['kbeta'] * jnp.exp(cn['gc'])], axis=1)
        cn['x'] = bdot(cn['t'], rhs)
    for cn in chains:
        cn['s'] = s_ref[cn['bb'] * nst + cn['j']]
        cn['v_new'] = cn['x'][:, :hd] - bdot(cn['x'][:, hd:], cn['s'])
    if with_out:
        for cn in chains:
            qk = lax.dot_general(cn['q'], cn['k'], nt_dims, preferred_element_type=F32)
            qk = jnp.where(cn['incl'], qk * cn['decay'], 0.0)
            o = bdot(cn['q'].astype(F32) * jnp.exp(cn['gc']), cn['s']) + bdot(qk, cn['v_new'])
            cn['o_ref'][cn['bb'], :, cn['cols']] = o
    for cn in chains:
        kdec = cn['kf'] * jnp.exp(cn['g_last'] - cn['gc'])
        s_ref[cn['bb'] * nst + cn['j']] = cn['s'] * jnp.exp(cn['g_last']) + lax.dot_general(
            kdec.astype(BF16), cn['v_new'].astype(BF16), tn_dims, preferred_element_type=F32)

    @pl.when(c == nc - 1)
    def _():
        sfin_ref[...] = s_ref[...].reshape(sfin_ref.shape)


def gdn_scan(q, k, v, gcol, bcol, grow, s0, b, length, with_out):
    n = b * length
    ch = GDN_CHUNK
    nc = length // ch
    nst = 2 * GDN_HEADS
    bpb = min(GDN_BATCHES_PER_STEP, b)
    fwd = lambda bg, c: (bg, c, 0)
    bwd = lambda bg, c: (bg, nc - 1 - c, 0)
    fwd4 = lambda bg, c: (bg, c, 0, 0)
    bwd4 = lambda bg, c: (bg, nc - 1 - c, 0, 0)
    wide = lambda f: pl.BlockSpec((bpb, ch, GDN_W), f)
    narrow = lambda f: pl.BlockSpec((bpb, ch, nst), f)
    rows = lambda f: pl.BlockSpec((bpb, 1, nst, ch), f)
    state = pl.BlockSpec((bpb, nst, GDN_HEAD_DIM, GDN_HEAD_DIM), lambda bg, c: (bg, 0, 0, 0))
    out_specs = [state]
    out_shape = [jax.ShapeDtypeStruct((b, nst, GDN_HEAD_DIM, GDN_HEAD_DIM), F32)]
    if with_out:
        out_specs = [wide(fwd), wide(bwd)] + out_specs
        out_shape = [jax.ShapeDtypeStruct((b, length, GDN_W), F32)] * 2 + out_shape
    body = functools.partial(_gdn_chunk_body, nc=nc, with_out=with_out, bpb=bpb)
    q3, k3, v3 = (t.reshape(b, length, GDN_W) for t in (q, k, v))
    gcol3, bcol3 = gcol.reshape(b, length, nst), bcol.reshape(b, length, nst)
    grow4 = grow.reshape(b, nc, nst, ch)
    outs = pl.pallas_call(
        body,
        grid=(b // bpb, nc),
        in_specs=[wide(fwd), wide(fwd), wide(fwd), wide(bwd), wide(bwd), wide(bwd),
                  narrow(fwd), narrow(bwd), narrow(fwd), narrow(bwd), rows(fwd4), rows(bwd4), state],
        out_specs=out_specs,
        out_shape=out_shape,
        scratch_shapes=[pltpu.VMEM((bpb * nst, GDN_HEAD_DIM, GDN_HEAD_DIM), F32)],
        compiler_params=_cp(("parallel", "arbitrary")),
        name="gdn_scan",
    )(q3, k3, v3, q3, k3, v3, gcol3, gcol3, bcol3, bcol3, grow4, grow4, s0)
    if with_out:
        return outs[0].reshape(n, GDN_W), outs[1].reshape(n, GDN_W), outs[2]
    return outs


def _gdn_out_body(of_ref, ob_ref, z_ref, nw_ref, y_ref):
    o = of_ref[...] + ob_ref[...]
    z = z_ref[...].astype(F32)
    hd = GDN_HEAD_DIM
    for h in range(GDN_HEADS):
        cols = slice(h * hd, (h + 1) * hd)
        oh = o[:, cols]
        yh = oh * lax.rsqrt(jnp.mean(oh * oh, axis=-1, keepdims=True) + NORM_EPS) * nw_ref[...]
        zh = z[:, cols]
        y_ref[:, cols] = (yh * (zh * _sigmoid(zh))).astype(BF16)


def gdn_output_gate(o_f, o_b, main, norm_w):
    n = o_f.shape[0]
    tm = min(512, n)
    nw = norm_w.reshape(1, GDN_HEAD_DIM).astype(F32)
    return pl.pallas_call(
        _gdn_out_body,
        grid=(n // tm,),
        in_specs=[pl.BlockSpec((tm, GDN_W), lambda i: (i, 0)),
                  pl.BlockSpec((tm, GDN_W), lambda i: (i, 0)),
                  pl.BlockSpec((tm, GDN_W), lambda i: (i, (OFF_GDN + 3 * GDN_W) // GDN_W)),
                  pl.BlockSpec((1, GDN_HEAD_DIM), lambda i: (0, 0))],
        out_specs=pl.BlockSpec((tm, GDN_W), lambda i: (i, 0)),
        out_shape=jax.ShapeDtypeStruct((n, GDN_W), BF16),
        compiler_params=_cp(("parallel",)),
        name="gdn_out",
    )(o_f, o_b, main, nw)


def gdn_branch(main_l, misc_l, main_c, misc_c, b, seq, ctx_len, conv_w, a_log, dt_bias, norm_w, ctx_out):
    nst = 2 * GDN_HEADS
    lane0 = MLA_ROPE

    def gates(gcf, gcb, beta, length):
        gcol = jnp.concatenate([gcf[:, lane0:lane0 + GDN_HEADS], gcb[:, lane0 + GDN_HEADS:lane0 + nst]], axis=1)
        bcol = beta[:, lane0 + nst:lane0 + 2 * nst]
        grow = jnp.transpose(gcol.reshape(-1, GDN_CHUNK, nst), (0, 2, 1))
        return gcol, bcol, grow

    qc, kc, vc, gcf, gcb, beta = gdn_prepare(main_c, misc_c, b, ctx_len, conv_w, a_log, dt_bias)
    gcol_c, bcol_c, grow_c = gates(gcf, gcb, beta, ctx_len)
    ql, kl, vl, gcf, gcb, beta = gdn_prepare(main_l, misc_l, b, seq, conv_w, a_log, dt_bias)
    gcol_l, bcol_l, grow_l = gates(gcf, gcb, beta, seq)
    s0 = jnp.zeros((b, nst, GDN_HEAD_DIM, GDN_HEAD_DIM), F32)
    outs_c = gdn_scan(qc, kc, vc, gcol_c, bcol_c, grow_c, s0, b, ctx_len, ctx_out)
    s_ctx = outs_c[-1]
    of_l, ob_l, _ = gdn_scan(ql, kl, vl, gcol_l, bcol_l, grow_l, s_ctx, b, seq, True)
    out_l = gdn_output_gate(of_l, ob_l, main_l, norm_w)
    out_c = gdn_output_gate(outs_c[0], outs_c[1], main_c, norm_w) if ctx_out else None
    return out_l, out_c


def _hy_conv_body(x_ref, xp_ref, xn_ref, cw_ref, v_out, x1_out, x2_out):
    prev, nxt = _halo_rows(xp_ref, xn_ref)
    y = _conv3(x_ref[...].astype(F32), prev, nxt, cw_ref)
    w = HY_WIDTH
    v_out[...] = y[:, :w].astype(BF16)
    x1_out[...] = y[:, w:2 * w].astype(BF16)
    x2_out[...] = y[:, 2 * w:3 * w].astype(BF16)


def hyena_short_conv(main, b, length, conv_w):
    n = b * length
    w3 = (HY_ORDER + 1) * HY_WIDTH
    tm = min(256, length)
    nt = length // tm
    cw = conv_w.astype(F32)
    row = pl.BlockSpec((tm, HY_WIDTH), lambda bi, i: (bi * nt + i, 0))
    return pl.pallas_call(
        _hy_conv_body,
        grid=(b, nt),
        in_specs=_halo_specs(tm, w3, OFF_HY // w3, nt, n) + [pl.BlockSpec(cw.shape, lambda bi, i: (0, 0))],
        out_specs=[row, row, row],
        out_shape=[jax.ShapeDtypeStruct((n, HY_WIDTH), BF16)] * 3,
        compiler_params=_cp(("parallel", "parallel")),
        name="hyena_conv",
    )(main, main, main, cw)


def _dft_consts(length):
    n = 2 * length
    n2 = 64 if length >= 2048 else 16
    n1 = n // n2
    nk1 = n1 // 2 + 8
    k1 = np.arange(nk1)
    t1 = np.arange(n1 // 2)
    ang1 = 2.0 * np.pi * np.outer(k1, t1) / n1
    f_first = np.concatenate([np.cos(ang1), -np.sin(ang1)], axis=0)
    t2 = np.arange(n2)
    ang2 = 2.0 * np.pi * np.outer(t2, t2) / n2
    c2, s2 = np.cos(ang2), np.sin(ang2)
    g_fwd = np.block([[c2, s2], [-s2, c2]])
    g_inv = g_fwd.T
    angt = 2.0 * np.pi * np.outer(k1, t2) / n
    tw_r, tw_i = np.cos(angt)[:, :, None], -np.sin(angt)[:, :, None]
    tt = np.arange(n1 // 4, 3 * n1 // 4)
    ang3 = 2.0 * np.pi * np.outer(tt, k1) / n1
    fold = np.where((k1 == 0) | (k1 == n1 // 2), 1.0, np.where(k1 < n1 // 2, 2.0, 0.0))[None, :]
    f_last = np.concatenate([np.cos(ang3) * fold, -np.sin(ang3) * fold], axis=1) / n
    bf = lambda a: jnp.asarray(a, BF16)
    return dict(n1=n1, nk1=nk1, n2=n2, f_first=bf(f_first), g_fwd=bf(g_fwd), g_inv=bf(g_inv),
                tw_r=jnp.asarray(tw_r, F32), tw_i=jnp.asarray(tw_i, F32), f_last=bf(f_last))


def _hy_first_body(f_ref, z_ref, a_ref):
    a_ref[0] = jnp.dot(f_ref[...], z_ref[0], preferred_element_type=F32).astype(BF16)


def hyena_dft_first(zv, consts):
    b, half, cols = zv.shape
    n1 = consts['nk1']
    tn = min(4096, cols)
    f = consts['f_first']
    return pl.pallas_call(
        _hy_first_body,
        grid=(b, cols // tn),
        in_specs=[pl.BlockSpec(f.shape, lambda bi, j: (0, 0)),
                  pl.BlockSpec((1, half, tn), lambda bi, j: (bi, 0, j))],
        out_specs=pl.BlockSpec((1, 2 * n1, tn), lambda bi, j: (bi, 0, j)),
        out_shape=jax.ShapeDtypeStruct((b, 2 * n1, cols), BF16),
        compiler_params=_cp(("parallel", "parallel")),
        name="hyena_dft_first",
    )(f, zv)


def _hy_mid_body(a_ref, twr_ref, twi_ref, gf_ref, *rest, kt, spectrum_only):
    if spectrum_only:
        (o_ref,) = rest
    else:
        gi_ref, h_ref, o_ref = rest
    n2 = gf_ref.shape[0] // 2

    ks = range(kt)
    tw = [(twr_ref[i], twi_ref[i]) for i in ks]
    a = [(a_ref[0, 0, i].astype(F32), a_ref[0, 1, i].astype(F32)) for i in ks]
    b = [jnp.concatenate([ar * twr - ai * twi, ar * twi + ai * twr], axis=0).astype(BF16)
         for (ar, ai), (twr, twi) in zip(a, tw)]
    z = [jnp.dot(gf_ref[...], bb, preferred_element_type=F32) for bb in b]
    if spectrum_only:
        for i in ks:
            o_ref[0, 0, i] = z[i][:n2]
            o_ref[0, 1, i] = z[i][n2:]
        return
    y = []
    for i in ks:
        zr, zi = z[i][:n2], z[i][n2:]
        hr, hi = h_ref[0, i], h_ref[1, i]
        y.append(jnp.concatenate([zr * hr - zi * hi, zr * hi + zi * hr], axis=0).astype(BF16))
    w = [jnp.dot(gi_ref[...], yy, preferred_element_type=F32) for yy in y]
    for i in ks:
        wr, wi = w[i][:n2], w[i][n2:]
        twr, twi = tw[i]
        o_ref[0, 0, i] = (wr * twr + wi * twi).astype(BF16)
        o_ref[0, 1, i] = (wi * twr - wr * twi).astype(BF16)


def hyena_dft_mid(a5, consts, spectrum=None):
    b, _, n1, n2, c = a5.shape
    kt = 8
    only = spectrum is None
    blk = pl.BlockSpec((1, 2, kt, n2, c), lambda bi, j: (bi, 0, j, 0, 0))
    tw = pl.BlockSpec((kt, n2, 1), lambda bi, j: (j, 0, 0))
    g = pl.BlockSpec((2 * n2, 2 * n2), lambda bi, j: (0, 0))
    in_specs = [blk, tw, tw, g]
    args = [a5, consts['tw_r'], consts['tw_i'], consts['g_fwd']]
    if not only:
        in_specs += [g, pl.BlockSpec((2, kt, n2, c), lambda bi, j: (0, j, 0, 0))]
        args += [consts['g_inv'], spectrum]
    body = functools.partial(_hy_mid_body, kt=kt, spectrum_only=only)
    return pl.pallas_call(
        body,
        grid=(b, n1 // kt),
        in_specs=in_specs,
        out_specs=blk,
        out_shape=jax.ShapeDtypeStruct(a5.shape, F32 if only else BF16),
        compiler_params=_cp(("parallel", "parallel")),
        name="hyena_dft_mid",
    )(*args)


def _hy_last_body(f_ref, b_ref, z_ref, x_ref, bias_ref, o_ref):
    y = jnp.dot(f_ref[...], b_ref[0], preferred_element_type=F32)
    z = z_ref[0].astype(F32)
    o_ref[0] = (x_ref[0].astype(F32) * (y + bias_ref[...] * z)).astype(BF16)


def hyena_dft_last(bv, zv, xv, bias_row, consts):
    b, rows2, cols = bv.shape
    half = consts['n1'] // 2
    tn = min(4096, cols)
    f = consts['f_last']
    sig = pl.BlockSpec((1, half, tn), lambda bi, j: (bi, 0, j))
    return pl.pallas_call(
        _hy_last_body,
        grid=(b, cols // tn),
        in_specs=[pl.BlockSpec(f.shape, lambda bi, j: (0, 0)),
                  pl.BlockSpec((1, rows2, tn), lambda bi, j: (bi, 0, j)),
                  sig, sig,
                  pl.BlockSpec((1, tn), lambda bi, j: (0, j))],
        out_specs=sig,
        out_shape=jax.ShapeDtypeStruct((b, half, cols), BF16),
        compiler_params=_cp(("parallel", "parallel")),
        name="hyena_dft_last",
    )(f, bv, zv, xv, bias_row)


def hyena_branch(main, b, length, conv_w, filt, bias):
    consts = _dft_consts(length)
    n1, nk1, n2 = consts['n1'], consts['nk1'], consts['n2']
    c = HY_WIDTH
    cols = n2 * c
    view = lambda t: t.reshape(b, n1 // 2, cols)
    v, x1, x2 = (view(t) for t in hyena_short_conv(main, b, length, conv_w))
    hv = jnp.transpose(filt, (1, 0, 2)).astype(BF16).reshape(HY_ORDER, n1 // 2, cols)
    h_first = hyena_dft_first(hv, consts).reshape(HY_ORDER, 2, nk1, n2, c)
    spectra = hyena_dft_mid(h_first, consts)
    z = v
    for o, gate in enumerate((x1, x2)):
        a5 = hyena_dft_first(z, consts).reshape(b, 2, nk1, n2, c)
        bm = hyena_dft_mid(a5, consts, spectra[o]).reshape(b, 2 * nk1, cols)
        bias_row = jnp.tile(bias[o].astype(F32), n2).reshape(1, cols)
        z = hyena_dft_last(bm, z, gate, bias_row, consts)
    return z.reshape(b * length, c)


def _ada_body(c_ref, w_ref, b_ref, o_ref):
    cv = c_ref[...]
    s = cv * _sigmoid(cv)
    o_ref[0] = jnp.dot(s, w_ref[0], precision=lax.Precision.HIGHEST, preferred_element_type=F32) + b_ref[0]


def ada_modulation(c, c_ctx, w_ada, b_ada):
    depth, d, d6 = w_ada.shape
    c8 = jnp.zeros((MOD_ROWS, d), F32).at[:c.shape[0]].set(c).at[CTX_MOD_ROW].set(c_ctx)
    tn = 512
    out = pl.pallas_call(
        _ada_body,
        grid=(depth, d6 // tn),
        in_specs=[pl.BlockSpec((MOD_ROWS, d), lambda l, j: (0, 0)),
                  pl.BlockSpec((1, d, tn), lambda l, j: (l, 0, j)),
                  pl.BlockSpec((1, 1, tn), lambda l, j: (l, 0, j))],
        out_specs=pl.BlockSpec((1, MOD_ROWS, tn), lambda l, j: (l, 0, j)),
        out_shape=jax.ShapeDtypeStruct((depth, MOD_ROWS, d6), F32),
        compiler_params=_cp(("parallel", "parallel")),
        name="ada_mod",
    )(c8, w_ada, b_ada.reshape(depth, 1, d6))
    return out.reshape(depth, MOD_ROWS, 6, d)


def _row_tile(cap, n_rows, per_batch):
    return min(cap, n_rows if per_batch is None else per_batch)


def _mod_row_fn(n_rows, tm, per_batch):
    if per_batch is None:
        return lambda i: CTX_MOD_ROW
    tiles = per_batch // tm
    return lambda i: i // tiles


def _normmod_body(x_ref, nw_ref, sh_ref, sc_ref, o_ref):
    xf = x_ref[...]
    y = xf * lax.rsqrt(jnp.mean(xf * xf, axis=-1, keepdims=True) + NORM_EPS) * nw_ref[...]
    o_ref[...] = (y * (1.0 + sc_ref[0]) + sh_ref[0]).astype(o_ref.dtype)


def norm_modulate(x, nw, shift, scale, per_batch, out_dtype=BF16):
    n, d = x.shape
    tm = _row_tile(512, n, per_batch)
    rf = _mod_row_fn(n, tm, per_batch)
    return pl.pallas_call(
        _normmod_body,
        grid=(n // tm,),
        in_specs=[pl.BlockSpec((tm, d), lambda i: (i, 0)),
                  pl.BlockSpec((1, d), lambda i: (0, 0)),
                  pl.BlockSpec((1, 1, d), lambda i: (rf(i), 0, 0)),
                  pl.BlockSpec((1, 1, d), lambda i: (rf(i), 0, 0))],
        out_specs=pl.BlockSpec((tm, d), lambda i: (i, 0)),
        out_shape=jax.ShapeDtypeStruct((n, d), out_dtype),
        compiler_params=_cp(("parallel",)),
        name="norm_mod",
    )(x, nw.reshape(1, d), shift.reshape(MOD_ROWS, 1, d), scale.reshape(MOD_ROWS, 1, d))


def _mm_body(a_ref, w_ref, o_ref):
    o_ref[...] = jnp.dot(a_ref[...], w_ref[...], preferred_element_type=F32).astype(o_ref.dtype)


def matmul(a, w, out_dtype, tn):
    n, k = a.shape
    m = w.shape[1]
    tm = min(2048, n)
    return pl.pallas_call(
        _mm_body,
        grid=(n // tm, m // tn),
        in_specs=[pl.BlockSpec((tm, k), lambda i, j: (i, 0)),
                  pl.BlockSpec((k, tn), lambda i, j: (0, j))],
        out_specs=pl.BlockSpec((tm, tn), lambda i, j: (i, j)),
        out_shape=jax.ShapeDtypeStruct((n, m), out_dtype),
        compiler_params=_cp(("parallel", "parallel")),
        name="proj",
    )(a, w)


def _norm_proj_body(x_ref, nw_ref, sh_ref, sc_ref, w_ref, o_ref, h_ref):
    @pl.when(pl.program_id(1) == 0)
    def _():
        xf = x_ref[...]
        y = xf * lax.rsqrt(jnp.mean(xf * xf, axis=-1, keepdims=True) + NORM_EPS) * nw_ref[...]
        h_ref[...] = (y * (1.0 + sc_ref[0]) + sh_ref[0]).astype(h_ref.dtype)

    o_ref[...] = jnp.dot(h_ref[...], w_ref[...], preferred_element_type=F32).astype(o_ref.dtype)


def norm_modulate_project(x, nw, shift, scale, per_batch, w, tn):
    n, d = x.shape
    m = w.shape[1]
    tm = _row_tile(1024, n, per_batch)
    rf = _mod_row_fn(n, tm, per_batch)
    return pl.pallas_call(
        _norm_proj_body,
        grid=(n // tm, m // tn),
        in_specs=[pl.BlockSpec((tm, d), lambda i, j: (i, 0)),
                  pl.BlockSpec((1, d), lambda i, j: (0, 0)),
                  pl.BlockSpec((1, 1, d), lambda i, j: (rf(i), 0, 0)),
                  pl.BlockSpec((1, 1, d), lambda i, j: (rf(i), 0, 0)),
                  pl.BlockSpec((d, tn), lambda i, j: (0, j))],
        out_specs=[pl.BlockSpec((tm, tn), lambda i, j: (i, j)),
                   pl.BlockSpec((tm, d), lambda i, j: (i, 0))],
        out_shape=[jax.ShapeDtypeStruct((n, m), BF16), jax.ShapeDtypeStruct((n, d), BF16)],
        compiler_params=_cp(("parallel", "arbitrary")),
        name="norm_proj",
    )(x, nw.reshape(1, d), shift.reshape(MOD_ROWS, 1, d), scale.reshape(MOD_ROWS, 1, d), w)


def _mm_res_body(a_ref, w_ref, x_ref, g_ref, o_ref):
    y = jnp.dot(a_ref[...], w_ref[...], preferred_element_type=F32)
    o_ref[...] = x_ref[...] + g_ref[0] * y


def matmul_gated_residual(a, w, x, gate, per_batch):
    n, k = a.shape
    d = w.shape[1]
    tm = _row_tile(1024, n, per_batch)
    tn = min(1024, d)
    rf = _mod_row_fn(n, tm, per_batch)
    return pl.pallas_call(
        _mm_res_body,
        grid=(n // tm, d // tn),
        in_specs=[pl.BlockSpec((tm, k), lambda i, j: (i, 0)),
                  pl.BlockSpec((k, tn), lambda i, j: (0, j)),
                  pl.BlockSpec((tm, tn), lambda i, j: (i, j)),
                  pl.BlockSpec((1, 1, tn), lambda i, j: (rf(i), 0, j))],
        out_specs=pl.BlockSpec((tm, tn), lambda i, j: (i, j)),
        out_shape=jax.ShapeDtypeStruct((n, d), F32),
        compiler_params=_cp(("parallel", "parallel")),
        name="out_proj_residual",
    )(a, w, x, gate.reshape(MOD_ROWS, 1, d))


def _merge_body(h_ref, *refs):
    b_refs, wg_refs = refs[:N_BRANCH], refs[N_BRANCH:2 * N_BRANCH]
    wb_ref, o_ref = refs[2 * N_BRANCH:]
    h = h_ref[...]
    acc = None
    for n in range(N_BRANCH):
        gate = jnp.dot(h, wg_refs[n][...], preferred_element_type=F32)
        proj = jnp.dot(b_refs[n][...], wb_ref[n], preferred_element_type=F32)
        term = _sigmoid(gate) * proj
        acc = term if acc is None else acc + term
    o_ref[...] = acc.astype(o_ref.dtype)


def merge_branches_gated(h, branches, w_gate, w_branch):
    n, d = h.shape
    bw = branches[0].shape[1]
    tm = min(1024, n)
    tn = min(512, d)
    nj = d // tn
    gate_spec = lambda b: pl.BlockSpec((d, tn), lambda i, j: (0, b * nj + j))
    return pl.pallas_call(
        _merge_body,
        grid=(n // tm, nj),
        in_specs=[pl.BlockSpec((tm, d), lambda i, j: (i, 0))]
                 + [pl.BlockSpec((tm, bw), lambda i, j: (i, 0))] * N_BRANCH
                 + [gate_spec(b) for b in range(N_BRANCH)]
                 + [pl.BlockSpec((N_BRANCH, bw, tn), lambda i, j: (0, 0, j))],
        out_specs=pl.BlockSpec((tm, tn), lambda i, j: (i, j)),
        out_shape=jax.ShapeDtypeStruct((n, d), BF16),
        compiler_params=_cp(("parallel", "parallel")),
        name="gate_merge",
    )(h, *branches, *([w_gate] * N_BRANCH), w_branch)


def _mla_prep_body(cq_ref, ckv_ref, misc_ref, qnw_ref, kvnw_ref, wqa_ref, wqb_ref, wk_ref, wv_ref,
                   ska_ref, skb_ref, cq_tab, sq_tab, q_out, k_out, v_out):
    def norm(v, w_ref):
        vf = v.astype(F32)
        return (vf * lax.rsqrt(jnp.mean(vf * vf, axis=-1, keepdims=True) + NORM_EPS) * w_ref[...]).astype(BF16)

    xq = norm(cq_ref[...], qnw_ref)
    xkv = norm(ckv_ref[...], kvnw_ref)
    cos, sin = cq_tab[...], sq_tab[...]
    misc = misc_ref[...].astype(BF16)
    kr = (jnp.dot(misc, ska_ref[...], preferred_element_type=F32) * cos
          + jnp.dot(misc, skb_ref[...], preferred_element_type=F32) * sin)
    for h in range(MLA_HEADS):
        qa = jnp.dot(xq, wqa_ref[h], preferred_element_type=F32)
        qb = jnp.dot(xq, wqb_ref[h], preferred_element_type=F32)
        q_out[0, h] = (qa * cos + qb * sin).astype(BF16)
        k_out[0, h] = (jnp.dot(xkv, wk_ref[h], preferred_element_type=F32) + kr).astype(BF16)
        v_out[0, h] = jnp.dot(xkv, wv_ref[h], preferred_element_type=F32).astype(BF16)


def _mla_weights(q_norm_w, kv_norm_w, w_uq, w_ukv):
    dk = MLA_NOPE + MLA_ROPE
    half = MLA_ROPE // 2
    scale = dk ** -0.5
    wq = jnp.transpose(w_uq, (1, 0, 2)) * scale
    nope0 = jnp.zeros(wq.shape[:2] + (MLA_NOPE,), F32)
    wq_rot = jnp.concatenate([nope0, -wq[..., MLA_NOPE + half:], wq[..., MLA_NOPE:MLA_NOPE + half]], axis=-1)
    wkv = jnp.transpose(w_ukv, (1, 0, 2))
    wk = jnp.concatenate([wkv[..., :MLA_NOPE], jnp.zeros(wkv.shape[:2] + (MLA_ROPE,), F32)], axis=-1)
    wv = wkv[..., MLA_NOPE:]
    eye = jnp.eye(MLA_ROPE, dtype=F32)
    rot = jnp.concatenate([-eye[:, half:], eye[:, :half]], axis=-1)
    pad_r = MISC_W - MLA_ROPE
    ska = jnp.pad(eye, ((0, pad_r), (MLA_NOPE, 0)))
    skb = jnp.pad(rot, ((0, pad_r), (MLA_NOPE, 0)))
    return tuple(t.astype(BF16) for t in (wq, wq_rot, wk, wv, ska, skb))


def _mla_tables(rope, length):
    dk = MLA_NOPE + MLA_ROPE
    if rope is None:
        return jnp.ones((length, dk), F32), jnp.zeros((length, dk), F32)
    cos, sin = rope
    ones = jnp.ones((length, MLA_NOPE), F32)
    return (jnp.concatenate([ones, cos, cos], axis=-1),
            jnp.concatenate([0.0 * ones, sin, sin], axis=-1))


def mla_prepare(main, misc, b, length, weights, q_norm_w, kv_norm_w, tables):
    wq, wq_rot, wk, wv, ska, skb = weights
    cos, sin = tables
    dk = MLA_NOPE + MLA_ROPE
    tm = min(512, length)
    nt = length // tm
    full = lambda a: pl.BlockSpec(a.shape, lambda bi, i: (0,) * a.ndim)
    qnw = q_norm_w.reshape(1, -1)
    kvnw = kv_norm_w.reshape(1, -1)
    outs = pl.pallas_call(
        _mla_prep_body,
        grid=(b, nt),
        in_specs=[pl.BlockSpec((tm, MLA_Q_LORA), lambda bi, i: (bi * nt + i, OFF_CQ // MLA_Q_LORA)),
                  pl.BlockSpec((tm, MLA_KV_LORA), lambda bi, i: (bi * nt + i, OFF_CKV // MLA_KV_LORA)),
                  pl.BlockSpec((tm, MISC_W), lambda bi, i: (bi * nt + i, 0)),
                  full(qnw), full(kvnw), full(wq), full(wq_rot), full(wk), full(wv), full(ska), full(skb),
                  pl.BlockSpec((tm, dk), lambda bi, i: (i, 0)),
                  pl.BlockSpec((tm, dk), lambda bi, i: (i, 0))],
        out_specs=[pl.BlockSpec((1, MLA_HEADS, tm, dk), lambda bi, i: (bi, 0, i, 0)),
                   pl.BlockSpec((1, MLA_HEADS, tm, dk), lambda bi, i: (bi, 0, i, 0)),
                   pl.BlockSpec((1, MLA_HEADS, tm, MLA_V), lambda bi, i: (bi, 0, i, 0))],
        out_shape=[jax.ShapeDtypeStruct((b, MLA_HEADS, length, dk), BF16),
                   jax.ShapeDtypeStruct((b, MLA_HEADS, length, dk), BF16),
                   jax.ShapeDtypeStruct((b, MLA_HEADS, length, MLA_V), BF16)],
        compiler_params=_cp(("parallel", "parallel")),
        name="mla_prep",
    )(main, main, misc, qnw, kvnw, wq, wq_rot, wk, wv, ska, skb, cos, sin)
    return outs


def _gqa_prep_body(q_ref, k_ref, v_ref, qnw_ref, knw_ref, gsum_ref, rot_ref, cos_ref, sin_ref,
                   q_out, k_out, v_out):
    cos, sin = cos_ref[...], sin_ref[...]

    def prep(v, nw, width):
        vf = v.astype(F32)
        sq = vf * vf
        hi = sq.astype(BF16)
        lo = (sq - hi.astype(F32)).astype(BF16)
        g = gsum_ref[:width, :width]
        ss = jnp.dot(hi, g, preferred_element_type=F32) + jnp.dot(lo, g, preferred_element_type=F32)
        xn = vf * lax.rsqrt(ss * (1.0 / GQA_HEAD_DIM) + NORM_EPS) * nw
        xr = jnp.dot(xn.astype(BF16), rot_ref[:width, :width], preferred_element_type=F32)
        return xn * cos[:, :width] + xr * sin[:, :width]

    qf = prep(q_ref[...], qnw_ref[...], GQA_HEADS * GQA_HEAD_DIM) * GQA_HEAD_DIM ** -0.5
    kf = prep(k_ref[...], knw_ref[...], LANES)
    q_out[...] = qf.astype(BF16)
    k_out[...] = kf.astype(BF16)
    v_out[...] = v_ref[...]


def gqa_prepare(main, b, length, q_norm_w, k_norm_w, rope):
    n = b * length
    qw = GQA_HEADS * GQA_HEAD_DIM
    kw = GQA_KV_HEADS * GQA_HEAD_DIM
    half = GQA_HEAD_DIM // 2
    if rope is None:
        cos = jnp.ones((length, qw), F32)
        sin = jnp.zeros((length, qw), F32)
    else:
        cos = jnp.tile(jnp.concatenate([rope[0], rope[0]], axis=-1), (1, GQA_HEADS))
        sin = jnp.tile(jnp.concatenate([rope[1], rope[1]], axis=-1), (1, GQA_HEADS))
    head = np.arange(qw) // GQA_HEAD_DIM
    gsum = jnp.asarray(head[:, None] == head[None, :], BF16)
    eye = np.eye(GQA_HEAD_DIM, dtype=np.float32)
    rot1 = np.concatenate([-eye[:, half:], eye[:, :half]], axis=-1)
    rot = jnp.asarray(np.kron(np.eye(GQA_HEADS, dtype=np.float32), rot1), BF16)
    tm = min(512, length)
    nt = length // tm
    full = lambda a: pl.BlockSpec(a.shape, lambda bi, i: (0,) * a.ndim)
    qnw = jnp.tile(q_norm_w, GQA_HEADS).reshape(1, qw)
    knw = jnp.tile(k_norm_w, GQA_KV_HEADS).reshape(1, kw)
    return pl.pallas_call(
        _gqa_prep_body,
        grid=(b, nt),
        in_specs=[pl.BlockSpec((tm, qw), lambda bi, i: (bi * nt + i, OFF_GQ // qw)),
                  pl.BlockSpec((tm, kw), lambda bi, i: (bi * nt + i, OFF_GK // kw)),
                  pl.BlockSpec((tm, kw), lambda bi, i: (bi * nt + i, OFF_GV // kw)),
                  full(qnw), full(knw), full(gsum), full(rot),
                  pl.BlockSpec((tm, qw), lambda bi, i: (i, 0)),
                  pl.BlockSpec((tm, qw), lambda bi, i: (i, 0))],
        out_specs=[pl.BlockSpec((tm, qw), lambda bi, i: (bi * nt + i, 0)),
                   pl.BlockSpec((tm, kw), lambda bi, i: (bi * nt + i, 0)),
                   pl.BlockSpec((tm, kw), lambda bi, i: (bi * nt + i, 0))],
        out_shape=[jax.ShapeDtypeStruct((n, qw), BF16),
                   jax.ShapeDtypeStruct((n, kw), BF16),
                   jax.ShapeDtypeStruct((n, kw), BF16)],
        compiler_params=_cp(("parallel", "parallel")),
        name="gqa_prep",
    )(main, main, main, qnw, knw, gsum, rot, cos, sin)


FLASH_CHAIN_ROWS = 256


def _flash_body(q_ref, k_ref, v_ref, kc_ref, vc_ref, o_ref, m_ref, l_ref, acc_ref, *, nk, has_ctx, chains):
    ki = pl.program_id(3)

    @pl.when(ki == 0)
    def _():
        m_ref[...] = jnp.full_like(m_ref, -jnp.inf)
        l_ref[...] = jnp.zeros_like(l_ref)
        acc_ref[...] = jnp.zeros_like(acc_ref)

    def step(k, vt):
        ss = [jnp.dot(k, q_ref[0, 0, gi, :, r0:r0 + rc], preferred_element_type=F32)
              for gi, r0, rc in chains]
        m_prev = [m_ref[ci] for ci in range(len(chains))]
        m_new = [jnp.maximum(mp, jnp.max(s, axis=0, keepdims=True)) for mp, s in zip(m_prev, ss)]
        ps = [jnp.exp(s - mn) for s, mn in zip(ss, m_new)]
        alphas = [jnp.exp(mp - mn) for mp, mn in zip(m_prev, m_new)]
        pv = [jnp.dot(vt, p.astype(BF16), preferred_element_type=F32) for p in ps]
        for ci in range(len(chains)):
            l_ref[ci] = alphas[ci] * l_ref[ci] + jnp.sum(ps[ci], axis=0, keepdims=True)
            acc_ref[ci] = alphas[ci] * acc_ref[ci] + pv[ci]
            m_ref[ci] = m_new[ci]

    @pl.when(ki < nk)
    def _():
        step(k_ref[0, 0], v_ref[0, 0])

    if has_ctx:
        @pl.when(ki == nk)
        def _():
            step(kc_ref[0, 0], vc_ref[0, 0])

    @pl.when(ki == nk - 1 + int(has_ctx))
    def _():
        for ci, (gi, r0, rc) in enumerate(chains):
            o_ref[0, 0, gi, :, r0:r0 + rc] = (acc_ref[ci] / l_ref[ci]).astype(o_ref.dtype)


def flash_attention(q, k, v, kc, vc, tq, tk):
    b, hkv, g, sq, dk = q.shape
    sk = k.shape[2]
    dv = v.shape[3]
    tq = min(tq, sq)
    tk = min(tk, sk)
    nk = sk // tk
    has_ctx = kc is not None
    if not has_ctx:
        kc, vc = k[:, :, :LANES], v[:, :, :LANES]
    skc = kc.shape[2]
    rc = min(FLASH_CHAIN_ROWS, tq)
    chains = tuple((gi, r0, rc) for gi in range(g) for r0 in range(0, tq, rc))
    nch = len(chains)
    body = functools.partial(_flash_body, nk=nk, has_ctx=has_ctx, chains=chains)
    qt = jnp.swapaxes(q, 3, 4)
    vt = jnp.swapaxes(v, 2, 3)
    vct = jnp.swapaxes(vc, 2, 3)
    out_t = pl.pallas_call(
        body,
        grid=(b, hkv, sq // tq, nk + int(has_ctx)),
        in_specs=[pl.BlockSpec((1, 1, g, dk, tq), lambda bi, h, qi, ki: (bi, h, 0, 0, qi)),
                  pl.BlockSpec((1, 1, tk, dk), lambda bi, h, qi, ki: (bi, h, jnp.minimum(ki, nk - 1), 0)),
                  pl.BlockSpec((1, 1, dv, tk), lambda bi, h, qi, ki: (bi, h, 0, jnp.minimum(ki, nk - 1))),
                  pl.BlockSpec((1, 1, skc, dk), lambda bi, h, qi, ki: (bi, h, 0, 0)),
                  pl.BlockSpec((1, 1, dv, skc), lambda bi, h, qi, ki: (bi, h, 0, 0))],
        out_specs=pl.BlockSpec((1, 1, g, dv, tq), lambda bi, h, qi, ki: (bi, h, 0, 0, qi)),
        out_shape=jax.ShapeDtypeStruct((b, hkv, g, dv, sq), BF16),
        scratch_shapes=[pltpu.VMEM((nch, 1, rc), F32), pltpu.VMEM((nch, 1, rc), F32),
                        pltpu.VMEM((nch, dv, rc), F32)],
        compiler_params=_cp(("parallel", "parallel", "parallel", "arbitrary")),
        name="flash_attention",
    )(qt, k, vt, kc, vct)
    return jnp.swapaxes(out_t, 3, 4)


def _router_body(x_ref, nw_ref, sh_ref, sc_ref, wrh_ref, wrl_ref, rb_ref, tri_ref,
                 h_ref, ri_ref, rw_ref, cnt_ref, carry_ref):
    i = pl.program_id(0)

    @pl.when(i == 0)
    def _():
        carry_ref[...] = jnp.zeros_like(carry_ref)

    xf = x_ref[...]
    y = xf * lax.rsqrt(jnp.mean(xf * xf, axis=-1, keepdims=True) + NORM_EPS) * nw_ref[...]
    h = y * (1.0 + sc_ref[0]) + sh_ref[0]
    h_ref[...] = h
    hi = h.astype(BF16)
    lo = (h - hi.astype(F32)).astype(BF16)
    nt = (((1,), (1,)), ((), ()))
    logits = (lax.dot_general(wrh_ref[...], hi, nt, preferred_element_type=F32)
              + lax.dot_general(wrh_ref[...], lo, nt, preferred_element_type=F32)
              + lax.dot_general(wrl_ref[...], hi, nt, preferred_element_type=F32))
    scores = _sigmoid(logits)
    sel = scores + rb_ref[...]
    s = [scores[e:e + 1] for e in range(N_EXPERTS)]
    v = [sel[e:e + 1] for e in range(N_EXPERTS)]
    epg = EXPERTS_PER_GROUP
    gscore = []
    for gi in range(N_GROUPS):
        mem = v[gi * epg:(gi + 1) * epg]
        best = None
        for a in range(epg):
            for c in range(a + 1, epg):
                pair = mem[a] + mem[c]
                best = pair if best is None else jnp.maximum(best, pair)
        gscore.append(best)
    is_best = []
    for gi in range(N_GROUPS):
        ok = None
        for gj in range(N_GROUPS):
            if gj == gi:
                continue
            c = (gscore[gi] > gscore[gj]) if gj < gi else (gscore[gi] >= gscore[gj])
            ok = c if ok is None else (ok & c)
        is_best.append(ok)
    chosen = []
    for e in range(N_EXPERTS):
        gi = e // epg
        rank = jnp.zeros_like(v[e])
        for e2 in range(gi * epg, (gi + 1) * epg):
            if e2 == e:
                continue
            ahead = (v[e2] >= v[e]) if e2 < e else (v[e2] > v[e])
            rank = rank + jnp.where(ahead, 1.0, 0.0)
        chosen.append(is_best[gi] & (rank < TOP_K))
    chosen_f = jnp.concatenate([jnp.where(cm, 1.0, 0.0) for cm in chosen], axis=0)
    total = None
    for e in range(N_EXPERTS):
        t = jnp.where(chosen[e], s[e], 0.0)
        total = t if total is None else total + t
    pos = jnp.dot(chosen_f.astype(BF16), tri_ref[...], preferred_element_type=F32) + carry_ref[...]
    carry_ref[...] += jnp.sum(chosen_f, axis=-1, keepdims=True)
    cnt_ref[...] = jnp.broadcast_to(carry_ref[...], cnt_ref.shape)
    e_lo = jnp.full_like(v[0], float(N_EXPERTS))
    e_hi = jnp.full_like(v[0], -1.0)
    for e in range(N_EXPERTS):
        e_lo = jnp.where(chosen[e], jnp.minimum(e_lo, float(e)), e_lo)
        e_hi = jnp.where(chosen[e], jnp.maximum(e_hi, float(e)), e_hi)
    zero = jnp.zeros_like(v[0])
    w_lo, w_hi, p_lo, p_hi = zero, zero, zero, zero
    for e in range(N_EXPERTS):
        pe = pos[e:e + 1]
        w_lo = jnp.where(e_lo == float(e), s[e], w_lo)
        w_hi = jnp.where(e_hi == float(e), s[e], w_hi)
        p_lo = jnp.where(e_lo == float(e), pe, p_lo)
        p_hi = jnp.where(e_hi == float(e), pe, p_hi)
    ri_ref[...] = jnp.concatenate([e_lo, e_hi, p_lo, p_hi, zero, zero, zero, zero], axis=0).astype(jnp.int32)
    rw_ref[...] = jnp.concatenate([w_lo / total, w_hi / total, zero, zero, zero, zero, zero, zero], axis=0)


def norm_route(x, nw, shift, scale, per_batch, w_router, router_bias):
    n, d = x.shape
    tm = _row_tile(512, n, per_batch)
    rf = _mod_row_fn(n, tm, per_batch)
    wr_t = w_router.T
    wr_hi = wr_t.astype(BF16)
    wr_lo = (wr_t - wr_hi.astype(F32)).astype(BF16)
    tri = jnp.asarray(np.triu(np.ones((tm, tm), np.float32), 1), BF16)
    const = lambda a: pl.BlockSpec(a.shape, lambda i: (0,) * a.ndim)
    rb = router_bias.reshape(N_EXPERTS, 1).astype(F32)
    return pl.pallas_call(
        _router_body,
        grid=(n // tm,),
        in_specs=[pl.BlockSpec((tm, d), lambda i: (i, 0)),
                  pl.BlockSpec((1, d), lambda i: (0, 0)),
                  pl.BlockSpec((1, 1, d), lambda i: (rf(i), 0, 0)),
                  pl.BlockSpec((1, 1, d), lambda i: (rf(i), 0, 0)),
                  const(wr_hi), const(wr_lo), const(rb), const(tri)],
        out_specs=[pl.BlockSpec((tm, d), lambda i: (i, 0)),
                   pl.BlockSpec((8, tm), lambda i: (0, i)),
                   pl.BlockSpec((8, tm), lambda i: (0, i)),
                   pl.BlockSpec((N_EXPERTS, LANES), lambda i: (0, 0))],
        out_shape=[jax.ShapeDtypeStruct((n, d), F32),
                   jax.ShapeDtypeStruct((8, n), jnp.int32),
                   jax.ShapeDtypeStruct((8, n), F32),
                   jax.ShapeDtypeStruct((N_EXPERTS, LANES), F32)],
        scratch_shapes=[pltpu.VMEM((N_EXPERTS, 1), F32)],
        compiler_params=_cp(("arbitrary",)),
        name="norm_route",
    )(x, nw.reshape(1, d), shift.reshape(MOD_ROWS, 1, d), scale.reshape(MOD_ROWS, 1, d),
      wr_hi, wr_lo, rb, tri)


def _dispatch_body(sa_ref, sb_ref, pad_ref, h_ref, xs_ref, zero_ref, sem, *, tm, n_pad):
    i = pl.program_id(0)
    base = i * tm

    def row_copy(src, r, slot):
        return pltpu.make_async_copy(src.at[pl.ds(r, 1)], xs_ref.at[pl.ds(slot, 1)], sem)

    @pl.when(i == 0)
    def _():
        zero_ref[...] = jnp.zeros_like(zero_ref)

        def fill(j, carry):
            row_copy(zero_ref, 0, pad_ref[2 * j]).start(priority=0)
            row_copy(zero_ref, 1, pad_ref[2 * j + 1]).start(priority=1)
            return carry
        lax.fori_loop(0, n_pad // 2, fill, 0, unroll=DMA_UNROLL)

        def drain(j, carry):
            row_copy(zero_ref, 0, 0).wait()
            return carry
        lax.fori_loop(0, n_pad, drain, 0, unroll=DMA_UNROLL)

    def issue(r, carry):
        row_copy(h_ref, r, sa_ref[base + r]).start(priority=0)
        row_copy(h_ref, r, sb_ref[base + r]).start(priority=1)
        return carry
    lax.fori_loop(0, tm, issue, 0, unroll=DMA_UNROLL)

    def drain2(r, carry):
        row_copy(h_ref, 0, 0).wait()
        row_copy(h_ref, 0, 0).wait()
        return carry
    lax.fori_loop(0, tm, drain2, 0, unroll=DMA_UNROLL)


def moe_dispatch(h, slot_a, slot_b, pad_slots, n_slots):
    n, d = h.shape
    tm = min(256, n)
    n_pad = pad_slots.shape[0]
    body = functools.partial(_dispatch_body, tm=tm, n_pad=n_pad)
    return pl.pallas_call(
        body,
        grid_spec=pltpu.PrefetchScalarGridSpec(
            num_scalar_prefetch=3,
            grid=(n // tm,),
            in_specs=[pl.BlockSpec((tm, d), lambda i, sa, sb, pd: (i, 0))],
            out_specs=pl.BlockSpec(memory_space=pl.ANY),
            scratch_shapes=[pltpu.VMEM((8, d), F32), pltpu.SemaphoreType.DMA(())]),
        out_shape=jax.ShapeDtypeStruct((n_slots, d), F32),
        compiler_params=_cp(("arbitrary",)),
        name="moe_dispatch",
    )(slot_a, slot_b, pad_slots, h)


def _experts_body(te_ref, nu_ref, xs_ref, wg_ref, wu_ref, wd_ref, y_ref):
    i = pl.program_id(0)

    @pl.when(i < nu_ref[0])
    def _():
        xb = xs_ref[...].astype(BF16)
        hg = jnp.dot(xb, wg_ref[0].astype(BF16), preferred_element_type=F32)
        hu = jnp.dot(xb, wu_ref[0].astype(BF16), preferred_element_type=F32)
        act = (hg * _sigmoid(hg) * hu).astype(BF16)
        y_ref[...] = jnp.dot(act, wd_ref[0].astype(BF16), preferred_element_type=F32)

    @pl.when(i >= nu_ref[0])
    def _():
        y_ref[...] = jnp.zeros_like(y_ref)


def moe_experts(xs, tile_expert, n_used, w_gate, w_up, w_down, base):
    s, d = xs.shape
    f = w_gate.shape[2]
    tm = MOE_TILE
    return pl.pallas_call(
        _experts_body,
        grid_spec=pltpu.PrefetchScalarGridSpec(
            num_scalar_prefetch=2,
            grid=(s // tm,),
            in_specs=[pl.BlockSpec((tm, d), lambda i, te, nu: (jnp.minimum(i, nu[0] - 1), 0)),
                      pl.BlockSpec((1, d, f), lambda i, te, nu: (te[i] + base, 0, 0)),
                      pl.BlockSpec((1, d, f), lambda i, te, nu: (te[i] + base, 0, 0)),
                      pl.BlockSpec((1, f, d), lambda i, te, nu: (te[i] + base, 0, 0))],
            out_specs=pl.BlockSpec((tm, d), lambda i, te, nu: (i, 0))),
        out_shape=jax.ShapeDtypeStruct((s, d), F32),
        compiler_params=_cp(("arbitrary",)),
        name="moe_experts",
    )(tile_expert, n_used, xs, w_gate, w_up, w_down)


DMA_UNROLL = 8


def _combine_body(sa_ref, sb_ref, x_ref, w_ref, g_ref, nw_ref, sh_ref, sc_ref, y_ref, *rest, tm, final):
    if final:
        o_ref, ba_ref, bb_ref, sem = rest
    else:
        o_ref, h_ref, ba_ref, bb_ref, sem = rest
    i = pl.program_id(0)
    n_tiles = pl.num_programs(0)

    def row_copy(slot, dst, buf, r):
        return pltpu.make_async_copy(y_ref.at[pl.ds(slot, 1)], dst.at[buf, pl.ds(r, 1)], sem.at[buf])

    def issue_tile(tile, buf):
        base = tile * tm

        def issue(r, carry):
            row_copy(sa_ref[base + r], ba_ref, buf, r).start(priority=0)
            row_copy(sb_ref[base + r], bb_ref, buf, r).start(priority=1)
            return carry
        lax.fori_loop(0, tm, issue, 0, unroll=DMA_UNROLL)

    @pl.when(i == 0)
    def _():
        issue_tile(0, 0)

    @pl.when(i + 1 < n_tiles)
    def _():
        issue_tile(i + 1, (i + 1) % 2)

    buf = i % 2

    def drain(r, carry):
        row_copy(0, ba_ref, buf, 0).wait()
        row_copy(0, bb_ref, buf, 0).wait()
        return carry
    lax.fori_loop(0, tm, drain, 0, unroll=DMA_UNROLL)

    w = w_ref[...]
    mix = w[:, 0:1] * ba_ref[buf] + w[:, 1:2] * bb_ref[buf]
    xn = x_ref[...] + g_ref[0] * mix
    y = xn * lax.rsqrt(jnp.mean(xn * xn, axis=-1, keepdims=True) + NORM_EPS) * nw_ref[...]
    if final:
        o_ref[...] = y
    else:
        o_ref[...] = xn
        h_ref[...] = (y * (1.0 + sc_ref[0]) + sh_ref[0]).astype(h_ref.dtype)


def moe_combine(x, y, slot_a, slot_b, wts, gate, per_batch, next_nw, next_shift, next_scale):
    n, d = x.shape
    tm = _row_tile(256, n, per_batch)
    rf = _mod_row_fn(n, tm, per_batch)
    final = next_shift is None
    if final:
        next_shift = next_scale = jnp.zeros((MOD_ROWS, d), F32)
    body = functools.partial(_combine_body, tm=tm, final=final)
    row = pl.BlockSpec((tm, d), lambda i, sa, sb: (i, 0))
    mod_row = pl.BlockSpec((1, 1, d), lambda i, sa, sb: (rf(i), 0, 0))
    out_specs = [row] if final else [row, row]
    out_shape = [jax.ShapeDtypeStruct((n, d), F32)] + ([] if final else [jax.ShapeDtypeStruct((n, d), BF16)])
    return pl.pallas_call(
        body,
        grid_spec=pltpu.PrefetchScalarGridSpec(
            num_scalar_prefetch=2,
            grid=(n // tm,),
            in_specs=[row,
                      pl.BlockSpec((tm, 8), lambda i, sa, sb: (i, 0)),
                      mod_row,
                      pl.BlockSpec((1, d), lambda i, sa, sb: (0, 0)),
                      mod_row, mod_row,
                      pl.BlockSpec(memory_space=pl.ANY)],
            out_specs=out_specs,
            scratch_shapes=[pltpu.VMEM((2, tm, d), F32), pltpu.VMEM((2, tm, d), F32),
                            pltpu.SemaphoreType.DMA((2,))]),
        out_shape=out_shape,
        compiler_params=_cp(("arbitrary",)),
        name="moe_combine",
    )(slot_a, slot_b, x, wts, gate.reshape(MOD_ROWS, 1, d), next_nw.reshape(1, d),
      next_shift.reshape(MOD_ROWS, 1, d), next_scale.reshape(MOD_ROWS, 1, d), y)


def moe_layer(x, nw, mod, per_batch, w_router, router_bias, w_gate, w_up, w_down, base, next_norm):
    n, d = x.shape
    h, route_i, route_w, counts = norm_route(x, nw, mod[:, 3], mod[:, 4], per_batch, w_router, router_bias)
    cnt = counts[:, 0].astype(jnp.int32)
    seg = ((cnt + MOE_TILE - 1) // MOE_TILE) * MOE_TILE
    off = jnp.concatenate([jnp.zeros((1,), jnp.int32), jnp.cumsum(seg)])
    n_slots = TOP_K * n + N_EXPERTS * MOE_TILE
    slot_a = off[route_i[0]] + route_i[2]
    slot_b = off[route_i[1]] + route_i[3]
    n_pad = n_slots - TOP_K * n
    padcnt = seg - cnt
    padstart = jnp.concatenate([jnp.zeros((1,), jnp.int32), jnp.cumsum(padcnt)])
    j = jnp.arange(n_pad, dtype=jnp.int32)
    count_le = lambda edges, v: jnp.sum((edges[None, :] <= v[:, None]).astype(jnp.int32), axis=1)
    e_of = jnp.clip(count_le(padstart, j) - 1, 0, N_EXPERTS)
    in_seg = off[jnp.minimum(e_of, N_EXPERTS - 1)] + cnt[jnp.minimum(e_of, N_EXPERTS - 1)] + (j - padstart[e_of])
    tail = off[N_EXPERTS] + (j - padstart[N_EXPERTS])
    pad_slots = jnp.where(e_of < N_EXPERTS, in_seg, tail).astype(jnp.int32)
    n_tiles = n_slots // MOE_TILE
    tile_start = jnp.arange(n_tiles, dtype=jnp.int32) * MOE_TILE
    n_used = (off[N_EXPERTS] // MOE_TILE).astype(jnp.int32).reshape(1)
    tile_expert = jnp.clip(count_le(off, tile_start) - 1, 0, N_EXPERTS - 1).astype(jnp.int32)
    last_used = tile_expert[jnp.maximum(n_used[0] - 1, 0)]
    tile_expert = jnp.where(jnp.arange(n_tiles) < n_used[0], tile_expert, last_used)

    xs = moe_dispatch(h, slot_a, slot_b, pad_slots, n_slots)
    y = moe_experts(xs, tile_expert, n_used, w_gate, w_up, w_down, base)
    wts = jnp.transpose(route_w)
    return moe_combine(x, y, slot_a, slot_b, wts, mod[:, 5], per_batch, *next_norm)


def _final_norm_body(x_ref, w_ref, o_ref):
    xf = x_ref[...]
    y = xf * lax.rsqrt(jnp.mean(xf * xf, axis=-1, keepdims=True) + NORM_EPS)
    o_ref[...] = y * w_ref[...]


def final_rms_norm(x, w):
    n, d = x.shape
    rows = 512
    return pl.pallas_call(
        _final_norm_body,
        grid=(n // rows,),
        in_specs=[pl.BlockSpec((rows, d), lambda i: (i, 0)), pl.BlockSpec((1, d), lambda i: (0, 0))],
        out_specs=pl.BlockSpec((rows, d), lambda i: (i, 0)),
        out_shape=jax.ShapeDtypeStruct((n, d), x.dtype),
        compiler_params=_cp(("parallel",)),
        name="final_norm",
    )(x, w.reshape(1, d))


def _reorder_w_in(w):
    o = np.cumsum((0,) + IN_WIDTHS)
    seg = lambda i: w[:, o[i]:o[i + 1]]
    main = jnp.concatenate([seg(0), seg(1), seg(2), seg(3), seg(6), seg(9), seg(12), seg(7), seg(10), seg(11)], axis=1)
    misc = jnp.concatenate([seg(8), seg(4), seg(5)], axis=1)
    misc = jnp.pad(misc, ((0, 0), (0, MISC_W - misc.shape[1])))
    return main.astype(BF16), misc.astype(BF16)


def _attention_branches(main_l, misc_l, main_c, misc_c, b, seq, ctx_len, ctx_out, rope_mla, rope_gqa,
                        mla_w, mla_qn, mla_kvn, gqa_qn, gqa_kn):
    g = GQA_HEADS // GQA_KV_HEADS
    hd = GQA_HEAD_DIM

    mq_l, mk_l, mv_l = mla_prepare(main_l, misc_l, b, seq, mla_w, mla_qn, mla_kvn, _mla_tables(rope_mla, seq))
    mq_c, mk_c, mv_c = mla_prepare(main_c, misc_c, b, ctx_len, mla_w, mla_qn, mla_kvn, _mla_tables(None, ctx_len))
    tk_all = (seq + ctx_len) // 2
    cat = lambda lat, ctx_: jnp.concatenate([lat, ctx_], axis=2)
    mla_l = flash_attention(mq_l[:, :, None], cat(mk_l, mk_c), cat(mv_l, mv_c), None, None, 1024, tk_all)
    mla_l = jnp.transpose(mla_l[:, :, 0], (0, 2, 1, 3)).reshape(b * seq, BRANCH_W)

    def split_heads(t, length, heads):
        return jnp.transpose(t.reshape(b, length, heads, hd), (0, 2, 1, 3))

    gq_l, gk_l, gv_l = gqa_prepare(main_l, b, seq, gqa_qn, gqa_kn, rope_gqa)
    gq_c, gk_c, gv_c = gqa_prepare(main_c, b, ctx_len, gqa_qn, gqa_kn, None)
    gq_l5 = split_heads(gq_l, seq, GQA_HEADS).reshape(b, GQA_KV_HEADS, g, seq, hd)
    gk_l4, gv_l4 = split_heads(gk_l, seq, GQA_KV_HEADS), split_heads(gv_l, seq, GQA_KV_HEADS)
    gk_c4, gv_c4 = split_heads(gk_c, ctx_len, GQA_KV_HEADS), split_heads(gv_c, ctx_len, GQA_KV_HEADS)
    gqa_l = flash_attention(gq_l5, cat(gk_l4, gk_c4), cat(gv_l4, gv_c4), None, None, 256, tk_all)
    gqa_l = jnp.transpose(gqa_l.reshape(b, GQA_HEADS, seq, hd), (0, 2, 1, 3)).reshape(b * seq, BRANCH_W)

    mla_c = gqa_c = None
    if ctx_out:
        mla_c = flash_attention(mq_c[:, :, None], mk_c, mv_c, None, None, 256, 256)
        mla_c = jnp.transpose(mla_c[:, :, 0], (0, 2, 1, 3)).reshape(b * ctx_len, BRANCH_W)
        gq_c5 = split_heads(gq_c, ctx_len, GQA_HEADS).reshape(b, GQA_KV_HEADS, g, ctx_len, hd)
        gqa_c = flash_attention(gq_c5, gk_c4, gv_c4, None, None, 256, 256)
        gqa_c = jnp.transpose(gqa_c.reshape(b, GQA_HEADS, ctx_len, hd), (0, 2, 1, 3)).reshape(b * ctx_len, BRANCH_W)
    return mla_l, gqa_l, mla_c, gqa_c


def kernel(x, c, ctx, c_ctx, w_ada, b_ada, norm1_w, norm2_w, w_in,
           gdn_conv_w, gdn_a_log, gdn_dt_bias, gdn_norm_w,
           mla_q_norm_w, mla_kv_norm_w, mla_w_uq, mla_w_ukv,
           gqa_q_norm_w, gqa_k_norm_w,
           hy_conv_w, hy_w1, hy_b1, hy_w2, hy_b2, hy_w3, hy_sin_freq, hy_bias,
           w_branch, w_out, w_router, router_bias,
           moe_w_gate, moe_w_up, moe_w_down, final_norm_w):
    b, seq, d = x.shape
    ctx_len = ctx.shape[1]
    rows = seq // GRID_W
    rope_mla = axial_rope_tables(rows, MLA_ROPE)
    rope_gqa = axial_rope_tables(rows, GQA_HEAD_DIM)
    mod_all = ada_modulation(c, c_ctx, w_ada, b_ada)
    xl = x.reshape(b * seq, d)
    xc = ctx.reshape(b * ctx_len, d)
    f32 = lambda t: t.astype(F32)
    for layer in range(DEPTH):
        ctx_out = layer < DEPTH - 1
        mod = mod_all[layer]
        w_main, w_misc = _reorder_w_in(w_in[layer][:, :MIX_IN])
        w_gates = w_in[layer][:, MIX_IN:].astype(BF16)
        w_br = w_branch[layer].astype(BF16)
        w_o = w_out[layer].astype(BF16)
        wg, wu, wd = (t.reshape((DEPTH * N_EXPERTS,) + t.shape[2:]) for t in (moe_w_gate, moe_w_up, moe_w_down))

        if layer == 0:
            main_l, hl = norm_modulate_project(xl, norm1_w[layer], mod[:, 0], mod[:, 1], seq, w_main, 1024)
            main_c, hc = norm_modulate_project(xc, norm1_w[layer], mod[:, 0], mod[:, 1], None, w_main, 1024)
        else:
            main_l, main_c = matmul(hl, w_main, BF16, 1024), matmul(hc, w_main, BF16, 1024)
        misc_l, misc_c = matmul(hl, w_misc, F32, MISC_W), matmul(hc, w_misc, F32, MISC_W)
        if ctx_out:
            nxt = mod_all[layer + 1]
            next_norm = (norm1_w[layer + 1], nxt[:, 0], nxt[:, 1])
        else:
            next_norm = (final_norm_w, None, None)

        gdn_l, gdn_c = gdn_branch(main_l, misc_l, main_c, misc_c, b, seq, ctx_len, gdn_conv_w[layer],
                                  gdn_a_log[layer], gdn_dt_bias[layer], gdn_norm_w[layer], ctx_out)

        mla_w = _mla_weights(mla_q_norm_w[layer], mla_kv_norm_w[layer], mla_w_uq[layer], mla_w_ukv[layer])
        mla_l, gqa_l, mla_c, gqa_c = _attention_branches(
            main_l, misc_l, main_c, misc_c, b, seq, ctx_len, ctx_out, rope_mla, rope_gqa,
            mla_w, mla_q_norm_w[layer], mla_kv_norm_w[layer], gqa_q_norm_w[layer], gqa_k_norm_w[layer])

        hy_params = (hy_w1[layer], hy_b1[layer], hy_w2[layer], hy_b2[layer], hy_w3[layer], hy_sin_freq[layer])
        hy_l = hyena_branch(main_l, b, seq, hy_conv_w[layer], hyena_filters(seq, *hy_params), hy_bias[layer])

        branches_l = [gdn_l.reshape(b * seq, BRANCH_W).astype(BF16), mla_l, gqa_l,
                      hy_l.reshape(b * seq, BRANCH_W).astype(BF16)]
        merged_l = merge_branches_gated(hl, branches_l, w_gates, w_br)

        if ctx_out:
            hy_c = hyena_branch(main_c, b, ctx_len, hy_conv_w[layer], hyena_filters(ctx_len, *hy_params),
                                hy_bias[layer])
            branches_c = [gdn_c.reshape(b * ctx_len, BRANCH_W).astype(BF16), mla_c, gqa_c,
                          hy_c.reshape(b * ctx_len, BRANCH_W).astype(BF16)]
            merged_c = merge_branches_gated(hc, branches_c, w_gates, w_br)
            xc = matmul_gated_residual(merged_c, w_o, xc, mod[:, 2], None)
            xc, hc = moe_layer(xc, norm2_w[layer], mod, None, w_router, router_bias, wg, wu, wd,
                               layer * N_EXPERTS, next_norm)

        xl = matmul_gated_residual(merged_l, w_o, xl, mod[:, 2], seq)
        outs = moe_layer(xl, norm2_w[layer], mod, seq, w_router, router_bias, wg, wu, wd,
                         layer * N_EXPERTS, next_norm)
        if ctx_out:
            xl, hl = outs
    return outs[0].reshape(b, seq, d)
```

```python
import math, functools
import jax, jax.numpy as jnp
from jax import lax
import numpy as np
from jax.experimental import pallas as pl
from jax.experimental.pallas import tpu as pltpu

D_MODEL = 2048
BATCH = 4
SEQ = 4096
DEPTH = 2

GRID_W = 64
CTX_LEN = 256
N_BRANCH = 4
BRANCH_W = 512
NORM_EPS = 1e-6
Q_BLOCK = 128
ROPE_THETA = 10000.0
SHORT_CONV = 3

GDN_HEADS = 4
GDN_HEAD_DIM = 128
GDN_CHUNK = 64

MLA_HEADS = 4
MLA_Q_LORA = 512
MLA_KV_LORA = 256
MLA_NOPE = 128
MLA_ROPE = 64
MLA_V = 128

GQA_HEADS = 8
GQA_KV_HEADS = 2
GQA_HEAD_DIM = 64

HY_WIDTH = 512
HY_ORDER = 2
HY_EMB = 33
HY_HIDDEN = 64
HY_DECAY_TARGET = 1e-2
HY_FAST_DECAY = 0.3
HY_SLOW_DECAY = 1.5

N_EXPERTS = 16
N_GROUPS = 4
EXPERTS_PER_GROUP = N_EXPERTS // N_GROUPS
TOP_K = 2
D_EXPERT = 512

GDN_W = GDN_HEADS * GDN_HEAD_DIM
IN_WIDTHS = (GDN_W, GDN_W, GDN_W, GDN_W, 2 * GDN_HEADS, 2 * GDN_HEADS,
             MLA_Q_LORA, MLA_KV_LORA, MLA_ROPE,
             GQA_HEADS * GQA_HEAD_DIM, GQA_KV_HEADS * GQA_HEAD_DIM, GQA_KV_HEADS * GQA_HEAD_DIM,
             (HY_ORDER + 1) * HY_WIDTH)
MIX_IN = sum(IN_WIDTHS)
IN_DIM = MIX_IN + N_BRANCH * D_MODEL

F32 = jnp.float32
BF16 = jnp.bfloat16
LANES = 128
MOD_ROWS = 8
CTX_MOD_ROW = BATCH
MOE_TILE = 512
VMEM_LIMIT = 56 << 20

MAIN_W = 5120
OFF_GDN, OFF_CQ, OFF_GQ, OFF_HY, OFF_CKV, OFF_GK, OFF_GV = 0, 2048, 2560, 3072, 4608, 4864, 4992
MISC_W = LANES


def _cp(sem):
    return pltpu.CompilerParams(dimension_semantics=sem, vmem_limit_bytes=VMEM_LIMIT)


def _sigmoid(v):
    return 0.5 * jnp.tanh(0.5 * v) + 0.5


def axial_rope_tables(rows, rot_dim):
    n_freq = rot_dim // 4
    freqs = ROPE_THETA ** (-jnp.arange(n_freq, dtype=jnp.float32) / n_freq)
    row = jnp.repeat(jnp.arange(rows, dtype=jnp.float32), GRID_W)
    col = jnp.tile(jnp.arange(GRID_W, dtype=jnp.float32), rows)
    ang = jnp.concatenate([row[:, None] * freqs, col[:, None] * freqs], axis=-1)
    return jnp.cos(ang), jnp.sin(ang)


def hyena_filters(length, w1, b1, w2, b2, w3, sin_freq):
    t = jnp.arange(length, dtype=jnp.float32)
    bands = (HY_EMB - 1) // 2
    f = jnp.linspace(1e-4, bands - 1, bands, dtype=jnp.float32)
    phase = (2.0 * math.pi / length) * t[:, None] * f[None, :]
    feats = jnp.concatenate([t[:, None] / (length - 1), jnp.cos(phase), -jnp.sin(phase)], axis=-1)
    hid = jnp.sin(sin_freq[0] * (feats @ w1 + b1))
    hid = jnp.sin(sin_freq[1] * (hid @ w2 + b2))
    filt = (hid @ w3).astype(jnp.float32)
    centre = length // 2
    dist = jnp.abs(t - centre) / centre
    deltas = jnp.abs(jnp.linspace(math.log(HY_DECAY_TARGET) / HY_SLOW_DECAY,
                                  math.log(HY_DECAY_TARGET) / HY_FAST_DECAY,
                                  HY_ORDER * HY_WIDTH, dtype=jnp.float32))
    filt = filt * jnp.exp(-dist[:, None] * deltas[None, :])
    filt = filt / jnp.sum(jnp.abs(filt), axis=0, keepdims=True)
    return filt.reshape(length, HY_ORDER, HY_WIDTH)


HALO_ROWS = 16
GDN_BATCHES_PER_STEP = 4


def _conv3(x, prev_row, next_row, w_ref):
    tm = x.shape[0]
    rows = lax.broadcasted_iota(jnp.int32, x.shape, 0)
    up = jnp.where(rows == 0, prev_row, pltpu.roll(x, 1, 0))
    dn = jnp.where(rows == tm - 1, next_row, pltpu.roll(x, tm - 1, 0))
    return w_ref[0:1, :] * up + w_ref[1:2, :] * x + w_ref[2:3, :] * dn


def _halo_rows(xp_ref, xn_ref):
    i = pl.program_id(1)
    prev = jnp.where(i == 0, 0.0, xp_ref[HALO_ROWS - 1:HALO_ROWS, :].astype(F32))
    nxt = jnp.where(i == pl.num_programs(1) - 1, 0.0, xn_ref[0:1, :].astype(F32))
    return prev, nxt


def _halo_specs(tm, width, col_block, nt, n_rows):
    per = tm // HALO_ROWS
    last = n_rows // HALO_ROWS - 1
    return [pl.BlockSpec((tm, width), lambda bi, i: (bi * nt + i, col_block)),
            pl.BlockSpec((HALO_ROWS, width), lambda bi, i: (jnp.maximum((bi * nt + i) * per - 1, 0), col_block)),
            pl.BlockSpec((HALO_ROWS, width), lambda bi, i: (jnp.minimum((bi * nt + i + 1) * per, last), col_block))]


def _split3(v):
    hi = v.astype(BF16)
    r1 = v - hi.astype(F32)
    mid = r1.astype(BF16)
    lo = (r1 - mid.astype(F32)).astype(BF16)
    return hi, mid, lo


def _gdn_prep_body(x_ref, xp_ref, xn_ref, misc_ref, cw_ref, alog_ref, dt_ref, gmask_ref, tp_ref, ts_ref,
                   q_out, k_out, v_out, gcf_out, gcb_out, beta_out):
    prev, nxt = _halo_rows(xp_ref, xn_ref)
    y = _conv3(x_ref[...].astype(F32), prev, nxt, cw_ref)
    y = y * _sigmoid(y)
    hd = GDN_HEAD_DIM
    for h in range(GDN_HEADS):
        qh = y[:, h * hd:(h + 1) * hd]
        kh = y[:, GDN_W + h * hd:GDN_W + (h + 1) * hd]
        qn = qh * lax.rsqrt(jnp.sum(qh * qh, axis=-1, keepdims=True) + NORM_EPS) * hd ** -0.5
        kn = kh * lax.rsqrt(jnp.sum(kh * kh, axis=-1, keepdims=True) + NORM_EPS)
        q_out[:, h * hd:(h + 1) * hd] = qn.astype(BF16)
        k_out[:, h * hd:(h + 1) * hd] = kn.astype(BF16)
    v_out[...] = y[:, 2 * GDN_W:3 * GDN_W].astype(BF16)
    m = misc_ref[...]
    a = m + dt_ref[...]
    softplus = jnp.maximum(a, 0.0) + jnp.log(1.0 + jnp.exp(-jnp.abs(a)))
    g = -(jnp.exp(alog_ref[...]) * gmask_ref[...]) * softplus
    beta_out[...] = _sigmoid(m)
    parts = _split3(g)
    gcf_out[...] = sum(jnp.dot(tp_ref[...], p, preferred_element_type=F32) for p in parts)
    gcb_out[...] = sum(jnp.dot(ts_ref[...], p, preferred_element_type=F32) for p in parts)


def gdn_prepare(main, misc, b, length, conv_w, a_log, dt_bias):
    n = b * length
    w3 = 3 * GDN_W
    tm = min(256, length)
    nt = length // tm
    lane0 = MLA_ROPE
    vec = lambda v: jnp.zeros((1, MISC_W), F32).at[0, lane0:lane0 + 2 * GDN_HEADS].set(v.reshape(-1))
    alog, dtb = vec(a_log), vec(dt_bias)
    gmask = vec(jnp.ones((2 * GDN_HEADS,), F32))
    r = np.arange(tm)
    same = (r[:, None] // GDN_CHUNK) == (r[None, :] // GDN_CHUNK)
    tpre = jnp.asarray(same & (r[None, :] <= r[:, None]), BF16)
    tsuf = jnp.asarray(same & (r[None, :] >= r[:, None]), BF16)
    const = lambda a: pl.BlockSpec(a.shape, lambda bi, i: (0,) * a.ndim)
    row = lambda width: pl.BlockSpec((tm, width), lambda bi, i: (bi * nt + i, 0))
    cw = conv_w.astype(F32)
    return pl.pallas_call(
        _gdn_prep_body,
        grid=(b, nt),
        in_specs=_halo_specs(tm, w3, OFF_GDN // w3, nt, n)
                 + [row(MISC_W), const(cw), const(alog), const(dtb), const(gmask), const(tpre), const(tsuf)],
        out_specs=[row(GDN_W), row(GDN_W), row(GDN_W), row(MISC_W), row(MISC_W), row(MISC_W)],
        out_shape=[jax.ShapeDtypeStruct((n, GDN_W), BF16)] * 3 + [jax.ShapeDtypeStruct((n, MISC_W), F32)] * 3,
        compiler_params=_cp(("parallel", "parallel")),
        name="gdn_prep",
    )(main, main, main, misc, cw, alog, dtb, gmask, tpre, tsuf)


def _gdn_chunk_body(qf_ref, kf_ref, vf_ref, qb_ref, kb_ref, vb_ref, gcf_ref, gcb_ref, bcf_ref, bcb_ref,
                    grf_ref, grb_ref, s0_ref, *rest, nc, with_out, bpb):
    if with_out:
        of_ref, ob_ref, sfin_ref, s_ref = rest
    else:
        sfin_ref, s_ref = rest
        of_ref = ob_ref = None
    c = pl.program_id(1)
    nst = 2 * GDN_HEADS

    @pl.when(c == 0)
    def _():
        s_ref[...] = s0_ref[...].reshape(s_ref.shape)

    ch = GDN_CHUNK
    hd = GDN_HEAD_DIM
    ii = lax.broadcasted_iota(jnp.int32, (ch, ch), 0)
    jj = lax.broadcasted_iota(jnp.int32, (ch, ch), 1)
    nt_dims = (((1,), (1,)), ((), ()))
    tn_dims = (((0,), (0,)), ((), ()))
    bdot = lambda a, b_: jnp.dot(a.astype(BF16), b_.astype(BF16), preferred_element_type=F32)
    eye = jnp.where(ii == jj, 1.0, 0.0)
    pair_masks = [((ii >> (l + 1)) == (jj >> (l + 1))) & ((ii >> l) != (jj >> l))
                  for l in range(int(math.log2(ch)))]
    dirs = ((qf_ref, kf_ref, vf_ref, gcf_ref, bcf_ref, grf_ref, of_ref, ii >= jj, ii > jj, ch - 1),
            (qb_ref, kb_ref, vb_ref, gcb_ref, bcb_ref, grb_ref, ob_ref, ii <= jj, ii < jj, 0))
    chains = []
    for bb in range(bpb):
        for d, (q_ref, k_ref, v_ref, gc_ref, bc_ref, gr_ref, o_ref, incl, strict, last_row) in enumerate(dirs):
            for h in range(GDN_HEADS):
                j = d * GDN_HEADS + h
                cols = slice(h * hd, (h + 1) * hd)
                cn = dict(bb=bb, j=j, cols=cols, o_ref=o_ref, incl=incl, strict=strict)
                cn['q'], cn['k'], cn['v'] = q_ref[bb, :, cols], k_ref[bb, :, cols], v_ref[bb, :, cols]
                cn['gc'] = gc_ref[bb, :, j:j + 1]
                cn['gr'] = gr_ref[bb, 0, j:j + 1, :]
                cn['beta'] = bc_ref[bb, :, j:j + 1]
                cn['g_last'] = gc_ref[bb, last_row:last_row + 1, j:j + 1]
                chains.append(cn)
    for cn in chains:
        incl = cn['incl']
        cn['decay'] = jnp.where(incl, jnp.exp(jnp.where(incl, cn['gc'] - cn['gr'], 0.0)), 0.0)
        cn['kf'] = cn['k'].astype(F32)
        cn['kbeta'] = cn['kf'] * cn['beta']
    for cn in chains:
        kk = lax.dot_general(cn['kbeta'].astype(BF16), cn['k'], nt_dims, preferred_element_type=F32)
        cn['a'] = jnp.where(cn['strict'], kk * cn['decay'], 0.0)
    for cn in chains:
        cn['t'] = eye - jnp.where(pair_masks[0], cn['a'], 0.0)
    for pm in pair_masks[1:]:
        for cn in chains:
            cn['tmp'] = bdot(cn['t'], jnp.where(pm, cn['a'], 0.0))
        for cn in chains:
            cn['t'] = cn['t'] - bdot(cn['tmp'], cn['t'])
    for cn in chains:
        rhs = jnp.concatenate([cn['v'].astype(F32) * cn['beta'], cn['kbeta'] * jnp.exp(cn['gc'])], axis=1)
        cn['x'] = bdot(cn['t'], rhs)
    for cn in chains:
        cn['s'] = s_ref[cn['bb'] * nst + cn['j']]
        cn['v_new'] = cn['x'][:, :hd] - bdot(cn['x'][:, hd:], cn['s'])
    if with_out:
        for cn in chains:
            qk = lax.dot_general(cn['q'], cn['k'], nt_dims, preferred_element_type=F32)
            qk = jnp.where(cn['incl'], qk * cn['decay'], 0.0)
            o = bdot(cn['q'].astype(F32) * jnp.exp(cn['gc']), cn['s']) + bdot(qk, cn['v_new'])
            cn['o_ref'][cn['bb'], :, cn['cols']] = o
    for cn in chains:
        kdec = cn['kf'] * jnp.exp(cn['g_last'] - cn['gc'])
        s_ref[cn['bb'] * nst + cn['j']] = cn['s'] * jnp.exp(cn['g_last']) + lax.dot_general(
            kdec.astype(BF16), cn['v_new'].astype(BF16), tn_dims, preferred_element_type=F32)

    @pl.when(c == nc - 1)
    def _():
        sfin_ref[...] = s_ref[...].reshape(sfin_ref.shape)


def gdn_scan(q, k, v, gcol, bcol, grow, s0, b, length, with_out):
    n = b * length
    ch = GDN_CHUNK
    nc = length // ch
    nst = 2 * GDN_HEADS
    bpb = min(GDN_BATCHES_PER_STEP, b)
    fwd = lambda bg, c: (bg, c, 0)
    bwd = lambda bg, c: (bg, nc - 1 - c, 0)
    fwd4 = lambda bg, c: (bg, c, 0, 0)
    bwd4 = lambda bg, c: (bg, nc - 1 - c, 0, 0)
    wide = lambda f: pl.BlockSpec((bpb, ch, GDN_W), f)
    narrow = lambda f: pl.BlockSpec((bpb, ch, nst), f)
    rows = lambda f: pl.BlockSpec((bpb, 1, nst, ch), f)
    state = pl.BlockSpec((bpb, nst, GDN_HEAD_DIM, GDN_HEAD_DIM), lambda bg, c: (bg, 0, 0, 0))
    out_specs = [state]
    out_shape = [jax.ShapeDtypeStruct((b, nst, GDN_HEAD_DIM, GDN_HEAD_DIM), F32)]
    if with_out:
        out_specs = [wide(fwd), wide(bwd)] + out_specs
        out_shape = [jax.ShapeDtypeStruct((b, length, GDN_W), F32)] * 2 + out_shape
    body = functools.partial(_gdn_chunk_body, nc=nc, with_out=with_out, bpb=bpb)
    q3, k3, v3 = (t.reshape(b, length, GDN_W) for t in (q, k, v))
    gcol3, bcol3 = gcol.reshape(b, length, nst), bcol.reshape(b, length, nst)
    grow4 = grow.reshape(b, nc, nst, ch)
    outs = pl.pallas_call(
        body,
        grid=(b // bpb, nc),
        in_specs=[wide(fwd), wide(fwd), wide(fwd), wide(bwd), wide(bwd), wide(bwd),
                  narrow(fwd), narrow(bwd), narrow(fwd), narrow(bwd), rows(fwd4), rows(bwd4), state],
        out_specs=out_specs,
        out_shape=out_shape,
        scratch_shapes=[pltpu.VMEM((bpb * nst, GDN_HEAD_DIM, GDN_HEAD_DIM), F32)],
        compiler_params=_cp(("parallel", "arbitrary")),
        name="gdn_scan",
    )(q3, k3, v3, q3, k3, v3, gcol3, gcol3, bcol3, bcol3, grow4, grow4, s0)
    if with_out:
        return outs[0].reshape(n, GDN_W), outs[1].reshape(n, GDN_W), outs[2]
    return outs


def _gdn_out_body(of_ref, ob_ref, z_ref, nw_ref, y_ref):
    o = of_ref[...] + ob_ref[...]
    z = z_ref[...].astype(F32)
    hd = GDN_HEAD_DIM
    for h in range(GDN_HEADS):
        cols = slice(h * hd, (h + 1) * hd)
        oh = o[:, cols]
        yh = oh * lax.rsqrt(jnp.mean(oh * oh, axis=-1, keepdims=True) + NORM_EPS) * nw_ref[...]
        zh = z[:, cols]
        y_ref[:, cols] = (yh * (zh * _sigmoid(zh))).astype(BF16)


def gdn_output_gate(o_f, o_b, main, norm_w):
    n = o_f.shape[0]
    tm = min(512, n)
    nw = norm_w.reshape(1, GDN_HEAD_DIM).astype(F32)
    return pl.pallas_call(
        _gdn_out_body,
        grid=(n // tm,),
        in_specs=[pl.BlockSpec((tm, GDN_W), lambda i: (i, 0)),
                  pl.BlockSpec((tm, GDN_W), lambda i: (i, 0)),
                  pl.BlockSpec((tm, GDN_W), lambda i: (i, (OFF_GDN + 3 * GDN_W) // GDN_W)),
                  pl.BlockSpec((1, GDN_HEAD_DIM), lambda i: (0, 0))],
        out_specs=pl.BlockSpec((tm, GDN_W), lambda i: (i, 0)),
        out_shape=jax.ShapeDtypeStruct((n, GDN_W), BF16),
        compiler_params=_cp(("parallel",)),
        name="gdn_out",
    )(o_f, o_b, main, nw)


def gdn_branch(main_l, misc_l, main_c, misc_c, b, seq, ctx_len, conv_w, a_log, dt_bias, norm_w, ctx_out):
    nst = 2 * GDN_HEADS
    lane0 = MLA_ROPE

    def gates(gcf, gcb, beta, length):
        gcol = jnp.concatenate([gcf[:, lane0:lane0 + GDN_HEADS], gcb[:, lane0 + GDN_HEADS:lane0 + nst]], axis=1)
        bcol = beta[:, lane0 + nst:lane0 + 2 * nst]
        grow = jnp.transpose(gcol.reshape(-1, GDN_CHUNK, nst), (0, 2, 1))
        return gcol, bcol, grow

    qc, kc, vc, gcf, gcb, beta = gdn_prepare(main_c, misc_c, b, ctx_len, conv_w, a_log, dt_bias)
    gcol_c, bcol_c, grow_c = gates(gcf, gcb, beta, ctx_len)
    ql, kl, vl, gcf, gcb, beta = gdn_prepare(main_l, misc_l, b, seq, conv_w, a_log, dt_bias)
    gcol_l, bcol_l, grow_l = gates(gcf, gcb, beta, seq)
    s0 = jnp.zeros((b, nst, GDN_HEAD_DIM, GDN_HEAD_DIM), F32)
    outs_c = gdn_scan(qc, kc, vc, gcol_c, bcol_c, grow_c, s0, b, ctx_len, ctx_out)
    s_ctx = outs_c[-1]
    of_l, ob_l, _ = gdn_scan(ql, kl, vl, gcol_l, bcol_l, grow_l, s_ctx, b, seq, True)
    out_l = gdn_output_gate(of_l, ob_l, main_l, norm_w)
    out_c = gdn_output_gate(outs_c[0], outs_c[1], main_c, norm_w) if ctx_out else None
    return out_l, out_c


def _hy_conv_body(x_ref, xp_ref, xn_ref, cw_ref, v_out, x1_out, x2_out):
    prev, nxt = _halo_rows(xp_ref, xn_ref)
    y = _conv3(x_ref[...].astype(F32), prev, nxt, cw_ref)
    w = HY_WIDTH
    v_out[...] = y[:, :w].astype(BF16)
    x1_out[...] = y[:, w:2 * w].astype(BF16)
    x2_out[...] = y[:, 2 * w:3 * w].astype(BF16)


def hyena_short_conv(main, b, length, conv_w):
    n = b * length
    w3 = (HY_ORDER + 1) * HY_WIDTH
    tm = min(256, length)
    nt = length // tm
    cw = conv_w.astype(F32)
    row = pl.BlockSpec((tm, HY_WIDTH), lambda bi, i: (bi * nt + i, 0))
    return pl.pallas_call(
        _hy_conv_body,
        grid=(b, nt),
        in_specs=_halo_specs(tm, w3, OFF_HY // w3, nt, n) + [pl.BlockSpec(cw.shape, lambda bi, i: (0, 0))],
        out_specs=[row, row, row],
        out_shape=[jax.ShapeDtypeStruct((n, HY_WIDTH), BF16)] * 3,
        compiler_params=_cp(("parallel", "parallel")),
        name="hyena_conv",
    )(main, main, main, cw)


def _dft_consts(length):
    n = 2 * length
    n2 = 64 if length >= 2048 else 16
    n1 = n // n2
    nk1 = n1 // 2 + 8
    k1 = np.arange(nk1)
    t1 = np.arange(n1 // 2)
    ang1 = 2.0 * np.pi * np.outer(k1, t1) / n1
    f_first = np.concatenate([np.cos(ang1), -np.sin(ang1)], axis=0)
    t2 = np.arange(n2)
    ang2 = 2.0 * np.pi * np.outer(t2, t2) / n2
    c2, s2 = np.cos(ang2), np.sin(ang2)
    g_fwd = np.block([[c2, s2], [-s2, c2]])
    g_inv = g_fwd.T
    angt = 2.0 * np.pi * np.outer(k1, t2) / n
    tw_r, tw_i = np.cos(angt)[:, :, None], -np.sin(angt)[:, :, None]
    tt = np.arange(n1 // 4, 3 * n1 // 4)
    ang3 = 2.0 * np.pi * np.outer(tt, k1) / n1
    fold = np.where((k1 == 0) | (k1 == n1 // 2), 1.0, np.where(k1 < n1 // 2, 2.0, 0.0))[None, :]
    f_last = np.concatenate([np.cos(ang3) * fold, -np.sin(ang3) * fold], axis=1) / n
    bf = lambda a: jnp.asarray(a, BF16)
    return dict(n1=n1, nk1=nk1, n2=n2, f_first=bf(f_first), g_fwd=bf(g_fwd), g_inv=bf(g_inv),
                tw_r=jnp.asarray(tw_r, F32), tw_i=jnp.asarray(tw_i, F32), f_last=bf(f_last))


def _hy_first_body(f_ref, z_ref, a_ref):
    a_ref[0] = jnp.dot(f_ref[...], z_ref[0], preferred_element_type=F32).astype(BF16)


def hyena_dft_first(zv, consts):
    b, half, cols = zv.shape
    n1 = consts['nk1']
    tn = min(4096, cols)
    f = consts['f_first']
    return pl.pallas_call(
        _hy_first_body,
        grid=(b, cols // tn),
        in_specs=[pl.BlockSpec(f.shape, lambda bi, j: (0, 0)),
                  pl.BlockSpec((1, half, tn), lambda bi, j: (bi, 0, j))],
        out_specs=pl.BlockSpec((1, 2 * n1, tn), lambda bi, j: (bi, 0, j)),
        out_shape=jax.ShapeDtypeStruct((b, 2 * n1, cols), BF16),
        compiler_params=_cp(("parallel", "parallel")),
        name="hyena_dft_first",
    )(f, zv)


def _hy_mid_body(a_ref, twr_ref, twi_ref, gf_ref, *rest, kt, spectrum_only):
    if spectrum_only:
        (o_ref,) = rest
    else:
        gi_ref, h_ref, o_ref = rest
    n2 = gf_ref.shape[0] // 2

    ks = range(kt)
    tw = [(twr_ref[i], twi_ref[i]) for i in ks]
    a = [(a_ref[0, 0, i].astype(F32), a_ref[0, 1, i].astype(F32)) for i in ks]
    b = [jnp.concatenate([ar * twr - ai * twi, ar * twi + ai * twr], axis=0).astype(BF16)
         for (ar, ai), (twr, twi) in zip(a, tw)]
    z = [jnp.dot(gf_ref[...], bb, preferred_element_type=F32) for bb in b]
    if spectrum_only:
        for i in ks:
            o_ref[0, 0, i] = z[i][:n2]
            o_ref[0, 1, i] = z[i][n2:]
        return
    y = []
    for i in ks:
        zr, zi = z[i][:n2], z[i][n2:]
        hr, hi = h_ref[0, i], h_ref[1, i]
        y.append(jnp.concatenate([zr * hr - zi * hi, zr * hi + zi * hr], axis=0).astype(BF16))
    w = [jnp.dot(gi_ref[...], yy, preferred_element_type=F32) for yy in y]
    for i in ks:
        wr, wi = w[i][:n2], w[i][n2:]
        twr, twi = tw[i]
        o_ref[0, 0, i] = (wr * twr + wi * twi).astype(BF16)
        o_ref[0, 1, i] = (wi * twr - wr * twi).astype(BF16)


def hyena_dft_mid(a5, consts, spectrum=None):
    b, _, n1, n2, c = a5.shape
    kt = 8
    only = spectrum is None
    blk = pl.BlockSpec((1, 2, kt, n2, c), lambda bi, j: (bi, 0, j, 0, 0))
    tw = pl.BlockSpec((kt, n2, 1), lambda bi, j: (j, 0, 0))
    g = pl.BlockSpec((2 * n2, 2 * n2), lambda bi, j: (0, 0))
    in_specs = [blk, tw, tw, g]
    args = [a5, consts['tw_r'], consts['tw_i'], consts['g_fwd']]
    if not only:
        in_specs += [g, pl.BlockSpec((2, kt, n2, c), lambda bi, j: (0, j, 0, 0))]
        args += [consts['g_inv'], spectrum]
    body = functools.partial(_hy_mid_body, kt=kt, spectrum_only=only)
    return pl.pallas_call(
        body,
        grid=(b, n1 // kt),
        in_specs=in_specs,
        out_specs=blk,
        out_shape=jax.ShapeDtypeStruct(a5.shape, F32 if only else BF16),
        compiler_params=_cp(("parallel", "parallel")),
        name="hyena_dft_mid",
    )(*args)


def _hy_last_body(f_ref, b_ref, z_ref, x_ref, bias_ref, o_ref):
    y = jnp.dot(f_ref[...], b_ref[0], preferred_element_type=F32)
    z = z_ref[0].astype(F32)
    o_ref[0] = (x_ref[0].astype(F32) * (y + bias_ref[...] * z)).astype(BF16)


def hyena_dft_last(bv, zv, xv, bias_row, consts):
    b, rows2, cols = bv.shape
    half = consts['n1'] // 2
    tn = min(4096, cols)
    f = consts['f_last']
    sig = pl.BlockSpec((1, half, tn), lambda bi, j: (bi, 0, j))
    return pl.pallas_call(
        _hy_last_body,
        grid=(b, cols // tn),
        in_specs=[pl.BlockSpec(f.shape, lambda bi, j: (0, 0)),
                  pl.BlockSpec((1, rows2, tn), lambda bi, j: (bi, 0, j)),
                  sig, sig,
                  pl.BlockSpec((1, tn), lambda bi, j: (0, j))],
        out_specs=sig,
        out_shape=jax.ShapeDtypeStruct((b, half, cols), BF16),
        compiler_params=_cp(("parallel", "parallel")),
        name="hyena_dft_last",
    )(f, bv, zv, xv, bias_row)


def hyena_branch(main, b, length, conv_w, filt, bias):
    consts = _dft_consts(length)
    n1, nk1, n2 = consts['n1'], consts['nk1'], consts['n2']
    c = HY_WIDTH
    cols = n2 * c
    view = lambda t: t.reshape(b, n1 // 2, cols)
    v, x1, x2 = (view(t) for t in hyena_short_conv(main, b, length, conv_w))
    hv = jnp.transpose(filt, (1, 0, 2)).astype(BF16).reshape(HY_ORDER, n1 // 2, cols)
    h_first = hyena_dft_first(hv, consts).reshape(HY_ORDER, 2, nk1, n2, c)
    spectra = hyena_dft_mid(h_first, consts)
    z = v
    for o, gate in enumerate((x1, x2)):
        a5 = hyena_dft_first(z, consts).reshape(b, 2, nk1, n2, c)
        bm = hyena_dft_mid(a5, consts, spectra[o]).reshape(b, 2 * nk1, cols)
        bias_row = jnp.tile(bias[o].astype(F32), n2).reshape(1, cols)
        z = hyena_dft_last(bm, z, gate, bias_row, consts)
    return z.reshape(b * length, c)


def _ada_body(c_ref, w_ref, b_ref, o_ref):
    cv = c_ref[...]
    s = cv * _sigmoid(cv)
    o_ref[0] = jnp.dot(s, w_ref[0], precision=lax.Precision.HIGHEST, preferred_element_type=F32) + b_ref[0]


def ada_modulation(c, c_ctx, w_ada, b_ada):
    depth, d, d6 = w_ada.shape
    c8 = jnp.zeros((MOD_ROWS, d), F32).at[:c.shape[0]].set(c).at[CTX_MOD_ROW].set(c_ctx)
    tn = 512
    out = pl.pallas_call(
        _ada_body,
        grid=(depth, d6 // tn),
        in_specs=[pl.BlockSpec((MOD_ROWS, d), lambda l, j: (0, 0)),
                  pl.BlockSpec((1, d, tn), lambda l, j: (l, 0, j)),
                  pl.BlockSpec((1, 1, tn), lambda l, j: (l, 0, j))],
        out_specs=pl.BlockSpec((1, MOD_ROWS, tn), lambda l, j: (l, 0, j)),
        out_shape=jax.ShapeDtypeStruct((depth, MOD_ROWS, d6), F32),
        compiler_params=_cp(("parallel", "parallel")),
        name="ada_mod",
    )(c8, w_ada, b_ada.reshape(depth, 1, d6))
    return out.reshape(depth, MOD_ROWS, 6, d)


def _row_tile(cap, n_rows, per_batch):
    return min(cap, n_rows if per_batch is None else per_batch)


def _mod_row_fn(n_rows, tm, per_batch):
    if per_batch is None:
        return lambda i: CTX_MOD_ROW
    tiles = per_batch // tm
    return lambda i: i // tiles


def _normmod_body(x_ref, nw_ref, sh_ref, sc_ref, o_ref):
    xf = x_ref[...]
    y = xf * lax.rsqrt(jnp.mean(xf * xf, axis=-1, keepdims=True) + NORM_EPS) * nw_ref[...]
    o_ref[...] = (y * (1.0 + sc_ref[0]) + sh_ref[0]).astype(o_ref.dtype)


def norm_modulate(x, nw, shift, scale, per_batch, out_dtype=BF16):
    n, d = x.shape
    tm = _row_tile(512, n, per_batch)
    rf = _mod_row_fn(n, tm, per_batch)
    return pl.pallas_call(
        _normmod_body,
        grid=(n // tm,),
        in_specs=[pl.BlockSpec((tm, d), lambda i: (i, 0)),
                  pl.BlockSpec((1, d), lambda i: (0, 0)),
                  pl.BlockSpec((1, 1, d), lambda i: (rf(i), 0, 0)),
                  pl.BlockSpec((1, 1, d), lambda i: (rf(i), 0, 0))],
        out_specs=pl.BlockSpec((tm, d), lambda i: (i, 0)),
        out_shape=jax.ShapeDtypeStruct((n, d), out_dtype),
        compiler_params=_cp(("parallel",)),
        name="norm_mod",
    )(x, nw.reshape(1, d), shift.reshape(MOD_ROWS, 1, d), scale.reshape(MOD_ROWS, 1, d))


def _mm_body(a_ref, w_ref, o_ref):
    o_ref[...] = jnp.dot(a_ref[...], w_ref[...], preferred_element_type=F32).astype(o_ref.dtype)


def matmul(a, w, out_dtype, tn):
    n, k = a.shape
    m = w.shape[1]
    tm = min(2048, n)
    return pl.pallas_call(
        _mm_body,
        grid=(n // tm, m // tn),
        in_specs=[pl.BlockSpec((tm, k), lambda i, j: (i, 0)),
                  pl.BlockSpec((k, tn), lambda i, j: (0, j))],
        out_specs=pl.BlockSpec((tm, tn), lambda i, j: (i, j)),
        out_shape=jax.ShapeDtypeStruct((n, m), out_dtype),
        compiler_params=_cp(("parallel", "parallel")),
        name="proj",
    )(a, w)


def _mm_res_body(a_ref, w_ref, x_ref, g_ref, o_ref):
    y = jnp.dot(a_ref[...], w_ref[...], preferred_element_type=F32)
    o_ref[...] = x_ref[...] + g_ref[0] * y


def matmul_gated_residual(a, w, x, gate, per_batch):
    n, k = a.shape
    d = w.shape[1]
    tm = _row_tile(1024, n, per_batch)
    tn = min(1024, d)
    rf = _mod_row_fn(n, tm, per_batch)
    return pl.pallas_call(
        _mm_res_body,
        grid=(n // tm, d // tn),
        in_specs=[pl.BlockSpec((tm, k), lambda i, j: (i, 0)),
                  pl.BlockSpec((k, tn), lambda i, j: (0, j)),
                  pl.BlockSpec((tm, tn), lambda i, j: (i, j)),
                  pl.BlockSpec((1, 1, tn), lambda i, j: (rf(i), 0, j))],
        out_specs=pl.BlockSpec((tm, tn), lambda i, j: (i, j)),
        out_shape=jax.ShapeDtypeStruct((n, d), F32),
        compiler_params=_cp(("parallel", "parallel")),
        name="out_proj_residual",
    )(a, w, x, gate.reshape(MOD_ROWS, 1, d))


def _merge_body(h_ref, *refs):
    b_refs, wg_refs = refs[:N_BRANCH], refs[N_BRANCH:2 * N_BRANCH]
    wb_ref, o_ref = refs[2 * N_BRANCH:]
    h = h_ref[...]
    acc = None
    for n in range(N_BRANCH):
        gate = jnp.dot(h, wg_refs[n][...], preferred_element_type=F32)
        proj = jnp.dot(b_refs[n][...], wb_ref[n], preferred_element_type=F32)
        term = _sigmoid(gate) * proj
        acc = term if acc is None else acc + term
    o_ref[...] = acc.astype(o_ref.dtype)


def merge_branches_gated(h, branches, w_gate, w_branch):
    n, d = h.shape
    bw = branches[0].shape[1]
    tm = min(1024, n)
    tn = min(512, d)
    nj = d // tn
    gate_spec = lambda b: pl.BlockSpec((d, tn), lambda i, j: (0, b * nj + j))
    return pl.pallas_call(
        _merge_body,
        grid=(n // tm, nj),
        in_specs=[pl.BlockSpec((tm, d), lambda i, j: (i, 0))]
                 + [pl.BlockSpec((tm, bw), lambda i, j: (i, 0))] * N_BRANCH
                 + [gate_spec(b) for b in range(N_BRANCH)]
                 + [pl.BlockSpec((N_BRANCH, bw, tn), lambda i, j: (0, 0, j))],
        out_specs=pl.BlockSpec((tm, tn), lambda i, j: (i, j)),
        out_shape=jax.ShapeDtypeStruct((n, d), BF16),
        compiler_params=_cp(("parallel", "parallel")),
        name="gate_merge",
    )(h, *branches, *([w_gate] * N_BRANCH), w_branch)


def _mla_prep_body(cq_ref, ckv_ref, misc_ref, qnw_ref, kvnw_ref, wqa_ref, wqb_ref, wk_ref, wv_ref,
                   ska_ref, skb_ref, cq_tab, sq_tab, q_out, k_out, v_out):
    def norm(v, w_ref):
        vf = v.astype(F32)
        return (vf * lax.rsqrt(jnp.mean(vf * vf, axis=-1, keepdims=True) + NORM_EPS) * w_ref[...]).astype(BF16)

    xq = norm(cq_ref[...], qnw_ref)
    xkv = norm(ckv_ref[...], kvnw_ref)
    cos, sin = cq_tab[...], sq_tab[...]
    misc = misc_ref[...].astype(BF16)
    kr = (jnp.dot(misc, ska_ref[...], preferred_element_type=F32) * cos
          + jnp.dot(misc, skb_ref[...], preferred_element_type=F32) * sin)
    for h in range(MLA_HEADS):
        qa = jnp.dot(xq, wqa_ref[h], preferred_element_type=F32)
        qb = jnp.dot(xq, wqb_ref[h], preferred_element_type=F32)
        q_out[0, h] = (qa * cos + qb * sin).astype(BF16)
        k_out[0, h] = (jnp.dot(xkv, wk_ref[h], preferred_element_type=F32) + kr).astype(BF16)
        v_out[0, h] = jnp.dot(xkv, wv_ref[h], preferred_element_type=F32).astype(BF16)


def _mla_weights(q_norm_w, kv_norm_w, w_uq, w_ukv):
    dk = MLA_NOPE + MLA_ROPE
    half = MLA_ROPE // 2
    scale = dk ** -0.5
    wq = jnp.transpose(w_uq, (1, 0, 2)) * scale
    nope0 = jnp.zeros(wq.shape[:2] + (MLA_NOPE,), F32)
    wq_rot = jnp.concatenate([nope0, -wq[..., MLA_NOPE + half:], wq[..., MLA_NOPE:MLA_NOPE + half]], axis=-1)
    wkv = jnp.transpose(w_ukv, (1, 0, 2))
    wk = jnp.concatenate([wkv[..., :MLA_NOPE], jnp.zeros(wkv.shape[:2] + (MLA_ROPE,), F32)], axis=-1)
    wv = wkv[..., MLA_NOPE:]
    eye = jnp.eye(MLA_ROPE, dtype=F32)
    rot = jnp.concatenate([-eye[:, half:], eye[:, :half]], axis=-1)
    pad_r = MISC_W - MLA_ROPE
    ska = jnp.pad(eye, ((0, pad_r), (MLA_NOPE, 0)))
    skb = jnp.pad(rot, ((0, pad_r), (MLA_NOPE, 0)))
    return tuple(t.astype(BF16) for t in (wq, wq_rot, wk, wv, ska, skb))


def _mla_tables(rope, length):
    dk = MLA_NOPE + MLA_ROPE
    if rope is None:
        return jnp.ones((length, dk), F32), jnp.zeros((length, dk), F32)
    cos, sin = rope
    ones = jnp.ones((length, MLA_NOPE), F32)
    return (jnp.concatenate([ones, cos, cos], axis=-1),
            jnp.concatenate([0.0 * ones, sin, sin], axis=-1))


def mla_prepare(main, misc, b, length, weights, q_norm_w, kv_norm_w, tables):
    wq, wq_rot, wk, wv, ska, skb = weights
    cos, sin = tables
    dk = MLA_NOPE + MLA_ROPE
    tm = min(512, length)
    nt = length // tm
    full = lambda a: pl.BlockSpec(a.shape, lambda bi, i: (0,) * a.ndim)
    qnw = q_norm_w.reshape(1, -1)
    kvnw = kv_norm_w.reshape(1, -1)
    outs = pl.pallas_call(
        _mla_prep_body,
        grid=(b, nt),
        in_specs=[pl.BlockSpec((tm, MLA_Q_LORA), lambda bi, i: (bi * nt + i, OFF_CQ // MLA_Q_LORA)),
                  pl.BlockSpec((tm, MLA_KV_LORA), lambda bi, i: (bi * nt + i, OFF_CKV // MLA_KV_LORA)),
                  pl.BlockSpec((tm, MISC_W), lambda bi, i: (bi * nt + i, 0)),
                  full(qnw), full(kvnw), full(wq), full(wq_rot), full(wk), full(wv), full(ska), full(skb),
                  pl.BlockSpec((tm, dk), lambda bi, i: (i, 0)),
                  pl.BlockSpec((tm, dk), lambda bi, i: (i, 0))],
        out_specs=[pl.BlockSpec((1, MLA_HEADS, tm, dk), lambda bi, i: (bi, 0, i, 0)),
                   pl.BlockSpec((1, MLA_HEADS, tm, dk), lambda bi, i: (bi, 0, i, 0)),
                   pl.BlockSpec((1, MLA_HEADS, tm, MLA_V), lambda bi, i: (bi, 0, i, 0))],
        out_shape=[jax.ShapeDtypeStruct((b, MLA_HEADS, length, dk), BF16),
                   jax.ShapeDtypeStruct((b, MLA_HEADS, length, dk), BF16),
                   jax.ShapeDtypeStruct((b, MLA_HEADS, length, MLA_V), BF16)],
        compiler_params=_cp(("parallel", "parallel")),
        name="mla_prep",
    )(main, main, misc, qnw, kvnw, wq, wq_rot, wk, wv, ska, skb, cos, sin)
    return outs


def _gqa_prep_body(q_ref, k_ref, v_ref, qnw_ref, knw_ref, gsum_ref, rot_ref, cos_ref, sin_ref,
                   q_out, k_out, v_out):
    cos, sin = cos_ref[...], sin_ref[...]

    def prep(v, nw, width):
        vf = v.astype(F32)
        sq = vf * vf
        hi = sq.astype(BF16)
        lo = (sq - hi.astype(F32)).astype(BF16)
        g = gsum_ref[:width, :width]
        ss = jnp.dot(hi, g, preferred_element_type=F32) + jnp.dot(lo, g, preferred_element_type=F32)
        xn = vf * lax.rsqrt(ss * (1.0 / GQA_HEAD_DIM) + NORM_EPS) * nw
        xr = jnp.dot(xn.astype(BF16), rot_ref[:width, :width], preferred_element_type=F32)
        return xn * cos[:, :width] + xr * sin[:, :width]

    qf = prep(q_ref[...], qnw_ref[...], GQA_HEADS * GQA_HEAD_DIM) * GQA_HEAD_DIM ** -0.5
    kf = prep(k_ref[...], knw_ref[...], LANES)
    q_out[...] = qf.astype(BF16)
    k_out[...] = kf.astype(BF16)
    v_out[...] = v_ref[...]


def gqa_prepare(main, b, length, q_norm_w, k_norm_w, rope):
    n = b * length
    qw = GQA_HEADS * GQA_HEAD_DIM
    kw = GQA_KV_HEADS * GQA_HEAD_DIM
    half = GQA_HEAD_DIM // 2
    if rope is None:
        cos = jnp.ones((length, qw), F32)
        sin = jnp.zeros((length, qw), F32)
    else:
        cos = jnp.tile(jnp.concatenate([rope[0], rope[0]], axis=-1), (1, GQA_HEADS))
        sin = jnp.tile(jnp.concatenate([rope[1], rope[1]], axis=-1), (1, GQA_HEADS))
    head = np.arange(qw) // GQA_HEAD_DIM
    gsum = jnp.asarray(head[:, None] == head[None, :], BF16)
    eye = np.eye(GQA_HEAD_DIM, dtype=np.float32)
    rot1 = np.concatenate([-eye[:, half:], eye[:, :half]], axis=-1)
    rot = jnp.asarray(np.kron(np.eye(GQA_HEADS, dtype=np.float32), rot1), BF16)
    tm = min(512, length)
    nt = length // tm
    full = lambda a: pl.BlockSpec(a.shape, lambda bi, i: (0,) * a.ndim)
    qnw = jnp.tile(q_norm_w, GQA_HEADS).reshape(1, qw)
    knw = jnp.tile(k_norm_w, GQA_KV_HEADS).reshape(1, kw)
    return pl.pallas_call(
        _gqa_prep_body,
        grid=(b, nt),
        in_specs=[pl.BlockSpec((tm, qw), lambda bi, i: (bi * nt + i, OFF_GQ // qw)),
                  pl.BlockSpec((tm, kw), lambda bi, i: (bi * nt + i, OFF_GK // kw)),
                  pl.BlockSpec((tm, kw), lambda bi, i: (bi * nt + i, OFF_GV // kw)),
                  full(qnw), full(knw), full(gsum), full(rot),
                  pl.BlockSpec((tm, qw), lambda bi, i: (i, 0)),
                  pl.BlockSpec((tm, qw), lambda bi, i: (i, 0))],
        out_specs=[pl.BlockSpec((tm, qw), lambda bi, i: (bi * nt + i, 0)),
                   pl.BlockSpec((tm, kw), lambda bi, i: (bi * nt + i, 0)),
                   pl.BlockSpec((tm, kw), lambda bi, i: (bi * nt + i, 0))],
        out_shape=[jax.ShapeDtypeStruct((n, qw), BF16),
                   jax.ShapeDtypeStruct((n, kw), BF16),
                   jax.ShapeDtypeStruct((n, kw), BF16)],
        compiler_params=_cp(("parallel", "parallel")),
        name="gqa_prep",
    )(main, main, main, qnw, knw, gsum, rot, cos, sin)


FLASH_CHAIN_ROWS = 256


def _flash_body(q_ref, k_ref, v_ref, o_ref, m_ref, l_ref, acc_ref, *, chains):
    ki = pl.program_id(3)

    @pl.when(ki == 0)
    def _():
        m_ref[...] = jnp.full_like(m_ref, -jnp.inf)
        l_ref[...] = jnp.zeros_like(l_ref)
        acc_ref[...] = jnp.zeros_like(acc_ref)

    k, vt = k_ref[0, 0], v_ref[0, 0]
    ss = [jnp.dot(k, q_ref[0, 0, gi, :, r0:r0 + rc], preferred_element_type=F32)
          for gi, r0, rc in chains]
    m_prev = [m_ref[ci] for ci in range(len(chains))]
    m_new = [jnp.maximum(mp, jnp.max(s, axis=0, keepdims=True)) for mp, s in zip(m_prev, ss)]
    ps = [jnp.exp(s - mn) for s, mn in zip(ss, m_new)]
    alphas = [jnp.exp(mp - mn) for mp, mn in zip(m_prev, m_new)]
    pv = [jnp.dot(vt, p.astype(BF16), preferred_element_type=F32) for p in ps]
    for ci in range(len(chains)):
        l_ref[ci] = alphas[ci] * l_ref[ci] + jnp.sum(ps[ci], axis=0, keepdims=True)
        acc_ref[ci] = alphas[ci] * acc_ref[ci] + pv[ci]
        m_ref[ci] = m_new[ci]

    @pl.when(ki == pl.num_programs(3) - 1)
    def _():
        for ci, (gi, r0, rc) in enumerate(chains):
            o_ref[0, 0, gi, :, r0:r0 + rc] = (acc_ref[ci] / l_ref[ci]).astype(o_ref.dtype)


def flash_attention(q, k, v, tq, tk):
    b, hkv, g, sq, dk = q.shape
    sk = k.shape[2]
    dv = v.shape[3]
    tq = min(tq, sq)
    tk = min(tk, sk)
    rc = min(FLASH_CHAIN_ROWS, tq)
    chains = tuple((gi, r0, rc) for gi in range(g) for r0 in range(0, tq, rc))
    nch = len(chains)
    body = functools.partial(_flash_body, chains=chains)
    qt = jnp.swapaxes(q, 3, 4)
    vt = jnp.swapaxes(v, 2, 3)
    out_t = pl.pallas_call(
        body,
        grid=(b, hkv, sq // tq, sk // tk),
        in_specs=[pl.BlockSpec((1, 1, g, dk, tq), lambda bi, h, qi, ki: (bi, h, 0, 0, qi)),
                  pl.BlockSpec((1, 1, tk, dk), lambda bi, h, qi, ki: (bi, h, ki, 0)),
                  pl.BlockSpec((1, 1, dv, tk), lambda bi, h, qi, ki: (bi, h, 0, ki))],
        out_specs=pl.BlockSpec((1, 1, g, dv, tq), lambda bi, h, qi, ki: (bi, h, 0, 0, qi)),
        out_shape=jax.ShapeDtypeStruct((b, hkv, g, dv, sq), BF16),
        scratch_shapes=[pltpu.VMEM((nch, 1, rc), F32), pltpu.VMEM((nch, 1, rc), F32),
                        pltpu.VMEM((nch, dv, rc), F32)],
        compiler_params=_cp(("parallel", "parallel", "parallel", "arbitrary")),
        name="flash_attention",
    )(qt, k, vt)
    return jnp.swapaxes(out_t, 3, 4)


def _router_body(x_ref, nw_ref, sh_ref, sc_ref, wrh_ref, wrl_ref, rb_ref, tri_ref,
                 h_ref, ri_ref, rw_ref, cnt_ref, carry_ref):
    i = pl.program_id(0)

    @pl.when(i == 0)
    def _():
        carry_ref[...] = jnp.zeros_like(carry_ref)

    xf = x_ref[...]
    y = xf * lax.rsqrt(jnp.mean(xf * xf, axis=-1, keepdims=True) + NORM_EPS) * nw_ref[...]
    h = y * (1.0 + sc_ref[0]) + sh_ref[0]
    h_ref[...] = h
    hi = h.astype(BF16)
    lo = (h - hi.astype(F32)).astype(BF16)
    nt = (((1,), (1,)), ((), ()))
    logits = (lax.dot_general(wrh_ref[...], hi, nt, preferred_element_type=F32)
              + lax.dot_general(wrh_ref[...], lo, nt, preferred_element_type=F32)
              + lax.dot_general(wrl_ref[...], hi, nt, preferred_element_type=F32))
    scores = _sigmoid(logits)
    sel = scores + rb_ref[...]
    s = [scores[e:e + 1] for e in range(N_EXPERTS)]
    v = [sel[e:e + 1] for e in range(N_EXPERTS)]
    epg = EXPERTS_PER_GROUP
    gscore = []
    for gi in range(N_GROUPS):
        mem = v[gi * epg:(gi + 1) * epg]
        best = None
        for a in range(epg):
            for c in range(a + 1, epg):
                pair = mem[a] + mem[c]
                best = pair if best is None else jnp.maximum(best, pair)
        gscore.append(best)
    is_best = []
    for gi in range(N_GROUPS):
        ok = None
        for gj in range(N_GROUPS):
            if gj == gi:
                continue
            c = (gscore[gi] > gscore[gj]) if gj < gi else (gscore[gi] >= gscore[gj])
            ok = c if ok is None else (ok & c)
        is_best.append(ok)
    chosen = []
    for e in range(N_EXPERTS):
        gi = e // epg
        rank = jnp.zeros_like(v[e])
        for e2 in range(gi * epg, (gi + 1) * epg):
            if e2 == e:
                continue
            ahead = (v[e2] >= v[e]) if e2 < e else (v[e2] > v[e])
            rank = rank + jnp.where(ahead, 1.0, 0.0)
        chosen.append(is_best[gi] & (rank < TOP_K))
    chosen_f = jnp.concatenate([jnp.where(cm, 1.0, 0.0) for cm in chosen], axis=0)
    total = None
    for e in range(N_EXPERTS):
        t = jnp.where(chosen[e], s[e], 0.0)
        total = t if total is None else total + t
    pos = jnp.dot(chosen_f.astype(BF16), tri_ref[...], preferred_element_type=F32) + carry_ref[...]
    carry_ref[...] += jnp.sum(chosen_f, axis=-1, keepdims=True)
    cnt_ref[...] = jnp.broadcast_to(carry_ref[...], cnt_ref.shape)
    e_lo = jnp.full_like(v[0], float(N_EXPERTS))
    e_hi = jnp.full_like(v[0], -1.0)
    for e in range(N_EXPERTS):
        e_lo = jnp.where(chosen[e], jnp.minimum(e_lo, float(e)), e_lo)
        e_hi = jnp.where(chosen[e], jnp.maximum(e_hi, float(e)), e_hi)
    zero = jnp.zeros_like(v[0])
    w_lo, w_hi, p_lo, p_hi = zero, zero, zero, zero
    for e in range(N_EXPERTS):
        pe = pos[e:e + 1]
        w_lo = jnp.where(e_lo == float(e), s[e], w_lo)
        w_hi = jnp.where(e_hi == float(e), s[e], w_hi)
        p_lo = jnp.where(e_lo == float(e), pe, p_lo)
        p_hi = jnp.where(e_hi == float(e), pe, p_hi)
    ri_ref[...] = jnp.concatenate([e_lo, e_hi, p_lo, p_hi, zero, zero, zero, zero], axis=0).astype(jnp.int32)
    rw_ref[...] = jnp.concatenate([w_lo / total, w_hi / total, zero, zero, zero, zero, zero, zero], axis=0)


def norm_route(x, nw, shift, scale, per_batch, w_router, router_bias):
    n, d = x.shape
    tm = _row_tile(512, n, per_batch)
    rf = _mod_row_fn(n, tm, per_batch)
    wr_t = w_router.T
    wr_hi = wr_t.astype(BF16)
    wr_lo = (wr_t - wr_hi.astype(F32)).astype(BF16)
    tri = jnp.asarray(np.triu(np.ones((tm, tm), np.float32), 1), BF16)
    const = lambda a: pl.BlockSpec(a.shape, lambda i: (0,) * a.ndim)
    rb = router_bias.reshape(N_EXPERTS, 1).astype(F32)
    return pl.pallas_call(
        _router_body,
        grid=(n // tm,),
        in_specs=[pl.BlockSpec((tm, d), lambda i: (i, 0)),
                  pl.BlockSpec((1, d), lambda i: (0, 0)),
                  pl.BlockSpec((1, 1, d), lambda i: (rf(i), 0, 0)),
                  pl.BlockSpec((1, 1, d), lambda i: (rf(i), 0, 0)),
                  const(wr_hi), const(wr_lo), const(rb), const(tri)],
        out_specs=[pl.BlockSpec((tm, d), lambda i: (i, 0)),
                   pl.BlockSpec((8, tm), lambda i: (0, i)),
                   pl.BlockSpec((8, tm), lambda i: (0, i)),
                   pl.BlockSpec((N_EXPERTS, LANES), lambda i: (0, 0))],
        out_shape=[jax.ShapeDtypeStruct((n, d), F32),
                   jax.ShapeDtypeStruct((8, n), jnp.int32),
                   jax.ShapeDtypeStruct((8, n), F32),
                   jax.ShapeDtypeStruct((N_EXPERTS, LANES), F32)],
        scratch_shapes=[pltpu.VMEM((N_EXPERTS, 1), F32)],
        compiler_params=_cp(("arbitrary",)),
        name="norm_route",
    )(x, nw.reshape(1, d), shift.reshape(MOD_ROWS, 1, d), scale.reshape(MOD_ROWS, 1, d),
      wr_hi, wr_lo, rb, tri)


def _dispatch_body(sa_ref, sb_ref, pad_ref, h_ref, xs_ref, zero_ref, sem, *, tm, n_pad):
    i = pl.program_id(0)
    base = i * tm

    def row_copy(src, r, slot):
        return pltpu.make_async_copy(src.at[pl.ds(r, 1)], xs_ref.at[pl.ds(slot, 1)], sem)

    @pl.when(i == 0)
    def _():
        zero_ref[...] = jnp.zeros_like(zero_ref)

        def fill(j, carry):
            row_copy(zero_ref, 0, pad_ref[2 * j]).start(priority=0)
            row_copy(zero_ref, 1, pad_ref[2 * j + 1]).start(priority=1)
            return carry
        lax.fori_loop(0, n_pad // 2, fill, 0, unroll=DMA_UNROLL)

        def drain(j, carry):
            row_copy(zero_ref, 0, 0).wait()
            return carry
        lax.fori_loop(0, n_pad, drain, 0, unroll=DMA_UNROLL)

    def issue(r, carry):
        row_copy(h_ref, r, sa_ref[base + r]).start(priority=0)
        row_copy(h_ref, r, sb_ref[base + r]).start(priority=1)
        return carry
    lax.fori_loop(0, tm, issue, 0, unroll=DMA_UNROLL)

    def drain2(r, carry):
        row_copy(h_ref, 0, 0).wait()
        row_copy(h_ref, 0, 0).wait()
        return carry
    lax.fori_loop(0, tm, drain2, 0, unroll=DMA_UNROLL)


def moe_dispatch(h, slot_a, slot_b, pad_slots, n_slots):
    n, d = h.shape
    tm = min(256, n)
    n_pad = pad_slots.shape[0]
    body = functools.partial(_dispatch_body, tm=tm, n_pad=n_pad)
    return pl.pallas_call(
        body,
        grid_spec=pltpu.PrefetchScalarGridSpec(
            num_scalar_prefetch=3,
            grid=(n // tm,),
            in_specs=[pl.BlockSpec((tm, d), lambda i, sa, sb, pd: (i, 0))],
            out_specs=pl.BlockSpec(memory_space=pl.ANY),
            scratch_shapes=[pltpu.VMEM((8, d), F32), pltpu.SemaphoreType.DMA(())]),
        out_shape=jax.ShapeDtypeStruct((n_slots, d), F32),
        compiler_params=_cp(("arbitrary",)),
        name="moe_dispatch",
    )(slot_a, slot_b, pad_slots, h)


def _experts_body(te_ref, nu_ref, xs_ref, wg_ref, wu_ref, wd_ref, y_ref):
    i = pl.program_id(0)

    @pl.when(i < nu_ref[0])
    def _():
        xb = xs_ref[...].astype(BF16)
        hg = jnp.dot(xb, wg_ref[0].astype(BF16), preferred_element_type=F32)
        hu = jnp.dot(xb, wu_ref[0].astype(BF16), preferred_element_type=F32)
        act = (hg * _sigmoid(hg) * hu).astype(BF16)
        y_ref[...] = jnp.dot(act, wd_ref[0].astype(BF16), preferred_element_type=F32)

    @pl.when(i >= nu_ref[0])
    def _():
        y_ref[...] = jnp.zeros_like(y_ref)


def moe_experts(xs, tile_expert, n_used, w_gate, w_up, w_down, base):
    s, d = xs.shape
    f = w_gate.shape[2]
    tm = MOE_TILE
    return pl.pallas_call(
        _experts_body,
        grid_spec=pltpu.PrefetchScalarGridSpec(
            num_scalar_prefetch=2,
            grid=(s // tm,),
            in_specs=[pl.BlockSpec((tm, d), lambda i, te, nu: (jnp.minimum(i, nu[0] - 1), 0)),
                      pl.BlockSpec((1, d, f), lambda i, te, nu: (te[i] + base, 0, 0)),
                      pl.BlockSpec((1, d, f), lambda i, te, nu: (te[i] + base, 0, 0)),
                      pl.BlockSpec((1, f, d), lambda i, te, nu: (te[i] + base, 0, 0))],
            out_specs=pl.BlockSpec((tm, d), lambda i, te, nu: (i, 0))),
        out_shape=jax.ShapeDtypeStruct((s, d), F32),
        compiler_params=_cp(("arbitrary",)),
        name="moe_experts",
    )(tile_expert, n_used, xs, w_gate, w_up, w_down)


DMA_UNROLL = 8


def _combine_body(sa_ref, sb_ref, x_ref, w_ref, g_ref, nw_ref, sh_ref, sc_ref, y_ref, *rest, tm, final):
    if final:
        o_ref, ba_ref, bb_ref, sem = rest
    else:
        o_ref, h_ref, ba_ref, bb_ref, sem = rest
    i = pl.program_id(0)
    n_tiles = pl.num_programs(0)

    def row_copy(slot, dst, buf, r):
        return pltpu.make_async_copy(y_ref.at[pl.ds(slot, 1)], dst.at[buf, pl.ds(r, 1)], sem.at[buf])

    def issue_tile(tile, buf):
        base = tile * tm

        def issue(r, carry):
            row_copy(sa_ref[base + r], ba_ref, buf, r).start(priority=0)
            row_copy(sb_ref[base + r], bb_ref, buf, r).start(priority=1)
            return carry
        lax.fori_loop(0, tm, issue, 0, unroll=DMA_UNROLL)

    @pl.when(i == 0)
    def _():
        issue_tile(0, 0)

    @pl.when(i + 1 < n_tiles)
    def _():
        issue_tile(i + 1, (i + 1) % 2)

    buf = i % 2

    def drain(r, carry):
        row_copy(0, ba_ref, buf, 0).wait()
        row_copy(0, bb_ref, buf, 0).wait()
        return carry
    lax.fori_loop(0, tm, drain, 0, unroll=DMA_UNROLL)

    w = w_ref[...]
    mix = w[:, 0:1] * ba_ref[buf] + w[:, 1:2] * bb_ref[buf]
    xn = x_ref[...] + g_ref[0] * mix
    y = xn * lax.rsqrt(jnp.mean(xn * xn, axis=-1, keepdims=True) + NORM_EPS) * nw_ref[...]
    if final:
        o_ref[...] = y
    else:
        o_ref[...] = xn
        h_ref[...] = (y * (1.0 + sc_ref[0]) + sh_ref[0]).astype(h_ref.dtype)


def moe_combine(x, y, slot_a, slot_b, wts, gate, per_batch, next_nw, next_shift, next_scale):
    n, d = x.shape
    tm = _row_tile(256, n, per_batch)
    rf = _mod_row_fn(n, tm, per_batch)
    final = next_shift is None
    if final:
        next_shift = next_scale = jnp.zeros((MOD_ROWS, d), F32)
    body = functools.partial(_combine_body, tm=tm, final=final)
    row = pl.BlockSpec((tm, d), lambda i, sa, sb: (i, 0))
    mod_row = pl.BlockSpec((1, 1, d), lambda i, sa, sb: (rf(i), 0, 0))
    out_specs = [row] if final else [row, row]
    out_shape = [jax.ShapeDtypeStruct((n, d), F32)] + ([] if final else [jax.ShapeDtypeStruct((n, d), BF16)])
    return pl.pallas_call(
        body,
        grid_spec=pltpu.PrefetchScalarGridSpec(
            num_scalar_prefetch=2,
            grid=(n // tm,),
            in_specs=[row,
                      pl.BlockSpec((tm, 8), lambda i, sa, sb: (i, 0)),
                      mod_row,
                      pl.BlockSpec((1, d), lambda i, sa, sb: (0, 0)),
                      mod_row, mod_row,
                      pl.BlockSpec(memory_space=pl.ANY)],
            out_specs=out_specs,
            scratch_shapes=[pltpu.VMEM((2, tm, d), F32), pltpu.VMEM((2, tm, d), F32),
                            pltpu.SemaphoreType.DMA((2,))]),
        out_shape=out_shape,
        compiler_params=_cp(("arbitrary",)),
        name="moe_combine",
    )(slot_a, slot_b, x, wts, gate.reshape(MOD_ROWS, 1, d), next_nw.reshape(1, d),
      next_shift.reshape(MOD_ROWS, 1, d), next_scale.reshape(MOD_ROWS, 1, d), y)


def moe_layer(x, nw, mod, per_batch, w_router, router_bias, w_gate, w_up, w_down, base, next_norm):
    n, d = x.shape
    h, route_i, route_w, counts = norm_route(x, nw, mod[:, 3], mod[:, 4], per_batch, w_router, router_bias)
    cnt = counts[:, 0].astype(jnp.int32)
    seg = ((cnt + MOE_TILE - 1) // MOE_TILE) * MOE_TILE
    off = jnp.concatenate([jnp.zeros((1,), jnp.int32), jnp.cumsum(seg)])
    n_slots = TOP_K * n + N_EXPERTS * MOE_TILE
    slot_a = off[route_i[0]] + route_i[2]
    slot_b = off[route_i[1]] + route_i[3]
    n_pad = n_slots - TOP_K * n
    padcnt = seg - cnt
    padstart = jnp.concatenate([jnp.zeros((1,), jnp.int32), jnp.cumsum(padcnt)])
    j = jnp.arange(n_pad, dtype=jnp.int32)
    count_le = lambda edges, v: jnp.sum((edges[None, :] <= v[:, None]).astype(jnp.int32), axis=1)
    e_of = jnp.clip(count_le(padstart, j) - 1, 0, N_EXPERTS)
    in_seg = off[jnp.minimum(e_of, N_EXPERTS - 1)] + cnt[jnp.minimum(e_of, N_EXPERTS - 1)] + (j - padstart[e_of])
    tail = off[N_EXPERTS] + (j - padstart[N_EXPERTS])
    pad_slots = jnp.where(e_of < N_EXPERTS, in_seg, tail).astype(jnp.int32)
    n_tiles = n_slots // MOE_TILE
    tile_start = jnp.arange(n_tiles, dtype=jnp.int32) * MOE_TILE
    n_used = (off[N_EXPERTS] // MOE_TILE).astype(jnp.int32).reshape(1)
    tile_expert = jnp.clip(count_le(off, tile_start) - 1, 0, N_EXPERTS - 1).astype(jnp.int32)
    last_used = tile_expert[jnp.maximum(n_used[0] - 1, 0)]
    tile_expert = jnp.where(jnp.arange(n_tiles) < n_used[0], tile_expert, last_used)

    xs = moe_dispatch(h, slot_a, slot_b, pad_slots, n_slots)
    y = moe_experts(xs, tile_expert, n_used, w_gate, w_up, w_down, base)
    wts = jnp.transpose(route_w)
    return moe_combine(x, y, slot_a, slot_b, wts, mod[:, 5], per_batch, *next_norm)


def _reorder_w_in(w):
    o = np.cumsum((0,) + IN_WIDTHS)
    seg = lambda i: w[:, o[i]:o[i + 1]]
    main = jnp.concatenate([seg(0), seg(1), seg(2), seg(3), seg(6), seg(9), seg(12), seg(7), seg(10), seg(11)], axis=1)
    misc = jnp.concatenate([seg(8), seg(4), seg(5)], axis=1)
    misc = jnp.pad(misc, ((0, 0), (0, MISC_W - misc.shape[1])))
    return main.astype(BF16), misc.astype(BF16)


def _attention_branches(main_l, misc_l, main_c, misc_c, b, seq, ctx_len, ctx_out, rope_mla, rope_gqa,
                        mla_w, mla_qn, mla_kvn, gqa_qn, gqa_kn):
    g = GQA_HEADS // GQA_KV_HEADS
    hd = GQA_HEAD_DIM

    mq_l, mk_l, mv_l = mla_prepare(main_l, misc_l, b, seq, mla_w, mla_qn, mla_kvn, _mla_tables(rope_mla, seq))
    mq_c, mk_c, mv_c = mla_prepare(main_c, misc_c, b, ctx_len, mla_w, mla_qn, mla_kvn, _mla_tables(None, ctx_len))
    tk_all = (seq + ctx_len) // 2
    cat = lambda lat, ctx_: jnp.concatenate([lat, ctx_], axis=2)
    mla_l = flash_attention(mq_l[:, :, None], cat(mk_l, mk_c), cat(mv_l, mv_c), 2048, tk_all)
    mla_l = jnp.transpose(mla_l[:, :, 0], (0, 2, 1, 3)).reshape(b * seq, BRANCH_W)

    def split_heads(t, length, heads):
        return jnp.transpose(t.reshape(b, length, heads, hd), (0, 2, 1, 3))

    gq_l, gk_l, gv_l = gqa_prepare(main_l, b, seq, gqa_qn, gqa_kn, rope_gqa)
    gq_c, gk_c, gv_c = gqa_prepare(main_c, b, ctx_len, gqa_qn, gqa_kn, None)
    gq_l5 = split_heads(gq_l, seq, GQA_HEADS).reshape(b, GQA_KV_HEADS, g, seq, hd)
    gk_l4, gv_l4 = split_heads(gk_l, seq, GQA_KV_HEADS), split_heads(gv_l, seq, GQA_KV_HEADS)
    gk_c4, gv_c4 = split_heads(gk_c, ctx_len, GQA_KV_HEADS), split_heads(gv_c, ctx_len, GQA_KV_HEADS)
    gqa_l = flash_attention(gq_l5, cat(gk_l4, gk_c4), cat(gv_l4, gv_c4), 512, tk_all)
    gqa_l = jnp.transpose(gqa_l.reshape(b, GQA_HEADS, seq, hd), (0, 2, 1, 3)).reshape(b * seq, BRANCH_W)

    mla_c = gqa_c = None
    if ctx_out:
        mla_c = flash_attention(mq_c[:, :, None], mk_c, mv_c, 256, 256)
        mla_c = jnp.transpose(mla_c[:, :, 0], (0, 2, 1, 3)).reshape(b * ctx_len, BRANCH_W)
        gq_c5 = split_heads(gq_c, ctx_len, GQA_HEADS).reshape(b, GQA_KV_HEADS, g, ctx_len, hd)
        gqa_c = flash_attention(gq_c5, gk_c4, gv_c4, 256, 256)
        gqa_c = jnp.transpose(gqa_c.reshape(b, GQA_HEADS, ctx_len, hd), (0, 2, 1, 3)).reshape(b * ctx_len, BRANCH_W)
    return mla_l, gqa_l, mla_c, gqa_c


def kernel(x, c, ctx, c_ctx, w_ada, b_ada, norm1_w, norm2_w, w_in,
           gdn_conv_w, gdn_a_log, gdn_dt_bias, gdn_norm_w,
           mla_q_norm_w, mla_kv_norm_w, mla_w_uq, mla_w_ukv,
           gqa_q_norm_w, gqa_k_norm_w,
           hy_conv_w, hy_w1, hy_b1, hy_w2, hy_b2, hy_w3, hy_sin_freq, hy_bias,
           w_branch, w_out, w_router, router_bias,
           moe_w_gate, moe_w_up, moe_w_down, final_norm_w):
    b, seq, d = x.shape
    ctx_len = ctx.shape[1]
    rows = seq // GRID_W
    rope_mla = axial_rope_tables(rows, MLA_ROPE)
    rope_gqa = axial_rope_tables(rows, GQA_HEAD_DIM)
    mod_all = ada_modulation(c, c_ctx, w_ada, b_ada)
    xl = x.reshape(b * seq, d)
    xc = ctx.reshape(b * ctx_len, d)
    f32 = lambda t: t.astype(F32)
    for layer in range(DEPTH):
        ctx_out = layer < DEPTH - 1
        mod = mod_all[layer]
        w_main, w_misc = _reorder_w_in(w_in[layer][:, :MIX_IN])
        w_gates = w_in[layer][:, MIX_IN:].astype(BF16)
        w_br = w_branch[layer].astype(BF16)
        w_o = w_out[layer].astype(BF16)
        wg, wu, wd = (t.reshape((DEPTH * N_EXPERTS,) + t.shape[2:]) for t in (moe_w_gate, moe_w_up, moe_w_down))

        if layer == 0:
            hl = norm_modulate(xl, norm1_w[layer], mod[:, 0], mod[:, 1], seq)
            hc = norm_modulate(xc, norm1_w[layer], mod[:, 0], mod[:, 1], None)
        main_l, main_c = matmul(hl, w_main, BF16, 1024), matmul(hc, w_main, BF16, 1024)
        misc_l, misc_c = matmul(hl, w_misc, F32, MISC_W), matmul(hc, w_misc, F32, MISC_W)
        if ctx_out:
            nxt = mod_all[layer + 1]
            next_norm = (norm1_w[layer + 1], nxt[:, 0], nxt[:, 1])
        else:
            next_norm = (final_norm_w, None, None)

        gdn_l, gdn_c = gdn_branch(main_l, misc_l, main_c, misc_c, b, seq, ctx_len, gdn_conv_w[layer],
                                  gdn_a_log[layer], gdn_dt_bias[layer], gdn_norm_w[layer], ctx_out)

        mla_w = _mla_weights(mla_q_norm_w[layer], mla_kv_norm_w[layer], mla_w_uq[layer], mla_w_ukv[layer])
        mla_l, gqa_l, mla_c, gqa_c = _attention_branches(
            main_l, misc_l, main_c, misc_c, b, seq, ctx_len, ctx_out, rope_mla, rope_gqa,
            mla_w, mla_q_norm_w[layer], mla_kv_norm_w[layer], gqa_q_norm_w[layer], gqa_k_norm_w[layer])

        hy_params = (hy_w1[layer], hy_b1[layer], hy_w2[layer], hy_b2[layer], hy_w3[layer], hy_sin_freq[layer])
        hy_l = hyena_branch(main_l, b, seq, hy_conv_w[layer], hyena_filters(seq, *hy_params), hy_bias[layer])

        branches_l = [gdn_l.reshape(b * seq, BRANCH_W).astype(BF16), mla_l, gqa_l,
                      hy_l.reshape(b * seq, BRANCH_W).astype(BF16)]
        merged_l = merge_branches_gated(hl, branches_l, w_gates, w_br)

        if ctx_out:
            hy_c = hyena_branch(main_c, b, ctx_len, hy_conv_w[layer], hyena_filters(ctx_len, *hy_params),
                                hy_bias[layer])
            branches_c = [gdn_c.reshape(b * ctx_len, BRANCH_W).astype(BF16), mla_c, gqa_c,
                          hy_c.reshape(b * ctx_len, BRANCH_W).astype(BF16)]
            merged_c = merge_branches_gated(hc, branches_c, w_gates, w_br)
            xc = matmul_gated_residual(merged_c, w_o, xc, mod[:, 2], None)
            xc, hc = moe_layer(xc, norm2_w[layer], mod, None, w_router, router_bias, wg, wu, wd,
                               layer * N_EXPERTS, next_norm)

        xl = matmul_gated_residual(merged_l, w_o, xl, mod[:, 2], seq)
        outs = moe_layer(xl, norm2_w[layer], mod, seq, w_router, router_bias, wg, wu, wd,
                         layer * N_EXPERTS, next_norm)
        if ctx_out:
            xl, hl = outs
    return outs[0].reshape(b, seq, d)
```

```python
import math, functools
import jax, jax.numpy as jnp
from jax import lax
import numpy as np
from jax.experimental import pallas as pl
from jax.experimental.pallas import tpu as pltpu

D_MODEL = 2048
BATCH = 4
SEQ = 4096
DEPTH = 2

GRID_W = 64
CTX_LEN = 256
N_BRANCH = 4
BRANCH_W = 512
NORM_EPS = 1e-6
Q_BLOCK = 128
ROPE_THETA = 10000.0
SHORT_CONV = 3

GDN_HEADS = 4
GDN_HEAD_DIM = 128
GDN_CHUNK = 64

MLA_HEADS = 4
MLA_Q_LORA = 512
MLA_KV_LORA = 256
MLA_NOPE = 128
MLA_ROPE = 64
MLA_V = 128

GQA_HEADS = 8
GQA_KV_HEADS = 2
GQA_HEAD_DIM = 64

HY_WIDTH = 512
HY_ORDER = 2
HY_EMB = 33
HY_HIDDEN = 64
HY_DECAY_TARGET = 1e-2
HY_FAST_DECAY = 0.3
HY_SLOW_DECAY = 1.5

N_EXPERTS = 16
N_GROUPS = 4
EXPERTS_PER_GROUP = N_EXPERTS // N_GROUPS
TOP_K = 2
D_EXPERT = 512

GDN_W = GDN_HEADS * GDN_HEAD_DIM
IN_WIDTHS = (GDN_W, GDN_W, GDN_W, GDN_W, 2 * GDN_HEADS, 2 * GDN_HEADS,
             MLA_Q_LORA, MLA_KV_LORA, MLA_ROPE,
             GQA_HEADS * GQA_HEAD_DIM, GQA_KV_HEADS * GQA_HEAD_DIM, GQA_KV_HEADS * GQA_HEAD_DIM,
             (HY_ORDER + 1) * HY_WIDTH)
MIX_IN = sum(IN_WIDTHS)
IN_DIM = MIX_IN + N_BRANCH * D_MODEL

F32 = jnp.float32
BF16 = jnp.bfloat16
LANES = 128
MOD_ROWS = 8
CTX_MOD_ROW = BATCH
MOE_TILE = 512
VMEM_LIMIT = 56 << 20

MAIN_W = 5120
OFF_GDN, OFF_CQ, OFF_GQ, OFF_HY, OFF_CKV, OFF_GK, OFF_GV = 0, 2048, 2560, 3072, 4608, 4864, 4992
MISC_W = LANES


def _cp(sem):
    return pltpu.CompilerParams(dimension_semantics=sem, vmem_limit_bytes=VMEM_LIMIT)


def _sigmoid(v):
    return 0.5 * jnp.tanh(0.5 * v) + 0.5


def axial_rope_tables(rows, rot_dim):
    n_freq = rot_dim // 4
    freqs = ROPE_THETA ** (-jnp.arange(n_freq, dtype=jnp.float32) / n_freq)
    row = jnp.repeat(jnp.arange(rows, dtype=jnp.float32), GRID_W)
    col = jnp.tile(jnp.arange(GRID_W, dtype=jnp.float32), rows)
    ang = jnp.concatenate([row[:, None] * freqs, col[:, None] * freqs], axis=-1)
    return jnp.cos(ang), jnp.sin(ang)


def hyena_filters(length, w1, b1, w2, b2, w3, sin_freq):
    t = jnp.arange(length, dtype=jnp.float32)
    bands = (HY_EMB - 1) // 2
    f = jnp.linspace(1e-4, bands - 1, bands, dtype=jnp.float32)
    phase = (2.0 * math.pi / length) * t[:, None] * f[None, :]
    feats = jnp.concatenate([t[:, None] / (length - 1), jnp.cos(phase), -jnp.sin(phase)], axis=-1)
    hid = jnp.sin(sin_freq[0] * (feats @ w1 + b1))
    hid = jnp.sin(sin_freq[1] * (hid @ w2 + b2))
    filt = (hid @ w3).astype(jnp.float32)
    centre = length // 2
    dist = jnp.abs(t - centre) / centre
    deltas = jnp.abs(jnp.linspace(math.log(HY_DECAY_TARGET) / HY_SLOW_DECAY,
                                  math.log(HY_DECAY_TARGET) / HY_FAST_DECAY,
                                  HY_ORDER * HY_WIDTH, dtype=jnp.float32))
    filt = filt * jnp.exp(-dist[:, None] * deltas[None, :])
    filt = filt / jnp.sum(jnp.abs(filt), axis=0, keepdims=True)
    return filt.reshape(length, HY_ORDER, HY_WIDTH)


HALO_ROWS = 16
GDN_BATCHES_PER_STEP = 4


def _conv3(x, prev_row, next_row, w_ref):
    tm = x.shape[0]
    rows = lax.broadcasted_iota(jnp.int32, x.shape, 0)
    up = jnp.where(rows == 0, prev_row, pltpu.roll(x, 1, 0))
    dn = jnp.where(rows == tm - 1, next_row, pltpu.roll(x, tm - 1, 0))
    return w_ref[0:1, :] * up + w_ref[1:2, :] * x + w_ref[2:3, :] * dn


def _halo_rows(xp_ref, xn_ref):
    i = pl.program_id(1)
    prev = jnp.where(i == 0, 0.0, xp_ref[HALO_ROWS - 1:HALO_ROWS, :].astype(F32))
    nxt = jnp.where(i == pl.num_programs(1) - 1, 0.0, xn_ref[0:1, :].astype(F32))
    return prev, nxt


def _halo_specs(tm, width, col_block, nt, n_rows):
    per = tm // HALO_ROWS
    last = n_rows // HALO_ROWS - 1
    return [pl.BlockSpec((tm, width), lambda bi, i: (bi * nt + i, col_block)),
            pl.BlockSpec((HALO_ROWS, width), lambda bi, i: (jnp.maximum((bi * nt + i) * per - 1, 0), col_block)),
            pl.BlockSpec((HALO_ROWS, width), lambda bi, i: (jnp.minimum((bi * nt + i + 1) * per, last), col_block))]


def _split3(v):
    hi = v.astype(BF16)
    r1 = v - hi.astype(F32)
    mid = r1.astype(BF16)
    lo = (r1 - mid.astype(F32)).astype(BF16)
    return hi, mid, lo


def _gdn_prep_body(x_ref, xp_ref, xn_ref, misc_ref, cw_ref, alog_ref, dt_ref, gmask_ref, tp_ref, ts_ref,
                   q_out, k_out, v_out, gcf_out, gcb_out, beta_out):
    prev, nxt = _halo_rows(xp_ref, xn_ref)
    y = _conv3(x_ref[...].astype(F32), prev, nxt, cw_ref)
    y = y * _sigmoid(y)
    hd = GDN_HEAD_DIM
    for h in range(GDN_HEADS):
        qh = y[:, h * hd:(h + 1) * hd]
        kh = y[:, GDN_W + h * hd:GDN_W + (h + 1) * hd]
        qn = qh * lax.rsqrt(jnp.sum(qh * qh, axis=-1, keepdims=True) + NORM_EPS) * hd ** -0.5
        kn = kh * lax.rsqrt(jnp.sum(kh * kh, axis=-1, keepdims=True) + NORM_EPS)
        q_out[:, h * hd:(h + 1) * hd] = qn.astype(BF16)
        k_out[:, h * hd:(h + 1) * hd] = kn.astype(BF16)
    v_out[...] = y[:, 2 * GDN_W:3 * GDN_W].astype(BF16)
    m = misc_ref[...]
    a = m + dt_ref[...]
    softplus = jnp.maximum(a, 0.0) + jnp.log(1.0 + jnp.exp(-jnp.abs(a)))
    g = -(jnp.exp(alog_ref[...]) * gmask_ref[...]) * softplus
    beta_out[...] = _sigmoid(m)
    parts = _split3(g)
    gcf_out[...] = sum(jnp.dot(tp_ref[...], p, preferred_element_type=F32) for p in parts)
    gcb_out[...] = sum(jnp.dot(ts_ref[...], p, preferred_element_type=F32) for p in parts)


def gdn_prepare(main, misc, b, length, conv_w, a_log, dt_bias):
    n = b * length
    w3 = 3 * GDN_W
    tm = min(256, length)
    nt = length // tm
    lane0 = MLA_ROPE
    vec = lambda v: jnp.zeros((1, MISC_W), F32).at[0, lane0:lane0 + 2 * GDN_HEADS].set(v.reshape(-1))
    alog, dtb = vec(a_log), vec(dt_bias)
    gmask = vec(jnp.ones((2 * GDN_HEADS,), F32))
    r = np.arange(tm)
    same = (r[:, None] // GDN_CHUNK) == (r[None, :] // GDN_CHUNK)
    tpre = jnp.asarray(same & (r[None, :] <= r[:, None]), BF16)
    tsuf = jnp.asarray(same & (r[None, :] >= r[:, None]), BF16)
    const = lambda a: pl.BlockSpec(a.shape, lambda bi, i: (0,) * a.ndim)
    row = lambda width: pl.BlockSpec((tm, width), lambda bi, i: (bi * nt + i, 0))
    cw = conv_w.astype(F32)
    return pl.pallas_call(
        _gdn_prep_body,
        grid=(b, nt),
        in_specs=_halo_specs(tm, w3, OFF_GDN // w3, nt, n)
                 + [row(MISC_W), const(cw), const(alog), const(dtb), const(gmask), const(tpre), const(tsuf)],
        out_specs=[row(GDN_W), row(GDN_W), row(GDN_W), row(MISC_W), row(MISC_W), row(MISC_W)],
        out_shape=[jax.ShapeDtypeStruct((n, GDN_W), BF16)] * 3 + [jax.ShapeDtypeStruct((n, MISC_W), F32)] * 3,
        compiler_params=_cp(("parallel", "parallel")),
        name="gdn_prep",
    )(main, main, main, misc, cw, alog, dtb, gmask, tpre, tsuf)


def _gdn_chunk_body(qf_ref, kf_ref, vf_ref, qb_ref, kb_ref, vb_ref, gcf_ref, gcb_ref, bcf_ref, bcb_ref,
                    grf_ref, grb_ref, s0_ref, *rest, nc, with_out, bpb):
    if with_out:
        of_ref, ob_ref, sfin_ref, s_ref = rest
    else:
        sfin_ref, s_ref = rest
        of_ref = ob_ref = None
    c = pl.program_id(1)
    nst = 2 * GDN_HEADS

    @pl.when(c == 0)
    def _():
        s_ref[...] = s0_ref[...].reshape(s_ref.shape)

    ch = GDN_CHUNK
    hd = GDN_HEAD_DIM
    ii = lax.broadcasted_iota(jnp.int32, (ch, ch), 0)
    jj = lax.broadcasted_iota(jnp.int32, (ch, ch), 1)
    nt_dims = (((1,), (1,)), ((), ()))
    tn_dims = (((0,), (0,)), ((), ()))
    bdot = lambda a, b_: jnp.dot(a.astype(BF16), b_.astype(BF16), preferred_element_type=F32)
    eye = jnp.where(ii == jj, 1.0, 0.0)
    pair_masks = [((ii >> (l + 1)) == (jj >> (l + 1))) & ((ii >> l) != (jj >> l))
                  for l in range(int(math.log2(ch)))]
    dirs = ((qf_ref, kf_ref, vf_ref, gcf_ref, bcf_ref, grf_ref, of_ref, ii >= jj, ii > jj, ch - 1),
            (qb_ref, kb_ref, vb_ref, gcb_ref, bcb_ref, grb_ref, ob_ref, ii <= jj, ii < jj, 0))
    chains = []
    for bb in range(bpb):
        for d, (q_ref, k_ref, v_ref, gc_ref, bc_ref, gr_ref, o_ref, incl, strict, last_row) in enumerate(dirs):
            for h in range(GDN_HEADS):
                j = d * GDN_HEADS + h
                cols = slice(h * hd, (h + 1) * hd)
                cn = dict(bb=bb, j=j, cols=cols, o_ref=o_ref, incl=incl, strict=strict)
                cn['q'], cn['k'], cn['v'] = q_ref[bb, :, cols], k_ref[bb, :, cols], v_ref[bb, :, cols]
                cn['gc'] = gc_ref[bb, :, j:j + 1]
                cn['gr'] = gr_ref[bb, 0, j:j + 1, :]
                cn['beta'] = bc_ref[bb, :, j:j + 1]
                cn['g_last'] = gc_ref[bb, last_row:last_row + 1, j:j + 1]
                chains.append(cn)
    for cn in chains:
        incl = cn['incl']
        cn['decay'] = jnp.where(incl, jnp.exp(jnp.where(incl, cn['gc'] - cn['gr'], 0.0)), 0.0)
        cn['kf'] = cn['k'].astype(F32)
        cn['kbeta'] = cn['kf'] * cn['beta']
    for cn in chains:
        kk = lax.dot_general(cn['kbeta'].astype(BF16), cn['k'], nt_dims, preferred_element_type=F32)
        cn['a'] = jnp.where(cn['strict'], kk * cn['decay'], 0.0)
    for cn in chains:
        cn['t'] = eye - jnp.where(pair_masks[0], cn['a'], 0.0)
    for pm in pair_masks[1:]:
        for cn in chains:
            cn['tmp'] = bdot(cn['t'], jnp.where(pm, cn['a'], 0.0))
        for cn in chains:
            cn['t'] = cn['t'] - bdot(cn['tmp'], cn['t'])
    for cn in chains:
        rhs = jnp.concatenate([cn['v'].astype(F32) * cn['beta'], cn['kbeta'] * jnp.exp(cn['gc'])], axis=1)
        cn['x'] = bdot(cn['t'], rhs)
    for cn in chains:
        cn['s'] = s_ref[cn['bb'] * nst + cn['j']]
        cn['v_new'] = cn['x'][:, :hd] - bdot(cn['x'][:, hd:], cn['s'])
    if with_out:
        for cn in chains:
            qk = lax.dot_general(cn['q'], cn['k'], nt_dims, preferred_element_type=F32)
            qk = jnp.where(cn['incl'], qk * cn['decay'], 0.0)
            o = bdot(cn['q'].astype(F32) * jnp.exp(cn['gc']), cn['s']) + bdot(qk, cn['v_new'])
            cn['o_ref'][cn['bb'], :, cn['cols']] = o
    for cn in chains:
        kdec = cn['kf'] * jnp.exp(cn['g_last'] - cn['gc'])
        s_ref[cn['bb'] * nst + cn['j']] = cn['s'] * jnp.exp(cn['g_last']) + lax.dot_general(
            kdec.astype(BF16), cn['v_new'].astype(BF16), tn_dims, preferred_element_type=F32)

    @pl.when(c == nc - 1)
    def _():
        sfin_ref[...] = s_ref[...].reshape(sfin_ref.shape)


def gdn_scan(q, k, v, gcol, bcol, grow, s0, b, length, with_out):
    n = b * length
    ch = GDN_CHUNK
    nc = length // ch
    nst = 2 * GDN_HEADS
    bpb = min(GDN_BATCHES_PER_STEP, b)
    fwd = lambda bg, c: (bg, c, 0)
    bwd = lambda bg, c: (bg, nc - 1 - c, 0)
    fwd4 = lambda bg, c: (bg, c, 0, 0)
    bwd4 = lambda bg, c: (bg, nc - 1 - c, 0, 0)
    wide = lambda f: pl.BlockSpec((bpb, ch, GDN_W), f)
    narrow = lambda f: pl.BlockSpec((bpb, ch, nst), f)
    rows = lambda f: pl.BlockSpec((bpb, 1, nst, ch), f)
    state = pl.BlockSpec((bpb, nst, GDN_HEAD_DIM, GDN_HEAD_DIM), lambda bg, c: (bg, 0, 0, 0))
    out_specs = [state]
    out_shape = [jax.ShapeDtypeStruct((b, nst, GDN_HEAD_DIM, GDN_HEAD_DIM), F32)]
    if with_out:
        out_specs = [wide(fwd), wide(bwd)] + out_specs
        out_shape = [jax.ShapeDtypeStruct((b, length, GDN_W), F32)] * 2 + out_shape
    body = functools.partial(_gdn_chunk_body, nc=nc, with_out=with_out, bpb=bpb)
    q3, k3, v3 = (t.reshape(b, length, GDN_W) for t in (q, k, v))
    gcol3, bcol3 = gcol.reshape(b, length, nst), bcol.reshape(b, length, nst)
    grow4 = grow.reshape(b, nc, nst, ch)
    outs = pl.pallas_call(
        body,
        grid=(b // bpb, nc),
        in_specs=[wide(fwd), wide(fwd), wide(fwd), wide(bwd), wide(bwd), wide(bwd),
                  narrow(fwd), narrow(bwd), narrow(fwd), narrow(bwd), rows(fwd4), rows(bwd4), state],
        out_specs=out_specs,
        out_shape=out_shape,
        scratch_shapes=[pltpu.VMEM((bpb * nst, GDN_HEAD_DIM, GDN_HEAD_DIM), F32)],
        compiler_params=_cp(("parallel", "arbitrary")),
        name="gdn_scan",
    )(q3, k3, v3, q3, k3, v3, gcol3, gcol3, bcol3, bcol3, grow4, grow4, s0)
    if with_out:
        return outs[0].reshape(n, GDN_W), outs[1].reshape(n, GDN_W), outs[2]
    return outs


def _gdn_out_body(of_ref, ob_ref, z_ref, nw_ref, y_ref):
    o = of_ref[...] + ob_ref[...]
    z = z_ref[...].astype(F32)
    hd = GDN_HEAD_DIM
    for h in range(GDN_HEADS):
        cols = slice(h * hd, (h + 1) * hd)
        oh = o[:, cols]
        yh = oh * lax.rsqrt(jnp.mean(oh * oh, axis=-1, keepdims=True) + NORM_EPS) * nw_ref[...]
        zh = z[:, cols]
        y_ref[:, cols] = (yh * (zh * _sigmoid(zh))).astype(BF16)


def gdn_output_gate(o_f, o_b, main, norm_w):
    n = o_f.shape[0]
    tm = min(512, n)
    nw = norm_w.reshape(1, GDN_HEAD_DIM).astype(F32)
    return pl.pallas_call(
        _gdn_out_body,
        grid=(n // tm,),
        in_specs=[pl.BlockSpec((tm, GDN_W), lambda i: (i, 0)),
                  pl.BlockSpec((tm, GDN_W), lambda i: (i, 0)),
                  pl.BlockSpec((tm, GDN_W), lambda i: (i, (OFF_GDN + 3 * GDN_W) // GDN_W)),
                  pl.BlockSpec((1, GDN_HEAD_DIM), lambda i: (0, 0))],
        out_specs=pl.BlockSpec((tm, GDN_W), lambda i: (i, 0)),
        out_shape=jax.ShapeDtypeStruct((n, GDN_W), BF16),
        compiler_params=_cp(("parallel",)),
        name="gdn_out",
    )(o_f, o_b, main, nw)


def gdn_branch(main_l, misc_l, main_c, misc_c, b, seq, ctx_len, conv_w, a_log, dt_bias, norm_w, ctx_out):
    nst = 2 * GDN_HEADS
    lane0 = MLA_ROPE

    def gates(gcf, gcb, beta, length):
        gcol = jnp.concatenate([gcf[:, lane0:lane0 + GDN_HEADS], gcb[:, lane0 + GDN_HEADS:lane0 + nst]], axis=1)
        bcol = beta[:, lane0 + nst:lane0 + 2 * nst]
        grow = jnp.transpose(gcol.reshape(-1, GDN_CHUNK, nst), (0, 2, 1))
        return gcol, bcol, grow

    qc, kc, vc, gcf, gcb, beta = gdn_prepare(main_c, misc_c, b, ctx_len, conv_w, a_log, dt_bias)
    gcol_c, bcol_c, grow_c = gates(gcf, gcb, beta, ctx_len)
    ql, kl, vl, gcf, gcb, beta = gdn_prepare(main_l, misc_l, b, seq, conv_w, a_log, dt_bias)
    gcol_l, bcol_l, grow_l = gates(gcf, gcb, beta, seq)
    s0 = jnp.zeros((b, nst, GDN_HEAD_DIM, GDN_HEAD_DIM), F32)
    outs_c = gdn_scan(qc, kc, vc, gcol_c, bcol_c, grow_c, s0, b, ctx_len, ctx_out)
    s_ctx = outs_c[-1]
    of_l, ob_l, _ = gdn_scan(ql, kl, vl, gcol_l, bcol_l, grow_l, s_ctx, b, seq, True)
    out_l = gdn_output_gate(of_l, ob_l, main_l, norm_w)
    out_c = gdn_output_gate(outs_c[0], outs_c[1], main_c, norm_w) if ctx_out else None
    return out_l, out_c


def _hy_conv_body(x_ref, xp_ref, xn_ref, cw_ref, v_out, x1_out, x2_out):
    prev, nxt = _halo_rows(xp_ref, xn_ref)
    y = _conv3(x_ref[...].astype(F32), prev, nxt, cw_ref)
    w = HY_WIDTH
    v_out[...] = y[:, :w].astype(BF16)
    x1_out[...] = y[:, w:2 * w].astype(BF16)
    x2_out[...] = y[:, 2 * w:3 * w].astype(BF16)


def hyena_short_conv(main, b, length, conv_w):
    n = b * length
    w3 = (HY_ORDER + 1) * HY_WIDTH
    tm = min(256, length)
    nt = length // tm
    cw = conv_w.astype(F32)
    row = pl.BlockSpec((tm, HY_WIDTH), lambda bi, i: (bi * nt + i, 0))
    return pl.pallas_call(
        _hy_conv_body,
        grid=(b, nt),
        in_specs=_halo_specs(tm, w3, OFF_HY // w3, nt, n) + [pl.BlockSpec(cw.shape, lambda bi, i: (0, 0))],
        out_specs=[row, row, row],
        out_shape=[jax.ShapeDtypeStruct((n, HY_WIDTH), BF16)] * 3,
        compiler_params=_cp(("parallel", "parallel")),
        name="hyena_conv",
    )(main, main, main, cw)


def _dft_consts(length):
    n = 2 * length
    n2 = 64 if length >= 2048 else 16
    n1 = n // n2
    nk1 = n1 // 2 + 8
    k1 = np.arange(nk1)
    t1 = np.arange(n1 // 2)
    ang1 = 2.0 * np.pi * np.outer(k1, t1) / n1
    f_first = np.concatenate([np.cos(ang1), -np.sin(ang1)], axis=0)
    t2 = np.arange(n2)
    ang2 = 2.0 * np.pi * np.outer(t2, t2) / n2
    c2, s2 = np.cos(ang2), np.sin(ang2)
    g_fwd = np.block([[c2, s2], [-s2, c2]])
    g_inv = g_fwd.T
    angt = 2.0 * np.pi * np.outer(k1, t2) / n
    tw_r, tw_i = np.cos(angt)[:, :, None], -np.sin(angt)[:, :, None]
    tt = np.arange(n1 // 4, 3 * n1 // 4)
    ang3 = 2.0 * np.pi * np.outer(tt, k1) / n1
    fold = np.where((k1 == 0) | (k1 == n1 // 2), 1.0, np.where(k1 < n1 // 2, 2.0, 0.0))[None, :]
    f_last = np.concatenate([np.cos(ang3) * fold, -np.sin(ang3) * fold], axis=1) / n
    bf = lambda a: jnp.asarray(a, F32).astype(BF16)
    return dict(n1=n1, nk1=nk1, n2=n2, f_first=bf(f_first), g_fwd=bf(g_fwd), g_inv=bf(g_inv),
                tw_r=jnp.asarray(tw_r, F32), tw_i=jnp.asarray(tw_i, F32), f_last=bf(f_last))


def _hy_first_body(f_ref, z_ref, a_ref):
    a_ref[0] = jnp.dot(f_ref[...], z_ref[0], preferred_element_type=F32).astype(BF16)


def hyena_dft_first(zv, consts):
    b, half, cols = zv.shape
    n1 = consts['nk1']
    tn = min(4096, cols)
    f = consts['f_first']
    return pl.pallas_call(
        _hy_first_body,
        grid=(b, cols // tn),
        in_specs=[pl.BlockSpec(f.shape, lambda bi, j: (0, 0)),
                  pl.BlockSpec((1, half, tn), lambda bi, j: (bi, 0, j))],
        out_specs=pl.BlockSpec((1, 2 * n1, tn), lambda bi, j: (bi, 0, j)),
        out_shape=jax.ShapeDtypeStruct((b, 2 * n1, cols), BF16),
        compiler_params=_cp(("parallel", "parallel")),
        name="hyena_dft_first",
    )(f, zv)


def _hy_mid_body(a_ref, twr_ref, twi_ref, gf_ref, *rest, kt, spectrum_only):
    if spectrum_only:
        (o_ref,) = rest
    else:
        gi_ref, h_ref, o_ref = rest
    n2 = gf_ref.shape[0] // 2

    ks = range(kt)
    tw = [(twr_ref[i], twi_ref[i]) for i in ks]
    a = [(a_ref[0, 0, i].astype(F32), a_ref[0, 1, i].astype(F32)) for i in ks]
    b = [jnp.concatenate([ar * twr - ai * twi, ar * twi + ai * twr], axis=0).astype(BF16)
         for (ar, ai), (twr, twi) in zip(a, tw)]
    z = [jnp.dot(gf_ref[...], bb, preferred_element_type=F32) for bb in b]
    if spectrum_only:
        for i in ks:
            o_ref[0, 0, i] = z[i][:n2]
            o_ref[0, 1, i] = z[i][n2:]
        return
    y = []
    for i in ks:
        zr, zi = z[i][:n2], z[i][n2:]
        hr, hi = h_ref[0, i], h_ref[1, i]
        y.append(jnp.concatenate([zr * hr - zi * hi, zr * hi + zi * hr], axis=0).astype(BF16))
    w = [jnp.dot(gi_ref[...], yy, preferred_element_type=F32) for yy in y]
    for i in ks:
        wr, wi = w[i][:n2], w[i][n2:]
        twr, twi = tw[i]
        o_ref[0, 0, i] = (wr * twr + wi * twi).astype(BF16)
        o_ref[0, 1, i] = (wi * twr - wr * twi).astype(BF16)


def hyena_dft_mid(a5, consts, spectrum=None):
    b, _, n1, n2, c = a5.shape
    kt = 8
    only = spectrum is None
    blk = pl.BlockSpec((1, 2, kt, n2, c), lambda bi, j: (bi, 0, j, 0, 0))
    tw = pl.BlockSpec((kt, n2, 1), lambda bi, j: (j, 0, 0))
    g = pl.BlockSpec((2 * n2, 2 * n2), lambda bi, j: (0, 0))
    in_specs = [blk, tw, tw, g]
    args = [a5, consts['tw_r'], consts['tw_i'], consts['g_fwd']]
    if not only:
        in_specs += [g, pl.BlockSpec((2, kt, n2, c), lambda bi, j: (0, j, 0, 0))]
        args += [consts['g_inv'], spectrum]
    body = functools.partial(_hy_mid_body, kt=kt, spectrum_only=only)
    return pl.pallas_call(
        body,
        grid=(b, n1 // kt),
        in_specs=in_specs,
        out_specs=blk,
        out_shape=jax.ShapeDtypeStruct(a5.shape, F32 if only else BF16),
        compiler_params=_cp(("parallel", "parallel")),
        name="hyena_dft_mid",
    )(*args)


def _hy_last_body(f_ref, b_ref, z_ref, x_ref, bias_ref, o_ref):
    y = jnp.dot(f_ref[...], b_ref[0], preferred_element_type=F32)
    z = z_ref[0].astype(F32)
    o_ref[0] = (x_ref[0].astype(F32) * (y + bias_ref[...] * z)).astype(BF16)


def hyena_dft_last(bv, zv, xv, bias_row, consts):
    b, rows2, cols = bv.shape
    half = consts['n1'] // 2
    tn = min(4096, cols)
    f = consts['f_last']
    sig = pl.BlockSpec((1, half, tn), lambda bi, j: (bi, 0, j))
    return pl.pallas_call(
        _hy_last_body,
        grid=(b, cols // tn),
        in_specs=[pl.BlockSpec(f.shape, lambda bi, j: (0, 0)),
                  pl.BlockSpec((1, rows2, tn), lambda bi, j: (bi, 0, j)),
                  sig, sig,
                  pl.BlockSpec((1, tn), lambda bi, j: (0, j))],
        out_specs=sig,
        out_shape=jax.ShapeDtypeStruct((b, half, cols), BF16),
        compiler_params=_cp(("parallel", "parallel")),
        name="hyena_dft_last",
    )(f, bv, zv, xv, bias_row)


def hyena_branch(main, b, length, conv_w, filt, bias):
    consts = _dft_consts(length)
    n1, nk1, n2 = consts['n1'], consts['nk1'], consts['n2']
    c = HY_WIDTH
    cols = n2 * c
    view = lambda t: t.reshape(b, n1 // 2, cols)
    v, x1, x2 = (view(t) for t in hyena_short_conv(main, b, length, conv_w))
    hv = jnp.transpose(filt, (1, 0, 2)).astype(BF16).reshape(HY_ORDER, n1 // 2, cols)
    h_first = hyena_dft_first(hv, consts).reshape(HY_ORDER, 2, nk1, n2, c)
    spectra = hyena_dft_mid(h_first, consts)
    z = v
    for o, gate in enumerate((x1, x2)):
        a5 = hyena_dft_first(z, consts).reshape(b, 2, nk1, n2, c)
        bm = hyena_dft_mid(a5, consts, spectra[o]).reshape(b, 2 * nk1, cols)
        bias_row = jnp.tile(bias[o].astype(F32), n2).reshape(1, cols)
        z = hyena_dft_last(bm, z, gate, bias_row, consts)
    return z.reshape(b * length, c)


def _ada_body(c_ref, w_ref, b_ref, o_ref):
    cv = c_ref[...]
    s = cv * _sigmoid(cv)
    o_ref[0] = jnp.dot(s, w_ref[0], precision=lax.Precision.HIGHEST, preferred_element_type=F32) + b_ref[0]


def ada_modulation(c, c_ctx, w_ada, b_ada):
    depth, d, d6 = w_ada.shape
    c8 = jnp.zeros((MOD_ROWS, d), F32).at[:c.shape[0]].set(c).at[CTX_MOD_ROW].set(c_ctx)
    tn = 512
    out = pl.pallas_call(
        _ada_body,
        grid=(depth, d6 // tn),
        in_specs=[pl.BlockSpec((MOD_ROWS, d), lambda l, j: (0, 0)),
                  pl.BlockSpec((1, d, tn), lambda l, j: (l, 0, j)),
                  pl.BlockSpec((1, 1, tn), lambda l, j: (l, 0, j))],
        out_specs=pl.BlockSpec((1, MOD_ROWS, tn), lambda l, j: (l, 0, j)),
        out_shape=jax.ShapeDtypeStruct((depth, MOD_ROWS, d6), F32),
        compiler_params=_cp(("parallel", "parallel")),
        name="ada_mod",
    )(c8, w_ada, b_ada.reshape(depth, 1, d6))
    return out.reshape(depth, MOD_ROWS, 6, d)


def _row_tile(cap, n_rows, per_batch):
    return min(cap, n_rows if per_batch is None else per_batch)


def _mod_row_fn(n_rows, tm, per_batch):
    if per_batch is None:
        return lambda i: CTX_MOD_ROW
    tiles = per_batch // tm
    return lambda i: i // tiles


def _normmod_body(x_ref, nw_ref, sh_ref, sc_ref, o_ref):
    xf = x_ref[...]
    y = xf * lax.rsqrt(jnp.mean(xf * xf, axis=-1, keepdims=True) + NORM_EPS) * nw_ref[...]
    o_ref[...] = (y * (1.0 + sc_ref[0]) + sh_ref[0]).astype(o_ref.dtype)


def norm_modulate(x, nw, shift, scale, per_batch, out_dtype=BF16):
    n, d = x.shape
    tm = _row_tile(512, n, per_batch)
    rf = _mod_row_fn(n, tm, per_batch)
    return pl.pallas_call(
        _normmod_body,
        grid=(n // tm,),
        in_specs=[pl.BlockSpec((tm, d), lambda i: (i, 0)),
                  pl.BlockSpec((1, d), lambda i: (0, 0)),
                  pl.BlockSpec((1, 1, d), lambda i: (rf(i), 0, 0)),
                  pl.BlockSpec((1, 1, d), lambda i: (rf(i), 0, 0))],
        out_specs=pl.BlockSpec((tm, d), lambda i: (i, 0)),
        out_shape=jax.ShapeDtypeStruct((n, d), out_dtype),
        compiler_params=_cp(("parallel",)),
        name="norm_mod",
    )(x, nw.reshape(1, d), shift.reshape(MOD_ROWS, 1, d), scale.reshape(MOD_ROWS, 1, d))


def _mm_body(a_ref, w_ref, o_ref):
    o_ref[...] = jnp.dot(a_ref[...], w_ref[...], preferred_element_type=F32).astype(o_ref.dtype)


def matmul(a, w, out_dtype, tn):
    n, k = a.shape
    m = w.shape[1]
    tm = min(2048, n)
    return pl.pallas_call(
        _mm_body,
        grid=(n // tm, m // tn),
        in_specs=[pl.BlockSpec((tm, k), lambda i, j: (i, 0)),
                  pl.BlockSpec((k, tn), lambda i, j: (0, j))],
        out_specs=pl.BlockSpec((tm, tn), lambda i, j: (i, j)),
        out_shape=jax.ShapeDtypeStruct((n, m), out_dtype),
        compiler_params=_cp(("parallel", "parallel")),
        name="proj",
    )(a, w)


def _mm_res_body(a_ref, w_ref, x_ref, g_ref, o_ref):
    y = jnp.dot(a_ref[...], w_ref[...], preferred_element_type=F32)
    o_ref[...] = x_ref[...] + g_ref[0] * y


def matmul_gated_residual(a, w, x, gate, per_batch):
    n, k = a.shape
    d = w.shape[1]
    tm = _row_tile(1024, n, per_batch)
    tn = min(1024, d)
    rf = _mod_row_fn(n, tm, per_batch)
    return pl.pallas_call(
        _mm_res_body,
        grid=(n // tm, d // tn),
        in_specs=[pl.BlockSpec((tm, k), lambda i, j: (i, 0)),
                  pl.BlockSpec((k, tn), lambda i, j: (0, j)),
                  pl.BlockSpec((tm, tn), lambda i, j: (i, j)),
                  pl.BlockSpec((1, 1, tn), lambda i, j: (rf(i), 0, j))],
        out_specs=pl.BlockSpec((tm, tn), lambda i, j: (i, j)),
        out_shape=jax.ShapeDtypeStruct((n, d), F32),
        compiler_params=_cp(("parallel", "parallel")),
        name="out_proj_residual",
    )(a, w, x, gate.reshape(MOD_ROWS, 1, d))


def _merge_body(h_ref, *refs):
    b_refs, wg_refs = refs[:N_BRANCH], refs[N_BRANCH:2 * N_BRANCH]
    wb_ref, o_ref = refs[2 * N_BRANCH:]
    h = h_ref[...]
    acc = None
    for n in range(N_BRANCH):
        gate = jnp.dot(h, wg_refs[n][...], preferred_element_type=F32)
        proj = jnp.dot(b_refs[n][...], wb_ref[n], preferred_element_type=F32)
        term = _sigmoid(gate) * proj
        acc = term if acc is None else acc + term
    o_ref[...] = acc.astype(o_ref.dtype)


def merge_branches_gated(h, branches, w_gate, w_branch):
    n, d = h.shape
    bw = branches[0].shape[1]
    tm = min(1024, n)
    tn = min(512, d)
    nj = d // tn
    gate_spec = lambda b: pl.BlockSpec((d, tn), lambda i, j: (0, b * nj + j))
    return pl.pallas_call(
        _merge_body,
        grid=(n // tm, nj),
        in_specs=[pl.BlockSpec((tm, d), lambda i, j: (i, 0))]
                 + [pl.BlockSpec((tm, bw), lambda i, j: (i, 0))] * N_BRANCH
                 + [gate_spec(b) for b in range(N_BRANCH)]
                 + [pl.BlockSpec((N_BRANCH, bw, tn), lambda i, j: (0, 0, j))],
        out_specs=pl.BlockSpec((tm, tn), lambda i, j: (i, j)),
        out_shape=jax.ShapeDtypeStruct((n, d), BF16),
        compiler_params=_cp(("parallel", "parallel")),
        name="gate_merge",
    )(h, *branches, *([w_gate] * N_BRANCH), w_branch)


def _mla_prep_body(cq_ref, ckv_ref, misc_ref, qnw_ref, kvnw_ref, wqa_ref, wqb_ref, wk_ref, wv_ref,
                   ska_ref, skb_ref, cq_tab, sq_tab, q_out, k_out, v_out):
    def norm(v, w_ref):
        vf = v.astype(F32)
        return (vf * lax.rsqrt(jnp.mean(vf * vf, axis=-1, keepdims=True) + NORM_EPS) * w_ref[...]).astype(BF16)

    xq = norm(cq_ref[...], qnw_ref)
    xkv = norm(ckv_ref[...], kvnw_ref)
    cos, sin = cq_tab[...], sq_tab[...]
    misc = misc_ref[...].astype(BF16)
    kr = (jnp.dot(misc, ska_ref[...], preferred_element_type=F32) * cos
          + jnp.dot(misc, skb_ref[...], preferred_element_type=F32) * sin)
    for h in range(MLA_HEADS):
        qa = jnp.dot(xq, wqa_ref[h], preferred_element_type=F32)
        qb = jnp.dot(xq, wqb_ref[h], preferred_element_type=F32)
        q_out[0, h] = (qa * cos + qb * sin).astype(BF16)
        k_out[0, h] = (jnp.dot(xkv, wk_ref[h], preferred_element_type=F32) + kr).astype(BF16)
        v_out[0, h] = jnp.dot(xkv, wv_ref[h], preferred_element_type=F32).astype(BF16)


def _mla_weights(q_norm_w, kv_norm_w, w_uq, w_ukv):
    dk = MLA_NOPE + MLA_ROPE
    half = MLA_ROPE // 2
    scale = dk ** -0.5
    wq = jnp.transpose(w_uq, (1, 0, 2)) * scale
    nope0 = jnp.zeros(wq.shape[:2] + (MLA_NOPE,), F32)
    wq_rot = jnp.concatenate([nope0, -wq[..., MLA_NOPE + half:], wq[..., MLA_NOPE:MLA_NOPE + half]], axis=-1)
    wkv = jnp.transpose(w_ukv, (1, 0, 2))
    wk = jnp.concatenate([wkv[..., :MLA_NOPE], jnp.zeros(wkv.shape[:2] + (MLA_ROPE,), F32)], axis=-1)
    wv = wkv[..., MLA_NOPE:]
    eye = jnp.eye(MLA_ROPE, dtype=F32)
    rot = jnp.concatenate([-eye[:, half:], eye[:, :half]], axis=-1)
    pad_r = MISC_W - MLA_ROPE
    ska = jnp.pad(eye, ((0, pad_r), (MLA_NOPE, 0)))
    skb = jnp.pad(rot, ((0, pad_r), (MLA_NOPE, 0)))
    return tuple(t.astype(BF16) for t in (wq, wq_rot, wk, wv, ska, skb))


def _mla_tables(rope, length):
    dk = MLA_NOPE + MLA_ROPE
    if rope is None:
        return jnp.ones((length, dk), F32), jnp.zeros((length, dk), F32)
    cos, sin = rope
    ones = jnp.ones((length, MLA_NOPE), F32)
    return (jnp.concatenate([ones, cos, cos], axis=-1),
            jnp.concatenate([0.0 * ones, sin, sin], axis=-1))


def mla_prepare(main, misc, b, length, weights, q_norm_w, kv_norm_w, tables):
    wq, wq_rot, wk, wv, ska, skb = weights
    cos, sin = tables
    dk = MLA_NOPE + MLA_ROPE
    tm = min(512, length)
    nt = length // tm
    full = lambda a: pl.BlockSpec(a.shape, lambda bi, i: (0,) * a.ndim)
    qnw = q_norm_w.reshape(1, -1)
    kvnw = kv_norm_w.reshape(1, -1)
    outs = pl.pallas_call(
        _mla_prep_body,
        grid=(b, nt),
        in_specs=[pl.BlockSpec((tm, MLA_Q_LORA), lambda bi, i: (bi * nt + i, OFF_CQ // MLA_Q_LORA)),
                  pl.BlockSpec((tm, MLA_KV_LORA), lambda bi, i: (bi * nt + i, OFF_CKV // MLA_KV_LORA)),
                  pl.BlockSpec((tm, MISC_W), lambda bi, i: (bi * nt + i, 0)),
                  full(qnw), full(kvnw), full(wq), full(wq_rot), full(wk), full(wv), full(ska), full(skb),
                  pl.BlockSpec((tm, dk), lambda bi, i: (i, 0)),
                  pl.BlockSpec((tm, dk), lambda bi, i: (i, 0))],
        out_specs=[pl.BlockSpec((1, MLA_HEADS, tm, dk), lambda bi, i: (bi, 0, i, 0)),
                   pl.BlockSpec((1, MLA_HEADS, tm, dk), lambda bi, i: (bi, 0, i, 0)),
                   pl.BlockSpec((1, MLA_HEADS, tm, MLA_V), lambda bi, i: (bi, 0, i, 0))],
        out_shape=[jax.ShapeDtypeStruct((b, MLA_HEADS, length, dk), BF16),
                   jax.ShapeDtypeStruct((b, MLA_HEADS, length, dk), BF16),
                   jax.ShapeDtypeStruct((b, MLA_HEADS, length, MLA_V), BF16)],
        compiler_params=_cp(("parallel", "parallel")),
        name="mla_prep",
    )(main, main, misc, qnw, kvnw, wq, wq_rot, wk, wv, ska, skb, cos, sin)
    return outs


def _gqa_prep_body(q_ref, k_ref, v_ref, qnw_ref, knw_ref, gsum_ref, rot_ref, cos_ref, sin_ref,
                   q_out, k_out, v_out):
    cos, sin = cos_ref[...], sin_ref[...]

    def prep(v, nw, width):
        vf = v.astype(F32)
        sq = vf * vf
        hi = sq.astype(BF16)
        lo = (sq - hi.astype(F32)).astype(BF16)
        g = gsum_ref[:width, :width]
        ss = jnp.dot(hi, g, preferred_element_type=F32) + jnp.dot(lo, g, preferred_element_type=F32)
        xn = vf * lax.rsqrt(ss * (1.0 / GQA_HEAD_DIM) + NORM_EPS) * nw
        xr = jnp.dot(xn.astype(BF16), rot_ref[:width, :width], preferred_element_type=F32)
        return xn * cos[:, :width] + xr * sin[:, :width]

    qf = prep(q_ref[...], qnw_ref[...], GQA_HEADS * GQA_HEAD_DIM) * GQA_HEAD_DIM ** -0.5
    kf = prep(k_ref[...], knw_ref[...], LANES)
    q_out[...] = qf.astype(BF16)
    k_out[...] = kf.astype(BF16)
    v_out[...] = v_ref[...]


def gqa_prepare(main, b, length, q_norm_w, k_norm_w, rope):
    n = b * length
    qw = GQA_HEADS * GQA_HEAD_DIM
    kw = GQA_KV_HEADS * GQA_HEAD_DIM
    half = GQA_HEAD_DIM // 2
    if rope is None:
        cos = jnp.ones((length, qw), F32)
        sin = jnp.zeros((length, qw), F32)
    else:
        cos = jnp.tile(jnp.concatenate([rope[0], rope[0]], axis=-1), (1, GQA_HEADS))
        sin = jnp.tile(jnp.concatenate([rope[1], rope[1]], axis=-1), (1, GQA_HEADS))
    head = np.arange(qw) // GQA_HEAD_DIM
    gsum = jnp.asarray(head[:, None] == head[None, :], BF16)
    eye = np.eye(GQA_HEAD_DIM, dtype=np.float32)
    rot1 = np.concatenate([-eye[:, half:], eye[:, :half]], axis=-1)
    rot = jnp.asarray(np.kron(np.eye(GQA_HEADS, dtype=np.float32), rot1), BF16)
    tm = min(512, length)
    nt = length // tm
    full = lambda a: pl.BlockSpec(a.shape, lambda bi, i: (0,) * a.ndim)
    qnw = jnp.tile(q_norm_w, GQA_HEADS).reshape(1, qw)
    knw = jnp.tile(k_norm_w, GQA_KV_HEADS).reshape(1, kw)
    return pl.pallas_call(
        _gqa_prep_body,
        grid=(b, nt),
        in_specs=[pl.BlockSpec((tm, qw), lambda bi, i: (bi * nt + i, OFF_GQ // qw)),
                  pl.BlockSpec((tm, kw), lambda bi, i: (bi * nt + i, OFF_GK // kw)),
                  pl.BlockSpec((tm, kw), lambda bi, i: (bi * nt + i, OFF_GV // kw)),
                  full(qnw), full(knw), full(gsum), full(rot),
                  pl.BlockSpec((tm, qw), lambda bi, i: (i, 0)),
                  pl.BlockSpec((tm, qw), lambda bi, i: (i, 0))],
        out_specs=[pl.BlockSpec((tm, qw), lambda bi, i: (bi * nt + i, 0)),
                   pl.BlockSpec((tm, kw), lambda bi, i: (bi * nt + i, 0)),
                   pl.BlockSpec((tm, kw), lambda bi, i: (bi * nt + i, 0))],
        out_shape=[jax.ShapeDtypeStruct((n, qw), BF16),
                   jax.ShapeDtypeStruct((n, kw), BF16),
                   jax.ShapeDtypeStruct((n, kw), BF16)],
        compiler_params=_cp(("parallel", "parallel")),
        name="gqa_prep",
    )(main, main, main, qnw, knw, gsum, rot, cos, sin)


FLASH_CHAIN_ROWS = 256


def _flash_body(q_ref, k_ref, v_ref, o_ref, m_ref, l_ref, acc_ref, *, chains):
    ki = pl.program_id(3)

    @pl.when(ki == 0)
    def _():
        m_ref[...] = jnp.full_like(m_ref, -jnp.inf)
        l_ref[...] = jnp.zeros_like(l_ref)
        acc_ref[...] = jnp.zeros_like(acc_ref)

    k, vt = k_ref[0, 0], v_ref[0, 0]
    ss = [jnp.dot(k, q_ref[0, 0, gi, :, r0:r0 + rc], preferred_element_type=F32)
          for gi, r0, rc in chains]
    m_prev = [m_ref[ci] for ci in range(len(chains))]
    m_new = [jnp.maximum(mp, jnp.max(s, axis=0, keepdims=True)) for mp, s in zip(m_prev, ss)]
    ps = [jnp.exp(s - mn) for s, mn in zip(ss, m_new)]
    alphas = [jnp.exp(mp - mn) for mp, mn in zip(m_prev, m_new)]
    pv = [jnp.dot(vt, p.astype(BF16), preferred_element_type=F32) for p in ps]
    for ci in range(len(chains)):
        l_ref[ci] = alphas[ci] * l_ref[ci] + jnp.sum(ps[ci], axis=0, keepdims=True)
        acc_ref[ci] = alphas[ci] * acc_ref[ci] + pv[ci]
        m_ref[ci] = m_new[ci]

    @pl.when(ki == pl.num_programs(3) - 1)
    def _():
        for ci, (gi, r0, rc) in enumerate(chains):
            o_ref[0, 0, gi, :, r0:r0 + rc] = (acc_ref[ci] / l_ref[ci]).astype(o_ref.dtype)


def flash_attention(q, k, v, tq, tk):
    b, hkv, g, sq, dk = q.shape
    sk = k.shape[2]
    dv = v.shape[3]
    tq = min(tq, sq)
    tk = min(tk, sk)
    rc = min(FLASH_CHAIN_ROWS, tq)
    chains = tuple((gi, r0, rc) for gi in range(g) for r0 in range(0, tq, rc))
    nch = len(chains)
    body = functools.partial(_flash_body, chains=chains)
    qt = jnp.swapaxes(q, 3, 4)
    vt = jnp.swapaxes(v, 2, 3)
    out_t = pl.pallas_call(
        body,
        grid=(b, hkv, sq // tq, sk // tk),
        in_specs=[pl.BlockSpec((1, 1, g, dk, tq), lambda bi, h, qi, ki: (bi, h, 0, 0, qi)),
                  pl.BlockSpec((1, 1, tk, dk), lambda bi, h, qi, ki: (bi, h, ki, 0)),
                  pl.BlockSpec((1, 1, dv, tk), lambda bi, h, qi, ki: (bi, h, 0, ki))],
        out_specs=pl.BlockSpec((1, 1, g, dv, tq), lambda bi, h, qi, ki: (bi, h, 0, 0, qi)),
        out_shape=jax.ShapeDtypeStruct((b, hkv, g, dv, sq), BF16),
        scratch_shapes=[pltpu.VMEM((nch, 1, rc), F32), pltpu.VMEM((nch, 1, rc), F32),
                        pltpu.VMEM((nch, dv, rc), F32)],
        compiler_params=_cp(("parallel", "parallel", "parallel", "arbitrary")),
        name="flash_attention",
    )(qt, k, vt)
    return jnp.swapaxes(out_t, 3, 4)


def _router_body(x_ref, nw_ref, sh_ref, sc_ref, wrh_ref, wrl_ref, rb_ref, tri_ref,
                 h_ref, ri_ref, rw_ref, cnt_ref, carry_ref):
    i = pl.program_id(0)

    @pl.when(i == 0)
    def _():
        carry_ref[...] = jnp.zeros_like(carry_ref)

    xf = x_ref[...]
    y = xf * lax.rsqrt(jnp.mean(xf * xf, axis=-1, keepdims=True) + NORM_EPS) * nw_ref[...]
    h = y * (1.0 + sc_ref[0]) + sh_ref[0]
    h_ref[...] = h
    hi = h.astype(BF16)
    lo = (h - hi.astype(F32)).astype(BF16)
    nt = (((1,), (1,)), ((), ()))
    logits = (lax.dot_general(wrh_ref[...], hi, nt, preferred_element_type=F32)
              + lax.dot_general(wrh_ref[...], lo, nt, preferred_element_type=F32)
              + lax.dot_general(wrl_ref[...], hi, nt, preferred_element_type=F32))
    scores = _sigmoid(logits)
    sel = scores + rb_ref[...]
    s = [scores[e:e + 1] for e in range(N_EXPERTS)]
    v = [sel[e:e + 1] for e in range(N_EXPERTS)]
    epg = EXPERTS_PER_GROUP
    gscore = []
    for gi in range(N_GROUPS):
        mem = v[gi * epg:(gi + 1) * epg]
        best = None
        for a in range(epg):
            for c in range(a + 1, epg):
                pair = mem[a] + mem[c]
                best = pair if best is None else jnp.maximum(best, pair)
        gscore.append(best)
    is_best = []
    for gi in range(N_GROUPS):
        ok = None
        for gj in range(N_GROUPS):
            if gj == gi:
                continue
            c = (gscore[gi] > gscore[gj]) if gj < gi else (gscore[gi] >= gscore[gj])
            ok = c if ok is None else (ok & c)
        is_best.append(ok)
    chosen = []
    for e in range(N_EXPERTS):
        gi = e // epg
        rank = jnp.zeros_like(v[e])
        for e2 in range(gi * epg, (gi + 1) * epg):
            if e2 == e:
                continue
            ahead = (v[e2] >= v[e]) if e2 < e else (v[e2] > v[e])
            rank = rank + jnp.where(ahead, 1.0, 0.0)
        chosen.append(is_best[gi] & (rank < TOP_K))
    chosen_f = jnp.concatenate([jnp.where(cm, 1.0, 0.0) for cm in chosen], axis=0)
    total = None
    for e in range(N_EXPERTS):
        t = jnp.where(chosen[e], s[e], 0.0)
        total = t if total is None else total + t
    pos = jnp.dot(chosen_f.astype(BF16), tri_ref[...], preferred_element_type=F32) + carry_ref[...]
    carry_ref[...] += jnp.sum(chosen_f, axis=-1, keepdims=True)
    cnt_ref[...] = jnp.broadcast_to(carry_ref[...], cnt_ref.shape)
    e_lo = jnp.full_like(v[0], float(N_EXPERTS))
    e_hi = jnp.full_like(v[0], -1.0)
    for e in range(N_EXPERTS):
        e_lo = jnp.where(chosen[e], jnp.minimum(e_lo, float(e)), e_lo)
        e_hi = jnp.where(chosen[e], jnp.maximum(e_hi, float(e)), e_hi)
    zero = jnp.zeros_like(v[0])
    w_lo, w_hi, p_lo, p_hi = zero, zero, zero, zero
    for e in range(N_EXPERTS):
        pe = pos[e:e + 1]
        w_lo = jnp.where(e_lo == float(e), s[e], w_lo)
        w_hi = jnp.where(e_hi == float(e), s[e], w_hi)
        p_lo = jnp.where(e_lo == float(e), pe, p_lo)
        p_hi = jnp.where(e_hi == float(e), pe, p_hi)
    ri_ref[...] = jnp.concatenate([e_lo, e_hi, p_lo, p_hi, zero, zero, zero, zero], axis=0).astype(jnp.int32)
    rw_ref[...] = jnp.concatenate([w_lo / total, w_hi / total, zero, zero, zero, zero, zero, zero], axis=0)


def norm_route(x, nw, shift, scale, per_batch, w_router, router_bias):
    n, d = x.shape
    tm = _row_tile(512, n, per_batch)
    rf = _mod_row_fn(n, tm, per_batch)
    wr_t = w_router.T
    wr_hi = wr_t.astype(BF16)
    wr_lo = (wr_t - wr_hi.astype(F32)).astype(BF16)
    tri = jnp.asarray(np.triu(np.ones((tm, tm), np.float32), 1), BF16)
    const = lambda a: pl.BlockSpec(a.shape, lambda i: (0,) * a.ndim)
    rb = router_bias.reshape(N_EXPERTS, 1).astype(F32)
    return pl.pallas_call(
        _router_body,
        grid=(n // tm,),
        in_specs=[pl.BlockSpec((tm, d), lambda i: (i, 0)),
                  pl.BlockSpec((1, d), lambda i: (0, 0)),
                  pl.BlockSpec((1, 1, d), lambda i: (rf(i), 0, 0)),
                  pl.BlockSpec((1, 1, d), lambda i: (rf(i), 0, 0)),
                  const(wr_hi), const(wr_lo), const(rb), const(tri)],
        out_specs=[pl.BlockSpec((tm, d), lambda i: (i, 0)),
                   pl.BlockSpec((8, tm), lambda i: (0, i)),
                   pl.BlockSpec((8, tm), lambda i: (0, i)),
                   pl.BlockSpec((N_EXPERTS, LANES), lambda i: (0, 0))],
        out_shape=[jax.ShapeDtypeStruct((n, d), F32),
                   jax.ShapeDtypeStruct((8, n), jnp.int32),
                   jax.ShapeDtypeStruct((8, n), F32),
                   jax.ShapeDtypeStruct((N_EXPERTS, LANES), F32)],
        scratch_shapes=[pltpu.VMEM((N_EXPERTS, 1), F32)],
        compiler_params=_cp(("arbitrary",)),
        name="norm_route",
    )(x, nw.reshape(1, d), shift.reshape(MOD_ROWS, 1, d), scale.reshape(MOD_ROWS, 1, d),
      wr_hi, wr_lo, rb, tri)


def _dispatch_body(sa_ref, sb_ref, pad_ref, h_ref, xs_ref, zero_ref, sem, *, tm, n_pad):
    i = pl.program_id(0)
    base = i * tm

    def row_copy(src, r, slot):
        return pltpu.make_async_copy(src.at[pl.ds(r, 1)], xs_ref.at[pl.ds(slot, 1)], sem)

    @pl.when(i == 0)
    def _():
        zero_ref[...] = jnp.zeros_like(zero_ref)

        def fill(j, carry):
            row_copy(zero_ref, 0, pad_ref[2 * j]).start(priority=0)
            row_copy(zero_ref, 1, pad_ref[2 * j + 1]).start(priority=1)
            return carry
        lax.fori_loop(0, n_pad // 2, fill, 0, unroll=DMA_UNROLL)

        def drain(j, carry):
            row_copy(zero_ref, 0, 0).wait()
            return carry
        lax.fori_loop(0, n_pad, drain, 0, unroll=DMA_UNROLL)

    def issue(r, carry):
        row_copy(h_ref, r, sa_ref[base + r]).start(priority=0)
        row_copy(h_ref, r, sb_ref[base + r]).start(priority=1)
        return carry
    lax.fori_loop(0, tm, issue, 0, unroll=DMA_UNROLL)

    def drain2(r, carry):
        row_copy(h_ref, 0, 0).wait()
        row_copy(h_ref, 0, 0).wait()
        return carry
    lax.fori_loop(0, tm, drain2, 0, unroll=DMA_UNROLL)


def moe_dispatch(h, slot_a, slot_b, pad_slots, n_slots):
    n, d = h.shape
    tm = min(256, n)
    n_pad = pad_slots.shape[0]
    body = functools.partial(_dispatch_body, tm=tm, n_pad=n_pad)
    return pl.pallas_call(
        body,
        grid_spec=pltpu.PrefetchScalarGridSpec(
            num_scalar_prefetch=3,
            grid=(n // tm,),
            in_specs=[pl.BlockSpec((tm, d), lambda i, sa, sb, pd: (i, 0))],
            out_specs=pl.BlockSpec(memory_space=pl.ANY),
            scratch_shapes=[pltpu.VMEM((8, d), F32), pltpu.SemaphoreType.DMA(())]),
        out_shape=jax.ShapeDtypeStruct((n_slots, d), F32),
        compiler_params=_cp(("arbitrary",)),
        name="moe_dispatch",
    )(slot_a, slot_b, pad_slots, h)


def _experts_body(te_ref, nu_ref, xs_ref, wg_ref, wu_ref, wd_ref, y_ref):
    i = pl.program_id(0)

    @pl.when(i < nu_ref[0])
    def _():
        xb = xs_ref[...].astype(BF16)
        hg = jnp.dot(xb, wg_ref[0].astype(BF16), preferred_element_type=F32)
        hu = jnp.dot(xb, wu_ref[0].astype(BF16), preferred_element_type=F32)
        act = (hg * _sigmoid(hg) * hu).astype(BF16)
        y_ref[...] = jnp.dot(act, wd_ref[0].astype(BF16), preferred_element_type=F32)

    @pl.when(i >= nu_ref[0])
    def _():
        y_ref[...] = jnp.zeros_like(y_ref)


def moe_experts(xs, tile_expert, n_used, w_gate, w_up, w_down, base):
    s, d = xs.shape
    f = w_gate.shape[2]
    tm = MOE_TILE
    return pl.pallas_call(
        _experts_body,
        grid_spec=pltpu.PrefetchScalarGridSpec(
            num_scalar_prefetch=2,
            grid=(s // tm,),
            in_specs=[pl.BlockSpec((tm, d), lambda i, te, nu: (jnp.minimum(i, nu[0] - 1), 0)),
                      pl.BlockSpec((1, d, f), lambda i, te, nu: (te[i] + base, 0, 0)),
                      pl.BlockSpec((1, d, f), lambda i, te, nu: (te[i] + base, 0, 0)),
                      pl.BlockSpec((1, f, d), lambda i, te, nu: (te[i] + base, 0, 0))],
            out_specs=pl.BlockSpec((tm, d), lambda i, te, nu: (i, 0))),
        out_shape=jax.ShapeDtypeStruct((s, d), F32),
        compiler_params=_cp(("arbitrary",)),
        name="moe_experts",
    )(tile_expert, n_used, xs, w_gate, w_up, w_down)


DMA_UNROLL = 8


def _combine_body(sa_ref, sb_ref, x_ref, w_ref, g_ref, nw_ref, sh_ref, sc_ref, y_ref, *rest, tm, final):
    if final:
        o_ref, ba_ref, bb_ref, sem = rest
    else:
        o_ref, h_ref, ba_ref, bb_ref, sem = rest
    i = pl.program_id(0)
    n_tiles = pl.num_programs(0)

    def row_copy(slot, dst, buf, r):
        return pltpu.make_async_copy(y_ref.at[pl.ds(slot, 1)], dst.at[buf, pl.ds(r, 1)], sem.at[buf])

    def issue_tile(tile, buf):
        base = tile * tm

        def issue(r, carry):
            row_copy(sa_ref[base + r], ba_ref, buf, r).start(priority=0)
            row_copy(sb_ref[base + r], bb_ref, buf, r).start(priority=1)
            return carry
        lax.fori_loop(0, tm, issue, 0, unroll=DMA_UNROLL)

    @pl.when(i == 0)
    def _():
        issue_tile(0, 0)

    @pl.when(i + 1 < n_tiles)
    def _():
        issue_tile(i + 1, (i + 1) % 2)

    buf = i % 2

    def drain(r, carry):
        row_copy(0, ba_ref, buf, 0).wait()
        row_copy(0, bb_ref, buf, 0).wait()
        return carry
    lax.fori_loop(0, tm, drain, 0, unroll=DMA_UNROLL)

    w = w_ref[...]
    mix = w[:, 0:1] * ba_ref[buf] + w[:, 1:2] * bb_ref[buf]
    xn = x_ref[...] + g_ref[0] * mix
    y = xn * lax.rsqrt(jnp.mean(xn * xn, axis=-1, keepdims=True) + NORM_EPS) * nw_ref[...]
    if final:
        o_ref[...] = y
    else:
        o_ref[...] = xn
        h_ref[...] = (y * (1.0 + sc_ref[0]) + sh_ref[0]).astype(h_ref.dtype)


def moe_combine(x, y, slot_a, slot_b, wts, gate, per_batch, next_nw, next_shift, next_scale):
    n, d = x.shape
    tm = _row_tile(256, n, per_batch)
    rf = _mod_row_fn(n, tm, per_batch)
    final = next_shift is None
    if final:
        next_shift = next_scale = jnp.zeros((MOD_ROWS, d), F32)
    body = functools.partial(_combine_body, tm=tm, final=final)
    row = pl.BlockSpec((tm, d), lambda i, sa, sb: (i, 0))
    mod_row = pl.BlockSpec((1, 1, d), lambda i, sa, sb: (rf(i), 0, 0))
    out_specs = [row] if final else [row, row]
    out_shape = [jax.ShapeDtypeStruct((n, d), F32)] + ([] if final else [jax.ShapeDtypeStruct((n, d), BF16)])
    return pl.pallas_call(
        body,
        grid_spec=pltpu.PrefetchScalarGridSpec(
            num_scalar_prefetch=2,
            grid=(n // tm,),
            in_specs=[row,
                      pl.BlockSpec((tm, 8), lambda i, sa, sb: (i, 0)),
                      mod_row,
                      pl.BlockSpec((1, d), lambda i, sa, sb: (0, 0)),
                      mod_row, mod_row,
                      pl.BlockSpec(memory_space=pl.ANY)],
            out_specs=out_specs,
            scratch_shapes=[pltpu.VMEM((2, tm, d), F32), pltpu.VMEM((2, tm, d), F32),
                            pltpu.SemaphoreType.DMA((2,))]),
        out_shape=out_shape,
        compiler_params=_cp(("arbitrary",)),
        name="moe_combine",
    )(slot_a, slot_b, x, wts, gate.reshape(MOD_ROWS, 1, d), next_nw.reshape(1, d),
      next_shift.reshape(MOD_ROWS, 1, d), next_scale.reshape(MOD_ROWS, 1, d), y)


def moe_layer(x, nw, mod, per_batch, w_router, router_bias, w_gate, w_up, w_down, base, next_norm):
    n, d = x.shape
    h, route_i, route_w, counts = norm_route(x, nw, mod[:, 3], mod[:, 4], per_batch, w_router, router_bias)
    cnt = counts[:, 0].astype(jnp.int32)
    seg = ((cnt + MOE_TILE - 1) // MOE_TILE) * MOE_TILE
    off = jnp.concatenate([jnp.zeros((1,), jnp.int32), jnp.cumsum(seg)])
    n_slots = TOP_K * n + N_EXPERTS * MOE_TILE
    slot_a = off[route_i[0]] + route_i[2]
    slot_b = off[route_i[1]] + route_i[3]
    n_pad = n_slots - TOP_K * n
    padcnt = seg - cnt
    padstart = jnp.concatenate([jnp.zeros((1,), jnp.int32), jnp.cumsum(padcnt)])
    j = jnp.arange(n_pad, dtype=jnp.int32)
    count_le = lambda edges, v: jnp.sum((edges[None, :] <= v[:, None]).astype(jnp.int32), axis=1)
    e_of = jnp.clip(count_le(padstart, j) - 1, 0, N_EXPERTS)
    in_seg = off[jnp.minimum(e_of, N_EXPERTS - 1)] + cnt[jnp.minimum(e_of, N_EXPERTS - 1)] + (j - padstart[e_of])
    tail = off[N_EXPERTS] + (j - padstart[N_EXPERTS])
    pad_slots = jnp.where(e_of < N_EXPERTS, in_seg, tail).astype(jnp.int32)
    n_tiles = n_slots // MOE_TILE
    tile_start = jnp.arange(n_tiles, dtype=jnp.int32) * MOE_TILE
    n_used = (off[N_EXPERTS] // MOE_TILE).astype(jnp.int32).reshape(1)
    tile_expert = jnp.clip(count_le(off, tile_start) - 1, 0, N_EXPERTS - 1).astype(jnp.int32)
    last_used = tile_expert[jnp.maximum(n_used[0] - 1, 0)]
    tile_expert = jnp.where(jnp.arange(n_tiles) < n_used[0], tile_expert, last_used)

    xs = moe_dispatch(h, slot_a, slot_b, pad_slots, n_slots)
    y = moe_experts(xs, tile_expert, n_used, w_gate, w_up, w_down, base)
    wts = jnp.transpose(route_w)
    return moe_combine(x, y, slot_a, slot_b, wts, mod[:, 5], per_batch, *next_norm)


def _reorder_w_in(w):
    o = np.cumsum((0,) + IN_WIDTHS)
    seg = lambda i: w[:, o[i]:o[i + 1]]
    main = jnp.concatenate([seg(0), seg(1), seg(2), seg(3), seg(6), seg(9), seg(12), seg(7), seg(10), seg(11)], axis=1)
    misc = jnp.concatenate([seg(8), seg(4), seg(5)], axis=1)
    misc = jnp.pad(misc, ((0, 0), (0, MISC_W - misc.shape[1])))
    return main.astype(BF16), misc.astype(BF16)


def _attention_branches(main_l, misc_l, main_c, misc_c, b, seq, ctx_len, ctx_out, rope_mla, rope_gqa,
                        mla_w, mla_qn, mla_kvn, gqa_qn, gqa_kn):
    g = GQA_HEADS // GQA_KV_HEADS
    hd = GQA_HEAD_DIM

    mq_l, mk_l, mv_l = mla_prepare(main_l, misc_l, b, seq, mla_w, mla_qn, mla_kvn, _mla_tables(rope_mla, seq))
    mq_c, mk_c, mv_c = mla_prepare(main_c, misc_c, b, ctx_len, mla_w, mla_qn, mla_kvn, _mla_tables(None, ctx_len))
    tk_all = (seq + ctx_len) // 2
    cat = lambda lat, ctx_: jnp.concatenate([lat, ctx_], axis=2)
    mla_l = flash_attention(mq_l[:, :, None], cat(mk_l, mk_c), cat(mv_l, mv_c), 2048, tk_all)
    mla_l = jnp.transpose(mla_l[:, :, 0], (0, 2, 1, 3)).reshape(b * seq, BRANCH_W)

    def split_heads(t, length, heads):
        return jnp.transpose(t.reshape(b, length, heads, hd), (0, 2, 1, 3))

    gq_l, gk_l, gv_l = gqa_prepare(main_l, b, seq, gqa_qn, gqa_kn, rope_gqa)
    gq_c, gk_c, gv_c = gqa_prepare(main_c, b, ctx_len, gqa_qn, gqa_kn, None)
    gq_l5 = split_heads(gq_l, seq, GQA_HEADS).reshape(b, GQA_KV_HEADS, g, seq, hd)
    gk_l4, gv_l4 = split_heads(gk_l, seq, GQA_KV_HEADS), split_heads(gv_l, seq, GQA_KV_HEADS)
    gk_c4, gv_c4 = split_heads(gk_c, ctx_len, GQA_KV_HEADS), split_heads(gv_c, ctx_len, GQA_KV_HEADS)
    gqa_l = flash_attention(gq_l5, cat(gk_l4, gk_c4), cat(gv_l4, gv_c4), 512, tk_all)
    gqa_l = jnp.transpose(gqa_l.reshape(b, GQA_HEADS, seq, hd), (0, 2, 1, 3)).reshape(b * seq, BRANCH_W)

    mla_c = gqa_c = None
    if ctx_out:
        mla_c = flash_attention(mq_c[:, :, None], mk_c, mv_c, 256, 256)
        mla_c = jnp.transpose(mla_c[:, :, 0], (0, 2, 1, 3)).reshape(b * ctx_len, BRANCH_W)
        gq_c5 = split_heads(gq_c, ctx_len, GQA_HEADS).reshape(b, GQA_KV_HEADS, g, ctx_len, hd)
        gqa_c = flash_attention(gq_c5, gk_c4, gv_c4, 256, 256)
        gqa_c = jnp.transpose(gqa_c.reshape(b, GQA_HEADS, ctx_len, hd), (0, 2, 1, 3)).reshape(b * ctx_len, BRANCH_W)
    return mla_l, gqa_l, mla_c, gqa_c


def kernel(x, c, ctx, c_ctx, w_ada, b_ada, norm1_w, norm2_w, w_in,
           gdn_conv_w, gdn_a_log, gdn_dt_bias, gdn_norm_w,
           mla_q_norm_w, mla_kv_norm_w, mla_w_uq, mla_w_ukv,
           gqa_q_norm_w, gqa_k_norm_w,
           hy_conv_w, hy_w1, hy_b1, hy_w2, hy_b2, hy_w3, hy_sin_freq, hy_bias,
           w_branch, w_out, w_router, router_bias,
           moe_w_gate, moe_w_up, moe_w_down, final_norm_w):
    b, seq, d = x.shape
    ctx_len = ctx.shape[1]
    rows = seq // GRID_W
    rope_mla = axial_rope_tables(rows, MLA_ROPE)
    rope_gqa = axial_rope_tables(rows, GQA_HEAD_DIM)
    mod_all = ada_modulation(c, c_ctx, w_ada, b_ada)
    xl = x.reshape(b * seq, d)
    xc = ctx.reshape(b * ctx_len, d)
    f32 = lambda t: t.astype(F32)
    for layer in range(DEPTH):
        ctx_out = layer < DEPTH - 1
        mod = mod_all[layer]
        w_main, w_misc = _reorder_w_in(w_in[layer][:, :MIX_IN])
        w_gates = w_in[layer][:, MIX_IN:].astype(BF16)
        w_br = w_branch[layer].astype(BF16)
        w_o = w_out[layer].astype(BF16)
        wg, wu, wd = (t.reshape((DEPTH * N_EXPERTS,) + t.shape[2:]) for t in (moe_w_gate, moe_w_up, moe_w_down))

        if layer == 0:
            hl = norm_modulate(xl, norm1_w[layer], mod[:, 0], mod[:, 1], seq)
            hc = norm_modulate(xc, norm1_w[layer], mod[:, 0], mod[:, 1], None)
        main_l, main_c = matmul(hl, w_main, BF16, 1024), matmul(hc, w_main, BF16, 1024)
        misc_l, misc_c = matmul(hl, w_misc, F32, MISC_W), matmul(hc, w_misc, F32, MISC_W)
        if ctx_out:
            nxt = mod_all[layer + 1]
            next_norm = (norm1_w[layer + 1], nxt[:, 0], nxt[:, 1])
        else:
            next_norm = (final_norm_w, None, None)

        gdn_l, gdn_c = gdn_branch(main_l, misc_l, main_c, misc_c, b, seq, ctx_len, gdn_conv_w[layer],
                                  gdn_a_log[layer], gdn_dt_bias[layer], gdn_norm_w[layer], ctx_out)

        mla_w = _mla_weights(mla_q_norm_w[layer], mla_kv_norm_w[layer], mla_w_uq[layer], mla_w_ukv[layer])
        mla_l, gqa_l, mla_c, gqa_c = _attention_branches(
            main_l, misc_l, main_c, misc_c, b, seq, ctx_len, ctx_out, rope_mla, rope_gqa,
            mla_w, mla_q_norm_w[layer], mla_kv_norm_w[layer], gqa_q_norm_w[layer], gqa_k_norm_w[layer])

        hy_params = (hy_w1[layer], hy_b1[layer], hy_w2[layer], hy_b2[layer], hy_w3[layer], hy_sin_freq[layer])
        hy_l = hyena_branch(main_l, b, seq, hy_conv_w[layer], hyena_filters(seq, *hy_params), hy_bias[layer])

        branches_l = [gdn_l.reshape(b * seq, BRANCH_W).astype(BF16), mla_l, gqa_l,
                      hy_l.reshape(b * seq, BRANCH_W).astype(BF16)]
        merged_l = merge_branches_gated(hl, branches_l, w_gates, w_br)

        if ctx_out:
            hy_c = hyena_branch(main_c, b, ctx_len, hy_conv_w[layer], hyena_filters(ctx_len, *hy_params),
                                hy_bias[layer])
            branches_c = [gdn_c.reshape(b * ctx_len, BRANCH_W).astype(BF16), mla_c, gqa_c,
                          hy_c.reshape(b * ctx_len, BRANCH_W).astype(BF16)]
            merged_c = merge_branches_gated(hc, branches_c, w_gates, w_br)
            xc = matmul_gated_residual(merged_c, w_o, xc, mod[:, 2], None)
            xc, hc = moe_layer(xc, norm2_w[layer], mod, None, w_router, router_bias, wg, wu, wd,
                               layer * N_EXPERTS, next_norm)

        xl = matmul_gated_residual(merged_l, w_o, xl, mod[:, 2], seq)
        outs = moe_layer(xl, norm2_w[layer], mod, seq, w_router, router_bias, wg, wu, wd,
                         layer * N_EXPERTS, next_norm)
        if ctx_out:
            xl, hl = outs
    return outs[0].reshape(b, seq, d)
```
